```python
import jax, jax.numpy as jnp
from jax import lax
import numpy as np

D_MODEL = 1024
BATCH = 8
SEQ = 2048
DEPTH = 4
DEC_BATCH = 128
DEC_SEQ = 1
PAST_LEN = 16384
PAGE_SIZE = 128

N_MEM = 256
D_LRU = D_MODEL // 2
LRU_BW = 64
LRU_BLOCKS = D_LRU // LRU_BW
LRU_CONV = 4
LRU_C = 8.0
D_RWKV = D_MODEL // 2
RWKV_HEAD = 64
RWKV_HEADS = D_RWKV // RWKV_HEAD
R_DECAY = 64
R_AAA = 64
R_GATE = 128
D_RWKV_IN = 3 * D_RWKV + R_DECAY + R_AAA + R_GATE
GN_EPS = 64e-5
D_POOL = D_MODEL // 2
POOL_WINDOWS = (2, 4, 8, 16)
POOL_GROUPS = len(POOL_WINDOWS)
POOL_GW = D_POOL // POOL_GROUPS
POOL_BUF = max(POOL_WINDOWS) - 1
D_GMLP = D_MODEL // 2
GMLP_GROUPS = 4
GMLP_GW = D_GMLP // GMLP_GROUPS
CHUNK = 128
N_BRANCH = 4
D_IN = 2 * D_LRU + D_RWKV_IN + D_POOL + 2 * D_GMLP + N_BRANCH * D_MODEL
X_HEADS = 4
X_HEAD_DIM = D_MODEL // X_HEADS
D_FF = 3 * D_MODEL
FFN_CONV = 3
EPS = 1e-6

kernel_name = "hybrid_lru_rwkv7_pool_gmlp_decoder_step"

F32 = jnp.float32


def rmsnorm(x, g):
    xf = x.astype(F32)
    y = xf * lax.rsqrt(jnp.mean(xf * xf, axis=-1, keepdims=True) + EPS)
    return (y * g.astype(F32)).astype(x.dtype)


def layernorm(x, g, b, eps):
    xf = x.astype(F32)
    mu = jnp.mean(xf, axis=-1, keepdims=True)
    var = jnp.mean(jnp.square(xf - mu), axis=-1, keepdims=True)
    return ((xf - mu) * lax.rsqrt(var + eps) * g.astype(F32) + b.astype(F32)).astype(x.dtype)


def causal_dwconv(x, buf, w, b):
    K = w.shape[0]
    T = x.shape[1]
    xp = jnp.concatenate([buf.astype(x.dtype), x], axis=1)
    y = b + sum(w[k] * xp[:, k:k + T] for k in range(K))
    return y, xp[:, -(K - 1):]


def linear_recurrence(a, b, h0):
    b = b.at[:, 0].add(a[:, 0] * h0)

    def comb(l, r):
        al, bl = l
        ar, br = r
        return al * ar, ar * bl + br

    _, h = lax.associative_scan(comb, (a, b), axis=1)
    return h


def rglru_branch(x_in, gate_in, conv_buf, h0, conv_w, conv_b, w_ra, b_ra, w_ix, b_ix, lam):
    xc, new_buf = causal_dwconv(x_in, conv_buf, conv_w, conv_b)
    B, T, _ = xc.shape
    xb = xc.reshape(B, T, LRU_BLOCKS, LRU_BW)
    r = jax.nn.sigmoid(jnp.einsum('btgi,gij->btgj', xb, w_ra).reshape(B, T, D_LRU) + b_ra)
    i = jax.nn.sigmoid(jnp.einsum('btgi,gij->btgj', xb, w_ix).reshape(B, T, D_LRU) + b_ix)
    log_a = -LRU_C * r.astype(F32) * jax.nn.softplus(-lam.astype(F32))
    a = jnp.exp(log_a)
    mult = jnp.sqrt(-jnp.expm1(2.0 * log_a))
    h = linear_recurrence(a, mult * (i * xc).astype(F32), h0.astype(F32))
    y = h.astype(x_in.dtype) * jax.nn.gelu(gate_in)
    return y, new_buf, h[:, -1].astype(h0.dtype)


def rwkv7_branch(p, shift_buf, S0, mu, w0, w2, a0, a2, g2, k_k, k_a, r_k, ln_g, ln_b):
    pp = jnp.concatenate([shift_buf.astype(p.dtype), p], axis=1)
    px = p + (pp[:, :-1] - p) * mu
    new_shift = pp[:, -1:]
    o1 = D_RWKV
    o2 = 2 * D_RWKV
    o3 = 3 * D_RWKV
    o4 = o3 + R_DECAY
    o5 = o4 + R_AAA
    r, k, v, w_lo, a_lo, g_lo = jnp.split(px, [o1, o2, o3, o4, o5], axis=-1)
    w_log = -jax.nn.softplus(-(w0 + jnp.tanh(w_lo) @ w2)) - 0.5
    decay = jnp.exp(-jnp.exp(w_log.astype(F32)))
    a = jax.nn.sigmoid(a0 + a_lo @ a2)
    g = jax.nn.sigmoid(g_lo) @ g2
    B, T, _ = r.shape

    def heads(z):
        return z.reshape(B, T, RWKV_HEADS, RWKV_HEAD).astype(F32)

    kk = heads(k * k_k)
    kk = kk * lax.rsqrt(jnp.maximum(jnp.sum(kk * kk, axis=-1, keepdims=True), 1e-24))
    k = k * (1.0 + (a - 1.0) * k_a)
    rh, kh, vh, ah, wh = heads(r), heads(k), heads(v), heads(a), heads(decay)

    def step(S, inp):
        r_t, w_t, k_t, v_t, kk_t, a_t = inp
        sa = jnp.einsum('bhij,bhj->bhi', S, -kk_t)
        S = (S * w_t[:, :, None, :] + sa[..., None] * (kk_t * a_t)[:, :, None, :]
             + v_t[..., None] * k_t[:, :, None, :])
        return S, jnp.einsum('bhij,bhj->bhi', S, r_t)

    xs = tuple(jnp.moveaxis(z, 1, 0) for z in (rh, wh, kh, vh, kk, ah))
    S_T, y = lax.scan(step, S0.astype(F32), xs)
    y = jnp.moveaxis(y, 0, 1)
    mean = jnp.mean(y, axis=-1, keepdims=True)
    var = jnp.mean(jnp.square(y - mean), axis=-1, keepdims=True)
    y = ((y - mean) * lax.rsqrt(var + GN_EPS)).reshape(B, T, D_RWKV) * ln_g + ln_b
    bonus = jnp.sum(rh * kh * r_k, axis=-1, keepdims=True) * vh
    y = (y + bonus.reshape(B, T, D_RWKV)).astype(p.dtype) * g
    return y, new_shift, S_T.astype(S0.dtype)


def pool_branch(c, buf, pos0, w_pool, scale):
    B, T, _ = c.shape
    cp = jnp.concatenate([buf.astype(c.dtype), c], axis=1)
    cs = jnp.concatenate([jnp.zeros((B, 1, D_POOL), F32), jnp.cumsum(cp.astype(F32), axis=1)], axis=1)
    pos = pos0 + jnp.arange(T)
    end = cs[:, POOL_BUF + 1:]
    means = []
    for gi, win in enumerate(POOL_WINDOWS):
        sl = slice(gi * POOL_GW, (gi + 1) * POOL_GW)
        start = cs[:, POOL_BUF + 1 - win:POOL_BUF + 1 - win + T, sl]
        cnt = jnp.minimum(pos + 1, win).astype(F32)[None, :, None]
        means.append((end[..., sl] - start) / cnt)
    mean = jnp.concatenate(means, axis=-1)
    d = (mean - c.astype(F32)).astype(c.dtype).reshape(B, T, POOL_GROUPS, POOL_GW)
    y = jnp.einsum('btgi,gij->btgj', d, w_pool).reshape(B, T, D_POOL) * scale
    return y, cp[:, -POOL_BUF:]


def gmlp_branch(u, v, pos0, ln_g, ln_b, w_s, b_s):
    u = jax.nn.gelu(u)
    z = layernorm(jax.nn.gelu(v), ln_g, ln_b, 1e-5)
    B, T, _ = z.shape
    zg = z.reshape(B, T, GMLP_GROUPS, GMLP_GW)
    if T % CHUNK == 0 and pos0 % CHUNK == 0:
        W = jnp.where(jnp.tril(jnp.ones((CHUNK, CHUNK), bool)), w_s, 0.0)
        zc = zg.reshape(B, T // CHUNK, CHUNK, GMLP_GROUPS, GMLP_GW)
        s = jnp.einsum('gij,bnjgc->bnigc', W, zc) + b_s[:, :, None]
        s = s.reshape(B, T, D_GMLP)
    else:
        pos = pos0 + jnp.arange(T)
        off = pos % CHUNK
        cid = pos // CHUNK
        M = w_s[:, off[:, None], off[None, :]]
        valid = (cid[:, None] == cid[None, :]) & (pos[None, :] <= pos[:, None])
        M = jnp.where(valid, M, 0.0)
        s = jnp.einsum('gij,bjgc->bigc', M, zg) + b_s[off][:, :, None]
        s = s.reshape(B, T, D_GMLP)
    return u * s, z


def mem_kv(mem, g, w_k, w_v):
    mn = rmsnorm(mem, g)
    B = mem.shape[0]
    k = (mn @ w_k).reshape(B, N_MEM, X_HEADS, X_HEAD_DIM)
    v = (mn @ w_v).reshape(B, N_MEM, X_HEADS, X_HEAD_DIM)
    return k, v


def cross_attn(xn, k, v, w_q, w_o):
    B, T, _ = xn.shape
    q = (xn @ w_q).reshape(B, T, X_HEADS, X_HEAD_DIM)
    s = jnp.einsum('bthd,bmhd->bhtm', q, k.astype(q.dtype)).astype(F32) * (X_HEAD_DIM ** -0.5)
    pr = jax.nn.softmax(s, axis=-1).astype(xn.dtype)
    o = jnp.einsum('bhtm,bmhd->bthd', pr, v.astype(xn.dtype)).reshape(B, T, D_MODEL)
    return o @ w_o


def conv_ffn(xn, buf, w_up, conv_w, conv_b, w_down):
    hc, new_buf = causal_dwconv(xn @ w_up, buf, conv_w, conv_b)
    gate, val = jnp.split(hc, 2, axis=-1)
    return (jax.nn.gelu(gate) * val) @ w_down, new_buf


def layer(x, mk, mv, st, pos0, prm):
    lru_conv, lru_h, rw_shift, rw_S, pool_buf, ffn_buf = st
    B, T, _ = x.shape
    xn = rmsnorm(x, prm['g_mix'])
    z = xn @ prm['w_in']
    o1 = D_LRU
    o2 = 2 * D_LRU
    o3 = o2 + D_RWKV_IN
    o4 = o3 + D_POOL
    o5 = o4 + D_GMLP
    o6 = o5 + D_GMLP
    xa, ga, pb, pc, u, v, gates = jnp.split(z, [o1, o2, o3, o4, o5, o6], axis=-1)
    yA, n_lru_conv, n_lru_h = rglru_branch(xa, ga, lru_conv, lru_h, prm['lru_conv_w'], prm['lru_conv_b'],
                                           prm['lru_w_ra'], prm['lru_b_ra'], prm['lru_w_ix'], prm['lru_b_ix'],
                                           prm['lru_lambda'])
    yB, n_shift, n_S = rwkv7_branch(pb, rw_shift, rw_S, prm['rwkv_mu'], prm['rwkv_w0'], prm['rwkv_w2'],
                                    prm['rwkv_a0'], prm['rwkv_a2'], prm['rwkv_g2'], prm['rwkv_k_k'],
                                    prm['rwkv_k_a'], prm['rwkv_r_k'], prm['rwkv_ln_g'], prm['rwkv_ln_b'])
    yC, n_pool = pool_branch(pc, pool_buf, pos0, prm['pool_w'], prm['pool_scale'])
    yD, chunk_v = gmlp_branch(u, v, pos0, prm['gmlp_ln_g'], prm['gmlp_ln_b'], prm['gmlp_w_s'], prm['gmlp_b_s'])
    gs = jax.nn.sigmoid(gates).reshape(B, T, N_BRANCH, D_MODEL)
    merged = (gs[:, :, 0] * (yA @ prm['w_pa']) + gs[:, :, 1] * (yB @ prm['w_pb'])
              + gs[:, :, 2] * (yC @ prm['w_pc']) + gs[:, :, 3] * (yD @ prm['w_pd']))
    x = x + merged @ prm['w_o']
    x = x + cross_attn(rmsnorm(x, prm['g_xattn']), mk, mv, prm['w_xq'], prm['w_xo'])
    f, n_ffn = conv_ffn(rmsnorm(x, prm['g_ffn']), ffn_buf, prm['w_up'], prm['ffn_conv_w'],
                        prm['ffn_conv_b'], prm['w_down'])
    x = x + f
    return x, (n_lru_conv, n_lru_h, n_shift, n_S, n_pool, n_ffn), chunk_v


def setup_inputs(seed: int = 0) -> dict:
    key = jax.random.key(seed)
    ks = iter(jax.random.split(key, 64))

    def nrm(shape, scale=1.0):
        return jax.random.normal(next(ks), shape, F32) * scale

    def gain(shape):
        return 1.0 + nrm(shape, 0.05)

    u = jax.random.uniform(next(ks), (DEPTH, D_LRU), F32, 0.9, 0.999)
    a_base = u ** (1.0 / LRU_C)
    lru_lambda = jnp.log(a_base) - jnp.log1p(-a_base)
    return {
        'x_prompt': nrm((BATCH, SEQ, D_MODEL)),
        'x_sample': nrm((DEC_BATCH, DEC_SEQ, D_MODEL)),
        'state_lru_conv': nrm((DEPTH, DEC_BATCH, LRU_CONV - 1, D_LRU)),
        'state_lru_h': nrm((DEPTH, DEC_BATCH, D_LRU), 0.5),
        'state_rwkv_shift': nrm((DEPTH, DEC_BATCH, 1, D_RWKV_IN)),
        'state_rwkv_S': nrm((DEPTH, DEC_BATCH, RWKV_HEADS, RWKV_HEAD, RWKV_HEAD), 0.5),
        'state_pool': nrm((DEPTH, DEC_BATCH, POOL_BUF, D_POOL)),
        'state_ffn_conv': nrm((DEPTH, DEC_BATCH, FFN_CONV - 1, 2 * D_FF)),
        'cache_mem_k': nrm((DEPTH, DEC_BATCH, N_MEM, X_HEADS, X_HEAD_DIM)),
        'cache_mem_v': nrm((DEPTH, DEC_BATCH, N_MEM, X_HEADS, X_HEAD_DIM)),
        'mem_prompt': nrm((BATCH, N_MEM, D_MODEL)),
        'g_mix': gain((DEPTH, D_MODEL)),
        'w_in': nrm((DEPTH, D_MODEL, D_IN), D_MODEL ** -0.5),
        'lru_conv_w': nrm((DEPTH, LRU_CONV, D_LRU), LRU_CONV ** -0.5),
        'lru_conv_b': nrm((DEPTH, D_LRU), 0.02),
        'lru_w_ra': nrm((DEPTH, LRU_BLOCKS, LRU_BW, LRU_BW), LRU_BW ** -0.5),
        'lru_b_ra': nrm((DEPTH, D_LRU), 0.02),
        'lru_w_ix': nrm((DEPTH, LRU_BLOCKS, LRU_BW, LRU_BW), LRU_BW ** -0.5),
        'lru_b_ix': nrm((DEPTH, D_LRU), 0.02),
        'lru_lambda': lru_lambda,
        'rwkv_mu': jax.random.uniform(next(ks), (DEPTH, D_RWKV_IN), F32),
        'rwkv_w0': nrm((DEPTH, D_RWKV), 0.5),
        'rwkv_w2': nrm((DEPTH, R_DECAY, D_RWKV), 0.5 * R_DECAY ** -0.5),
        'rwkv_a0': nrm((DEPTH, D_RWKV), 0.5),
        'rwkv_a2': nrm((DEPTH, R_AAA, D_RWKV), R_AAA ** -0.5),
        'rwkv_g2': nrm((DEPTH, R_GATE, D_RWKV), R_GATE ** -0.5),
        'rwkv_k_k': 0.85 + nrm((DEPTH, D_RWKV), 0.1),
        'rwkv_k_a': 1.0 + nrm((DEPTH, D_RWKV), 0.1),
        'rwkv_r_k': nrm((DEPTH, RWKV_HEADS, RWKV_HEAD), 0.1),
        'rwkv_ln_g': gain((DEPTH, D_RWKV)),
        'rwkv_ln_b': nrm((DEPTH, D_RWKV), 0.02),
        'pool_w': nrm((DEPTH, POOL_GROUPS, POOL_GW, POOL_GW), POOL_GW ** -0.5),
        'pool_scale': 1.0 + nrm((DEPTH, D_POOL), 0.1),
        'gmlp_ln_g': gain((DEPTH, D_GMLP)),
        'gmlp_ln_b': nrm((DEPTH, D_GMLP), 0.02),
        'gmlp_w_s': nrm((DEPTH, GMLP_GROUPS, CHUNK, CHUNK), CHUNK ** -0.5),
        'gmlp_b_s': 1.0 + nrm((DEPTH, CHUNK, GMLP_GROUPS), 0.02),
        'w_pa': nrm((DEPTH, D_LRU, D_MODEL), D_LRU ** -0.5),
        'w_pb': nrm((DEPTH, D_RWKV, D_MODEL), D_RWKV ** -0.5),
        'w_pc': nrm((DEPTH, D_POOL, D_MODEL), D_POOL ** -0.5),
        'w_pd': nrm((DEPTH, D_GMLP, D_MODEL), D_GMLP ** -0.5),
        'w_o': nrm((DEPTH, D_MODEL, D_MODEL), D_MODEL ** -0.5),
        'g_xattn': gain((DEPTH, D_MODEL)),
        'g_mem': gain((DEPTH, D_MODEL)),
        'w_xq': nrm((DEPTH, D_MODEL, D_MODEL), D_MODEL ** -0.5),
        'w_xk': nrm((DEPTH, D_MODEL, D_MODEL), D_MODEL ** -0.5),
        'w_xv': nrm((DEPTH, D_MODEL, D_MODEL), D_MODEL ** -0.5),
        'w_xo': nrm((DEPTH, D_MODEL, D_MODEL), D_MODEL ** -0.5),
        'g_ffn': gain((DEPTH, D_MODEL)),
        'w_up': nrm((DEPTH, D_MODEL, 2 * D_FF), D_MODEL ** -0.5),
        'ffn_conv_w': nrm((DEPTH, FFN_CONV, 2 * D_FF), FFN_CONV ** -0.5),
        'ffn_conv_b': nrm((DEPTH, 2 * D_FF), 0.02),
        'w_down': nrm((DEPTH, D_FF, D_MODEL), D_FF ** -0.5),
        'g_final': gain((D_MODEL,)),
    }


def reference(x_prompt, x_sample, state_lru_conv, state_lru_h, state_rwkv_shift, state_rwkv_S, state_pool,
              state_ffn_conv, cache_mem_k, cache_mem_v, mem_prompt, g_mix, w_in, lru_conv_w, lru_conv_b,
              lru_w_ra, lru_b_ra, lru_w_ix, lru_b_ix, lru_lambda, rwkv_mu, rwkv_w0, rwkv_w2, rwkv_a0, rwkv_a2,
              rwkv_g2, rwkv_k_k, rwkv_k_a, rwkv_r_k, rwkv_ln_g, rwkv_ln_b, pool_w, pool_scale, gmlp_ln_g,
              gmlp_ln_b, gmlp_w_s, gmlp_b_s, w_pa, w_pb, w_pc, w_pd, w_o, g_xattn, g_mem, w_xq, w_xk, w_xv,
              w_xo, g_ffn, w_up, ffn_conv_w, ffn_conv_b, w_down, g_final):
    B = x_prompt.shape[0]
    dt = x_prompt.dtype
    p_init = (jnp.zeros((B, LRU_CONV - 1, D_LRU), dt), jnp.zeros((B, D_LRU), dt),
              jnp.zeros((B, 1, D_RWKV_IN), dt), jnp.zeros((B, RWKV_HEADS, RWKV_HEAD, RWKV_HEAD), dt),
              jnp.zeros((B, POOL_BUF, D_POOL), dt), jnp.zeros((B, FFN_CONV - 1, 2 * D_FF), dt))
    hp, hs = x_prompt, x_sample
    p_states, s_states, p_mk, p_mv, s_cv = [], [], [], [], []
    for l in range(DEPTH):
        prm = dict(g_mix=g_mix[l], w_in=w_in[l], lru_conv_w=lru_conv_w[l], lru_conv_b=lru_conv_b[l],
                   lru_w_ra=lru_w_ra[l], lru_b_ra=lru_b_ra[l], lru_w_ix=lru_w_ix[l], lru_b_ix=lru_b_ix[l],
                   lru_lambda=lru_lambda[l], rwkv_mu=rwkv_mu[l], rwkv_w0=rwkv_w0[l], rwkv_w2=rwkv_w2[l],
                   rwkv_a0=rwkv_a0[l], rwkv_a2=rwkv_a2[l], rwkv_g2=rwkv_g2[l], rwkv_k_k=rwkv_k_k[l],
                   rwkv_k_a=rwkv_k_a[l], rwkv_r_k=rwkv_r_k[l], rwkv_ln_g=rwkv_ln_g[l], rwkv_ln_b=rwkv_ln_b[l],
                   pool_w=pool_w[l], pool_scale=pool_scale[l], gmlp_ln_g=gmlp_ln_g[l], gmlp_ln_b=gmlp_ln_b[l],
                   gmlp_w_s=gmlp_w_s[l], gmlp_b_s=gmlp_b_s[l], w_pa=w_pa[l], w_pb=w_pb[l], w_pc=w_pc[l],
                   w_pd=w_pd[l], w_o=w_o[l], g_xattn=g_xattn[l], w_xq=w_xq[l], w_xo=w_xo[l], g_ffn=g_ffn[l],
                   w_up=w_up[l], ffn_conv_w=ffn_conv_w[l], ffn_conv_b=ffn_conv_b[l], w_down=w_down[l])
        mk, mv = mem_kv(mem_prompt, g_mem[l], w_xk[l], w_xv[l])
        hp, p_new, _ = layer(hp, mk, mv, p_init, 0, prm)
        s_st = (state_lru_conv[l], state_lru_h[l], state_rwkv_shift[l], state_rwkv_S[l], state_pool[l],
                state_ffn_conv[l])
        hs, s_new, cv = layer(hs, cache_mem_k[l], cache_mem_v[l], s_st, PAST_LEN, prm)
        p_states.append(p_new)
        s_states.append(s_new)
        p_mk.append(mk)
        p_mv.append(mv)
        s_cv.append(cv)
    y_prompt = rmsnorm(hp, g_final)
    y_sample = rmsnorm(hs, g_final)

    def stk(lst, i):
        return jnp.stack([s[i] for s in lst], axis=0)

    return (y_prompt, y_sample,
            stk(p_states, 0), stk(s_states, 0),
            stk(p_states, 1), stk(s_states, 1),
            stk(p_states, 2), stk(s_states, 2),
            stk(p_states, 3), stk(s_states, 3),
            stk(p_states, 4), stk(s_states, 4),
            stk(p_states, 5), stk(s_states, 5),
            jnp.stack(p_mk, axis=0), jnp.stack(p_mv, axis=0),
            jnp.stack(s_cv, axis=0))
```

```python
import functools
import math

import jax
import jax.numpy as jnp
from jax import lax
from jax.experimental import pallas as pl
from jax.experimental.pallas import tpu as pltpu

F32 = jnp.float32
BF16 = jnp.bfloat16

SUBLANES = 8
LANES = 128

D_MODEL = 1024
DEPTH = 4
N_MEM = 256
D_LRU = 512
LRU_C = 8.0
D_RWKV = 512
RWKV_HEAD = 64
RWKV_HEADS = D_RWKV // RWKV_HEAD
R_DECAY = 64
R_AAA = 64
R_GATE = 128
D_RWKV_IN = 3 * D_RWKV + R_DECAY + R_AAA + R_GATE
GN_EPS = 64e-5
D_POOL = 512
POOL_WINDOWS = (2, 4, 8, 16)
POOL_GW = D_POOL // len(POOL_WINDOWS)
POOL_BUF = max(POOL_WINDOWS) - 1
POOL_HIST = 16
D_GMLP = 512
GMLP_GROUPS = 4
CHUNK = 128
N_BRANCH = 4
X_HEADS = 4
X_HEAD_DIM = D_MODEL // X_HEADS
D_FF = 3 * D_MODEL
EPS = 1e-6
PAST_LEN = 16384

O_A = 0
O_B = 2 * D_LRU
O_C = O_B + D_RWKV_IN
O_D = O_C + D_POOL
O_G = O_D + 2 * D_GMLP
D_IN = O_G + N_BRANCH * D_MODEL

TM_MIX = 512
TM_RWKV = 256
FF_CW = 768
S_RWKV_BB = 16
S_ATT_BB = 8


def _dot(a, b):
    return jnp.dot(a.astype(BF16), b.astype(BF16), preferred_element_type=F32)


def _rms(x, g):
    return x * lax.rsqrt(jnp.mean(x * x, axis=-1, keepdims=True) + EPS) * g


def _gelu(x):
    c = math.sqrt(2.0 / math.pi)
    return 0.5 * x * (1.0 + jnp.tanh(c * (x + 0.044715 * (x * x * x))))


def _softplus(x):
    return jnp.maximum(x, 0.0) + jnp.log1p(jnp.exp(-jnp.abs(x)))


def _sigmoid(x):
    return jax.nn.sigmoid(x)


def _shift_rows(hist, cur, s):
    n = hist.shape[0]
    return pltpu.roll(jnp.concatenate([hist, cur], axis=0), s, 0)[n:]


def _head_ones():
    r = lax.broadcasted_iota(jnp.int32, (LANES, LANES), 0) // RWKV_HEAD
    c = lax.broadcasted_iota(jnp.int32, (LANES, LANES), 1) // RWKV_HEAD
    return jnp.where(r == c, 1.0, 0.0).astype(BF16)


def _segsum(x, ones):
    outs = []
    for p in range(x.shape[1] // LANES):
        xb = x[:, p * LANES:(p + 1) * LANES]
        hi = xb.astype(BF16)
        lo = (xb - hi.astype(F32)).astype(BF16)
        outs.append(jnp.dot(hi, ones, preferred_element_type=F32)
                    + jnp.dot(lo, ones, preferred_element_type=F32))
    return jnp.concatenate(outs, axis=-1)


def _blockdiag_dot(x, w_ref):
    outs = []
    for p in range(x.shape[1] // LANES):
        outs.append(jnp.dot(x[:, p * LANES:(p + 1) * LANES].astype(BF16), w_ref[p],
                            preferred_element_type=F32))
    return jnp.concatenate(outs, axis=-1)


def _lru_gates(xc, wra_ref, bra_ref, wix_ref, bix_ref, lam_ref):
    r = _sigmoid(_blockdiag_dot(xc, wra_ref) + bra_ref[...])
    i = _sigmoid(_blockdiag_dot(xc, wix_ref) + bix_ref[...])
    log_a = -LRU_C * r * _softplus(-lam_ref[...])
    a = jnp.exp(log_a)
    mult = jnp.sqrt(-jnp.tanh(log_a) * (1.0 + a * a))
    return a, mult * (i * xc)


def _rwkv_pre(p, prev, mu, wwa_ref, w0, a0, g2_ref, k_k, k_a, ones):
    px = p + (prev - p) * mu
    r = px[:, 0:D_RWKV]
    k = px[:, D_RWKV:2 * D_RWKV]
    v = px[:, 2 * D_RWKV:3 * D_RWKV]
    lo = px[:, 3 * D_RWKV:3 * D_RWKV + R_DECAY + R_AAA]
    g_lo = px[:, 3 * D_RWKV + R_DECAY + R_AAA:]
    lane = lax.broadcasted_iota(jnp.int32, lo.shape, 1)
    wa = _dot(jnp.where(lane < R_DECAY, jnp.tanh(lo), lo), wwa_ref[...])
    w_log = -_softplus(-(w0 + wa[:, :D_RWKV])) - 0.5
    decay = jnp.exp(-jnp.exp(w_log))
    a = _sigmoid(a0 + wa[:, D_RWKV:])
    g = _dot(_sigmoid(g_lo), g2_ref[...])
    kk = k * k_k
    kk = kk * lax.rsqrt(jnp.maximum(_segsum(kk * kk, ones), 1e-24))
    k2 = k * (1.0 + (a - 1.0) * k_a)
    return r, decay, k2, v, kk, a, g


def _rwkv_step(S, r, w, k, v, kk, a, eye):
    sa = -jnp.sum(S * kk, axis=1, keepdims=True)
    vcol = jnp.sum(eye * v, axis=1, keepdims=True)
    S = S * w + sa * (kk * a) + vcol * k
    ycol = jnp.sum(S * r, axis=1, keepdims=True)
    yrow = jnp.sum(eye * ycol, axis=0, keepdims=True)
    return S, yrow


def _rwkv_post(y, r, k2, v, g, r_k, ln_g, ln_b, ones):
    inv = 1.0 / RWKV_HEAD
    mean = _segsum(y, ones) * inv
    yc = y - mean
    var = _segsum(yc * yc, ones) * inv
    yn = yc * lax.rsqrt(var + GN_EPS) * ln_g + ln_b
    bonus = _segsum(r * k2 * r_k, ones) * v
    return (yn + bonus) * g


def _eye64():
    r = lax.broadcasted_iota(jnp.int32, (RWKV_HEAD, RWKV_HEAD), 0)
    c = lax.broadcasted_iota(jnp.int32, (RWKV_HEAD, RWKV_HEAD), 1)
    return jnp.where(r == c, 1.0, 0.0).astype(F32)


def _pool_project(d, wp_ref, scale):
    return _blockdiag_dot(d, wp_ref) * scale


def _gmlp_uz(z2, ln_g, ln_b):
    u = _gelu(z2[:, :D_GMLP])
    v = _gelu(z2[:, D_GMLP:])
    mu = jnp.mean(v, axis=-1, keepdims=True)
    vc = v - mu
    var = jnp.mean(vc * vc, axis=-1, keepdims=True)
    return u, vc * lax.rsqrt(var + 1e-5) * ln_g + ln_b


def _p_lru_body(x_ref, g_ref, w_ref, cw_ref, cb_ref, wra_ref, bra_ref, wix_ref, bix_ref, lam_ref,
                y_ref, convo_ref, ho_ref, hist_ref, h_ref):
    @pl.when(pl.program_id(1) == 0)
    def _():
        hist_ref[...] = jnp.zeros_like(hist_ref)
        h_ref[...] = jnp.zeros_like(h_ref)

    z = _dot(_rms(x_ref[...], g_ref[...]), w_ref[...])
    xa = z[:, :D_LRU]
    ga = z[:, D_LRU:]
    tm = xa.shape[0]
    hist = hist_ref[...]
    cw = cw_ref[...]
    xc = cb_ref[...] + cw[3:4] * xa
    for s in (1, 2, 3):
        xc = xc + cw[3 - s:4 - s] * _shift_rows(hist, xa, s)
    a, b = _lru_gates(xc, wra_ref, bra_ref, wix_ref, bix_ref, lam_ref)
    row = lax.broadcasted_iota(jnp.int32, a.shape, 0)
    s = 1
    while s < tm:
        m = row >= s
        b = jnp.where(m, a * pltpu.roll(b, s, 0) + b, b)
        a = jnp.where(m, a * pltpu.roll(a, s, 0), a)
        s *= 2
    h = a * h_ref[SUBLANES - 1:SUBLANES, :] + b
    y_ref[...] = (h * _gelu(ga)).astype(y_ref.dtype)
    hist_ref[...] = xa[tm - SUBLANES:]
    h_ref[...] = h[tm - SUBLANES:]
    convo_ref[...] = xa[tm - SUBLANES:]
    ho_ref[...] = h[tm - SUBLANES:]


def _p_rwkv_body(x_ref, g_ref, w_ref, mu_ref, wwa_ref, w0_ref, a0_ref, g2_ref, kk_ref, ka_ref,
                 rk_ref, lng_ref, lnb_ref,
                 y_ref, shifto_ref, so_ref,
                 hist_ref, s_ref, r_sc, w_sc, k_sc, v_sc, kk_sc, a_sc, y_sc):
    @pl.when(pl.program_id(1) == 0)
    def _():
        hist_ref[...] = jnp.zeros_like(hist_ref)
        s_ref[...] = jnp.zeros_like(s_ref)

    ones = _head_ones()
    p = _dot(_rms(x_ref[...], g_ref[...]), w_ref[...])
    tm = p.shape[0]
    prev = _shift_rows(hist_ref[...], p, 1)
    r, decay, k2, v, kk, a, g = _rwkv_pre(p, prev, mu_ref[...], wwa_ref, w0_ref[...], a0_ref[...],
                                           g2_ref, kk_ref[...], ka_ref[...], ones)
    hist_ref[...] = p[tm - SUBLANES:]
    shifto_ref[...] = p[tm - SUBLANES:]
    for sc, val in ((r_sc, r), (w_sc, decay), (k_sc, k2), (v_sc, v), (kk_sc, kk), (a_sc, a)):
        for h in range(RWKV_HEADS):
            sc[h] = val[:, h * RWKV_HEAD:(h + 1) * RWKV_HEAD]
    eye = _eye64()

    def body(t, carry):
        for h in range(RWKV_HEADS):
            rows = [sc[h, pl.ds(t, 1), :] for sc in (r_sc, w_sc, k_sc, v_sc, kk_sc, a_sc)]
            S, yrow = _rwkv_step(s_ref[h], *rows, eye)
            s_ref[h] = S
            y_sc[h, pl.ds(t, 1), :] = yrow
        return carry

    lax.fori_loop(0, tm, body, 0)
    y = jnp.concatenate([y_sc[h] for h in range(RWKV_HEADS)], axis=-1)
    yb = _rwkv_post(y, r, k2, v, g, rk_ref[...], lng_ref[...], lnb_ref[...], ones)
    y_ref[...] = yb.astype(y_ref.dtype)
    so_ref[...] = s_ref[...]


def _p_pool_body(x_ref, g_ref, w_ref, wp_ref, sc_ref, y_ref, poolo_ref, hist_ref):
    t = pl.program_id(1)

    @pl.when(t == 0)
    def _():
        hist_ref[...] = jnp.zeros_like(hist_ref)

    pc = _dot(_rms(x_ref[...], g_ref[...]), w_ref[...])
    tm = pc.shape[0]
    X = jnp.concatenate([hist_ref[...], pc], axis=0)
    pos = t * tm + lax.broadcasted_iota(jnp.int32, (tm, POOL_GW), 0)
    ds = []
    for gi, win in enumerate(POOL_WINDOWS):
        sl = slice(gi * POOL_GW, (gi + 1) * POOL_GW)
        s = X[:, sl]
        sh = 1
        while sh < win:
            s = s + pltpu.roll(s, sh, 0)
            sh *= 2
        cnt = jnp.minimum(pos + 1, win).astype(F32)
        ds.append(s[POOL_HIST:] / cnt - pc[:, sl])
    y = _pool_project(jnp.concatenate(ds, axis=-1), wp_ref, sc_ref[...])
    y_ref[...] = y.astype(y_ref.dtype)
    hist_ref[...] = pc[tm - POOL_HIST:]
    poolo_ref[...] = pc[tm - POOL_HIST:]


def _p_gmlp_body(x_ref, g_ref, w_ref, lng_ref, lnb_ref, ws_ref, bsb_ref, y_ref):
    z2 = _dot(_rms(x_ref[...], g_ref[...]), w_ref[...])
    tm = z2.shape[0]
    u, zv = _gmlp_uz(z2, lng_ref[...], lnb_ref[...])
    tril = (lax.broadcasted_iota(jnp.int32, (CHUNK, CHUNK), 0)
            >= lax.broadcasted_iota(jnp.int32, (CHUNK, CHUNK), 1))
    ws = [jnp.where(tril, ws_ref[gi], 0.0).astype(BF16) for gi in range(GMLP_GROUPS)]
    zb = zv.astype(BF16)
    rows = []
    for c in range(tm // CHUNK):
        cols = [jnp.dot(ws[gi], zb[c * CHUNK:(c + 1) * CHUNK, gi * LANES:(gi + 1) * LANES],
                        preferred_element_type=F32) for gi in range(GMLP_GROUPS)]
        rows.append(jnp.concatenate(cols, axis=-1) + bsb_ref[...])
    s = jnp.concatenate(rows, axis=0)
    y_ref[...] = (u * s).astype(y_ref.dtype)


def _merge_body(x_ref, g_ref, wg_ref, ya_ref, yb_ref, yc_ref, yd_ref, wp_ref, wo_ref, o_ref):
    x = x_ref[...]
    xn = _rms(x, g_ref[...]).astype(BF16)
    merged = None
    for i, y_ref in enumerate((ya_ref, yb_ref, yc_ref, yd_ref)):
        gate = _sigmoid(jnp.dot(xn, wg_ref[:, i * D_MODEL:(i + 1) * D_MODEL],
                                preferred_element_type=F32))
        term = gate * jnp.dot(y_ref[...], wp_ref[i], preferred_element_type=F32)
        merged = term if merged is None else merged + term
    o_ref[...] = x + _dot(merged, wo_ref[...])


def _softmax_rows(s):
    e = jnp.exp(s - jnp.max(s, axis=-1, keepdims=True))
    return e / jnp.sum(e, axis=-1, keepdims=True)


def _p_xattn_body(x_ref, g_ref, wq_ref, k_ref, v_ref, wo_ref, o_ref):
    x = x_ref[...]
    q = _dot(_rms(x, g_ref[...]), wq_ref[...]).astype(BF16)
    kb = k_ref[...].astype(BF16)
    vb = v_ref[...].astype(BF16)
    outs = []
    for h in range(X_HEADS):
        sl = slice(h * X_HEAD_DIM, (h + 1) * X_HEAD_DIM)
        s = lax.dot_general(q[:, sl], kb[:, sl], (((1,), (1,)), ((), ())),
                            preferred_element_type=F32) * (X_HEAD_DIM ** -0.5)
        outs.append(jnp.dot(_softmax_rows(s).astype(BF16), vb[:, sl], preferred_element_type=F32))
    o_ref[...] = x + _dot(jnp.concatenate(outs, axis=-1), wo_ref[...])


def _ffn_conv(hist, h, cw, cb):
    return cb + cw[2:3] * h + cw[1:2] * _shift_rows(hist, h, 1) + cw[0:1] * _shift_rows(hist, h, 2)


def _p_ffn_body(x_ref, g_ref, wug_ref, wuv_ref, cwg_ref, cwv_ref, cbg_ref, cbv_ref, wd_ref,
                o_ref, ffo_ref, hist_ref):
    @pl.when(pl.program_id(1) == 0)
    def _():
        hist_ref[...] = jnp.zeros_like(hist_ref)

    x = x_ref[...]
    tm = x.shape[0]
    xn = _rms(x, g_ref[...]).astype(BF16)
    acc = x
    for j in range(D_FF // FF_CW):
        cs = slice(j * FF_CW, (j + 1) * FF_CW)
        vs = slice(D_FF + j * FF_CW, D_FF + (j + 1) * FF_CW)
        hg = jnp.dot(xn, wug_ref[:, cs], preferred_element_type=F32)
        hv = jnp.dot(xn, wuv_ref[:, cs], preferred_element_type=F32)
        cg = _ffn_conv(hist_ref[:, cs], hg, cwg_ref[:, cs], cbg_ref[:, cs])
        cv = _ffn_conv(hist_ref[:, vs], hv, cwv_ref[:, cs], cbv_ref[:, cs])
        acc = acc + _dot(_gelu(cg) * cv, wd_ref[cs, :])
        hist_ref[:, cs] = hg[tm - SUBLANES:]
        hist_ref[:, vs] = hv[tm - SUBLANES:]
    o_ref[...] = acc
    ffo_ref[...] = hist_ref[...]


def _memkv_body(m_ref, g_ref, wk_ref, wv_ref, k_ref, v_ref):
    mn = _rms(m_ref[...], g_ref[...]).astype(BF16)
    k_ref[...] = jnp.dot(mn, wk_ref[...], preferred_element_type=F32)
    v_ref[...] = jnp.dot(mn, wv_ref[...], preferred_element_type=F32)


def _final_norm_body(x_ref, g_ref, o_ref):
    o_ref[...] = _rms(x_ref[...], g_ref[...])


def _s_mix_body(x_ref, g_ref, wa_ref, wb_ref, wc_ref, wd_ref,
                lconv_ref, lh_ref, cw_ref, cb_ref, wra_ref, bra_ref, wix_ref, bix_ref, lam_ref,
                shift_ref, mu_ref, wwa_ref, w0_ref, a0_ref, g2_ref, kk_ref, ka_ref,
                pool_ref, wp_ref, psc_ref,
                lng_ref, lnb_ref, wsd_ref, bsr_ref,
                ya_ref, xa_ref, h_ref, yc_ref, pc_ref, yd_ref, zv_ref, p_ref,
                r_ref, k2_ref, v_ref, gg_ref, rh_ref, wh_ref, kh_ref, vh_ref, kkh_ref, ah_ref):
    xn = _rms(x_ref[...], g_ref[...]).astype(BF16)
    z = jnp.dot(xn, wa_ref[...], preferred_element_type=F32)
    xa = z[:, :D_LRU]
    cw = cw_ref[...]
    xc = cb_ref[...] + cw[0:1] * lconv_ref[0] + cw[1:2] * lconv_ref[1] + cw[2:3] * lconv_ref[2] + cw[3:4] * xa
    a, b = _lru_gates(xc, wra_ref, bra_ref, wix_ref, bix_ref, lam_ref)
    h = a * lh_ref[...] + b
    ya_ref[...] = (h * _gelu(z[:, D_LRU:])).astype(ya_ref.dtype)
    xa_ref[...] = xa
    h_ref[...] = h
    ones = _head_ones()
    p = jnp.dot(xn, wb_ref[...], preferred_element_type=F32)
    r, decay, k2, v, kk, aa, g = _rwkv_pre(p, shift_ref[...], mu_ref[...], wwa_ref, w0_ref[...],
                                            a0_ref[...], g2_ref, kk_ref[...], ka_ref[...], ones)
    p_ref[...] = p
    r_ref[...] = r
    k2_ref[...] = k2
    v_ref[...] = v
    gg_ref[...] = g
    for ref, val in ((rh_ref, r), (wh_ref, decay), (kh_ref, k2), (vh_ref, v), (kkh_ref, kk), (ah_ref, aa)):
        for hd in range(RWKV_HEADS):
            ref[hd] = val[:, hd * RWKV_HEAD:(hd + 1) * RWKV_HEAD]
    pc = jnp.dot(xn, wc_ref[...], preferred_element_type=F32)
    ds = []
    for gi, win in enumerate(POOL_WINDOWS):
        sl = slice(gi * POOL_GW, (gi + 1) * POOL_GW)
        s = pc[:, sl]
        for j in range(POOL_BUF - (win - 1), POOL_BUF):
            s = s + pool_ref[j][:, sl]
        ds.append(s / float(min(PAST_LEN + 1, win)) - pc[:, sl])
    yc_ref[...] = _pool_project(jnp.concatenate(ds, axis=-1), wp_ref, psc_ref[...]).astype(yc_ref.dtype)
    pc_ref[...] = pc
    u, zv = _gmlp_uz(jnp.dot(xn, wd_ref[...], preferred_element_type=F32), lng_ref[...], lnb_ref[...])
    yd_ref[...] = (u * (wsd_ref[...] * zv + bsr_ref[...])).astype(yd_ref.dtype)
    zv_ref[...] = zv


def _s_rwkv_body(s_ref, rh_ref, wh_ref, kh_ref, vh_ref, kkh_ref, ah_ref,
                 r_ref, k2_ref, v_ref, g_ref, rk_ref, lng_ref, lnb_ref,
                 y_ref, so_ref, y_sc):
    eye = _eye64()
    bb = s_ref.shape[0]

    def body(b, carry):
        for h in range(RWKV_HEADS):
            rows = [ref[h, pl.ds(b, 1), :] for ref in (rh_ref, wh_ref, kh_ref, vh_ref, kkh_ref, ah_ref)]
            S, yrow = _rwkv_step(s_ref[b, h], *rows, eye)
            so_ref[b, h] = S
            y_sc[h, pl.ds(b, 1), :] = yrow
        return carry

    lax.fori_loop(0, bb, body, 0)
    y = jnp.concatenate([y_sc[h] for h in range(RWKV_HEADS)], axis=-1)
    yb = _rwkv_post(y, r_ref[...], k2_ref[...], v_ref[...], g_ref[...], rk_ref[...], lng_ref[...],
                    lnb_ref[...], _head_ones())
    y_ref[...] = yb.astype(y_ref.dtype)


def _s_xattn_body(x_ref, g_ref, wq_ref, k_ref, v_ref, wo_ref, o_ref, q_sc, a_sc):
    i = pl.program_id(0)
    bb = k_ref.shape[0]

    @pl.when(i == 0)
    def _():
        q_sc[...] = _dot(_rms(x_ref[...], g_ref[...]), wq_ref[...])

    hmask = (lax.broadcasted_iota(jnp.int32, (SUBLANES, D_MODEL), 1) // X_HEAD_DIM
             == lax.broadcasted_iota(jnp.int32, (SUBLANES, D_MODEL), 0))
    for j in range(bb):
        row = pl.ds(i * bb + j, 1)
        q8 = jnp.where(hmask, q_sc[row, :], 0.0).astype(BF16)
        s = lax.dot_general(q8, k_ref[j].astype(BF16), (((1,), (1,)), ((), ())),
                            preferred_element_type=F32) * (X_HEAD_DIM ** -0.5)
        o8 = jnp.dot(_softmax_rows(s).astype(BF16), v_ref[j].astype(BF16), preferred_element_type=F32)
        a_sc[row, :] = jnp.sum(jnp.where(hmask, o8, 0.0), axis=0, keepdims=True)

    @pl.when(i == pl.num_programs(0) - 1)
    def _():
        o_ref[...] = x_ref[...] + _dot(a_sc[...], wo_ref[...])


def _s_ffn_body(x_ref, g_ref, wug_ref, wuv_ref, sg_ref, sv_ref, cwg_ref, cwv_ref, cbg_ref, cbv_ref,
                wd_ref, o_ref, hg_ref, hv_ref, acc_ref):
    j = pl.program_id(0)

    @pl.when(j == 0)
    def _():
        acc_ref[...] = x_ref[...]

    xn = _rms(x_ref[...], g_ref[...]).astype(BF16)
    hg = jnp.dot(xn, wug_ref[...], preferred_element_type=F32)
    hv = jnp.dot(xn, wuv_ref[...], preferred_element_type=F32)
    cwg = cwg_ref[...]
    cwv = cwv_ref[...]
    cg = cbg_ref[...] + cwg[0:1] * sg_ref[0] + cwg[1:2] * sg_ref[1] + cwg[2:3] * hg
    cv = cbv_ref[...] + cwv[0:1] * sv_ref[0] + cwv[1:2] * sv_ref[1] + cwv[2:3] * hv
    acc_ref[...] += _dot(_gelu(cg) * cv, wd_ref[...])
    hg_ref[...] = hg
    hv_ref[...] = hv

    @pl.when(j == pl.num_programs(0) - 1)
    def _():
        o_ref[...] = acc_ref[...]


def _params(n_grid, vmem_mb):
    return pltpu.CompilerParams(dimension_semantics=("arbitrary",) * n_grid,
                                vmem_limit_bytes=vmem_mb << 20)


def _whole(shape):
    return pl.BlockSpec(tuple(shape), lambda *_: (0,) * len(shape))


def _layer(arr, l):
    shape = arr.shape[1:]
    return pl.BlockSpec((None,) + tuple(shape), lambda *_: (l,) + (0,) * len(shape))


def _sds(shape, dtype=F32):
    return jax.ShapeDtypeStruct(tuple(shape), dtype)


def _pair_blockdiag(w):
    L = w.shape[0]
    w = w.reshape(L, 4, 2, RWKV_HEAD, RWKV_HEAD)
    z = jnp.zeros_like(w[:, :, 0])
    top = jnp.concatenate([w[:, :, 0], z], axis=-1)
    bot = jnp.concatenate([z, w[:, :, 1]], axis=-1)
    return jnp.concatenate([top, bot], axis=-2)


def _vec(a):
    return a.reshape(a.shape[0], 1, -1)


def kernel(x_prompt, x_sample, state_lru_conv, state_lru_h, state_rwkv_shift, state_rwkv_S, state_pool, state_ffn_conv, cache_mem_k, cache_mem_v, mem_prompt, g_mix, w_in, lru_conv_w, lru_conv_b, lru_w_ra, lru_b_ra, lru_w_ix, lru_b_ix, lru_lambda, rwkv_mu, rwkv_w0, rwkv_w2, rwkv_a0, rwkv_a2, rwkv_g2, rwkv_k_k, rwkv_k_a, rwkv_r_k, rwkv_ln_g, rwkv_ln_b, pool_w, pool_scale, gmlp_ln_g, gmlp_ln_b, gmlp_w_s, gmlp_b_s, w_pa, w_pb, w_pc, w_pd, w_o, g_xattn, g_mem, w_xq, w_xk, w_xv, w_xo, g_ffn, w_up, ffn_conv_w, ffn_conv_b, w_down, g_final):
    B, T, D = x_prompt.shape
    SB = x_sample.shape[0]
    assert D == D_MODEL and w_in.shape == (DEPTH, D_MODEL, D_IN) and x_sample.shape[1] == 1
    assert T % TM_MIX == 0 and T % TM_RWKV == 0 and SB % S_RWKV_BB == 0 and SB % S_ATT_BB == 0

    wA = w_in[:, :, O_A:O_B].astype(BF16)
    wB = w_in[:, :, O_B:O_C].astype(BF16)
    wC = w_in[:, :, O_C:O_D].astype(BF16)
    wD = w_in[:, :, O_D:O_G].astype(BF16)
    wG = w_in[:, :, O_G:].astype(BF16)
    wra = _pair_blockdiag(lru_w_ra).astype(BF16)
    wix = _pair_blockdiag(lru_w_ix).astype(BF16)
    zer = jnp.zeros((DEPTH, R_DECAY, D_RWKV), F32)
    wwa = jnp.concatenate([jnp.concatenate([rwkv_w2, zer], axis=-1),
                           jnp.concatenate([zer, rwkv_a2], axis=-1)], axis=1).astype(BF16)
    g2 = rwkv_g2.astype(BF16)
    wpool = pool_w.astype(BF16)
    bsb = jnp.repeat(gmlp_b_s, LANES, axis=-1)
    off = PAST_LEN % CHUNK
    wsd = jnp.repeat(gmlp_w_s[:, :, off, off], LANES, axis=-1)[:, None, :]
    bsr = jnp.repeat(gmlp_b_s[:, off, :], LANES, axis=-1)[:, None, :]
    wP = jnp.stack([w_pa, w_pb, w_pc, w_pd], axis=1).astype(BF16)
    wO = w_o.astype(BF16)
    wXq, wXk, wXv, wXo = (w.astype(BF16) for w in (w_xq, w_xk, w_xv, w_xo))
    wUg = w_up[:, :, :D_FF].astype(BF16)
    wUv = w_up[:, :, D_FF:].astype(BF16)
    wDn = w_down.astype(BF16)
    cwg, cwv = ffn_conv_w[:, :, :D_FF], ffn_conv_w[:, :, D_FF:]
    cbg, cbv = _vec(ffn_conv_b[:, :D_FF]), _vec(ffn_conv_b[:, D_FF:])
    gmix, gxat, gffn, gmem = _vec(g_mix), _vec(g_xattn), _vec(g_ffn), _vec(g_mem)
    lcb, bra, bix, lam = _vec(lru_conv_b), _vec(lru_b_ra), _vec(lru_b_ix), _vec(lru_lambda)
    mu, w0, a0, kkw, kaw = _vec(rwkv_mu), _vec(rwkv_w0), _vec(rwkv_a0), _vec(rwkv_k_k), _vec(rwkv_k_a)
    rk = rwkv_r_k.reshape(DEPTH, 1, D_RWKV)
    rlg, rlb = _vec(rwkv_ln_g), _vec(rwkv_ln_b)
    psc, glg, glb = _vec(pool_scale), _vec(gmlp_ln_g), _vec(gmlp_ln_b)
    gfin = g_final.reshape(1, D_MODEL)

    kv_shape = _sds((DEPTH, B, N_MEM, D_MODEL))
    kv_spec = pl.BlockSpec((None, None, N_MEM, D_MODEL), lambda l, b: (l, b, 0, 0))
    wl_spec = pl.BlockSpec((None, D_MODEL, D_MODEL), lambda l, b: (l, 0, 0))
    p_mk, p_mv = pl.pallas_call(
        _memkv_body, grid=(DEPTH, B),
        in_specs=[pl.BlockSpec((None, N_MEM, D_MODEL), lambda l, b: (b, 0, 0)),
                  pl.BlockSpec((None, 1, D_MODEL), lambda l, b: (l, 0, 0)), wl_spec, wl_spec],
        out_specs=[kv_spec, kv_spec], out_shape=[kv_shape, kv_shape],
        compiler_params=_params(2, 32), name="memkv")(mem_prompt, gmem, wXk, wXv)

    nt = T // TM_MIX
    ntr = T // TM_RWKV
    xt_spec = pl.BlockSpec((None, TM_MIX, D_MODEL), lambda b, t: (b, t, 0))
    yt_spec = pl.BlockSpec((None, TM_MIX, D_LRU), lambda b, t: (b, t, 0))

    def tail_spec(rows, width):
        return pl.BlockSpec((None, rows, width), lambda b, t: (b, 0, 0))

    hp = x_prompt
    hs = x_sample.reshape(SB, D_MODEL)
    outs = {k: [] for k in ("p_conv", "s_conv", "p_h", "s_h", "p_shift", "s_shift", "p_S", "s_S",
                            "p_pool", "s_pool", "p_ffn", "s_ffn", "s_cv")}

    for l in range(DEPTH):
        L = functools.partial(_layer, l=l)

        yA, p_conv8, p_h8 = pl.pallas_call(
            _p_lru_body, grid=(B, nt),
            in_specs=[xt_spec, L(gmix), L(wA), L(lru_conv_w), L(lcb), L(wra), L(bra), L(wix), L(bix), L(lam)],
            out_specs=[yt_spec, tail_spec(SUBLANES, D_LRU), tail_spec(SUBLANES, D_LRU)],
            out_shape=[_sds((B, T, D_LRU), BF16), _sds((B, SUBLANES, D_LRU)), _sds((B, SUBLANES, D_LRU))],
            scratch_shapes=[pltpu.VMEM((SUBLANES, D_LRU), F32), pltpu.VMEM((SUBLANES, D_LRU), F32)],
            compiler_params=_params(2, 48), name="p_lru")(
                hp, gmix, wA, lru_conv_w, lcb, wra, bra, wix, bix, lam)

        head_sc = pltpu.VMEM((RWKV_HEADS, TM_RWKV, RWKV_HEAD), F32)
        yB, p_shift8, p_S = pl.pallas_call(
            _p_rwkv_body, grid=(B, ntr),
            in_specs=[pl.BlockSpec((None, TM_RWKV, D_MODEL), lambda b, t: (b, t, 0)),
                      L(gmix), L(wB), L(mu), L(wwa), L(w0), L(a0), L(g2), L(kkw), L(kaw), L(rk), L(rlg), L(rlb)],
            out_specs=[pl.BlockSpec((None, TM_RWKV, D_RWKV), lambda b, t: (b, t, 0)),
                       tail_spec(SUBLANES, D_RWKV_IN),
                       pl.BlockSpec((None, RWKV_HEADS, RWKV_HEAD, RWKV_HEAD), lambda b, t: (b, 0, 0, 0))],
            out_shape=[_sds((B, T, D_RWKV), BF16), _sds((B, SUBLANES, D_RWKV_IN)),
                       _sds((B, RWKV_HEADS, RWKV_HEAD, RWKV_HEAD))],
            scratch_shapes=[pltpu.VMEM((SUBLANES, D_RWKV_IN), F32),
                            pltpu.VMEM((RWKV_HEADS, RWKV_HEAD, RWKV_HEAD), F32)] + [head_sc] * 7,
            compiler_params=_params(2, 48), name="p_rwkv")(
                hp, gmix, wB, mu, wwa, w0, a0, g2, kkw, kaw, rk, rlg, rlb)

        yC, p_pool16 = pl.pallas_call(
            _p_pool_body, grid=(B, nt),
            in_specs=[xt_spec, L(gmix), L(wC), L(wpool), L(psc)],
            out_specs=[yt_spec, tail_spec(POOL_HIST, D_POOL)],
            out_shape=[_sds((B, T, D_POOL), BF16), _sds((B, POOL_HIST, D_POOL))],
            scratch_shapes=[pltpu.VMEM((POOL_HIST, D_POOL), F32)],
            compiler_params=_params(2, 48), name="p_pool")(hp, gmix, wC, wpool, psc)

        yD = pl.pallas_call(
            _p_gmlp_body, grid=(B, nt),
            in_specs=[xt_spec, L(gmix), L(wD), L(glg), L(glb), L(gmlp_w_s), L(bsb)],
            out_specs=yt_spec, out_shape=_sds((B, T, D_GMLP), BF16),
            compiler_params=_params(2, 48), name="p_gmlp")(hp, gmix, wD, glg, glb, gmlp_w_s, bsb)

        def merge(x2d, ys, tm):
            rows = x2d.shape[0]
            xs = pl.BlockSpec((tm, D_MODEL), lambda i: (i, 0))
            ysp = pl.BlockSpec((tm, D_LRU), lambda i: (i, 0))
            return pl.pallas_call(
                _merge_body, grid=(rows // tm,),
                in_specs=[xs, L(gmix), L(wG), ysp, ysp, ysp, ysp, L(wP), L(wO)],
                out_specs=xs, out_shape=_sds((rows, D_MODEL)),
                compiler_params=_params(1, 56), name="merge")(x2d, gmix, wG, *ys, wP, wO)

        hp = merge(hp.reshape(B * T, D_MODEL), [y.reshape(B * T, -1) for y in (yA, yB, yC, yD)],
                   TM_MIX).reshape(B, T, D_MODEL)

        kvb_spec = pl.BlockSpec((None, None, N_MEM, D_MODEL), lambda b, t: (l, b, 0, 0))
        hp = pl.pallas_call(
            _p_xattn_body, grid=(B, nt),
            in_specs=[xt_spec, L(gxat), L(wXq), kvb_spec, kvb_spec, L(wXo)],
            out_specs=xt_spec, out_shape=_sds((B, T, D_MODEL)),
            compiler_params=_params(2, 48), name="p_xattn")(hp, gxat, wXq, p_mk, p_mv, wXo)

        hp, p_ffn8 = pl.pallas_call(
            _p_ffn_body, grid=(B, nt),
            in_specs=[xt_spec, L(gffn), L(wUg), L(wUv), L(cwg), L(cwv), L(cbg), L(cbv), L(wDn)],
            out_specs=[xt_spec, tail_spec(SUBLANES, 2 * D_FF)],
            out_shape=[_sds((B, T, D_MODEL)), _sds((B, SUBLANES, 2 * D_FF))],
            scratch_shapes=[pltpu.VMEM((SUBLANES, 2 * D_FF), F32)],
            compiler_params=_params(2, 56), name="p_ffn")(hp, gffn, wUg, wUv, cwg, cwv, cbg, cbv, wDn)

        outs["p_conv"].append(p_conv8[:, SUBLANES - 3:])
        outs["p_h"].append(p_h8[:, SUBLANES - 1])
        outs["p_shift"].append(p_shift8[:, SUBLANES - 1:])
        outs["p_S"].append(p_S)
        outs["p_pool"].append(p_pool16[:, POOL_HIST - POOL_BUF:])
        outs["p_ffn"].append(p_ffn8[:, SUBLANES - 2:])

        lconv = jnp.swapaxes(state_lru_conv[l], 0, 1)
        spool = jnp.swapaxes(state_pool[l], 0, 1)
        sffn = jnp.swapaxes(state_ffn_conv[l], 0, 1)
        shift = state_rwkv_shift[l].reshape(SB, D_RWKV_IN)
        row512 = _sds((SB, D_LRU))
        heads = _sds((RWKV_HEADS, SB, RWKV_HEAD))
        mix_in = [hs, gmix[l], wA[l], wB[l], wC[l], wD[l],
                  lconv, state_lru_h[l], lru_conv_w[l], lcb[l], wra[l], bra[l], wix[l], bix[l], lam[l],
                  shift, mu[l], wwa[l], w0[l], a0[l], g2[l], kkw[l], kaw[l],
                  spool, wpool[l], psc[l],
                  glg[l], glb[l], wsd[l], bsr[l]]
        mix_out = [_sds((SB, D_LRU), BF16), row512, row512, _sds((SB, D_POOL), BF16), row512,
                   _sds((SB, D_GMLP), BF16), row512, _sds((SB, D_RWKV_IN)),
                   row512, row512, row512, row512, heads, heads, heads, heads, heads, heads]
        (yA, s_xa, s_h, yC, s_pc, yD, s_zv, s_p, s_r, s_k2, s_v, s_g,
         rh, wh, kh, vh, kkh, ah) = pl.pallas_call(
            _s_mix_body, grid=(1,),
            in_specs=[_whole(a.shape) for a in mix_in],
            out_specs=[_whole(o.shape) for o in mix_out], out_shape=mix_out,
            compiler_params=_params(1, 56), name="s_mix")(*mix_in)

        hd_spec = pl.BlockSpec((RWKV_HEADS, S_RWKV_BB, RWKV_HEAD), lambda i: (0, i, 0))
        rb_spec = pl.BlockSpec((S_RWKV_BB, D_RWKV), lambda i: (i, 0))
        st_spec = pl.BlockSpec((None, S_RWKV_BB, RWKV_HEADS, RWKV_HEAD, RWKV_HEAD), lambda i: (l, i, 0, 0, 0))
        so_spec = pl.BlockSpec((S_RWKV_BB, RWKV_HEADS, RWKV_HEAD, RWKV_HEAD), lambda i: (i, 0, 0, 0))
        yB, s_S = pl.pallas_call(
            _s_rwkv_body, grid=(SB // S_RWKV_BB,),
            in_specs=[st_spec] + [hd_spec] * 6 + [rb_spec] * 4 + [L(rk), L(rlg), L(rlb)],
            out_specs=[rb_spec, so_spec],
            out_shape=[_sds((SB, D_RWKV), BF16), _sds((SB, RWKV_HEADS, RWKV_HEAD, RWKV_HEAD))],
            scratch_shapes=[pltpu.VMEM((RWKV_HEADS, S_RWKV_BB, RWKV_HEAD), F32)],
            compiler_params=_params(1, 48), name="s_rwkv")(
                state_rwkv_S, rh, wh, kh, vh, kkh, ah, s_r, s_k2, s_v, s_g, rk, rlg, rlb)

        hs = merge(hs, [yA, yB, yC, yD], SB)

        kc_spec = pl.BlockSpec((None, S_ATT_BB, N_MEM, D_MODEL), lambda i: (l, i, 0, 0))
        xs_spec = _whole((SB, D_MODEL))
        hs = pl.pallas_call(
            _s_xattn_body, grid=(SB // S_ATT_BB,),
            in_specs=[xs_spec, L(gxat), L(wXq), kc_spec, kc_spec, L(wXo)],
            out_specs=xs_spec, out_shape=_sds((SB, D_MODEL)),
            scratch_shapes=[pltpu.VMEM((SB, D_MODEL), F32), pltpu.VMEM((SB, D_MODEL), F32)],
            compiler_params=_params(1, 56), name="s_xattn")(
                hs, gxat, wXq, cache_mem_k.reshape(DEPTH, SB, N_MEM, D_MODEL),
                cache_mem_v.reshape(DEPTH, SB, N_MEM, D_MODEL), wXo)

        ncf = D_FF // FF_CW
        wu_spec = pl.BlockSpec((None, D_MODEL, FF_CW), lambda j: (l, 0, j))
        sg_spec = pl.BlockSpec((2, SB, FF_CW), lambda j: (0, 0, j))
        sv_spec = pl.BlockSpec((2, SB, FF_CW), lambda j: (0, 0, j + ncf))
        cw_spec = pl.BlockSpec((None, 3, FF_CW), lambda j: (l, 0, j))
        cb_spec = pl.BlockSpec((None, 1, FF_CW), lambda j: (l, 0, j))
        hh_spec = pl.BlockSpec((SB, FF_CW), lambda j: (0, j))
        hs, s_hg, s_hv = pl.pallas_call(
            _s_ffn_body, grid=(ncf,),
            in_specs=[xs_spec, L(gffn), wu_spec, wu_spec, sg_spec, sv_spec, cw_spec, cw_spec, cb_spec, cb_spec,
                      pl.BlockSpec((None, FF_CW, D_MODEL), lambda j: (l, j, 0))],
            out_specs=[xs_spec, hh_spec, hh_spec],
            out_shape=[_sds((SB, D_MODEL)), _sds((SB, D_FF)), _sds((SB, D_FF))],
            scratch_shapes=[pltpu.VMEM((SB, D_MODEL), F32)],
            compiler_params=_params(1, 48), name="s_ffn")(
                hs, gffn, wUg, wUv, sffn, sffn, cwg, cwv, cbg, cbv, wDn)

        outs["s_conv"].append(jnp.concatenate([state_lru_conv[l][:, 1:], s_xa[:, None]], axis=1))
        outs["s_h"].append(s_h)
        outs["s_shift"].append(s_p[:, None])
        outs["s_S"].append(s_S)
        outs["s_pool"].append(jnp.concatenate([state_pool[l][:, 1:], s_pc[:, None]], axis=1))
        outs["s_ffn"].append(jnp.concatenate(
            [state_ffn_conv[l][:, 1:], jnp.concatenate([s_hg, s_hv], axis=-1)[:, None]], axis=1))
        outs["s_cv"].append(s_zv[:, None])

    def final_norm(x2d, tm):
        rows = x2d.shape[0]
        spec = pl.BlockSpec((tm, D_MODEL), lambda i: (i, 0))
        return pl.pallas_call(_final_norm_body, grid=(rows // tm,),
                              in_specs=[spec, _whole((1, D_MODEL))], out_specs=spec,
                              out_shape=_sds((rows, D_MODEL)),
                              compiler_params=_params(1, 32), name="final_norm")(x2d, gfin)

    y_prompt = final_norm(hp.reshape(B * T, D_MODEL), 1024).reshape(B, T, D_MODEL)
    y_sample = final_norm(hs, SB).reshape(SB, 1, D_MODEL)
    stk = {k: jnp.stack(v, axis=0) for k, v in outs.items()}
    mem_shape = (DEPTH, B, N_MEM, X_HEADS, X_HEAD_DIM)
    return (y_prompt, y_sample,
            stk["p_conv"], stk["s_conv"], stk["p_h"], stk["s_h"],
            stk["p_shift"], stk["s_shift"], stk["p_S"], stk["s_S"],
            stk["p_pool"], stk["s_pool"], stk["p_ffn"], stk["s_ffn"],
            p_mk.reshape(mem_shape), p_mv.reshape(mem_shape), stk["s_cv"])
```

```python
import functools
import math

import jax
import jax.numpy as jnp
from jax import lax
from jax.experimental import pallas as pl
from jax.experimental.pallas import tpu as pltpu

F32 = jnp.float32
BF16 = jnp.bfloat16

SUBLANES = 8
LANES = 128

D_MODEL = 1024
DEPTH = 4
N_MEM = 256
D_LRU = 512
LRU_C = 8.0
D_RWKV = 512
RWKV_HEAD = 64
RWKV_HEADS = D_RWKV // RWKV_HEAD
R_DECAY = 64
R_AAA = 64
R_GATE = 128
D_RWKV_IN = 3 * D_RWKV + R_DECAY + R_AAA + R_GATE
GN_EPS = 64e-5
D_POOL = 512
POOL_WINDOWS = (2, 4, 8, 16)
POOL_GW = D_POOL // len(POOL_WINDOWS)
POOL_BUF = max(POOL_WINDOWS) - 1
POOL_HIST = 16
D_GMLP = 512
GMLP_GROUPS = 4
CHUNK = 128
N_BRANCH = 4
X_HEADS = 4
X_HEAD_DIM = D_MODEL // X_HEADS
D_FF = 3 * D_MODEL
EPS = 1e-6
PAST_LEN = 16384

O_A = 0
O_B = 2 * D_LRU
O_C = O_B + D_RWKV_IN
O_D = O_C + D_POOL
O_G = O_D + 2 * D_GMLP
D_IN = O_G + N_BRANCH * D_MODEL

TM_MIX = 512
TM_RWKV = 256
FF_CW = 768
RWKV_CHUNK = 64
S_RWKV_BB = 16
S_ATT_BB = 8


def _dot(a, b):
    return jnp.dot(a.astype(BF16), b.astype(BF16), preferred_element_type=F32)


def _rms(x, g):
    return x * lax.rsqrt(jnp.mean(x * x, axis=-1, keepdims=True) + EPS) * g


def _gelu(x):
    c = math.sqrt(2.0 / math.pi)
    return 0.5 * x * (1.0 + jnp.tanh(c * (x + 0.044715 * (x * x * x))))


def _softplus(x):
    return jnp.maximum(x, 0.0) + jnp.log1p(jnp.exp(-jnp.abs(x)))


def _sigmoid(x):
    return jax.nn.sigmoid(x)


def _shift_rows(hist, cur, s):
    n = hist.shape[0]
    return pltpu.roll(jnp.concatenate([hist, cur], axis=0), s, 0)[n:]


def _head_ones():
    r = lax.broadcasted_iota(jnp.int32, (LANES, LANES), 0) // RWKV_HEAD
    c = lax.broadcasted_iota(jnp.int32, (LANES, LANES), 1) // RWKV_HEAD
    return jnp.where(r == c, 1.0, 0.0).astype(BF16)


def _segsum(x, ones):
    outs = []
    for p in range(x.shape[1] // LANES):
        xb = x[:, p * LANES:(p + 1) * LANES]
        hi = xb.astype(BF16)
        lo = (xb - hi.astype(F32)).astype(BF16)
        outs.append(jnp.dot(hi, ones, preferred_element_type=F32)
                    + jnp.dot(lo, ones, preferred_element_type=F32))
    return jnp.concatenate(outs, axis=-1)


def _blockdiag_dot(x, w_ref):
    outs = []
    for p in range(x.shape[1] // LANES):
        outs.append(jnp.dot(x[:, p * LANES:(p + 1) * LANES].astype(BF16), w_ref[p],
                            preferred_element_type=F32))
    return jnp.concatenate(outs, axis=-1)


def _lru_gates(xc, wra_ref, bra_ref, wix_ref, bix_ref, lam_ref):
    r = _sigmoid(_blockdiag_dot(xc, wra_ref) + bra_ref[...])
    i = _sigmoid(_blockdiag_dot(xc, wix_ref) + bix_ref[...])
    log_a = -LRU_C * r * _softplus(-lam_ref[...])
    a = jnp.exp(log_a)
    mult = jnp.sqrt(-jnp.tanh(log_a) * (1.0 + a * a))
    return a, mult * (i * xc)


def _rwkv_pre(p, prev, mu, wwa_ref, w0, a0, g2_ref, k_k, k_a, ones):
    px = p + (prev - p) * mu
    r = px[:, 0:D_RWKV]
    k = px[:, D_RWKV:2 * D_RWKV]
    v = px[:, 2 * D_RWKV:3 * D_RWKV]
    lo = px[:, 3 * D_RWKV:3 * D_RWKV + R_DECAY + R_AAA]
    g_lo = px[:, 3 * D_RWKV + R_DECAY + R_AAA:]
    lane = lax.broadcasted_iota(jnp.int32, lo.shape, 1)
    wa = _dot(jnp.where(lane < R_DECAY, jnp.tanh(lo), lo), wwa_ref[...])
    logw = -jnp.exp(-_softplus(-(w0 + wa[:, :D_RWKV])) - 0.5)
    a = _sigmoid(a0 + wa[:, D_RWKV:])
    g = _dot(_sigmoid(g_lo), g2_ref[...])
    kk = k * k_k
    kk = kk * lax.rsqrt(jnp.maximum(_segsum(kk * kk, ones), 1e-24))
    k2 = k * (1.0 + (a - 1.0) * k_a)
    return r, logw, k2, v, kk, a, g


def _rwkv_step(S, r, w, k, v, kk, a, eye):
    sa = -jnp.sum(S * kk, axis=1, keepdims=True)
    vcol = jnp.sum(eye * v, axis=1, keepdims=True)
    S = S * w + sa * (kk * a) + vcol * k
    ycol = jnp.sum(S * r, axis=1, keepdims=True)
    yrow = jnp.sum(eye * ycol, axis=0, keepdims=True)
    return S, yrow


def _rwkv_post(y, r, k2, v, g, r_k, ln_g, ln_b, ones):
    inv = 1.0 / RWKV_HEAD
    mean = _segsum(y, ones) * inv
    yc = y - mean
    var = _segsum(yc * yc, ones) * inv
    yn = yc * lax.rsqrt(var + GN_EPS) * ln_g + ln_b
    bonus = _segsum(r * k2 * r_k, ones) * v
    return (yn + bonus) * g


def _eye64():
    r = lax.broadcasted_iota(jnp.int32, (RWKV_HEAD, RWKV_HEAD), 0)
    c = lax.broadcasted_iota(jnp.int32, (RWKV_HEAD, RWKV_HEAD), 1)
    return jnp.where(r == c, 1.0, 0.0).astype(F32)


def _pool_project(d, wp_ref, scale):
    return _blockdiag_dot(d, wp_ref) * scale


def _gmlp_uz(z2, ln_g, ln_b):
    u = _gelu(z2[:, :D_GMLP])
    v = _gelu(z2[:, D_GMLP:])
    mu = jnp.mean(v, axis=-1, keepdims=True)
    vc = v - mu
    var = jnp.mean(vc * vc, axis=-1, keepdims=True)
    return u, vc * lax.rsqrt(var + 1e-5) * ln_g + ln_b


def _p_lru_body(x_ref, g_ref, w_ref, cw_ref, cb_ref, wra_ref, bra_ref, wix_ref, bix_ref, lam_ref,
                y_ref, convo_ref, ho_ref, hist_ref, h_ref):
    @pl.when(pl.program_id(1) == 0)
    def _():
        hist_ref[...] = jnp.zeros_like(hist_ref)
        h_ref[...] = jnp.zeros_like(h_ref)

    z = _dot(_rms(x_ref[...], g_ref[...]), w_ref[...])
    xa = z[:, :D_LRU]
    ga = z[:, D_LRU:]
    tm = xa.shape[0]
    hist = hist_ref[...]
    cw = cw_ref[...]
    xc = cb_ref[...] + cw[3:4] * xa
    for s in (1, 2, 3):
        xc = xc + cw[3 - s:4 - s] * _shift_rows(hist, xa, s)
    a, b = _lru_gates(xc, wra_ref, bra_ref, wix_ref, bix_ref, lam_ref)
    row = lax.broadcasted_iota(jnp.int32, a.shape, 0)
    s = 1
    while s < tm:
        m = row >= s
        b = jnp.where(m, a * pltpu.roll(b, s, 0) + b, b)
        a = jnp.where(m, a * pltpu.roll(a, s, 0), a)
        s *= 2
    h = a * h_ref[SUBLANES - 1:SUBLANES, :] + b
    y_ref[...] = (h * _gelu(ga)).astype(y_ref.dtype)
    hist_ref[...] = xa[tm - SUBLANES:]
    h_ref[...] = h[tm - SUBLANES:]
    convo_ref[...] = xa[tm - SUBLANES:]
    ho_ref[...] = h[tm - SUBLANES:]


def _p_rwkv_body(x_ref, g_ref, w_ref, mu_ref, wwa_ref, w0_ref, a0_ref, g2_ref, kk_ref, ka_ref,
                 rk_ref, lng_ref, lnb_ref,
                 y_ref, shifto_ref, so_ref,
                 hist_ref, st_ref, at_sc, rt_sc, bt_sc, kt_sc, v_sc, bh_sc, kh_sc, ge_sc, y_sc,
                 g_sc, q_sc, ry_sc, yc_sc, gd_sc):
    @pl.when(pl.program_id(1) == 0)
    def _():
        hist_ref[...] = jnp.zeros_like(hist_ref)
        st_ref[...] = jnp.zeros_like(st_ref)

    C = RWKV_CHUNK
    ones = _head_ones()
    p = _dot(_rms(x_ref[...], g_ref[...]), w_ref[...])
    tm = p.shape[0]
    nc = tm // C
    prev = _shift_rows(hist_ref[...], p, 1)
    r, logw, k2, v, kk, a, g = _rwkv_pre(p, prev, mu_ref[...], wwa_ref, w0_ref[...], a0_ref[...],
                                          g2_ref, kk_ref[...], ka_ref[...], ones)
    hist_ref[...] = p[tm - SUBLANES:]
    shifto_ref[...] = p[tm - SUBLANES:]
    rowc = lax.broadcasted_iota(jnp.int32, logw.shape, 0) % C
    lg = logw
    s = 1
    while s < C:
        lg = lg + jnp.where(rowc >= s, pltpu.roll(lg, s, 0), 0.0)
        s *= 2
    lg_end = jnp.concatenate(
        [jnp.broadcast_to(lg[(c + 1) * C - 1:(c + 1) * C, :], (C, D_RWKV)) for c in range(nc)], axis=0)
    inv_gam = jnp.exp(-lg)
    to_end = jnp.exp(lg_end - lg)
    b = kk * a
    for sc, val in ((at_sc, -kk * jnp.exp(lg - logw)), (rt_sc, r * jnp.exp(lg)), (bt_sc, b * inv_gam),
                    (kt_sc, k2 * inv_gam), (v_sc, v), (bh_sc, b * to_end), (kh_sc, k2 * to_end),
                    (ge_sc, jnp.exp(lg_end))):
        for h in range(RWKV_HEADS):
            sc[h] = val[:, h * RWKV_HEAD:(h + 1) * RWKV_HEAD]

    eye = _eye64()
    ri = lax.broadcasted_iota(jnp.int32, (C, C), 0)
    ci = lax.broadcasted_iota(jnp.int32, (C, C), 1)
    r2 = lax.broadcasted_iota(jnp.int32, (C, 2 * C), 0)
    c2 = lax.broadcasted_iota(jnp.int32, (C, 2 * C), 1) % C
    strict2 = c2 < r2
    incl2 = c2 <= r2
    pair = (ri // 2) == (ci // 2)
    levels = []
    nb = 2
    while nb < C:
        levels.append(((ri // (2 * nb)) == (ci // (2 * nb))) & ((ri // nb) != (ci // nb)))
        nb *= 2
    zeros_c = jnp.zeros((C, RWKV_HEAD), BF16)
    tn = (((0,), (0,)), ((), ()))
    nt_ = (((1,), (1,)), ((), ()))

    def mm(x, y):
        return jnp.dot(x, y, preferred_element_type=F32)

    def phase1(c, carry):
        rows = pl.ds(pl.multiple_of(c * C, C), C)
        for h in range(RWKV_HEADS):
            At, Rt, Bt, Kt, V, Bh, Kh = (sc[h, rows, :] for sc in
                                         (at_sc, rt_sc, bt_sc, kt_sc, v_sc, bh_sc, kh_sc))
            atb = At.astype(BF16)
            vb = V.astype(BF16)
            ar = jnp.concatenate([atb, Rt.astype(BF16)], axis=0)
            bk = jnp.concatenate([Bt, Kt], axis=0).astype(BF16)
            prod = lax.dot_general(ar, bk, nt_, preferred_element_type=F32)
            top = jnp.where(strict2, prod[:C], 0.0)
            bot = jnp.where(incl2, prod[C:], 0.0)
            lab = top[:, :C]
            lak_v = mm(top.astype(BF16), jnp.concatenate([zeros_c, vb], axis=0))
            T = eye + jnp.where(pair, lab, 0.0)
            for mk in levels:
                tb = T.astype(BF16)
                T = T + mm(mm(tb, jnp.where(mk, lab, 0.0).astype(BF16)).astype(BF16), tb)
            tb = T.astype(BF16)
            tab = mm(tb, atb).astype(BF16)
            cv = jnp.concatenate([mm(tb, lak_v.astype(BF16)).astype(BF16), vb], axis=0)
            bkh = jnp.concatenate([Bh, Kh], axis=0).astype(BF16)
            g_sc[c, h] = lax.dot_general(bkh[:C], tab, tn, preferred_element_type=F32)
            q_sc[c, h] = lax.dot_general(bkh, cv, tn, preferred_element_type=F32)
            ry_sc[c, h] = Rt + mm(bot[:, :C].astype(BF16), tab)
            yc_sc[c, h] = mm(bot.astype(BF16), cv)
            gcol = jnp.sum(eye * ge_sc[h, pl.ds(pl.multiple_of(c * C, C), 1), :], axis=1, keepdims=True)
            gd_sc[c, h] = jnp.broadcast_to(gcol, (C, RWKV_HEAD))
        return carry

    lax.fori_loop(0, nc, phase1, 0)

    def phase2(c, carry):
        rows = pl.ds(pl.multiple_of(c * C, C), C)
        for h in range(RWKV_HEADS):
            st = st_ref[h]
            sb = st.astype(BF16)
            y_sc[h, rows, :] = mm(ry_sc[c, h].astype(BF16), sb) + yc_sc[c, h]
            st_ref[h] = st * gd_sc[c, h] + mm(g_sc[c, h].astype(BF16), sb) + q_sc[c, h]
        return carry

    lax.fori_loop(0, nc, phase2, 0)
    y = jnp.concatenate([y_sc[h] for h in range(RWKV_HEADS)], axis=-1)
    yb = _rwkv_post(y, r, k2, v, g, rk_ref[...], lng_ref[...], lnb_ref[...], ones)
    y_ref[...] = yb.astype(y_ref.dtype)
    for h in range(RWKV_HEADS):
        so_ref[h] = st_ref[h].T


def _p_pool_body(x_ref, g_ref, w_ref, wp_ref, sc_ref, y_ref, poolo_ref, hist_ref):
    t = pl.program_id(1)

    @pl.when(t == 0)
    def _():
        hist_ref[...] = jnp.zeros_like(hist_ref)

    pc = _dot(_rms(x_ref[...], g_ref[...]), w_ref[...])
    tm = pc.shape[0]
    X = jnp.concatenate([hist_ref[...], pc], axis=0)
    pos = t * tm + lax.broadcasted_iota(jnp.int32, (tm, POOL_GW), 0)
    ds = []
    for gi, win in enumerate(POOL_WINDOWS):
        sl = slice(gi * POOL_GW, (gi + 1) * POOL_GW)
        s = X[:, sl]
        sh = 1
        while sh < win:
            s = s + pltpu.roll(s, sh, 0)
            sh *= 2
        cnt = jnp.minimum(pos + 1, win).astype(F32)
        ds.append(s[POOL_HIST:] / cnt - pc[:, sl])
    y = _pool_project(jnp.concatenate(ds, axis=-1), wp_ref, sc_ref[...])
    y_ref[...] = y.astype(y_ref.dtype)
    hist_ref[...] = pc[tm - POOL_HIST:]
    poolo_ref[...] = pc[tm - POOL_HIST:]


def _p_gmlp_body(x_ref, g_ref, w_ref, lng_ref, lnb_ref, ws_ref, bsb_ref, y_ref):
    z2 = _dot(_rms(x_ref[...], g_ref[...]), w_ref[...])
    tm = z2.shape[0]
    u, zv = _gmlp_uz(z2, lng_ref[...], lnb_ref[...])
    tril = (lax.broadcasted_iota(jnp.int32, (CHUNK, CHUNK), 0)
            >= lax.broadcasted_iota(jnp.int32, (CHUNK, CHUNK), 1))
    ws = [jnp.where(tril, ws_ref[gi], 0.0).astype(BF16) for gi in range(GMLP_GROUPS)]
    zb = zv.astype(BF16)
    rows = []
    for c in range(tm // CHUNK):
        cols = [jnp.dot(ws[gi], zb[c * CHUNK:(c + 1) * CHUNK, gi * LANES:(gi + 1) * LANES],
                        preferred_element_type=F32) for gi in range(GMLP_GROUPS)]
        rows.append(jnp.concatenate(cols, axis=-1) + bsb_ref[...])
    s = jnp.concatenate(rows, axis=0)
    y_ref[...] = (u * s).astype(y_ref.dtype)


def _merge_body(x_ref, g_ref, wg_ref, ya_ref, yb_ref, yc_ref, yd_ref, wp_ref, wo_ref, o_ref):
    x = x_ref[...]
    xn = _rms(x, g_ref[...]).astype(BF16)
    merged = None
    for i, y_ref in enumerate((ya_ref, yb_ref, yc_ref, yd_ref)):
        gate = _sigmoid(jnp.dot(xn, wg_ref[:, i * D_MODEL:(i + 1) * D_MODEL],
                                preferred_element_type=F32))
        term = gate * jnp.dot(y_ref[...], wp_ref[i], preferred_element_type=F32)
        merged = term if merged is None else merged + term
    o_ref[...] = x + _dot(merged, wo_ref[...])


def _softmax_rows(s):
    e = jnp.exp(s - jnp.max(s, axis=-1, keepdims=True))
    return e / jnp.sum(e, axis=-1, keepdims=True)


def _p_xattn_body(x_ref, g_ref, wq_ref, k_ref, v_ref, wo_ref, o_ref):
    x = x_ref[...]
    q = _dot(_rms(x, g_ref[...]), wq_ref[...]).astype(BF16)
    kb = k_ref[...].astype(BF16)
    vb = v_ref[...].astype(BF16)
    outs = []
    for h in range(X_HEADS):
        sl = slice(h * X_HEAD_DIM, (h + 1) * X_HEAD_DIM)
        s = lax.dot_general(q[:, sl], kb[:, sl], (((1,), (1,)), ((), ())),
                            preferred_element_type=F32) * (X_HEAD_DIM ** -0.5)
        outs.append(jnp.dot(_softmax_rows(s).astype(BF16), vb[:, sl], preferred_element_type=F32))
    o_ref[...] = x + _dot(jnp.concatenate(outs, axis=-1), wo_ref[...])


def _ffn_conv(hist, h, cw, cb):
    return cb + cw[2:3] * h + cw[1:2] * _shift_rows(hist, h, 1) + cw[0:1] * _shift_rows(hist, h, 2)


def _p_ffn_body(x_ref, g_ref, wug_ref, wuv_ref, cwg_ref, cwv_ref, cbg_ref, cbv_ref, wd_ref,
                o_ref, ffo_ref, hist_ref):
    @pl.when(pl.program_id(1) == 0)
    def _():
        hist_ref[...] = jnp.zeros_like(hist_ref)

    x = x_ref[...]
    tm = x.shape[0]
    xn = _rms(x, g_ref[...]).astype(BF16)
    acc = x
    for j in range(D_FF // FF_CW):
        cs = slice(j * FF_CW, (j + 1) * FF_CW)
        vs = slice(D_FF + j * FF_CW, D_FF + (j + 1) * FF_CW)
        hg = jnp.dot(xn, wug_ref[:, cs], preferred_element_type=F32)
        hv = jnp.dot(xn, wuv_ref[:, cs], preferred_element_type=F32)
        cg = _ffn_conv(hist_ref[:, cs], hg, cwg_ref[:, cs], cbg_ref[:, cs])
        cv = _ffn_conv(hist_ref[:, vs], hv, cwv_ref[:, cs], cbv_ref[:, cs])
        acc = acc + _dot(_gelu(cg) * cv, wd_ref[cs, :])
        hist_ref[:, cs] = hg[tm - SUBLANES:]
        hist_ref[:, vs] = hv[tm - SUBLANES:]
    o_ref[...] = acc
    ffo_ref[...] = hist_ref[...]


def _memkv_body(m_ref, g_ref, wk_ref, wv_ref, k_ref, v_ref):
    mn = _rms(m_ref[...], g_ref[...]).astype(BF16)
    k_ref[...] = jnp.dot(mn, wk_ref[...], preferred_element_type=F32)
    v_ref[...] = jnp.dot(mn, wv_ref[...], preferred_element_type=F32)


def _final_norm_body(x_ref, g_ref, o_ref):
    o_ref[...] = _rms(x_ref[...], g_ref[...])


def _s_mix_body(x_ref, g_ref, wa_ref, wb_ref, wc_ref, wd_ref,
                lconv_ref, lh_ref, cw_ref, cb_ref, wra_ref, bra_ref, wix_ref, bix_ref, lam_ref,
                shift_ref, mu_ref, wwa_ref, w0_ref, a0_ref, g2_ref, kk_ref, ka_ref,
                pool_ref, wp_ref, psc_ref,
                lng_ref, lnb_ref, wsd_ref, bsr_ref,
                ya_ref, xa_ref, h_ref, yc_ref, pc_ref, yd_ref, zv_ref, p_ref,
                r_ref, k2_ref, v_ref, gg_ref, rh_ref, wh_ref, kh_ref, vh_ref, kkh_ref, ah_ref):
    xn = _rms(x_ref[...], g_ref[...]).astype(BF16)
    z = jnp.dot(xn, wa_ref[...], preferred_element_type=F32)
    xa = z[:, :D_LRU]
    cw = cw_ref[...]
    xc = cb_ref[...] + cw[0:1] * lconv_ref[0] + cw[1:2] * lconv_ref[1] + cw[2:3] * lconv_ref[2] + cw[3:4] * xa
    a, b = _lru_gates(xc, wra_ref, bra_ref, wix_ref, bix_ref, lam_ref)
    h = a * lh_ref[...] + b
    ya_ref[...] = (h * _gelu(z[:, D_LRU:])).astype(ya_ref.dtype)
    xa_ref[...] = xa
    h_ref[...] = h
    ones = _head_ones()
    p = jnp.dot(xn, wb_ref[...], preferred_element_type=F32)
    r, logw, k2, v, kk, aa, g = _rwkv_pre(p, shift_ref[...], mu_ref[...], wwa_ref, w0_ref[...],
                                           a0_ref[...], g2_ref, kk_ref[...], ka_ref[...], ones)
    decay = jnp.exp(logw)
    p_ref[...] = p
    r_ref[...] = r
    k2_ref[...] = k2
    v_ref[...] = v
    gg_ref[...] = g
    for ref, val in ((rh_ref, r), (wh_ref, decay), (kh_ref, k2), (vh_ref, v), (kkh_ref, kk), (ah_ref, aa)):
        for hd in range(RWKV_HEADS):
            ref[hd] = val[:, hd * RWKV_HEAD:(hd + 1) * RWKV_HEAD]
    pc = jnp.dot(xn, wc_ref[...], preferred_element_type=F32)
    ds = []
    for gi, win in enumerate(POOL_WINDOWS):
        sl = slice(gi * POOL_GW, (gi + 1) * POOL_GW)
        s = pc[:, sl]
        for j in range(POOL_BUF - (win - 1), POOL_BUF):
            s = s + pool_ref[j][:, sl]
        ds.append(s / float(min(PAST_LEN + 1, win)) - pc[:, sl])
    yc_ref[...] = _pool_project(jnp.concatenate(ds, axis=-1), wp_ref, psc_ref[...]).astype(yc_ref.dtype)
    pc_ref[...] = pc
    u, zv = _gmlp_uz(jnp.dot(xn, wd_ref[...], preferred_element_type=F32), lng_ref[...], lnb_ref[...])
    yd_ref[...] = (u * (wsd_ref[...] * zv + bsr_ref[...])).astype(yd_ref.dtype)
    zv_ref[...] = zv


def _s_rwkv_body(s_ref, rh_ref, wh_ref, kh_ref, vh_ref, kkh_ref, ah_ref,
                 r_ref, k2_ref, v_ref, g_ref, rk_ref, lng_ref, lnb_ref,
                 y_ref, so_ref, y_sc):
    eye = _eye64()
    bb = s_ref.shape[0]

    def body(b, carry):
        for h in range(RWKV_HEADS):
            rows = [ref[h, pl.ds(b, 1), :] for ref in (rh_ref, wh_ref, kh_ref, vh_ref, kkh_ref, ah_ref)]
            S, yrow = _rwkv_step(s_ref[b, h], *rows, eye)
            so_ref[b, h] = S
            y_sc[h, pl.ds(b, 1), :] = yrow
        return carry

    lax.fori_loop(0, bb, body, 0)
    y = jnp.concatenate([y_sc[h] for h in range(RWKV_HEADS)], axis=-1)
    yb = _rwkv_post(y, r_ref[...], k2_ref[...], v_ref[...], g_ref[...], rk_ref[...], lng_ref[...],
                    lnb_ref[...], _head_ones())
    y_ref[...] = yb.astype(y_ref.dtype)


def _s_xattn_body(x_ref, g_ref, wq_ref, k_ref, v_ref, wo_ref, o_ref, q_sc, a_sc):
    i = pl.program_id(0)
    bb = k_ref.shape[0]

    @pl.when(i == 0)
    def _():
        q_sc[...] = _dot(_rms(x_ref[...], g_ref[...]), wq_ref[...])

    hmask = (lax.broadcasted_iota(jnp.int32, (SUBLANES, D_MODEL), 1) // X_HEAD_DIM
             == lax.broadcasted_iota(jnp.int32, (SUBLANES, D_MODEL), 0))
    for j in range(bb):
        row = pl.ds(i * bb + j, 1)
        q8 = jnp.where(hmask, q_sc[row, :], 0.0).astype(BF16)
        s = lax.dot_general(q8, k_ref[j].astype(BF16), (((1,), (1,)), ((), ())),
                            preferred_element_type=F32) * (X_HEAD_DIM ** -0.5)
        o8 = jnp.dot(_softmax_rows(s).astype(BF16), v_ref[j].astype(BF16), preferred_element_type=F32)
        a_sc[row, :] = jnp.sum(jnp.where(hmask, o8, 0.0), axis=0, keepdims=True)

    @pl.when(i == pl.num_programs(0) - 1)
    def _():
        o_ref[...] = x_ref[...] + _dot(a_sc[...], wo_ref[...])


def _s_ffn_body(x_ref, g_ref, wug_ref, wuv_ref, sg_ref, sv_ref, cwg_ref, cwv_ref, cbg_ref, cbv_ref,
                wd_ref, o_ref, hg_ref, hv_ref, acc_ref):
    j = pl.program_id(0)

    @pl.when(j == 0)
    def _():
        acc_ref[...] = x_ref[...]

    xn = _rms(x_ref[...], g_ref[...]).astype(BF16)
    hg = jnp.dot(xn, wug_ref[...], preferred_element_type=F32)
    hv = jnp.dot(xn, wuv_ref[...], preferred_element_type=F32)
    cwg = cwg_ref[...]
    cwv = cwv_ref[...]
    cg = cbg_ref[...] + cwg[0:1] * sg_ref[0] + cwg[1:2] * sg_ref[1] + cwg[2:3] * hg
    cv = cbv_ref[...] + cwv[0:1] * sv_ref[0] + cwv[1:2] * sv_ref[1] + cwv[2:3] * hv
    acc_ref[...] += _dot(_gelu(cg) * cv, wd_ref[...])
    hg_ref[...] = hg
    hv_ref[...] = hv

    @pl.when(j == pl.num_programs(0) - 1)
    def _():
        o_ref[...] = acc_ref[...]


def _params(n_grid, vmem_mb):
    return pltpu.CompilerParams(dimension_semantics=("arbitrary",) * n_grid,
                                vmem_limit_bytes=vmem_mb << 20)


def _whole(shape):
    return pl.BlockSpec(tuple(shape), lambda *_: (0,) * len(shape))


def _layer(arr, l):
    shape = arr.shape[1:]
    return pl.BlockSpec((None,) + tuple(shape), lambda *_: (l,) + (0,) * len(shape))


def _sds(shape, dtype=F32):
    return jax.ShapeDtypeStruct(tuple(shape), dtype)


def _pair_blockdiag(w):
    L = w.shape[0]
    w = w.reshape(L, 4, 2, RWKV_HEAD, RWKV_HEAD)
    z = jnp.zeros_like(w[:, :, 0])
    top = jnp.concatenate([w[:, :, 0], z], axis=-1)
    bot = jnp.concatenate([z, w[:, :, 1]], axis=-1)
    return jnp.concatenate([top, bot], axis=-2)


def _vec(a):
    return a.reshape(a.shape[0], 1, -1)


def kernel(x_prompt, x_sample, state_lru_conv, state_lru_h, state_rwkv_shift, state_rwkv_S, state_pool, state_ffn_conv, cache_mem_k, cache_mem_v, mem_prompt, g_mix, w_in, lru_conv_w, lru_conv_b, lru_w_ra, lru_b_ra, lru_w_ix, lru_b_ix, lru_lambda, rwkv_mu, rwkv_w0, rwkv_w2, rwkv_a0, rwkv_a2, rwkv_g2, rwkv_k_k, rwkv_k_a, rwkv_r_k, rwkv_ln_g, rwkv_ln_b, pool_w, pool_scale, gmlp_ln_g, gmlp_ln_b, gmlp_w_s, gmlp_b_s, w_pa, w_pb, w_pc, w_pd, w_o, g_xattn, g_mem, w_xq, w_xk, w_xv, w_xo, g_ffn, w_up, ffn_conv_w, ffn_conv_b, w_down, g_final):
    B, T, D = x_prompt.shape
    SB = x_sample.shape[0]
    assert D == D_MODEL and w_in.shape == (DEPTH, D_MODEL, D_IN) and x_sample.shape[1] == 1
    assert T % TM_MIX == 0 and T % TM_RWKV == 0 and SB % S_RWKV_BB == 0 and SB % S_ATT_BB == 0

    wA = w_in[:, :, O_A:O_B].astype(BF16)
    wB = w_in[:, :, O_B:O_C].astype(BF16)
    wC = w_in[:, :, O_C:O_D].astype(BF16)
    wD = w_in[:, :, O_D:O_G].astype(BF16)
    wG = w_in[:, :, O_G:].astype(BF16)
    wra = _pair_blockdiag(lru_w_ra).astype(BF16)
    wix = _pair_blockdiag(lru_w_ix).astype(BF16)
    zer = jnp.zeros((DEPTH, R_DECAY, D_RWKV), F32)
    wwa = jnp.concatenate([jnp.concatenate([rwkv_w2, zer], axis=-1),
                           jnp.concatenate([zer, rwkv_a2], axis=-1)], axis=1).astype(BF16)
    g2 = rwkv_g2.astype(BF16)
    wpool = pool_w.astype(BF16)
    bsb = jnp.repeat(gmlp_b_s, LANES, axis=-1)
    off = PAST_LEN % CHUNK
    wsd = jnp.repeat(gmlp_w_s[:, :, off, off], LANES, axis=-1)[:, None, :]
    bsr = jnp.repeat(gmlp_b_s[:, off, :], LANES, axis=-1)[:, None, :]
    wP = jnp.stack([w_pa, w_pb, w_pc, w_pd], axis=1).astype(BF16)
    wO = w_o.astype(BF16)
    wXq, wXk, wXv, wXo = (w.astype(BF16) for w in (w_xq, w_xk, w_xv, w_xo))
    wUg = w_up[:, :, :D_FF].astype(BF16)
    wUv = w_up[:, :, D_FF:].astype(BF16)
    wDn = w_down.astype(BF16)
    cwg, cwv = ffn_conv_w[:, :, :D_FF], ffn_conv_w[:, :, D_FF:]
    cbg, cbv = _vec(ffn_conv_b[:, :D_FF]), _vec(ffn_conv_b[:, D_FF:])
    gmix, gxat, gffn, gmem = _vec(g_mix), _vec(g_xattn), _vec(g_ffn), _vec(g_mem)
    lcb, bra, bix, lam = _vec(lru_conv_b), _vec(lru_b_ra), _vec(lru_b_ix), _vec(lru_lambda)
    mu, w0, a0, kkw, kaw = _vec(rwkv_mu), _vec(rwkv_w0), _vec(rwkv_a0), _vec(rwkv_k_k), _vec(rwkv_k_a)
    rk = rwkv_r_k.reshape(DEPTH, 1, D_RWKV)
    rlg, rlb = _vec(rwkv_ln_g), _vec(rwkv_ln_b)
    psc, glg, glb = _vec(pool_scale), _vec(gmlp_ln_g), _vec(gmlp_ln_b)
    gfin = g_final.reshape(1, D_MODEL)

    kv_shape = _sds((DEPTH, B, N_MEM, D_MODEL))
    kv_spec = pl.BlockSpec((None, None, N_MEM, D_MODEL), lambda l, b: (l, b, 0, 0))
    wl_spec = pl.BlockSpec((None, D_MODEL, D_MODEL), lambda l, b: (l, 0, 0))
    p_mk, p_mv = pl.pallas_call(
        _memkv_body, grid=(DEPTH, B),
        in_specs=[pl.BlockSpec((None, N_MEM, D_MODEL), lambda l, b: (b, 0, 0)),
                  pl.BlockSpec((None, 1, D_MODEL), lambda l, b: (l, 0, 0)), wl_spec, wl_spec],
        out_specs=[kv_spec, kv_spec], out_shape=[kv_shape, kv_shape],
        compiler_params=_params(2, 32), name="memkv")(mem_prompt, gmem, wXk, wXv)

    nt = T // TM_MIX
    ntr = T // TM_RWKV
    xt_spec = pl.BlockSpec((None, TM_MIX, D_MODEL), lambda b, t: (b, t, 0))
    yt_spec = pl.BlockSpec((None, TM_MIX, D_LRU), lambda b, t: (b, t, 0))

    def tail_spec(rows, width):
        return pl.BlockSpec((None, rows, width), lambda b, t: (b, 0, 0))

    hp = x_prompt
    hs = x_sample.reshape(SB, D_MODEL)
    outs = {k: [] for k in ("p_conv", "s_conv", "p_h", "s_h", "p_shift", "s_shift", "p_S", "s_S",
                            "p_pool", "s_pool", "p_ffn", "s_ffn", "s_cv")}

    for l in range(DEPTH):
        L = functools.partial(_layer, l=l)

        yA, p_conv8, p_h8 = pl.pallas_call(
            _p_lru_body, grid=(B, nt),
            in_specs=[xt_spec, L(gmix), L(wA), L(lru_conv_w), L(lcb), L(wra), L(bra), L(wix), L(bix), L(lam)],
            out_specs=[yt_spec, tail_spec(SUBLANES, D_LRU), tail_spec(SUBLANES, D_LRU)],
            out_shape=[_sds((B, T, D_LRU), BF16), _sds((B, SUBLANES, D_LRU)), _sds((B, SUBLANES, D_LRU))],
            scratch_shapes=[pltpu.VMEM((SUBLANES, D_LRU), F32), pltpu.VMEM((SUBLANES, D_LRU), F32)],
            compiler_params=_params(2, 48), name="p_lru")(
                hp, gmix, wA, lru_conv_w, lcb, wra, bra, wix, bix, lam)

        head_sc = pltpu.VMEM((RWKV_HEADS, TM_RWKV, RWKV_HEAD), F32)
        chunk_sc = pltpu.VMEM((TM_RWKV // RWKV_CHUNK, RWKV_HEADS, RWKV_CHUNK, RWKV_HEAD), F32)
        yB, p_shift8, p_S = pl.pallas_call(
            _p_rwkv_body, grid=(B, ntr),
            in_specs=[pl.BlockSpec((None, TM_RWKV, D_MODEL), lambda b, t: (b, t, 0)),
                      L(gmix), L(wB), L(mu), L(wwa), L(w0), L(a0), L(g2), L(kkw), L(kaw), L(rk), L(rlg), L(rlb)],
            out_specs=[pl.BlockSpec((None, TM_RWKV, D_RWKV), lambda b, t: (b, t, 0)),
                       tail_spec(SUBLANES, D_RWKV_IN),
                       pl.BlockSpec((None, RWKV_HEADS, RWKV_HEAD, RWKV_HEAD), lambda b, t: (b, 0, 0, 0))],
            out_shape=[_sds((B, T, D_RWKV), BF16), _sds((B, SUBLANES, D_RWKV_IN)),
                       _sds((B, RWKV_HEADS, RWKV_HEAD, RWKV_HEAD))],
            scratch_shapes=[pltpu.VMEM((SUBLANES, D_RWKV_IN), F32),
                            pltpu.VMEM((RWKV_HEADS, RWKV_HEAD, RWKV_HEAD), F32)]
                           + [head_sc] * 9 + [chunk_sc] * 5,
            compiler_params=_params(2, 48), name="p_rwkv")(
                hp, gmix, wB, mu, wwa, w0, a0, g2, kkw, kaw, rk, rlg, rlb)

        yC, p_pool16 = pl.pallas_call(
            _p_pool_body, grid=(B, nt),
            in_specs=[xt_spec, L(gmix), L(wC), L(wpool), L(psc)],
            out_specs=[yt_spec, tail_spec(POOL_HIST, D_POOL)],
            out_shape=[_sds((B, T, D_POOL), BF16), _sds((B, POOL_HIST, D_POOL))],
            scratch_shapes=[pltpu.VMEM((POOL_HIST, D_POOL), F32)],
            compiler_params=_params(2, 48), name="p_pool")(hp, gmix, wC, wpool, psc)

        yD = pl.pallas_call(
            _p_gmlp_body, grid=(B, nt),
            in_specs=[xt_spec, L(gmix), L(wD), L(glg), L(glb), L(gmlp_w_s), L(bsb)],
            out_specs=yt_spec, out_shape=_sds((B, T, D_GMLP), BF16),
            compiler_params=_params(2, 48), name="p_gmlp")(hp, gmix, wD, glg, glb, gmlp_w_s, bsb)

        def merge(x2d, ys, tm):
            rows = x2d.shape[0]
            xs = pl.BlockSpec((tm, D_MODEL), lambda i: (i, 0))
            ysp = pl.BlockSpec((tm, D_LRU), lambda i: (i, 0))
            return pl.pallas_call(
                _merge_body, grid=(rows // tm,),
                in_specs=[xs, L(gmix), L(wG), ysp, ysp, ysp, ysp, L(wP), L(wO)],
                out_specs=xs, out_shape=_sds((rows, D_MODEL)),
                compiler_params=_params(1, 56), name="merge")(x2d, gmix, wG, *ys, wP, wO)

        hp = merge(hp.reshape(B * T, D_MODEL), [y.reshape(B * T, -1) for y in (yA, yB, yC, yD)],
                   TM_MIX).reshape(B, T, D_MODEL)

        kvb_spec = pl.BlockSpec((None, None, N_MEM, D_MODEL), lambda b, t: (l, b, 0, 0))
        hp = pl.pallas_call(
            _p_xattn_body, grid=(B, nt),
            in_specs=[xt_spec, L(gxat), L(wXq), kvb_spec, kvb_spec, L(wXo)],
            out_specs=xt_spec, out_shape=_sds((B, T, D_MODEL)),
            compiler_params=_params(2, 48), name="p_xattn")(hp, gxat, wXq, p_mk, p_mv, wXo)

        hp, p_ffn8 = pl.pallas_call(
            _p_ffn_body, grid=(B, nt),
            in_specs=[xt_spec, L(gffn), L(wUg), L(wUv), L(cwg), L(cwv), L(cbg), L(cbv), L(wDn)],
            out_specs=[xt_spec, tail_spec(SUBLANES, 2 * D_FF)],
            out_shape=[_sds((B, T, D_MODEL)), _sds((B, SUBLANES, 2 * D_FF))],
            scratch_shapes=[pltpu.VMEM((SUBLANES, 2 * D_FF), F32)],
            compiler_params=_params(2, 56), name="p_ffn")(hp, gffn, wUg, wUv, cwg, cwv, cbg, cbv, wDn)

        outs["p_conv"].append(p_conv8[:, SUBLANES - 3:])
        outs["p_h"].append(p_h8[:, SUBLANES - 1])
        outs["p_shift"].append(p_shift8[:, SUBLANES - 1:])
        outs["p_S"].append(p_S)
        outs["p_pool"].append(p_pool16[:, POOL_HIST - POOL_BUF:])
        outs["p_ffn"].append(p_ffn8[:, SUBLANES - 2:])

        lconv = jnp.swapaxes(state_lru_conv[l], 0, 1)
        spool = jnp.swapaxes(state_pool[l], 0, 1)
        sffn = jnp.swapaxes(state_ffn_conv[l], 0, 1)
        shift = state_rwkv_shift[l].reshape(SB, D_RWKV_IN)
        row512 = _sds((SB, D_LRU))
        heads = _sds((RWKV_HEADS, SB, RWKV_HEAD))
        mix_in = [hs, gmix[l], wA[l], wB[l], wC[l], wD[l],
                  lconv, state_lru_h[l], lru_conv_w[l], lcb[l], wra[l], bra[l], wix[l], bix[l], lam[l],
                  shift, mu[l], wwa[l], w0[l], a0[l], g2[l], kkw[l], kaw[l],
                  spool, wpool[l], psc[l],
                  glg[l], glb[l], wsd[l], bsr[l]]
        mix_out = [_sds((SB, D_LRU), BF16), row512, row512, _sds((SB, D_POOL), BF16), row512,
                   _sds((SB, D_GMLP), BF16), row512, _sds((SB, D_RWKV_IN)),
                   row512, row512, row512, row512, heads, heads, heads, heads, heads, heads]
        (yA, s_xa, s_h, yC, s_pc, yD, s_zv, s_p, s_r, s_k2, s_v, s_g,
         rh, wh, kh, vh, kkh, ah) = pl.pallas_call(
            _s_mix_body, grid=(1,),
            in_specs=[_whole(a.shape) for a in mix_in],
            out_specs=[_whole(o.shape) for o in mix_out], out_shape=mix_out,
            compiler_params=_params(1, 56), name="s_mix")(*mix_in)

        hd_spec = pl.BlockSpec((RWKV_HEADS, S_RWKV_BB, RWKV_HEAD), lambda i: (0, i, 0))
        rb_spec = pl.BlockSpec((S_RWKV_BB, D_RWKV), lambda i: (i, 0))
        st_spec = pl.BlockSpec((None, S_RWKV_BB, RWKV_HEADS, RWKV_HEAD, RWKV_HEAD), lambda i: (l, i, 0, 0, 0))
        so_spec = pl.BlockSpec((S_RWKV_BB, RWKV_HEADS, RWKV_HEAD, RWKV_HEAD), lambda i: (i, 0, 0, 0))
        yB, s_S = pl.pallas_call(
            _s_rwkv_body, grid=(SB // S_RWKV_BB,),
            in_specs=[st_spec] + [hd_spec] * 6 + [rb_spec] * 4 + [L(rk), L(rlg), L(rlb)],
            out_specs=[rb_spec, so_spec],
            out_shape=[_sds((SB, D_RWKV), BF16), _sds((SB, RWKV_HEADS, RWKV_HEAD, RWKV_HEAD))],
            scratch_shapes=[pltpu.VMEM((RWKV_HEADS, S_RWKV_BB, RWKV_HEAD), F32)],
            compiler_params=_params(1, 48), name="s_rwkv")(
                state_rwkv_S, rh, wh, kh, vh, kkh, ah, s_r, s_k2, s_v, s_g, rk, rlg, rlb)

        hs = merge(hs, [yA, yB, yC, yD], SB)

        kc_spec = pl.BlockSpec((None, S_ATT_BB, N_MEM, D_MODEL), lambda i: (l, i, 0, 0))
        xs_spec = _whole((SB, D_MODEL))
        hs = pl.pallas_call(
            _s_xattn_body, grid=(SB // S_ATT_BB,),
            in_specs=[xs_spec, L(gxat), L(wXq), kc_spec, kc_spec, L(wXo)],
            out_specs=xs_spec, out_shape=_sds((SB, D_MODEL)),
            scratch_shapes=[pltpu.VMEM((SB, D_MODEL), F32), pltpu.VMEM((SB, D_MODEL), F32)],
            compiler_params=_params(1, 56), name="s_xattn")(
                hs, gxat, wXq, cache_mem_k.reshape(DEPTH, SB, N_MEM, D_MODEL),
                cache_mem_v.reshape(DEPTH, SB, N_MEM, D_MODEL), wXo)

        ncf = D_FF // FF_CW
        wu_spec = pl.BlockSpec((None, D_MODEL, FF_CW), lambda j: (l, 0, j))
        sg_spec = pl.BlockSpec((2, SB, FF_CW), lambda j: (0, 0, j))
        sv_spec = pl.BlockSpec((2, SB, FF_CW), lambda j: (0, 0, j + ncf))
        cw_spec = pl.BlockSpec((None, 3, FF_CW), lambda j: (l, 0, j))
        cb_spec = pl.BlockSpec((None, 1, FF_CW), lambda j: (l, 0, j))
        hh_spec = pl.BlockSpec((SB, FF_CW), lambda j: (0, j))
        hs, s_hg, s_hv = pl.pallas_call(
            _s_ffn_body, grid=(ncf,),
            in_specs=[xs_spec, L(gffn), wu_spec, wu_spec, sg_spec, sv_spec, cw_spec, cw_spec, cb_spec, cb_spec,
                      pl.BlockSpec((None, FF_CW, D_MODEL), lambda j: (l, j, 0))],
            out_specs=[xs_spec, hh_spec, hh_spec],
            out_shape=[_sds((SB, D_MODEL)), _sds((SB, D_FF)), _sds((SB, D_FF))],
            scratch_shapes=[pltpu.VMEM((SB, D_MODEL), F32)],
            compiler_params=_params(1, 48), name="s_ffn")(
                hs, gffn, wUg, wUv, sffn, sffn, cwg, cwv, cbg, cbv, wDn)

        outs["s_conv"].append(jnp.concatenate([state_lru_conv[l][:, 1:], s_xa[:, None]], axis=1))
        outs["s_h"].append(s_h)
        outs["s_shift"].append(s_p[:, None])
        outs["s_S"].append(s_S)
        outs["s_pool"].append(jnp.concatenate([state_pool[l][:, 1:], s_pc[:, None]], axis=1))
        outs["s_ffn"].append(jnp.concatenate(
            [state_ffn_conv[l][:, 1:], jnp.concatenate([s_hg, s_hv], axis=-1)[:, None]], axis=1))
        outs["s_cv"].append(s_zv[:, None])

    def final_norm(x2d, tm):
        rows = x2d.shape[0]
        spec = pl.BlockSpec((tm, D_MODEL), lambda i: (i, 0))
        return pl.pallas_call(_final_norm_body, grid=(rows // tm,),
                              in_specs=[spec, _whole((1, D_MODEL))], out_specs=spec,
                              out_shape=_sds((rows, D_MODEL)),
                              compiler_params=_params(1, 32), name="final_norm")(x2d, gfin)

    y_prompt = final_norm(hp.reshape(B * T, D_MODEL), 1024).reshape(B, T, D_MODEL)
    y_sample = final_norm(hs, SB).reshape(SB, 1, D_MODEL)
    stk = {k: jnp.stack(v, axis=0) for k, v in outs.items()}
    mem_shape = (DEPTH, B, N_MEM, X_HEADS, X_HEAD_DIM)
    return (y_prompt, y_sample,
            stk["p_conv"], stk["s_conv"], stk["p_h"], stk["s_h"],
            stk["p_shift"], stk["s_shift"], stk["p_S"], stk["s_S"],
            stk["p_pool"], stk["s_pool"], stk["p_ffn"], stk["s_ffn"],
            p_mk.reshape(mem_shape), p_mv.reshape(mem_shape), stk["s_cv"])
```

```python
import functools
import math

import jax
import jax.numpy as jnp
from jax import lax
from jax.experimental import pallas as pl
from jax.experimental.pallas import tpu as pltpu

F32 = jnp.float32
BF16 = jnp.bfloat16

SUBLANES = 8
LANES = 128

D_MODEL = 1024
DEPTH = 4
N_MEM = 256
D_LRU = 512
LRU_C = 8.0
D_RWKV = 512
RWKV_HEAD = 64
RWKV_HEADS = D_RWKV // RWKV_HEAD
R_DECAY = 64
R_AAA = 64
R_GATE = 128
D_RWKV_IN = 3 * D_RWKV + R_DECAY + R_AAA + R_GATE
GN_EPS = 64e-5
D_POOL = 512
POOL_WINDOWS = (2, 4, 8, 16)
POOL_GW = D_POOL // len(POOL_WINDOWS)
POOL_BUF = max(POOL_WINDOWS) - 1
POOL_HIST = 16
D_GMLP = 512
GMLP_GROUPS = 4
CHUNK = 128
N_BRANCH = 4
X_HEADS = 4
X_HEAD_DIM = D_MODEL // X_HEADS
D_FF = 3 * D_MODEL
EPS = 1e-6
PAST_LEN = 16384

O_A = 0
O_B = 2 * D_LRU
O_C = O_B + D_RWKV_IN
O_D = O_C + D_POOL
O_G = O_D + 2 * D_GMLP
D_IN = O_G + N_BRANCH * D_MODEL

TM_MIX = 512
TM_RWKV = 256
FF_CW = 768
RWKV_CHUNK = 64
S_RWKV_BB = 16
S_ATT_BB = 8


def _dot(a, b):
    return jnp.dot(a.astype(BF16), b.astype(BF16), preferred_element_type=F32)


def _rms(x, g):
    return x * lax.rsqrt(jnp.mean(x * x, axis=-1, keepdims=True) + EPS) * g


def _gelu(x):
    c = math.sqrt(2.0 / math.pi)
    return 0.5 * x * (1.0 + jnp.tanh(c * (x + 0.044715 * (x * x * x))))


def _softplus(x):
    return jnp.maximum(x, 0.0) + jnp.log1p(jnp.exp(-jnp.abs(x)))


def _sigmoid(x):
    return jax.nn.sigmoid(x)


def _shift_rows(hist, cur, s):
    n = hist.shape[0]
    return pltpu.roll(jnp.concatenate([hist, cur], axis=0), s, 0)[n:]


def _head_ones():
    r = lax.broadcasted_iota(jnp.int32, (LANES, LANES), 0) // RWKV_HEAD
    c = lax.broadcasted_iota(jnp.int32, (LANES, LANES), 1) // RWKV_HEAD
    return jnp.where(r == c, 1.0, 0.0).astype(BF16)


def _segsum(x, ones):
    outs = []
    for p in range(x.shape[1] // LANES):
        xb = x[:, p * LANES:(p + 1) * LANES]
        hi = xb.astype(BF16)
        lo = (xb - hi.astype(F32)).astype(BF16)
        outs.append(jnp.dot(hi, ones, preferred_element_type=F32)
                    + jnp.dot(lo, ones, preferred_element_type=F32))
    return jnp.concatenate(outs, axis=-1)


def _blockdiag_dot(x, w_ref):
    outs = []
    for p in range(x.shape[1] // LANES):
        outs.append(jnp.dot(x[:, p * LANES:(p + 1) * LANES].astype(BF16), w_ref[p],
                            preferred_element_type=F32))
    return jnp.concatenate(outs, axis=-1)


def _lru_gates(xc, wra_ref, bra_ref, wix_ref, bix_ref, lam_ref):
    r = _sigmoid(_blockdiag_dot(xc, wra_ref) + bra_ref[...])
    i = _sigmoid(_blockdiag_dot(xc, wix_ref) + bix_ref[...])
    log_a = -LRU_C * r * _softplus(-lam_ref[...])
    a = jnp.exp(log_a)
    mult = jnp.sqrt(-jnp.tanh(log_a) * (1.0 + a * a))
    return a, mult * (i * xc)


def _rwkv_pre(p, prev, mu, wwa_ref, w0, a0, g2_ref, k_k, k_a, ones):
    px = p + (prev - p) * mu
    r = px[:, 0:D_RWKV]
    k = px[:, D_RWKV:2 * D_RWKV]
    v = px[:, 2 * D_RWKV:3 * D_RWKV]
    lo = px[:, 3 * D_RWKV:3 * D_RWKV + R_DECAY + R_AAA]
    g_lo = px[:, 3 * D_RWKV + R_DECAY + R_AAA:]
    lane = lax.broadcasted_iota(jnp.int32, lo.shape, 1)
    wa = _dot(jnp.where(lane < R_DECAY, jnp.tanh(lo), lo), wwa_ref[...])
    logw = -jnp.exp(-_softplus(-(w0 + wa[:, :D_RWKV])) - 0.5)
    a = _sigmoid(a0 + wa[:, D_RWKV:])
    g = _dot(_sigmoid(g_lo), g2_ref[...])
    kk = k * k_k
    kk = kk * lax.rsqrt(jnp.maximum(_segsum(kk * kk, ones), 1e-24))
    k2 = k * (1.0 + (a - 1.0) * k_a)
    return r, logw, k2, v, kk, a, g


def _rwkv_step(S, r, w, k, v, kk, a, eye):
    sa = -jnp.sum(S * kk, axis=1, keepdims=True)
    vcol = jnp.sum(eye * v, axis=1, keepdims=True)
    S = S * w + sa * (kk * a) + vcol * k
    ycol = jnp.sum(S * r, axis=1, keepdims=True)
    yrow = jnp.sum(eye * ycol, axis=0, keepdims=True)
    return S, yrow


def _rwkv_post(y, r, k2, v, g, r_k, ln_g, ln_b, ones):
    inv = 1.0 / RWKV_HEAD
    mean = _segsum(y, ones) * inv
    yc = y - mean
    var = _segsum(yc * yc, ones) * inv
    yn = yc * lax.rsqrt(var + GN_EPS) * ln_g + ln_b
    bonus = _segsum(r * k2 * r_k, ones) * v
    return (yn + bonus) * g


def _eye64():
    r = lax.broadcasted_iota(jnp.int32, (RWKV_HEAD, RWKV_HEAD), 0)
    c = lax.broadcasted_iota(jnp.int32, (RWKV_HEAD, RWKV_HEAD), 1)
    return jnp.where(r == c, 1.0, 0.0).astype(F32)


def _pool_project(d, wp_ref, scale):
    return _blockdiag_dot(d, wp_ref) * scale


def _gmlp_uz(z2, ln_g, ln_b):
    u = _gelu(z2[:, :D_GMLP])
    v = _gelu(z2[:, D_GMLP:])
    mu = jnp.mean(v, axis=-1, keepdims=True)
    vc = v - mu
    var = jnp.mean(vc * vc, axis=-1, keepdims=True)
    return u, vc * lax.rsqrt(var + 1e-5) * ln_g + ln_b


def _p_lru_body(x_ref, g_ref, w_ref, cw_ref, cb_ref, wra_ref, bra_ref, wix_ref, bix_ref, lam_ref,
                y_ref, convo_ref, ho_ref, hist_ref, h_ref):
    @pl.when(pl.program_id(1) == 0)
    def _():
        hist_ref[...] = jnp.zeros_like(hist_ref)
        h_ref[...] = jnp.zeros_like(h_ref)

    z = _dot(_rms(x_ref[...], g_ref[...]), w_ref[...])
    xa = z[:, :D_LRU]
    ga = z[:, D_LRU:]
    tm = xa.shape[0]
    hist = hist_ref[...]
    cw = cw_ref[...]
    xc = cb_ref[...] + cw[3:4] * xa
    for s in (1, 2, 3):
        xc = xc + cw[3 - s:4 - s] * _shift_rows(hist, xa, s)
    a, b = _lru_gates(xc, wra_ref, bra_ref, wix_ref, bix_ref, lam_ref)
    row = lax.broadcasted_iota(jnp.int32, a.shape, 0)
    s = 1
    while s < tm:
        m = row >= s
        b = jnp.where(m, a * pltpu.roll(b, s, 0) + b, b)
        a = jnp.where(m, a * pltpu.roll(a, s, 0), a)
        s *= 2
    h = a * h_ref[SUBLANES - 1:SUBLANES, :] + b
    y_ref[...] = (h * _gelu(ga)).astype(y_ref.dtype)
    hist_ref[...] = xa[tm - SUBLANES:]
    h_ref[...] = h[tm - SUBLANES:]
    convo_ref[...] = xa[tm - SUBLANES:]
    ho_ref[...] = h[tm - SUBLANES:]


def _p_rwkv_body(x_ref, g_ref, w_ref, mu_ref, wwa_ref, w0_ref, a0_ref, g2_ref, kk_ref, ka_ref,
                 rk_ref, lng_ref, lnb_ref,
                 y_ref, shifto_ref, so_ref,
                 hist_ref, st_ref, at_sc, rt_sc, bt_sc, kt_sc, v_sc, bh_sc, kh_sc, ge_sc, y_sc,
                 g_sc, q_sc, ry_sc, yc_sc, gd_sc):
    @pl.when(pl.program_id(1) == 0)
    def _():
        hist_ref[...] = jnp.zeros_like(hist_ref)
        st_ref[...] = jnp.zeros_like(st_ref)

    C = RWKV_CHUNK
    ones = _head_ones()
    p = _dot(_rms(x_ref[...], g_ref[...]), w_ref[...])
    tm = p.shape[0]
    nc = tm // C
    prev = _shift_rows(hist_ref[...], p, 1)
    r, logw, k2, v, kk, a, g = _rwkv_pre(p, prev, mu_ref[...], wwa_ref, w0_ref[...], a0_ref[...],
                                          g2_ref, kk_ref[...], ka_ref[...], ones)
    hist_ref[...] = p[tm - SUBLANES:]
    shifto_ref[...] = p[tm - SUBLANES:]
    rowc = lax.broadcasted_iota(jnp.int32, logw.shape, 0) % C
    lg = logw
    s = 1
    while s < C:
        lg = lg + jnp.where(rowc >= s, pltpu.roll(lg, s, 0), 0.0)
        s *= 2
    lg_end = jnp.concatenate(
        [jnp.broadcast_to(lg[(c + 1) * C - 1:(c + 1) * C, :], (C, D_RWKV)) for c in range(nc)], axis=0)
    inv_gam = jnp.exp(-lg)
    to_end = jnp.exp(lg_end - lg)
    b = kk * a
    for sc, val in ((at_sc, -kk * jnp.exp(lg - logw)), (rt_sc, r * jnp.exp(lg)), (bt_sc, b * inv_gam),
                    (kt_sc, k2 * inv_gam), (v_sc, v), (bh_sc, b * to_end), (kh_sc, k2 * to_end),
                    (ge_sc, jnp.exp(lg_end))):
        for h in range(RWKV_HEADS):
            sc[h] = val[:, h * RWKV_HEAD:(h + 1) * RWKV_HEAD]

    eye = _eye64()
    ri = lax.broadcasted_iota(jnp.int32, (C, C), 0)
    ci = lax.broadcasted_iota(jnp.int32, (C, C), 1)
    r2 = lax.broadcasted_iota(jnp.int32, (C, 2 * C), 0)
    c2 = lax.broadcasted_iota(jnp.int32, (C, 2 * C), 1) % C
    strict2 = c2 < r2
    incl2 = c2 <= r2
    pair = (ri // 2) == (ci // 2)
    levels = []
    nb = 2
    while nb < C:
        levels.append(((ri // (2 * nb)) == (ci // (2 * nb))) & ((ri // nb) != (ci // nb)))
        nb *= 2
    zeros_c = jnp.zeros((C, RWKV_HEAD), BF16)
    tn = (((0,), (0,)), ((), ()))
    nt_ = (((1,), (1,)), ((), ()))

    def mm(x, y):
        return jnp.dot(x, y, preferred_element_type=F32)

    H = range(RWKV_HEADS)

    def phase1(c, carry):
        rows = pl.ds(pl.multiple_of(c * C, C), C)
        At, Rt, Bt, Kt, V, Bh, Kh = ([sc[h, rows, :] for h in H] for sc in
                                     (at_sc, rt_sc, bt_sc, kt_sc, v_sc, bh_sc, kh_sc))
        atb = [x.astype(BF16) for x in At]
        vb = [x.astype(BF16) for x in V]
        prod = [lax.dot_general(jnp.concatenate([atb[h], Rt[h].astype(BF16)], axis=0),
                                jnp.concatenate([Bt[h], Kt[h]], axis=0).astype(BF16), nt_,
                                preferred_element_type=F32) for h in H]
        top = [jnp.where(strict2, x[:C], 0.0) for x in prod]
        bot = [jnp.where(incl2, x[C:], 0.0).astype(BF16) for x in prod]
        lab = [x[:, :C] for x in top]
        lak_v = [mm(top[h].astype(BF16), jnp.concatenate([zeros_c, vb[h]], axis=0)) for h in H]
        T = [eye + jnp.where(pair, x, 0.0) for x in lab]
        for mk in levels:
            tb = [x.astype(BF16) for x in T]
            u = [mm(tb[h], jnp.where(mk, lab[h], 0.0).astype(BF16)).astype(BF16) for h in H]
            T = [T[h] + mm(u[h], tb[h]) for h in H]
        tb = [x.astype(BF16) for x in T]
        tab = [mm(tb[h], atb[h]).astype(BF16) for h in H]
        cv = [jnp.concatenate([mm(tb[h], lak_v[h].astype(BF16)).astype(BF16), vb[h]], axis=0) for h in H]
        bkh = [jnp.concatenate([Bh[h], Kh[h]], axis=0).astype(BF16) for h in H]
        for h in H:
            g_sc[c, h] = lax.dot_general(bkh[h][:C], tab[h], tn, preferred_element_type=F32)
        for h in H:
            q_sc[c, h] = lax.dot_general(bkh[h], cv[h], tn, preferred_element_type=F32)
        for h in H:
            ry_sc[c, h] = Rt[h] + mm(bot[h][:, :C], tab[h])
        for h in H:
            yc_sc[c, h] = mm(bot[h], cv[h])
        for h in H:
            gcol = jnp.sum(eye * ge_sc[h, pl.ds(pl.multiple_of(c * C, C), 1), :], axis=1, keepdims=True)
            gd_sc[c, h] = jnp.broadcast_to(gcol, (C, RWKV_HEAD))
        return carry

    lax.fori_loop(0, nc, phase1, 0)

    def phase2(c, carry):
        rows = pl.ds(pl.multiple_of(c * C, C), C)
        st = [st_ref[h] for h in H]
        sb = [x.astype(BF16) for x in st]
        gs = [mm(g_sc[c, h].astype(BF16), sb[h]) for h in H]
        ys = [mm(ry_sc[c, h].astype(BF16), sb[h]) for h in H]
        for h in H:
            st_ref[h] = st[h] * gd_sc[c, h] + gs[h] + q_sc[c, h]
            y_sc[h, rows, :] = ys[h] + yc_sc[c, h]
        return carry

    lax.fori_loop(0, nc, phase2, 0)
    y = jnp.concatenate([y_sc[h] for h in range(RWKV_HEADS)], axis=-1)
    yb = _rwkv_post(y, r, k2, v, g, rk_ref[...], lng_ref[...], lnb_ref[...], ones)
    y_ref[...] = yb.astype(y_ref.dtype)
    for h in range(RWKV_HEADS):
        so_ref[h] = st_ref[h].T


def _p_pool_body(x_ref, g_ref, w_ref, wp_ref, sc_ref, y_ref, poolo_ref, hist_ref):
    t = pl.program_id(1)

    @pl.when(t == 0)
    def _():
        hist_ref[...] = jnp.zeros_like(hist_ref)

    pc = _dot(_rms(x_ref[...], g_ref[...]), w_ref[...])
    tm = pc.shape[0]
    X = jnp.concatenate([hist_ref[...], pc], axis=0)
    pos = t * tm + lax.broadcasted_iota(jnp.int32, (tm, POOL_GW), 0)
    ds = []
    for gi, win in enumerate(POOL_WINDOWS):
        sl = slice(gi * POOL_GW, (gi + 1) * POOL_GW)
        s = X[:, sl]
        sh = 1
        while sh < win:
            s = s + pltpu.roll(s, sh, 0)
            sh *= 2
        cnt = jnp.minimum(pos + 1, win).astype(F32)
        ds.append(s[POOL_HIST:] / cnt - pc[:, sl])
    y = _pool_project(jnp.concatenate(ds, axis=-1), wp_ref, sc_ref[...])
    y_ref[...] = y.astype(y_ref.dtype)
    hist_ref[...] = pc[tm - POOL_HIST:]
    poolo_ref[...] = pc[tm - POOL_HIST:]


def _p_gmlp_body(x_ref, g_ref, w_ref, lng_ref, lnb_ref, ws_ref, bsb_ref, y_ref):
    z2 = _dot(_rms(x_ref[...], g_ref[...]), w_ref[...])
    tm = z2.shape[0]
    u, zv = _gmlp_uz(z2, lng_ref[...], lnb_ref[...])
    tril = (lax.broadcasted_iota(jnp.int32, (CHUNK, CHUNK), 0)
            >= lax.broadcasted_iota(jnp.int32, (CHUNK, CHUNK), 1))
    ws = [jnp.where(tril, ws_ref[gi], 0.0).astype(BF16) for gi in range(GMLP_GROUPS)]
    zb = zv.astype(BF16)
    rows = []
    for c in range(tm // CHUNK):
        cols = [jnp.dot(ws[gi], zb[c * CHUNK:(c + 1) * CHUNK, gi * LANES:(gi + 1) * LANES],
                        preferred_element_type=F32) for gi in range(GMLP_GROUPS)]
        rows.append(jnp.concatenate(cols, axis=-1) + bsb_ref[...])
    s = jnp.concatenate(rows, axis=0)
    y_ref[...] = (u * s).astype(y_ref.dtype)


def _merge_body(x_ref, g_ref, wg_ref, ya_ref, yb_ref, yc_ref, yd_ref, wp_ref, wo_ref, o_ref):
    x = x_ref[...]
    xn = _rms(x, g_ref[...]).astype(BF16)
    merged = None
    for i, y_ref in enumerate((ya_ref, yb_ref, yc_ref, yd_ref)):
        gate = _sigmoid(jnp.dot(xn, wg_ref[:, i * D_MODEL:(i + 1) * D_MODEL],
                                preferred_element_type=F32))
        term = gate * jnp.dot(y_ref[...], wp_ref[i], preferred_element_type=F32)
        merged = term if merged is None else merged + term
    o_ref[...] = x + _dot(merged, wo_ref[...])


def _softmax_rows(s):
    e = jnp.exp(s - jnp.max(s, axis=-1, keepdims=True))
    return e / jnp.sum(e, axis=-1, keepdims=True)


def _p_xattn_body(x_ref, g_ref, wq_ref, k_ref, v_ref, wo_ref, o_ref):
    x = x_ref[...]
    q = _dot(_rms(x, g_ref[...]), wq_ref[...]).astype(BF16)
    kb = k_ref[...].astype(BF16)
    vb = v_ref[...].astype(BF16)
    outs = []
    for h in range(X_HEADS):
        sl = slice(h * X_HEAD_DIM, (h + 1) * X_HEAD_DIM)
        s = lax.dot_general(q[:, sl], kb[:, sl], (((1,), (1,)), ((), ())),
                            preferred_element_type=F32) * (X_HEAD_DIM ** -0.5)
        outs.append(jnp.dot(_softmax_rows(s).astype(BF16), vb[:, sl], preferred_element_type=F32))
    o_ref[...] = x + _dot(jnp.concatenate(outs, axis=-1), wo_ref[...])


def _ffn_conv(hist, h, cw, cb):
    return cb + cw[2:3] * h + cw[1:2] * _shift_rows(hist, h, 1) + cw[0:1] * _shift_rows(hist, h, 2)


def _p_ffn_body(x_ref, g_ref, wug_ref, wuv_ref, cwg_ref, cwv_ref, cbg_ref, cbv_ref, wd_ref,
                o_ref, ffo_ref, hist_ref):
    @pl.when(pl.program_id(1) == 0)
    def _():
        hist_ref[...] = jnp.zeros_like(hist_ref)

    x = x_ref[...]
    tm = x.shape[0]
    xn = _rms(x, g_ref[...]).astype(BF16)
    acc = x
    for j in range(D_FF // FF_CW):
        cs = slice(j * FF_CW, (j + 1) * FF_CW)
        vs = slice(D_FF + j * FF_CW, D_FF + (j + 1) * FF_CW)
        hg = jnp.dot(xn, wug_ref[:, cs], preferred_element_type=F32)
        hv = jnp.dot(xn, wuv_ref[:, cs], preferred_element_type=F32)
        cg = _ffn_conv(hist_ref[:, cs], hg, cwg_ref[:, cs], cbg_ref[:, cs])
        cv = _ffn_conv(hist_ref[:, vs], hv, cwv_ref[:, cs], cbv_ref[:, cs])
        acc = acc + _dot(_gelu(cg) * cv, wd_ref[cs, :])
        hist_ref[:, cs] = hg[tm - SUBLANES:]
        hist_ref[:, vs] = hv[tm - SUBLANES:]
    o_ref[...] = acc
    ffo_ref[...] = hist_ref[...]


def _memkv_body(m_ref, g_ref, wk_ref, wv_ref, k_ref, v_ref):
    mn = _rms(m_ref[...], g_ref[...]).astype(BF16)
    k_ref[...] = jnp.dot(mn, wk_ref[...], preferred_element_type=F32)
    v_ref[...] = jnp.dot(mn, wv_ref[...], preferred_element_type=F32)


def _final_norm_body(x_ref, g_ref, o_ref):
    o_ref[...] = _rms(x_ref[...], g_ref[...])


def _s_mix_body(x_ref, g_ref, wa_ref, wb_ref, wc_ref, wd_ref,
                lconv_ref, lh_ref, cw_ref, cb_ref, wra_ref, bra_ref, wix_ref, bix_ref, lam_ref,
                shift_ref, mu_ref, wwa_ref, w0_ref, a0_ref, g2_ref, kk_ref, ka_ref,
                pool_ref, wp_ref, psc_ref,
                lng_ref, lnb_ref, wsd_ref, bsr_ref,
                ya_ref, xa_ref, h_ref, yc_ref, pc_ref, yd_ref, zv_ref, p_ref,
                r_ref, k2_ref, v_ref, gg_ref, rh_ref, wh_ref, kh_ref, vh_ref, kkh_ref, ah_ref):
    xn = _rms(x_ref[...], g_ref[...]).astype(BF16)
    z = jnp.dot(xn, wa_ref[...], preferred_element_type=F32)
    xa = z[:, :D_LRU]
    cw = cw_ref[...]
    xc = cb_ref[...] + cw[0:1] * lconv_ref[0] + cw[1:2] * lconv_ref[1] + cw[2:3] * lconv_ref[2] + cw[3:4] * xa
    a, b = _lru_gates(xc, wra_ref, bra_ref, wix_ref, bix_ref, lam_ref)
    h = a * lh_ref[...] + b
    ya_ref[...] = (h * _gelu(z[:, D_LRU:])).astype(ya_ref.dtype)
    xa_ref[...] = xa
    h_ref[...] = h
    ones = _head_ones()
    p = jnp.dot(xn, wb_ref[...], preferred_element_type=F32)
    r, logw, k2, v, kk, aa, g = _rwkv_pre(p, shift_ref[...], mu_ref[...], wwa_ref, w0_ref[...],
                                           a0_ref[...], g2_ref, kk_ref[...], ka_ref[...], ones)
    decay = jnp.exp(logw)
    p_ref[...] = p
    r_ref[...] = r
    k2_ref[...] = k2
    v_ref[...] = v
    gg_ref[...] = g
    for ref, val in ((rh_ref, r), (wh_ref, decay), (kh_ref, k2), (vh_ref, v), (kkh_ref, kk), (ah_ref, aa)):
        for hd in range(RWKV_HEADS):
            ref[hd] = val[:, hd * RWKV_HEAD:(hd + 1) * RWKV_HEAD]
    pc = jnp.dot(xn, wc_ref[...], preferred_element_type=F32)
    ds = []
    for gi, win in enumerate(POOL_WINDOWS):
        sl = slice(gi * POOL_GW, (gi + 1) * POOL_GW)
        s = pc[:, sl]
        for j in range(POOL_BUF - (win - 1), POOL_BUF):
            s = s + pool_ref[j][:, sl]
        ds.append(s / float(min(PAST_LEN + 1, win)) - pc[:, sl])
    yc_ref[...] = _pool_project(jnp.concatenate(ds, axis=-1), wp_ref, psc_ref[...]).astype(yc_ref.dtype)
    pc_ref[...] = pc
    u, zv = _gmlp_uz(jnp.dot(xn, wd_ref[...], preferred_element_type=F32), lng_ref[...], lnb_ref[...])
    yd_ref[...] = (u * (wsd_ref[...] * zv + bsr_ref[...])).astype(yd_ref.dtype)
    zv_ref[...] = zv


def _s_rwkv_body(s_ref, rh_ref, wh_ref, kh_ref, vh_ref, kkh_ref, ah_ref,
                 r_ref, k2_ref, v_ref, g_ref, rk_ref, lng_ref, lnb_ref,
                 y_ref, so_ref, y_sc):
    eye = _eye64()
    bb = s_ref.shape[0]

    def body(b, carry):
        for h in range(RWKV_HEADS):
            rows = [ref[h, pl.ds(b, 1), :] for ref in (rh_ref, wh_ref, kh_ref, vh_ref, kkh_ref, ah_ref)]
            S, yrow = _rwkv_step(s_ref[b, h], *rows, eye)
            so_ref[b, h] = S
            y_sc[h, pl.ds(b, 1), :] = yrow
        return carry

    lax.fori_loop(0, bb, body, 0)
    y = jnp.concatenate([y_sc[h] for h in range(RWKV_HEADS)], axis=-1)
    yb = _rwkv_post(y, r_ref[...], k2_ref[...], v_ref[...], g_ref[...], rk_ref[...], lng_ref[...],
                    lnb_ref[...], _head_ones())
    y_ref[...] = yb.astype(y_ref.dtype)


def _s_xattn_body(x_ref, g_ref, wq_ref, k_ref, v_ref, wo_ref, o_ref, q_sc, a_sc):
    i = pl.program_id(0)
    bb = k_ref.shape[0]

    @pl.when(i == 0)
    def _():
        q_sc[...] = _dot(_rms(x_ref[...], g_ref[...]), wq_ref[...])

    hmask = (lax.broadcasted_iota(jnp.int32, (SUBLANES, D_MODEL), 1) // X_HEAD_DIM
             == lax.broadcasted_iota(jnp.int32, (SUBLANES, D_MODEL), 0))
    for j in range(bb):
        row = pl.ds(i * bb + j, 1)
        q8 = jnp.where(hmask, q_sc[row, :], 0.0).astype(BF16)
        s = lax.dot_general(q8, k_ref[j].astype(BF16), (((1,), (1,)), ((), ())),
                            preferred_element_type=F32) * (X_HEAD_DIM ** -0.5)
        o8 = jnp.dot(_softmax_rows(s).astype(BF16), v_ref[j].astype(BF16), preferred_element_type=F32)
        a_sc[row, :] = jnp.sum(jnp.where(hmask, o8, 0.0), axis=0, keepdims=True)

    @pl.when(i == pl.num_programs(0) - 1)
    def _():
        o_ref[...] = x_ref[...] + _dot(a_sc[...], wo_ref[...])


def _s_ffn_body(x_ref, g_ref, wug_ref, wuv_ref, sg_ref, sv_ref, cwg_ref, cwv_ref, cbg_ref, cbv_ref,
                wd_ref, o_ref, hg_ref, hv_ref, acc_ref):
    j = pl.program_id(0)

    @pl.when(j == 0)
    def _():
        acc_ref[...] = x_ref[...]

    xn = _rms(x_ref[...], g_ref[...]).astype(BF16)
    hg = jnp.dot(xn, wug_ref[...], preferred_element_type=F32)
    hv = jnp.dot(xn, wuv_ref[...], preferred_element_type=F32)
    cwg = cwg_ref[...]
    cwv = cwv_ref[...]
    cg = cbg_ref[...] + cwg[0:1] * sg_ref[0] + cwg[1:2] * sg_ref[1] + cwg[2:3] * hg
    cv = cbv_ref[...] + cwv[0:1] * sv_ref[0] + cwv[1:2] * sv_ref[1] + cwv[2:3] * hv
    acc_ref[...] += _dot(_gelu(cg) * cv, wd_ref[...])
    hg_ref[...] = hg
    hv_ref[...] = hv

    @pl.when(j == pl.num_programs(0) - 1)
    def _():
        o_ref[...] = acc_ref[...]


def _params(n_grid, vmem_mb):
    return pltpu.CompilerParams(dimension_semantics=("arbitrary",) * n_grid,
                                vmem_limit_bytes=vmem_mb << 20)


def _whole(shape):
    return pl.BlockSpec(tuple(shape), lambda *_: (0,) * len(shape))


def _layer(arr, l):
    shape = arr.shape[1:]
    return pl.BlockSpec((None,) + tuple(shape), lambda *_: (l,) + (0,) * len(shape))


def _sds(shape, dtype=F32):
    return jax.ShapeDtypeStruct(tuple(shape), dtype)


def _pair_blockdiag(w):
    L = w.shape[0]
    w = w.reshape(L, 4, 2, RWKV_HEAD, RWKV_HEAD)
    z = jnp.zeros_like(w[:, :, 0])
    top = jnp.concatenate([w[:, :, 0], z], axis=-1)
    bot = jnp.concatenate([z, w[:, :, 1]], axis=-1)
    return jnp.concatenate([top, bot], axis=-2)


def _vec(a):
    return a.reshape(a.shape[0], 1, -1)


def kernel(x_prompt, x_sample, state_lru_conv, state_lru_h, state_rwkv_shift, state_rwkv_S, state_pool, state_ffn_conv, cache_mem_k, cache_mem_v, mem_prompt, g_mix, w_in, lru_conv_w, lru_conv_b, lru_w_ra, lru_b_ra, lru_w_ix, lru_b_ix, lru_lambda, rwkv_mu, rwkv_w0, rwkv_w2, rwkv_a0, rwkv_a2, rwkv_g2, rwkv_k_k, rwkv_k_a, rwkv_r_k, rwkv_ln_g, rwkv_ln_b, pool_w, pool_scale, gmlp_ln_g, gmlp_ln_b, gmlp_w_s, gmlp_b_s, w_pa, w_pb, w_pc, w_pd, w_o, g_xattn, g_mem, w_xq, w_xk, w_xv, w_xo, g_ffn, w_up, ffn_conv_w, ffn_conv_b, w_down, g_final):
    B, T, D = x_prompt.shape
    SB = x_sample.shape[0]
    assert D == D_MODEL and w_in.shape == (DEPTH, D_MODEL, D_IN) and x_sample.shape[1] == 1
    assert T % TM_MIX == 0 and T % TM_RWKV == 0 and SB % S_RWKV_BB == 0 and SB % S_ATT_BB == 0

    wA = w_in[:, :, O_A:O_B].astype(BF16)
    wB = w_in[:, :, O_B:O_C].astype(BF16)
    wC = w_in[:, :, O_C:O_D].astype(BF16)
    wD = w_in[:, :, O_D:O_G].astype(BF16)
    wG = w_in[:, :, O_G:].astype(BF16)
    wra = _pair_blockdiag(lru_w_ra).astype(BF16)
    wix = _pair_blockdiag(lru_w_ix).astype(BF16)
    zer = jnp.zeros((DEPTH, R_DECAY, D_RWKV), F32)
    wwa = jnp.concatenate([jnp.concatenate([rwkv_w2, zer], axis=-1),
                           jnp.concatenate([zer, rwkv_a2], axis=-1)], axis=1).astype(BF16)
    g2 = rwkv_g2.astype(BF16)
    wpool = pool_w.astype(BF16)
    bsb = jnp.repeat(gmlp_b_s, LANES, axis=-1)
    off = PAST_LEN % CHUNK
    wsd = jnp.repeat(gmlp_w_s[:, :, off, off], LANES, axis=-1)[:, None, :]
    bsr = jnp.repeat(gmlp_b_s[:, off, :], LANES, axis=-1)[:, None, :]
    wP = jnp.stack([w_pa, w_pb, w_pc, w_pd], axis=1).astype(BF16)
    wO = w_o.astype(BF16)
    wXq, wXk, wXv, wXo = (w.astype(BF16) for w in (w_xq, w_xk, w_xv, w_xo))
    wUg = w_up[:, :, :D_FF].astype(BF16)
    wUv = w_up[:, :, D_FF:].astype(BF16)
    wDn = w_down.astype(BF16)
    cwg, cwv = ffn_conv_w[:, :, :D_FF], ffn_conv_w[:, :, D_FF:]
    cbg, cbv = _vec(ffn_conv_b[:, :D_FF]), _vec(ffn_conv_b[:, D_FF:])
    gmix, gxat, gffn, gmem = _vec(g_mix), _vec(g_xattn), _vec(g_ffn), _vec(g_mem)
    lcb, bra, bix, lam = _vec(lru_conv_b), _vec(lru_b_ra), _vec(lru_b_ix), _vec(lru_lambda)
    mu, w0, a0, kkw, kaw = _vec(rwkv_mu), _vec(rwkv_w0), _vec(rwkv_a0), _vec(rwkv_k_k), _vec(rwkv_k_a)
    rk = rwkv_r_k.reshape(DEPTH, 1, D_RWKV)
    rlg, rlb = _vec(rwkv_ln_g), _vec(rwkv_ln_b)
    psc, glg, glb = _vec(pool_scale), _vec(gmlp_ln_g), _vec(gmlp_ln_b)
    gfin = g_final.reshape(1, D_MODEL)

    kv_shape = _sds((DEPTH, B, N_MEM, D_MODEL))
    kv_spec = pl.BlockSpec((None, None, N_MEM, D_MODEL), lambda l, b: (l, b, 0, 0))
    wl_spec = pl.BlockSpec((None, D_MODEL, D_MODEL), lambda l, b: (l, 0, 0))
    p_mk, p_mv = pl.pallas_call(
        _memkv_body, grid=(DEPTH, B),
        in_specs=[pl.BlockSpec((None, N_MEM, D_MODEL), lambda l, b: (b, 0, 0)),
                  pl.BlockSpec((None, 1, D_MODEL), lambda l, b: (l, 0, 0)), wl_spec, wl_spec],
        out_specs=[kv_spec, kv_spec], out_shape=[kv_shape, kv_shape],
        compiler_params=_params(2, 32), name="memkv")(mem_prompt, gmem, wXk, wXv)

    nt = T // TM_MIX
    ntr = T // TM_RWKV
    xt_spec = pl.BlockSpec((None, TM_MIX, D_MODEL), lambda b, t: (b, t, 0))
    yt_spec = pl.BlockSpec((None, TM_MIX, D_LRU), lambda b, t: (b, t, 0))

    def tail_spec(rows, width):
        return pl.BlockSpec((None, rows, width), lambda b, t: (b, 0, 0))

    hp = x_prompt
    hs = x_sample.reshape(SB, D_MODEL)
    outs = {k: [] for k in ("p_conv", "s_conv", "p_h", "s_h", "p_shift", "s_shift", "p_S", "s_S",
                            "p_pool", "s_pool", "p_ffn", "s_ffn", "s_cv")}

    for l in range(DEPTH):
        L = functools.partial(_layer, l=l)

        yA, p_conv8, p_h8 = pl.pallas_call(
            _p_lru_body, grid=(B, nt),
            in_specs=[xt_spec, L(gmix), L(wA), L(lru_conv_w), L(lcb), L(wra), L(bra), L(wix), L(bix), L(lam)],
            out_specs=[yt_spec, tail_spec(SUBLANES, D_LRU), tail_spec(SUBLANES, D_LRU)],
            out_shape=[_sds((B, T, D_LRU), BF16), _sds((B, SUBLANES, D_LRU)), _sds((B, SUBLANES, D_LRU))],
            scratch_shapes=[pltpu.VMEM((SUBLANES, D_LRU), F32), pltpu.VMEM((SUBLANES, D_LRU), F32)],
            compiler_params=_params(2, 48), name="p_lru")(
                hp, gmix, wA, lru_conv_w, lcb, wra, bra, wix, bix, lam)

        head_sc = pltpu.VMEM((RWKV_HEADS, TM_RWKV, RWKV_HEAD), F32)
        chunk_sc = pltpu.VMEM((TM_RWKV // RWKV_CHUNK, RWKV_HEADS, RWKV_CHUNK, RWKV_HEAD), F32)
        yB, p_shift8, p_S = pl.pallas_call(
            _p_rwkv_body, grid=(B, ntr),
            in_specs=[pl.BlockSpec((None, TM_RWKV, D_MODEL), lambda b, t: (b, t, 0)),
                      L(gmix), L(wB), L(mu), L(wwa), L(w0), L(a0), L(g2), L(kkw), L(kaw), L(rk), L(rlg), L(rlb)],
            out_specs=[pl.BlockSpec((None, TM_RWKV, D_RWKV), lambda b, t: (b, t, 0)),
                       tail_spec(SUBLANES, D_RWKV_IN),
                       pl.BlockSpec((None, RWKV_HEADS, RWKV_HEAD, RWKV_HEAD), lambda b, t: (b, 0, 0, 0))],
            out_shape=[_sds((B, T, D_RWKV), BF16), _sds((B, SUBLANES, D_RWKV_IN)),
                       _sds((B, RWKV_HEADS, RWKV_HEAD, RWKV_HEAD))],
            scratch_shapes=[pltpu.VMEM((SUBLANES, D_RWKV_IN), F32),
                            pltpu.VMEM((RWKV_HEADS, RWKV_HEAD, RWKV_HEAD), F32)]
                           + [head_sc] * 9 + [chunk_sc] * 5,
            compiler_params=_params(2, 48), name="p_rwkv")(
                hp, gmix, wB, mu, wwa, w0, a0, g2, kkw, kaw, rk, rlg, rlb)

        yC, p_pool16 = pl.pallas_call(
            _p_pool_body, grid=(B, nt),
            in_specs=[xt_spec, L(gmix), L(wC), L(wpool), L(psc)],
            out_specs=[yt_spec, tail_spec(POOL_HIST, D_POOL)],
            out_shape=[_sds((B, T, D_POOL), BF16), _sds((B, POOL_HIST, D_POOL))],
            scratch_shapes=[pltpu.VMEM((POOL_HIST, D_POOL), F32)],
            compiler_params=_params(2, 48), name="p_pool")(hp, gmix, wC, wpool, psc)

        yD = pl.pallas_call(
            _p_gmlp_body, grid=(B, nt),
            in_specs=[xt_spec, L(gmix), L(wD), L(glg), L(glb), L(gmlp_w_s), L(bsb)],
            out_specs=yt_spec, out_shape=_sds((B, T, D_GMLP), BF16),
            compiler_params=_params(2, 48), name="p_gmlp")(hp, gmix, wD, glg, glb, gmlp_w_s, bsb)

        def merge(x2d, ys, tm):
            rows = x2d.shape[0]
            xs = pl.BlockSpec((tm, D_MODEL), lambda i: (i, 0))
            ysp = pl.BlockSpec((tm, D_LRU), lambda i: (i, 0))
            return pl.pallas_call(
                _merge_body, grid=(rows // tm,),
                in_specs=[xs, L(gmix), L(wG), ysp, ysp, ysp, ysp, L(wP), L(wO)],
                out_specs=xs, out_shape=_sds((rows, D_MODEL)),
                compiler_params=_params(1, 56), name="merge")(x2d, gmix, wG, *ys, wP, wO)

        hp = merge(hp.reshape(B * T, D_MODEL), [y.reshape(B * T, -1) for y in (yA, yB, yC, yD)],
                   TM_MIX).reshape(B, T, D_MODEL)

        kvb_spec = pl.BlockSpec((None, None, N_MEM, D_MODEL), lambda b, t: (l, b, 0, 0))
        hp = pl.pallas_call(
            _p_xattn_body, grid=(B, nt),
            in_specs=[xt_spec, L(gxat), L(wXq), kvb_spec, kvb_spec, L(wXo)],
            out_specs=xt_spec, out_shape=_sds((B, T, D_MODEL)),
            compiler_params=_params(2, 48), name="p_xattn")(hp, gxat, wXq, p_mk, p_mv, wXo)

        hp, p_ffn8 = pl.pallas_call(
            _p_ffn_body, grid=(B, nt),
            in_specs=[xt_spec, L(gffn), L(wUg), L(wUv), L(cwg), L(cwv), L(cbg), L(cbv), L(wDn)],
            out_specs=[xt_spec, tail_spec(SUBLANES, 2 * D_FF)],
            out_shape=[_sds((B, T, D_MODEL)), _sds((B, SUBLANES, 2 * D_FF))],
            scratch_shapes=[pltpu.VMEM((SUBLANES, 2 * D_FF), F32)],
            compiler_params=_params(2, 56), name="p_ffn")(hp, gffn, wUg, wUv, cwg, cwv, cbg, cbv, wDn)

        outs["p_conv"].append(p_conv8[:, SUBLANES - 3:])
        outs["p_h"].append(p_h8[:, SUBLANES - 1])
        outs["p_shift"].append(p_shift8[:, SUBLANES - 1:])
        outs["p_S"].append(p_S)
        outs["p_pool"].append(p_pool16[:, POOL_HIST - POOL_BUF:])
        outs["p_ffn"].append(p_ffn8[:, SUBLANES - 2:])

        lconv = jnp.swapaxes(state_lru_conv[l], 0, 1)
        spool = jnp.swapaxes(state_pool[l], 0, 1)
        sffn = jnp.swapaxes(state_ffn_conv[l], 0, 1)
        shift = state_rwkv_shift[l].reshape(SB, D_RWKV_IN)
        row512 = _sds((SB, D_LRU))
        heads = _sds((RWKV_HEADS, SB, RWKV_HEAD))
        mix_in = [hs, gmix[l], wA[l], wB[l], wC[l], wD[l],
                  lconv, state_lru_h[l], lru_conv_w[l], lcb[l], wra[l], bra[l], wix[l], bix[l], lam[l],
                  shift, mu[l], wwa[l], w0[l], a0[l], g2[l], kkw[l], kaw[l],
                  spool, wpool[l], psc[l],
                  glg[l], glb[l], wsd[l], bsr[l]]
        mix_out = [_sds((SB, D_LRU), BF16), row512, row512, _sds((SB, D_POOL), BF16), row512,
                   _sds((SB, D_GMLP), BF16), row512, _sds((SB, D_RWKV_IN)),
                   row512, row512, row512, row512, heads, heads, heads, heads, heads, heads]
        (yA, s_xa, s_h, yC, s_pc, yD, s_zv, s_p, s_r, s_k2, s_v, s_g,
         rh, wh, kh, vh, kkh, ah) = pl.pallas_call(
            _s_mix_body, grid=(1,),
            in_specs=[_whole(a.shape) for a in mix_in],
            out_specs=[_whole(o.shape) for o in mix_out], out_shape=mix_out,
            compiler_params=_params(1, 56), name="s_mix")(*mix_in)

        hd_spec = pl.BlockSpec((RWKV_HEADS, S_RWKV_BB, RWKV_HEAD), lambda i: (0, i, 0))
        rb_spec = pl.BlockSpec((S_RWKV_BB, D_RWKV), lambda i: (i, 0))
        st_spec = pl.BlockSpec((None, S_RWKV_BB, RWKV_HEADS, RWKV_HEAD, RWKV_HEAD), lambda i: (l, i, 0, 0, 0))
        so_spec = pl.BlockSpec((S_RWKV_BB, RWKV_HEADS, RWKV_HEAD, RWKV_HEAD), lambda i: (i, 0, 0, 0))
        yB, s_S = pl.pallas_call(
            _s_rwkv_body, grid=(SB // S_RWKV_BB,),
            in_specs=[st_spec] + [hd_spec] * 6 + [rb_spec] * 4 + [L(rk), L(rlg), L(rlb)],
            out_specs=[rb_spec, so_spec],
            out_shape=[_sds((SB, D_RWKV), BF16), _sds((SB, RWKV_HEADS, RWKV_HEAD, RWKV_HEAD))],
            scratch_shapes=[pltpu.VMEM((RWKV_HEADS, S_RWKV_BB, RWKV_HEAD), F32)],
            compiler_params=_params(1, 48), name="s_rwkv")(
                state_rwkv_S, rh, wh, kh, vh, kkh, ah, s_r, s_k2, s_v, s_g, rk, rlg, rlb)

        hs = merge(hs, [yA, yB, yC, yD], SB)

        kc_spec = pl.BlockSpec((None, S_ATT_BB, N_MEM, D_MODEL), lambda i: (l, i, 0, 0))
        xs_spec = _whole((SB, D_MODEL))
        hs = pl.pallas_call(
            _s_xattn_body, grid=(SB // S_ATT_BB,),
            in_specs=[xs_spec, L(gxat), L(wXq), kc_spec, kc_spec, L(wXo)],
            out_specs=xs_spec, out_shape=_sds((SB, D_MODEL)),
            scratch_shapes=[pltpu.VMEM((SB, D_MODEL), F32), pltpu.VMEM((SB, D_MODEL), F32)],
            compiler_params=_params(1, 56), name="s_xattn")(
                hs, gxat, wXq, cache_mem_k.reshape(DEPTH, SB, N_MEM, D_MODEL),
                cache_mem_v.reshape(DEPTH, SB, N_MEM, D_MODEL), wXo)

        ncf = D_FF // FF_CW
        wu_spec = pl.BlockSpec((None, D_MODEL, FF_CW), lambda j: (l, 0, j))
        sg_spec = pl.BlockSpec((2, SB, FF_CW), lambda j: (0, 0, j))
        sv_spec = pl.BlockSpec((2, SB, FF_CW), lambda j: (0, 0, j + ncf))
        cw_spec = pl.BlockSpec((None, 3, FF_CW), lambda j: (l, 0, j))
        cb_spec = pl.BlockSpec((None, 1, FF_CW), lambda j: (l, 0, j))
        hh_spec = pl.BlockSpec((SB, FF_CW), lambda j: (0, j))
        hs, s_hg, s_hv = pl.pallas_call(
            _s_ffn_body, grid=(ncf,),
            in_specs=[xs_spec, L(gffn), wu_spec, wu_spec, sg_spec, sv_spec, cw_spec, cw_spec, cb_spec, cb_spec,
                      pl.BlockSpec((None, FF_CW, D_MODEL), lambda j: (l, j, 0))],
            out_specs=[xs_spec, hh_spec, hh_spec],
            out_shape=[_sds((SB, D_MODEL)), _sds((SB, D_FF)), _sds((SB, D_FF))],
            scratch_shapes=[pltpu.VMEM((SB, D_MODEL), F32)],
            compiler_params=_params(1, 48), name="s_ffn")(
                hs, gffn, wUg, wUv, sffn, sffn, cwg, cwv, cbg, cbv, wDn)

        outs["s_conv"].append(jnp.concatenate([state_lru_conv[l][:, 1:], s_xa[:, None]], axis=1))
        outs["s_h"].append(s_h)
        outs["s_shift"].append(s_p[:, None])
        outs["s_S"].append(s_S)
        outs["s_pool"].append(jnp.concatenate([state_pool[l][:, 1:], s_pc[:, None]], axis=1))
        outs["s_ffn"].append(jnp.concatenate(
            [state_ffn_conv[l][:, 1:], jnp.concatenate([s_hg, s_hv], axis=-1)[:, None]], axis=1))
        outs["s_cv"].append(s_zv[:, None])

    def final_norm(x2d, tm):
        rows = x2d.shape[0]
        spec = pl.BlockSpec((tm, D_MODEL), lambda i: (i, 0))
        return pl.pallas_call(_final_norm_body, grid=(rows // tm,),
                              in_specs=[spec, _whole((1, D_MODEL))], out_specs=spec,
                              out_shape=_sds((rows, D_MODEL)),
                              compiler_params=_params(1, 32), name="final_norm")(x2d, gfin)

    y_prompt = final_norm(hp.reshape(B * T, D_MODEL), 1024).reshape(B, T, D_MODEL)
    y_sample = final_norm(hs, SB).reshape(SB, 1, D_MODEL)
    stk = {k: jnp.stack(v, axis=0) for k, v in outs.items()}
    mem_shape = (DEPTH, B, N_MEM, X_HEADS, X_HEAD_DIM)
    return (y_prompt, y_sample,
            stk["p_conv"], stk["s_conv"], stk["p_h"], stk["s_h"],
            stk["p_shift"], stk["s_shift"], stk["p_S"], stk["s_S"],
            stk["p_pool"], stk["s_pool"], stk["p_ffn"], stk["s_ffn"],
            p_mk.reshape(mem_shape), p_mv.reshape(mem_shape), stk["s_cv"])
```

```python
import functools
import math

import jax
import jax.numpy as jnp
from jax import lax
from jax.experimental import pallas as pl
from jax.experimental.pallas import tpu as pltpu

F32 = jnp.float32
BF16 = jnp.bfloat16

SUBLANES = 8
LANES = 128

D_MODEL = 1024
DEPTH = 4
N_MEM = 256
D_LRU = 512
LRU_C = 8.0
D_RWKV = 512
RWKV_HEAD = 64
RWKV_HEADS = D_RWKV // RWKV_HEAD
R_DECAY = 64
R_AAA = 64
R_GATE = 128
D_RWKV_IN = 3 * D_RWKV + R_DECAY + R_AAA + R_GATE
GN_EPS = 64e-5
D_POOL = 512
POOL_WINDOWS = (2, 4, 8, 16)
POOL_GW = D_POOL // len(POOL_WINDOWS)
POOL_BUF = max(POOL_WINDOWS) - 1
POOL_HIST = 16
D_GMLP = 512
GMLP_GROUPS = 4
CHUNK = 128
N_BRANCH = 4
X_HEADS = 4
X_HEAD_DIM = D_MODEL // X_HEADS
D_FF = 3 * D_MODEL
EPS = 1e-6
PAST_LEN = 16384

O_A = 0
O_B = 2 * D_LRU
O_C = O_B + D_RWKV_IN
O_D = O_C + D_POOL
O_G = O_D + 2 * D_GMLP
D_IN = O_G + N_BRANCH * D_MODEL

TM_MIX = 512
TM_RWKV = 256
FF_CW = 768
RWKV_CHUNK = 64
S_RWKV_BB = 16
S_ATT_BB = 8


def _dot(a, b):
    return jnp.dot(a.astype(BF16), b.astype(BF16), preferred_element_type=F32)


def _rms(x, g):
    return x * lax.rsqrt(jnp.mean(x * x, axis=-1, keepdims=True) + EPS) * g


def _gelu(x):
    c = math.sqrt(2.0 / math.pi)
    return 0.5 * x * (1.0 + jnp.tanh(c * (x + 0.044715 * (x * x * x))))


def _softplus(x):
    return jnp.maximum(x, 0.0) + jnp.log1p(jnp.exp(-jnp.abs(x)))


def _sigmoid(x):
    return jax.nn.sigmoid(x)


def _shift_rows(hist, cur, s):
    n = hist.shape[0]
    return pltpu.roll(jnp.concatenate([hist, cur], axis=0), s, 0)[n:]


def _head_ones():
    r = lax.broadcasted_iota(jnp.int32, (LANES, LANES), 0) // RWKV_HEAD
    c = lax.broadcasted_iota(jnp.int32, (LANES, LANES), 1) // RWKV_HEAD
    return jnp.where(r == c, 1.0, 0.0).astype(BF16)


def _segsum(x, ones):
    outs = []
    for p in range(x.shape[1] // LANES):
        xb = x[:, p * LANES:(p + 1) * LANES]
        hi = xb.astype(BF16)
        lo = (xb - hi.astype(F32)).astype(BF16)
        outs.append(jnp.dot(hi, ones, preferred_element_type=F32)
                    + jnp.dot(lo, ones, preferred_element_type=F32))
    return jnp.concatenate(outs, axis=-1)


def _blockdiag_dot(x, w_ref):
    outs = []
    for p in range(x.shape[1] // LANES):
        outs.append(jnp.dot(x[:, p * LANES:(p + 1) * LANES].astype(BF16), w_ref[p],
                            preferred_element_type=F32))
    return jnp.concatenate(outs, axis=-1)


def _lru_gates(xc, wra_ref, bra_ref, wix_ref, bix_ref, lam_ref):
    r = _sigmoid(_blockdiag_dot(xc, wra_ref) + bra_ref[...])
    i = _sigmoid(_blockdiag_dot(xc, wix_ref) + bix_ref[...])
    log_a = -LRU_C * r * _softplus(-lam_ref[...])
    a = jnp.exp(log_a)
    mult = jnp.sqrt(-jnp.tanh(log_a) * (1.0 + a * a))
    return a, mult * (i * xc)


def _rwkv_pre(p, prev, mu, wwa_ref, w0, a0, g2_ref, k_k, k_a, ones):
    px = p + (prev - p) * mu
    r = px[:, 0:D_RWKV]
    k = px[:, D_RWKV:2 * D_RWKV]
    v = px[:, 2 * D_RWKV:3 * D_RWKV]
    lo = px[:, 3 * D_RWKV:3 * D_RWKV + R_DECAY + R_AAA]
    g_lo = px[:, 3 * D_RWKV + R_DECAY + R_AAA:]
    lane = lax.broadcasted_iota(jnp.int32, lo.shape, 1)
    wa = _dot(jnp.where(lane < R_DECAY, jnp.tanh(lo), lo), wwa_ref[...])
    logw = -jnp.exp(-_softplus(-(w0 + wa[:, :D_RWKV])) - 0.5)
    a = _sigmoid(a0 + wa[:, D_RWKV:])
    g = _dot(_sigmoid(g_lo), g2_ref[...])
    kk = k * k_k
    kk = kk * lax.rsqrt(jnp.maximum(_segsum(kk * kk, ones), 1e-24))
    k2 = k * (1.0 + (a - 1.0) * k_a)
    return r, logw, k2, v, kk, a, g


def _rwkv_step(S, r, w, k, v, kk, a, eye):
    sa = -jnp.sum(S * kk, axis=1, keepdims=True)
    vcol = jnp.sum(eye * v, axis=1, keepdims=True)
    S = S * w + sa * (kk * a) + vcol * k
    ycol = jnp.sum(S * r, axis=1, keepdims=True)
    yrow = jnp.sum(eye * ycol, axis=0, keepdims=True)
    return S, yrow


def _rwkv_post(y, r, k2, v, g, r_k, ln_g, ln_b, ones):
    inv = 1.0 / RWKV_HEAD
    mean = _segsum(y, ones) * inv
    yc = y - mean
    var = _segsum(yc * yc, ones) * inv
    yn = yc * lax.rsqrt(var + GN_EPS) * ln_g + ln_b
    bonus = _segsum(r * k2 * r_k, ones) * v
    return (yn + bonus) * g


def _eye64():
    r = lax.broadcasted_iota(jnp.int32, (RWKV_HEAD, RWKV_HEAD), 0)
    c = lax.broadcasted_iota(jnp.int32, (RWKV_HEAD, RWKV_HEAD), 1)
    return jnp.where(r == c, 1.0, 0.0).astype(F32)


def _pool_project(d, wp_ref, scale):
    return _blockdiag_dot(d, wp_ref) * scale


def _gmlp_uz(z2, ln_g, ln_b):
    u = _gelu(z2[:, :D_GMLP])
    v = _gelu(z2[:, D_GMLP:])
    mu = jnp.mean(v, axis=-1, keepdims=True)
    vc = v - mu
    var = jnp.mean(vc * vc, axis=-1, keepdims=True)
    return u, vc * lax.rsqrt(var + 1e-5) * ln_g + ln_b


def _p_lru_body(x_ref, g_ref, w_ref, cw_ref, cb_ref, wra_ref, bra_ref, wix_ref, bix_ref, lam_ref,
                y_ref, convo_ref, ho_ref, hist_ref, h_ref):
    @pl.when(pl.program_id(1) == 0)
    def _():
        hist_ref[...] = jnp.zeros_like(hist_ref)
        h_ref[...] = jnp.zeros_like(h_ref)

    z = _dot(_rms(x_ref[...], g_ref[...]), w_ref[...])
    xa = z[:, :D_LRU]
    ga = z[:, D_LRU:]
    tm = xa.shape[0]
    hist = hist_ref[...]
    cw = cw_ref[...]
    xc = cb_ref[...] + cw[3:4] * xa
    for s in (1, 2, 3):
        xc = xc + cw[3 - s:4 - s] * _shift_rows(hist, xa, s)
    a, b = _lru_gates(xc, wra_ref, bra_ref, wix_ref, bix_ref, lam_ref)
    row = lax.broadcasted_iota(jnp.int32, a.shape, 0)
    s = 1
    while s < tm:
        m = row >= s
        b = jnp.where(m, a * pltpu.roll(b, s, 0) + b, b)
        a = jnp.where(m, a * pltpu.roll(a, s, 0), a)
        s *= 2
    h = a * h_ref[SUBLANES - 1:SUBLANES, :] + b
    y_ref[...] = (h * _gelu(ga)).astype(y_ref.dtype)
    hist_ref[...] = xa[tm - SUBLANES:]
    h_ref[...] = h[tm - SUBLANES:]
    convo_ref[...] = xa[tm - SUBLANES:]
    ho_ref[...] = h[tm - SUBLANES:]


def _p_rwkv_body(x_ref, g_ref, w_ref, mu_ref, wwa_ref, w0_ref, a0_ref, g2_ref, kk_ref, ka_ref,
                 rk_ref, lng_ref, lnb_ref,
                 y_ref, shifto_ref, so_ref,
                 hist_ref, st_ref, at_sc, rt_sc, bt_sc, kt_sc, v_sc, bh_sc, kh_sc, ge_sc, y_sc,
                 g_sc, q_sc, ry_sc, yc_sc, gd_sc):
    @pl.when(pl.program_id(1) == 0)
    def _():
        hist_ref[...] = jnp.zeros_like(hist_ref)
        st_ref[...] = jnp.zeros_like(st_ref)

    C = RWKV_CHUNK
    ones = _head_ones()
    p = _dot(_rms(x_ref[...], g_ref[...]), w_ref[...])
    tm = p.shape[0]
    nc = tm // C
    prev = _shift_rows(hist_ref[...], p, 1)
    r, logw, k2, v, kk, a, g = _rwkv_pre(p, prev, mu_ref[...], wwa_ref, w0_ref[...], a0_ref[...],
                                          g2_ref, kk_ref[...], ka_ref[...], ones)
    hist_ref[...] = p[tm - SUBLANES:]
    shifto_ref[...] = p[tm - SUBLANES:]
    rowc = lax.broadcasted_iota(jnp.int32, logw.shape, 0) % C
    lg = logw
    s = 1
    while s < C:
        lg = lg + jnp.where(rowc >= s, pltpu.roll(lg, s, 0), 0.0)
        s *= 2
    lg_end = jnp.concatenate(
        [jnp.broadcast_to(lg[(c + 1) * C - 1:(c + 1) * C, :], (C, D_RWKV)) for c in range(nc)], axis=0)
    inv_gam = jnp.exp(-lg)
    to_end = jnp.exp(lg_end - lg)
    b = kk * a
    for sc, val in ((at_sc, -kk * jnp.exp(lg - logw)), (rt_sc, r * jnp.exp(lg)), (bt_sc, b * inv_gam),
                    (kt_sc, k2 * inv_gam), (v_sc, v), (bh_sc, b * to_end), (kh_sc, k2 * to_end),
                    (ge_sc, jnp.exp(lg_end))):
        for h in range(RWKV_HEADS):
            sc[h] = val[:, h * RWKV_HEAD:(h + 1) * RWKV_HEAD]

    eye = _eye64()
    ri = lax.broadcasted_iota(jnp.int32, (C, C), 0)
    ci = lax.broadcasted_iota(jnp.int32, (C, C), 1)
    r2 = lax.broadcasted_iota(jnp.int32, (C, 2 * C), 0)
    c2 = lax.broadcasted_iota(jnp.int32, (C, 2 * C), 1) % C
    strict2 = c2 < r2
    incl2 = c2 <= r2
    pair = (ri // 2) == (ci // 2)
    levels = []
    nb = 2
    while nb < C:
        levels.append(((ri // (2 * nb)) == (ci // (2 * nb))) & ((ri // nb) != (ci // nb)))
        nb *= 2
    zeros_c = jnp.zeros((C, RWKV_HEAD), BF16)
    tn = (((0,), (0,)), ((), ()))
    nt_ = (((1,), (1,)), ((), ()))

    def mm(x, y):
        return jnp.dot(x, y, preferred_element_type=F32)

    H = range(RWKV_HEADS)

    def phase1(c, carry):
        rows = pl.ds(pl.multiple_of(c * C, C), C)
        At, Rt, Bt, Kt, V, Bh, Kh = ([sc[h, rows, :] for h in H] for sc in
                                     (at_sc, rt_sc, bt_sc, kt_sc, v_sc, bh_sc, kh_sc))
        atb = [x.astype(BF16) for x in At]
        vb = [x.astype(BF16) for x in V]
        prod = [lax.dot_general(jnp.concatenate([atb[h], Rt[h].astype(BF16)], axis=0),
                                jnp.concatenate([Bt[h], Kt[h]], axis=0).astype(BF16), nt_,
                                preferred_element_type=F32) for h in H]
        top = [jnp.where(strict2, x[:C], 0.0) for x in prod]
        bot = [jnp.where(incl2, x[C:], 0.0).astype(BF16) for x in prod]
        lab = [x[:, :C] for x in top]
        lak_v = [mm(top[h].astype(BF16), jnp.concatenate([zeros_c, vb[h]], axis=0)) for h in H]
        T = [eye + jnp.where(pair, x, 0.0) for x in lab]
        for mk in levels:
            tb = [x.astype(BF16) for x in T]
            u = [mm(tb[h], jnp.where(mk, lab[h], 0.0).astype(BF16)).astype(BF16) for h in H]
            T = [T[h] + mm(u[h], tb[h]) for h in H]
        tb = [x.astype(BF16) for x in T]
        tab = [mm(tb[h], atb[h]).astype(BF16) for h in H]
        cv = [jnp.concatenate([mm(tb[h], lak_v[h].astype(BF16)).astype(BF16), vb[h]], axis=0) for h in H]
        bkh = [jnp.concatenate([Bh[h], Kh[h]], axis=0).astype(BF16) for h in H]
        for h in H:
            g_sc[c, h] = lax.dot_general(bkh[h][:C], tab[h], tn, preferred_element_type=F32)
        for h in H:
            q_sc[c, h] = lax.dot_general(bkh[h], cv[h], tn, preferred_element_type=F32)
        for h in H:
            ry_sc[c, h] = Rt[h] + mm(bot[h][:, :C], tab[h])
        for h in H:
            yc_sc[c, h] = mm(bot[h], cv[h])
        for h in H:
            gcol = jnp.sum(eye * ge_sc[h, pl.ds(pl.multiple_of(c * C, C), 1), :], axis=1, keepdims=True)
            gd_sc[c, h] = jnp.broadcast_to(gcol, (C, RWKV_HEAD))
        return carry

    lax.fori_loop(0, nc, phase1, 0)

    def phase2(c, carry):
        rows = pl.ds(pl.multiple_of(c * C, C), C)
        st = [st_ref[h] for h in H]
        sb = [x.astype(BF16) for x in st]
        gs = [mm(g_sc[c, h].astype(BF16), sb[h]) for h in H]
        ys = [mm(ry_sc[c, h].astype(BF16), sb[h]) for h in H]
        for h in H:
            st_ref[h] = st[h] * gd_sc[c, h] + gs[h] + q_sc[c, h]
            y_sc[h, rows, :] = ys[h] + yc_sc[c, h]
        return carry

    lax.fori_loop(0, nc, phase2, 0)
    y = jnp.concatenate([y_sc[h] for h in range(RWKV_HEADS)], axis=-1)
    yb = _rwkv_post(y, r, k2, v, g, rk_ref[...], lng_ref[...], lnb_ref[...], ones)
    y_ref[...] = yb.astype(y_ref.dtype)
    for h in range(RWKV_HEADS):
        so_ref[h] = st_ref[h].T


def _p_pool_body(x_ref, g_ref, w_ref, wp_ref, sc_ref, y_ref, poolo_ref, hist_ref):
    t = pl.program_id(1)

    @pl.when(t == 0)
    def _():
        hist_ref[...] = jnp.zeros_like(hist_ref)

    pc = _dot(_rms(x_ref[...], g_ref[...]), w_ref[...])
    tm = pc.shape[0]
    X = jnp.concatenate([hist_ref[...], pc], axis=0)
    pos = t * tm + lax.broadcasted_iota(jnp.int32, (tm, POOL_GW), 0)
    ds = []
    for gi, win in enumerate(POOL_WINDOWS):
        sl = slice(gi * POOL_GW, (gi + 1) * POOL_GW)
        s = X[:, sl]
        sh = 1
        while sh < win:
            s = s + pltpu.roll(s, sh, 0)
            sh *= 2
        cnt = jnp.minimum(pos + 1, win).astype(F32)
        ds.append(s[POOL_HIST:] / cnt - pc[:, sl])
    y = _pool_project(jnp.concatenate(ds, axis=-1), wp_ref, sc_ref[...])
    y_ref[...] = y.astype(y_ref.dtype)
    hist_ref[...] = pc[tm - POOL_HIST:]
    poolo_ref[...] = pc[tm - POOL_HIST:]


def _p_gmlp_body(x_ref, g_ref, w_ref, lng_ref, lnb_ref, ws_ref, bsb_ref, y_ref):
    z2 = _dot(_rms(x_ref[...], g_ref[...]), w_ref[...])
    tm = z2.shape[0]
    u, zv = _gmlp_uz(z2, lng_ref[...], lnb_ref[...])
    tril = (lax.broadcasted_iota(jnp.int32, (CHUNK, CHUNK), 0)
            >= lax.broadcasted_iota(jnp.int32, (CHUNK, CHUNK), 1))
    ws = [jnp.where(tril, ws_ref[gi], 0.0).astype(BF16) for gi in range(GMLP_GROUPS)]
    zb = zv.astype(BF16)
    rows = []
    for c in range(tm // CHUNK):
        cols = [jnp.dot(ws[gi], zb[c * CHUNK:(c + 1) * CHUNK, gi * LANES:(gi + 1) * LANES],
                        preferred_element_type=F32) for gi in range(GMLP_GROUPS)]
        rows.append(jnp.concatenate(cols, axis=-1) + bsb_ref[...])
    s = jnp.concatenate(rows, axis=0)
    y_ref[...] = (u * s).astype(y_ref.dtype)


def _merge_body(x_ref, g_ref, wg_ref, ya_ref, yb_ref, yc_ref, yd_ref, wp_ref, wo_ref, o_ref):
    x = x_ref[...]
    xn = _rms(x, g_ref[...]).astype(BF16)
    merged = None
    for i, y_ref in enumerate((ya_ref, yb_ref, yc_ref, yd_ref)):
        gate = _sigmoid(jnp.dot(xn, wg_ref[:, i * D_MODEL:(i + 1) * D_MODEL],
                                preferred_element_type=F32))
        term = gate * jnp.dot(y_ref[...], wp_ref[i], preferred_element_type=F32)
        merged = term if merged is None else merged + term
    o_ref[...] = x + _dot(merged, wo_ref[...])


def _softmax_rows(s):
    e = jnp.exp(s - jnp.max(s, axis=-1, keepdims=True))
    return e / jnp.sum(e, axis=-1, keepdims=True)


def _p_xattn_body(x_ref, g_ref, wq_ref, k_ref, v_ref, wo_ref, o_ref):
    x = x_ref[...]
    q = _dot(_rms(x, g_ref[...]), wq_ref[...]).astype(BF16)
    kb = k_ref[...].astype(BF16)
    vb = v_ref[...].astype(BF16)
    outs = []
    for h in range(X_HEADS):
        sl = slice(h * X_HEAD_DIM, (h + 1) * X_HEAD_DIM)
        s = lax.dot_general(q[:, sl], kb[:, sl], (((1,), (1,)), ((), ())),
                            preferred_element_type=F32) * (X_HEAD_DIM ** -0.5)
        outs.append(jnp.dot(_softmax_rows(s).astype(BF16), vb[:, sl], preferred_element_type=F32))
    o_ref[...] = x + _dot(jnp.concatenate(outs, axis=-1), wo_ref[...])


def _ffn_conv(hist, h, cw, cb):
    return cb + cw[2:3] * h + cw[1:2] * _shift_rows(hist, h, 1) + cw[0:1] * _shift_rows(hist, h, 2)


def _p_ffn_body(x_ref, g_ref, wug_ref, wuv_ref, cwg_ref, cwv_ref, cbg_ref, cbv_ref, wd_ref,
                o_ref, ffo_ref, hist_ref):
    @pl.when(pl.program_id(1) == 0)
    def _():
        hist_ref[...] = jnp.zeros_like(hist_ref)

    x = x_ref[...]
    tm = x.shape[0]
    xn = _rms(x, g_ref[...]).astype(BF16)
    acc = x
    for j in range(D_FF // FF_CW):
        cs = slice(j * FF_CW, (j + 1) * FF_CW)
        vs = slice(D_FF + j * FF_CW, D_FF + (j + 1) * FF_CW)
        hg = jnp.dot(xn, wug_ref[:, cs], preferred_element_type=F32)
        hv = jnp.dot(xn, wuv_ref[:, cs], preferred_element_type=F32)
        cg = _ffn_conv(hist_ref[:, cs], hg, cwg_ref[:, cs], cbg_ref[:, cs])
        cv = _ffn_conv(hist_ref[:, vs], hv, cwv_ref[:, cs], cbv_ref[:, cs])
        acc = acc + _dot(_gelu(cg) * cv, wd_ref[cs, :])
        hist_ref[:, cs] = hg[tm - SUBLANES:]
        hist_ref[:, vs] = hv[tm - SUBLANES:]
    o_ref[...] = acc
    ffo_ref[...] = hist_ref[...]


def _memkv_body(m_ref, g_ref, wk_ref, wv_ref, k_ref, v_ref):
    mn = _rms(m_ref[...], g_ref[...]).astype(BF16)
    k_ref[...] = jnp.dot(mn, wk_ref[...], preferred_element_type=F32)
    v_ref[...] = jnp.dot(mn, wv_ref[...], preferred_element_type=F32)


def _final_norm_body(x_ref, g_ref, o_ref):
    o_ref[...] = _rms(x_ref[...], g_ref[...])


def _s_mix_body(x_ref, g_ref, wa_ref, wb_ref, wc_ref, wd_ref,
                lconv_ref, lh_ref, cw_ref, cb_ref, wra_ref, bra_ref, wix_ref, bix_ref, lam_ref,
                shift_ref, mu_ref, wwa_ref, w0_ref, a0_ref, g2_ref, kk_ref, ka_ref,
                pool_ref, wp_ref, psc_ref,
                lng_ref, lnb_ref, wsd_ref, bsr_ref,
                ya_ref, xa_ref, h_ref, yc_ref, pc_ref, yd_ref, zv_ref, p_ref,
                r_ref, k2_ref, v_ref, gg_ref, rh_ref, wh_ref, kh_ref, vh_ref, kkh_ref, ah_ref):
    xn = _rms(x_ref[...], g_ref[...]).astype(BF16)
    z = jnp.dot(xn, wa_ref[...], preferred_element_type=F32)
    xa = z[:, :D_LRU]
    cw = cw_ref[...]
    xc = cb_ref[...] + cw[0:1] * lconv_ref[0] + cw[1:2] * lconv_ref[1] + cw[2:3] * lconv_ref[2] + cw[3:4] * xa
    a, b = _lru_gates(xc, wra_ref, bra_ref, wix_ref, bix_ref, lam_ref)
    h = a * lh_ref[...] + b
    ya_ref[...] = (h * _gelu(z[:, D_LRU:])).astype(ya_ref.dtype)
    xa_ref[...] = xa
    h_ref[...] = h
    ones = _head_ones()
    p = jnp.dot(xn, wb_ref[...], preferred_element_type=F32)
    r, logw, k2, v, kk, aa, g = _rwkv_pre(p, shift_ref[...], mu_ref[...], wwa_ref, w0_ref[...],
                                           a0_ref[...], g2_ref, kk_ref[...], ka_ref[...], ones)
    decay = jnp.exp(logw)
    p_ref[...] = p
    r_ref[...] = r
    k2_ref[...] = k2
    v_ref[...] = v
    gg_ref[...] = g
    for ref, val in ((rh_ref, r), (wh_ref, decay), (kh_ref, k2), (vh_ref, v), (kkh_ref, kk), (ah_ref, aa)):
        for hd in range(RWKV_HEADS):
            ref[hd] = val[:, hd * RWKV_HEAD:(hd + 1) * RWKV_HEAD]
    pc = jnp.dot(xn, wc_ref[...], preferred_element_type=F32)
    ds = []
    for gi, win in enumerate(POOL_WINDOWS):
        sl = slice(gi * POOL_GW, (gi + 1) * POOL_GW)
        s = pc[:, sl]
        for j in range(POOL_BUF - (win - 1), POOL_BUF):
            s = s + pool_ref[j][:, sl]
        ds.append(s / float(min(PAST_LEN + 1, win)) - pc[:, sl])
    yc_ref[...] = _pool_project(jnp.concatenate(ds, axis=-1), wp_ref, psc_ref[...]).astype(yc_ref.dtype)
    pc_ref[...] = pc
    u, zv = _gmlp_uz(jnp.dot(xn, wd_ref[...], preferred_element_type=F32), lng_ref[...], lnb_ref[...])
    yd_ref[...] = (u * (wsd_ref[...] * zv + bsr_ref[...])).astype(yd_ref.dtype)
    zv_ref[...] = zv


def _s_rwkv_body(s_ref, rh_ref, wh_ref, kh_ref, vh_ref, kkh_ref, ah_ref,
                 r_ref, k2_ref, v_ref, g_ref, rk_ref, lng_ref, lnb_ref,
                 y_ref, so_ref, y_sc):
    eye = _eye64()
    bb = s_ref.shape[0]

    def body(b, carry):
        for h in range(RWKV_HEADS):
            rows = [ref[h, pl.ds(b, 1), :] for ref in (rh_ref, wh_ref, kh_ref, vh_ref, kkh_ref, ah_ref)]
            S, yrow = _rwkv_step(s_ref[b, h], *rows, eye)
            so_ref[b, h] = S
            y_sc[h, pl.ds(b, 1), :] = yrow
        return carry

    lax.fori_loop(0, bb, body, 0)
    y = jnp.concatenate([y_sc[h] for h in range(RWKV_HEADS)], axis=-1)
    yb = _rwkv_post(y, r_ref[...], k2_ref[...], v_ref[...], g_ref[...], rk_ref[...], lng_ref[...],
                    lnb_ref[...], _head_ones())
    y_ref[...] = yb.astype(y_ref.dtype)


def _s_xattn_body(x_ref, g_ref, wq_ref, k_ref, v_ref, wo_ref, o_ref, q_sc, a_sc):
    i = pl.program_id(0)
    bb = k_ref.shape[0]

    @pl.when(i == 0)
    def _():
        q_sc[...] = _dot(_rms(x_ref[...], g_ref[...]), wq_ref[...])

    halves = X_HEAD_DIM // LANES

    def q_lanes(c, h):
        return slice(h * X_HEAD_DIM + c * LANES, h * X_HEAD_DIM + (c + 1) * LANES)

    for j in range(bb):
        row = pl.ds(i * bb + j, 1)
        qrow = q_sc[row, :]
        q8 = jnp.concatenate([qrow[:, q_lanes(c, h)] for c in range(halves) for h in range(X_HEADS)], axis=0)
        k3 = k_ref[j].reshape(N_MEM, SUBLANES, LANES)
        part = jnp.sum(k3 * q8[None], axis=-1, keepdims=True)
        part = jnp.broadcast_to(part, k3.shape)
        s = (part + pltpu.roll(part, X_HEADS, 1)) * (X_HEAD_DIM ** -0.5)
        e = jnp.exp(s - jnp.max(s, axis=0, keepdims=True))
        pr = e / jnp.sum(e, axis=0, keepdims=True)
        o8 = jnp.sum(pr * v_ref[j].reshape(N_MEM, SUBLANES, LANES), axis=0)
        a_sc[row, :] = jnp.concatenate([o8[c * X_HEADS + h:c * X_HEADS + h + 1, :]
                                        for h in range(X_HEADS) for c in range(halves)], axis=1)

    @pl.when(i == pl.num_programs(0) - 1)
    def _():
        o_ref[...] = x_ref[...] + _dot(a_sc[...], wo_ref[...])


def _s_ffn_body(x_ref, g_ref, wug_ref, wuv_ref, sg_ref, sv_ref, cwg_ref, cwv_ref, cbg_ref, cbv_ref,
                wd_ref, o_ref, hg_ref, hv_ref, acc_ref):
    j = pl.program_id(0)

    @pl.when(j == 0)
    def _():
        acc_ref[...] = x_ref[...]

    xn = _rms(x_ref[...], g_ref[...]).astype(BF16)
    hg = jnp.dot(xn, wug_ref[...], preferred_element_type=F32)
    hv = jnp.dot(xn, wuv_ref[...], preferred_element_type=F32)
    cwg = cwg_ref[...]
    cwv = cwv_ref[...]
    cg = cbg_ref[...] + cwg[0:1] * sg_ref[0] + cwg[1:2] * sg_ref[1] + cwg[2:3] * hg
    cv = cbv_ref[...] + cwv[0:1] * sv_ref[0] + cwv[1:2] * sv_ref[1] + cwv[2:3] * hv
    acc_ref[...] += _dot(_gelu(cg) * cv, wd_ref[...])
    hg_ref[...] = hg
    hv_ref[...] = hv

    @pl.when(j == pl.num_programs(0) - 1)
    def _():
        o_ref[...] = acc_ref[...]


def _params(n_grid, vmem_mb):
    return pltpu.CompilerParams(dimension_semantics=("arbitrary",) * n_grid,
                                vmem_limit_bytes=vmem_mb << 20)


def _whole(shape):
    return pl.BlockSpec(tuple(shape), lambda *_: (0,) * len(shape))


def _layer(arr, l):
    shape = arr.shape[1:]
    return pl.BlockSpec((None,) + tuple(shape), lambda *_: (l,) + (0,) * len(shape))


def _sds(shape, dtype=F32):
    return jax.ShapeDtypeStruct(tuple(shape), dtype)


def _pair_blockdiag(w):
    L = w.shape[0]
    w = w.reshape(L, 4, 2, RWKV_HEAD, RWKV_HEAD)
    z = jnp.zeros_like(w[:, :, 0])
    top = jnp.concatenate([w[:, :, 0], z], axis=-1)
    bot = jnp.concatenate([z, w[:, :, 1]], axis=-1)
    return jnp.concatenate([top, bot], axis=-2)


def _vec(a):
    return a.reshape(a.shape[0], 1, -1)


def kernel(x_prompt, x_sample, state_lru_conv, state_lru_h, state_rwkv_shift, state_rwkv_S, state_pool, state_ffn_conv, cache_mem_k, cache_mem_v, mem_prompt, g_mix, w_in, lru_conv_w, lru_conv_b, lru_w_ra, lru_b_ra, lru_w_ix, lru_b_ix, lru_lambda, rwkv_mu, rwkv_w0, rwkv_w2, rwkv_a0, rwkv_a2, rwkv_g2, rwkv_k_k, rwkv_k_a, rwkv_r_k, rwkv_ln_g, rwkv_ln_b, pool_w, pool_scale, gmlp_ln_g, gmlp_ln_b, gmlp_w_s, gmlp_b_s, w_pa, w_pb, w_pc, w_pd, w_o, g_xattn, g_mem, w_xq, w_xk, w_xv, w_xo, g_ffn, w_up, ffn_conv_w, ffn_conv_b, w_down, g_final):
    B, T, D = x_prompt.shape
    SB = x_sample.shape[0]
    assert D == D_MODEL and w_in.shape == (DEPTH, D_MODEL, D_IN) and x_sample.shape[1] == 1
    assert T % TM_MIX == 0 and T % TM_RWKV == 0 and SB % S_RWKV_BB == 0 and SB % S_ATT_BB == 0

    wA = w_in[:, :, O_A:O_B].astype(BF16)
    wB = w_in[:, :, O_B:O_C].astype(BF16)
    wC = w_in[:, :, O_C:O_D].astype(BF16)
    wD = w_in[:, :, O_D:O_G].astype(BF16)
    wG = w_in[:, :, O_G:].astype(BF16)
    wra = _pair_blockdiag(lru_w_ra).astype(BF16)
    wix = _pair_blockdiag(lru_w_ix).astype(BF16)
    zer = jnp.zeros((DEPTH, R_DECAY, D_RWKV), F32)
    wwa = jnp.concatenate([jnp.concatenate([rwkv_w2, zer], axis=-1),
                           jnp.concatenate([zer, rwkv_a2], axis=-1)], axis=1).astype(BF16)
    g2 = rwkv_g2.astype(BF16)
    wpool = pool_w.astype(BF16)
    bsb = jnp.repeat(gmlp_b_s, LANES, axis=-1)
    off = PAST_LEN % CHUNK
    wsd = jnp.repeat(gmlp_w_s[:, :, off, off], LANES, axis=-1)[:, None, :]
    bsr = jnp.repeat(gmlp_b_s[:, off, :], LANES, axis=-1)[:, None, :]
    wP = jnp.stack([w_pa, w_pb, w_pc, w_pd], axis=1).astype(BF16)
    wO = w_o.astype(BF16)
    wXq, wXk, wXv, wXo = (w.astype(BF16) for w in (w_xq, w_xk, w_xv, w_xo))
    wUg = w_up[:, :, :D_FF].astype(BF16)
    wUv = w_up[:, :, D_FF:].astype(BF16)
    wDn = w_down.astype(BF16)
    cwg, cwv = ffn_conv_w[:, :, :D_FF], ffn_conv_w[:, :, D_FF:]
    cbg, cbv = _vec(ffn_conv_b[:, :D_FF]), _vec(ffn_conv_b[:, D_FF:])
    gmix, gxat, gffn, gmem = _vec(g_mix), _vec(g_xattn), _vec(g_ffn), _vec(g_mem)
    lcb, bra, bix, lam = _vec(lru_conv_b), _vec(lru_b_ra), _vec(lru_b_ix), _vec(lru_lambda)
    mu, w0, a0, kkw, kaw = _vec(rwkv_mu), _vec(rwkv_w0), _vec(rwkv_a0), _vec(rwkv_k_k), _vec(rwkv_k_a)
    rk = rwkv_r_k.reshape(DEPTH, 1, D_RWKV)
    rlg, rlb = _vec(rwkv_ln_g), _vec(rwkv_ln_b)
    psc, glg, glb = _vec(pool_scale), _vec(gmlp_ln_g), _vec(gmlp_ln_b)
    gfin = g_final.reshape(1, D_MODEL)

    def cache_rows(c):
        c = c.reshape(DEPTH, SB, N_MEM, X_HEADS, X_HEAD_DIM // LANES, LANES)
        return jnp.swapaxes(c, 3, 4).reshape(DEPTH, SB, N_MEM * SUBLANES, LANES)

    cache_k, cache_v = cache_rows(cache_mem_k), cache_rows(cache_mem_v)

    kv_shape = _sds((DEPTH, B, N_MEM, D_MODEL))
    kv_spec = pl.BlockSpec((None, None, N_MEM, D_MODEL), lambda l, b: (l, b, 0, 0))
    wl_spec = pl.BlockSpec((None, D_MODEL, D_MODEL), lambda l, b: (l, 0, 0))
    p_mk, p_mv = pl.pallas_call(
        _memkv_body, grid=(DEPTH, B),
        in_specs=[pl.BlockSpec((None, N_MEM, D_MODEL), lambda l, b: (b, 0, 0)),
                  pl.BlockSpec((None, 1, D_MODEL), lambda l, b: (l, 0, 0)), wl_spec, wl_spec],
        out_specs=[kv_spec, kv_spec], out_shape=[kv_shape, kv_shape],
        compiler_params=_params(2, 32), name="memkv")(mem_prompt, gmem, wXk, wXv)

    nt = T // TM_MIX
    ntr = T // TM_RWKV
    xt_spec = pl.BlockSpec((None, TM_MIX, D_MODEL), lambda b, t: (b, t, 0))
    yt_spec = pl.BlockSpec((None, TM_MIX, D_LRU), lambda b, t: (b, t, 0))

    def tail_spec(rows, width):
        return pl.BlockSpec((None, rows, width), lambda b, t: (b, 0, 0))

    hp = x_prompt
    hs = x_sample.reshape(SB, D_MODEL)
    outs = {k: [] for k in ("p_conv", "s_conv", "p_h", "s_h", "p_shift", "s_shift", "p_S", "s_S",
                            "p_pool", "s_pool", "p_ffn", "s_ffn", "s_cv")}

    for l in range(DEPTH):
        L = functools.partial(_layer, l=l)

        yA, p_conv8, p_h8 = pl.pallas_call(
            _p_lru_body, grid=(B, nt),
            in_specs=[xt_spec, L(gmix), L(wA), L(lru_conv_w), L(lcb), L(wra), L(bra), L(wix), L(bix), L(lam)],
            out_specs=[yt_spec, tail_spec(SUBLANES, D_LRU), tail_spec(SUBLANES, D_LRU)],
            out_shape=[_sds((B, T, D_LRU), BF16), _sds((B, SUBLANES, D_LRU)), _sds((B, SUBLANES, D_LRU))],
            scratch_shapes=[pltpu.VMEM((SUBLANES, D_LRU), F32), pltpu.VMEM((SUBLANES, D_LRU), F32)],
            compiler_params=_params(2, 48), name="p_lru")(
                hp, gmix, wA, lru_conv_w, lcb, wra, bra, wix, bix, lam)

        head_sc = pltpu.VMEM((RWKV_HEADS, TM_RWKV, RWKV_HEAD), F32)
        chunk_sc = pltpu.VMEM((TM_RWKV // RWKV_CHUNK, RWKV_HEADS, RWKV_CHUNK, RWKV_HEAD), F32)
        yB, p_shift8, p_S = pl.pallas_call(
            _p_rwkv_body, grid=(B, ntr),
            in_specs=[pl.BlockSpec((None, TM_RWKV, D_MODEL), lambda b, t: (b, t, 0)),
                      L(gmix), L(wB), L(mu), L(wwa), L(w0), L(a0), L(g2), L(kkw), L(kaw), L(rk), L(rlg), L(rlb)],
            out_specs=[pl.BlockSpec((None, TM_RWKV, D_RWKV), lambda b, t: (b, t, 0)),
                       tail_spec(SUBLANES, D_RWKV_IN),
                       pl.BlockSpec((None, RWKV_HEADS, RWKV_HEAD, RWKV_HEAD), lambda b, t: (b, 0, 0, 0))],
            out_shape=[_sds((B, T, D_RWKV), BF16), _sds((B, SUBLANES, D_RWKV_IN)),
                       _sds((B, RWKV_HEADS, RWKV_HEAD, RWKV_HEAD))],
            scratch_shapes=[pltpu.VMEM((SUBLANES, D_RWKV_IN), F32),
                            pltpu.VMEM((RWKV_HEADS, RWKV_HEAD, RWKV_HEAD), F32)]
                           + [head_sc] * 9 + [chunk_sc] * 5,
            compiler_params=_params(2, 48), name="p_rwkv")(
                hp, gmix, wB, mu, wwa, w0, a0, g2, kkw, kaw, rk, rlg, rlb)

        yC, p_pool16 = pl.pallas_call(
            _p_pool_body, grid=(B, nt),
            in_specs=[xt_spec, L(gmix), L(wC), L(wpool), L(psc)],
            out_specs=[yt_spec, tail_spec(POOL_HIST, D_POOL)],
            out_shape=[_sds((B, T, D_POOL), BF16), _sds((B, POOL_HIST, D_POOL))],
            scratch_shapes=[pltpu.VMEM((POOL_HIST, D_POOL), F32)],
            compiler_params=_params(2, 48), name="p_pool")(hp, gmix, wC, wpool, psc)

        yD = pl.pallas_call(
            _p_gmlp_body, grid=(B, nt),
            in_specs=[xt_spec, L(gmix), L(wD), L(glg), L(glb), L(gmlp_w_s), L(bsb)],
            out_specs=yt_spec, out_shape=_sds((B, T, D_GMLP), BF16),
            compiler_params=_params(2, 48), name="p_gmlp")(hp, gmix, wD, glg, glb, gmlp_w_s, bsb)

        def merge(x2d, ys, tm):
            rows = x2d.shape[0]
            xs = pl.BlockSpec((tm, D_MODEL), lambda i: (i, 0))
            ysp = pl.BlockSpec((tm, D_LRU), lambda i: (i, 0))
            return pl.pallas_call(
                _merge_body, grid=(rows // tm,),
                in_specs=[xs, L(gmix), L(wG), ysp, ysp, ysp, ysp, L(wP), L(wO)],
                out_specs=xs, out_shape=_sds((rows, D_MODEL)),
                compiler_params=_params(1, 56), name="merge")(x2d, gmix, wG, *ys, wP, wO)

        hp = merge(hp.reshape(B * T, D_MODEL), [y.reshape(B * T, -1) for y in (yA, yB, yC, yD)],
                   TM_MIX).reshape(B, T, D_MODEL)

        kvb_spec = pl.BlockSpec((None, None, N_MEM, D_MODEL), lambda b, t: (l, b, 0, 0))
        hp = pl.pallas_call(
            _p_xattn_body, grid=(B, nt),
            in_specs=[xt_spec, L(gxat), L(wXq), kvb_spec, kvb_spec, L(wXo)],
            out_specs=xt_spec, out_shape=_sds((B, T, D_MODEL)),
            compiler_params=_params(2, 48), name="p_xattn")(hp, gxat, wXq, p_mk, p_mv, wXo)

        hp, p_ffn8 = pl.pallas_call(
            _p_ffn_body, grid=(B, nt),
            in_specs=[xt_spec, L(gffn), L(wUg), L(wUv), L(cwg), L(cwv), L(cbg), L(cbv), L(wDn)],
            out_specs=[xt_spec, tail_spec(SUBLANES, 2 * D_FF)],
            out_shape=[_sds((B, T, D_MODEL)), _sds((B, SUBLANES, 2 * D_FF))],
            scratch_shapes=[pltpu.VMEM((SUBLANES, 2 * D_FF), F32)],
            compiler_params=_params(2, 56), name="p_ffn")(hp, gffn, wUg, wUv, cwg, cwv, cbg, cbv, wDn)

        outs["p_conv"].append(p_conv8[:, SUBLANES - 3:])
        outs["p_h"].append(p_h8[:, SUBLANES - 1])
        outs["p_shift"].append(p_shift8[:, SUBLANES - 1:])
        outs["p_S"].append(p_S)
        outs["p_pool"].append(p_pool16[:, POOL_HIST - POOL_BUF:])
        outs["p_ffn"].append(p_ffn8[:, SUBLANES - 2:])

        lconv = jnp.swapaxes(state_lru_conv[l], 0, 1)
        spool = jnp.swapaxes(state_pool[l], 0, 1)
        sffn = jnp.swapaxes(state_ffn_conv[l], 0, 1)
        shift = state_rwkv_shift[l].reshape(SB, D_RWKV_IN)
        row512 = _sds((SB, D_LRU))
        heads = _sds((RWKV_HEADS, SB, RWKV_HEAD))
        mix_in = [hs, gmix[l], wA[l], wB[l], wC[l], wD[l],
                  lconv, state_lru_h[l], lru_conv_w[l], lcb[l], wra[l], bra[l], wix[l], bix[l], lam[l],
                  shift, mu[l], wwa[l], w0[l], a0[l], g2[l], kkw[l], kaw[l],
                  spool, wpool[l], psc[l],
                  glg[l], glb[l], wsd[l], bsr[l]]
        mix_out = [_sds((SB, D_LRU), BF16), row512, row512, _sds((SB, D_POOL), BF16), row512,
                   _sds((SB, D_GMLP), BF16), row512, _sds((SB, D_RWKV_IN)),
                   row512, row512, row512, row512, heads, heads, heads, heads, heads, heads]
        (yA, s_xa, s_h, yC, s_pc, yD, s_zv, s_p, s_r, s_k2, s_v, s_g,
         rh, wh, kh, vh, kkh, ah) = pl.pallas_call(
            _s_mix_body, grid=(1,),
            in_specs=[_whole(a.shape) for a in mix_in],
            out_specs=[_whole(o.shape) for o in mix_out], out_shape=mix_out,
            compiler_params=_params(1, 56), name="s_mix")(*mix_in)

        hd_spec = pl.BlockSpec((RWKV_HEADS, S_RWKV_BB, RWKV_HEAD), lambda i: (0, i, 0))
        rb_spec = pl.BlockSpec((S_RWKV_BB, D_RWKV), lambda i: (i, 0))
        st_spec = pl.BlockSpec((None, S_RWKV_BB, RWKV_HEADS, RWKV_HEAD, RWKV_HEAD), lambda i: (l, i, 0, 0, 0))
        so_spec = pl.BlockSpec((S_RWKV_BB, RWKV_HEADS, RWKV_HEAD, RWKV_HEAD), lambda i: (i, 0, 0, 0))
        yB, s_S = pl.pallas_call(
            _s_rwkv_body, grid=(SB // S_RWKV_BB,),
            in_specs=[st_spec] + [hd_spec] * 6 + [rb_spec] * 4 + [L(rk), L(rlg), L(rlb)],
            out_specs=[rb_spec, so_spec],
            out_shape=[_sds((SB, D_RWKV), BF16), _sds((SB, RWKV_HEADS, RWKV_HEAD, RWKV_HEAD))],
            scratch_shapes=[pltpu.VMEM((RWKV_HEADS, S_RWKV_BB, RWKV_HEAD), F32)],
            compiler_params=_params(1, 48), name="s_rwkv")(
                state_rwkv_S, rh, wh, kh, vh, kkh, ah, s_r, s_k2, s_v, s_g, rk, rlg, rlb)

        hs = merge(hs, [yA, yB, yC, yD], SB)

        kc_spec = pl.BlockSpec((None, S_ATT_BB, N_MEM * SUBLANES, LANES), lambda i: (l, i, 0, 0))
        xs_spec = _whole((SB, D_MODEL))
        hs = pl.pallas_call(
            _s_xattn_body, grid=(SB // S_ATT_BB,),
            in_specs=[xs_spec, L(gxat), L(wXq), kc_spec, kc_spec, L(wXo)],
            out_specs=xs_spec, out_shape=_sds((SB, D_MODEL)),
            scratch_shapes=[pltpu.VMEM((SB, D_MODEL), F32), pltpu.VMEM((SB, D_MODEL), F32)],
            compiler_params=_params(1, 56), name="s_xattn")(
                hs, gxat, wXq, cache_k, cache_v, wXo)

        ncf = D_FF // FF_CW
        wu_spec = pl.BlockSpec((None, D_MODEL, FF_CW), lambda j: (l, 0, j))
        sg_spec = pl.BlockSpec((2, SB, FF_CW), lambda j: (0, 0, j))
        sv_spec = pl.BlockSpec((2, SB, FF_CW), lambda j: (0, 0, j + ncf))
        cw_spec = pl.BlockSpec((None, 3, FF_CW), lambda j: (l, 0, j))
        cb_spec = pl.BlockSpec((None, 1, FF_CW), lambda j: (l, 0, j))
        hh_spec = pl.BlockSpec((SB, FF_CW), lambda j: (0, j))
        hs, s_hg, s_hv = pl.pallas_call(
            _s_ffn_body, grid=(ncf,),
            in_specs=[xs_spec, L(gffn), wu_spec, wu_spec, sg_spec, sv_spec, cw_spec, cw_spec, cb_spec, cb_spec,
                      pl.BlockSpec((None, FF_CW, D_MODEL), lambda j: (l, j, 0))],
            out_specs=[xs_spec, hh_spec, hh_spec],
            out_shape=[_sds((SB, D_MODEL)), _sds((SB, D_FF)), _sds((SB, D_FF))],
            scratch_shapes=[pltpu.VMEM((SB, D_MODEL), F32)],
            compiler_params=_params(1, 48), name="s_ffn")(
                hs, gffn, wUg, wUv, sffn, sffn, cwg, cwv, cbg, cbv, wDn)

        outs["s_conv"].append(jnp.concatenate([state_lru_conv[l][:, 1:], s_xa[:, None]], axis=1))
        outs["s_h"].append(s_h)
        outs["s_shift"].append(s_p[:, None])
        outs["s_S"].append(s_S)
        outs["s_pool"].append(jnp.concatenate([state_pool[l][:, 1:], s_pc[:, None]], axis=1))
        outs["s_ffn"].append(jnp.concatenate(
            [state_ffn_conv[l][:, 1:], jnp.concatenate([s_hg, s_hv], axis=-1)[:, None]], axis=1))
        outs["s_cv"].append(s_zv[:, None])

    def final_norm(x2d, tm):
        rows = x2d.shape[0]
        spec = pl.BlockSpec((tm, D_MODEL), lambda i: (i, 0))
        return pl.pallas_call(_final_norm_body, grid=(rows // tm,),
                              in_specs=[spec, _whole((1, D_MODEL))], out_specs=spec,
                              out_shape=_sds((rows, D_MODEL)),
                              compiler_params=_params(1, 32), name="final_norm")(x2d, gfin)

    y_prompt = final_norm(hp.reshape(B * T, D_MODEL), 1024).reshape(B, T, D_MODEL)
    y_sample = final_norm(hs, SB).reshape(SB, 1, D_MODEL)
    stk = {k: jnp.stack(v, axis=0) for k, v in outs.items()}
    mem_shape = (DEPTH, B, N_MEM, X_HEADS, X_HEAD_DIM)
    return (y_prompt, y_sample,
            stk["p_conv"], stk["s_conv"], stk["p_h"], stk["s_h"],
            stk["p_shift"], stk["s_shift"], stk["p_S"], stk["s_S"],
            stk["p_pool"], stk["s_pool"], stk["p_ffn"], stk["s_ffn"],
            p_mk.reshape(mem_shape), p_mv.reshape(mem_shape), stk["s_cv"])
```

```python
import functools
import math

import jax
import jax.numpy as jnp
from jax import lax
from jax.experimental import pallas as pl
from jax.experimental.pallas import tpu as pltpu

F32 = jnp.float32
BF16 = jnp.bfloat16

SUBLANES = 8
LANES = 128

D_MODEL = 1024
DEPTH = 4
N_MEM = 256
D_LRU = 512
LRU_C = 8.0
D_RWKV = 512
RWKV_HEAD = 64
RWKV_HEADS = D_RWKV // RWKV_HEAD
R_DECAY = 64
R_AAA = 64
R_GATE = 128
D_RWKV_IN = 3 * D_RWKV + R_DECAY + R_AAA + R_GATE
GN_EPS = 64e-5
D_POOL = 512
POOL_WINDOWS = (2, 4, 8, 16)
POOL_GW = D_POOL // len(POOL_WINDOWS)
POOL_BUF = max(POOL_WINDOWS) - 1
POOL_HIST = 16
D_GMLP = 512
GMLP_GROUPS = 4
CHUNK = 128
N_BRANCH = 4
X_HEADS = 4
X_HEAD_DIM = D_MODEL // X_HEADS
D_FF = 3 * D_MODEL
EPS = 1e-6
PAST_LEN = 16384

O_A = 0
O_B = 2 * D_LRU
O_C = O_B + D_RWKV_IN
O_D = O_C + D_POOL
O_G = O_D + 2 * D_GMLP
D_IN = O_G + N_BRANCH * D_MODEL

TM_MIX = 512
TM_RWKV = 256
FF_CW = 768
RWKV_CHUNK = 64
RWKV_GROUP = 4
S_RWKV_BB = 16
S_ATT_BB = 8


def _dot(a, b):
    return jnp.dot(a.astype(BF16), b.astype(BF16), preferred_element_type=F32)


def _rms(x, g):
    return x * lax.rsqrt(jnp.mean(x * x, axis=-1, keepdims=True) + EPS) * g


def _gelu(x):
    c = math.sqrt(2.0 / math.pi)
    return 0.5 * x * (1.0 + jnp.tanh(c * (x + 0.044715 * (x * x * x))))


def _softplus(x):
    return jnp.maximum(x, 0.0) + jnp.log1p(jnp.exp(-jnp.abs(x)))


def _sigmoid(x):
    return jax.nn.sigmoid(x)


def _shift_rows(hist, cur, s):
    n = hist.shape[0]
    return pltpu.roll(jnp.concatenate([hist, cur], axis=0), s, 0)[n:]


def _head_ones():
    r = lax.broadcasted_iota(jnp.int32, (LANES, LANES), 0) // RWKV_HEAD
    c = lax.broadcasted_iota(jnp.int32, (LANES, LANES), 1) // RWKV_HEAD
    return jnp.where(r == c, 1.0, 0.0).astype(BF16)


def _segsum(x, ones):
    outs = []
    for p in range(x.shape[1] // LANES):
        xb = x[:, p * LANES:(p + 1) * LANES]
        hi = xb.astype(BF16)
        lo = (xb - hi.astype(F32)).astype(BF16)
        outs.append(jnp.dot(hi, ones, preferred_element_type=F32)
                    + jnp.dot(lo, ones, preferred_element_type=F32))
    return jnp.concatenate(outs, axis=-1)


def _blockdiag_dot(x, w_ref):
    outs = []
    for p in range(x.shape[1] // LANES):
        outs.append(jnp.dot(x[:, p * LANES:(p + 1) * LANES].astype(BF16), w_ref[p],
                            preferred_element_type=F32))
    return jnp.concatenate(outs, axis=-1)


def _lru_gates(xc, wra_ref, bra_ref, wix_ref, bix_ref, lam_ref):
    r = _sigmoid(_blockdiag_dot(xc, wra_ref) + bra_ref[...])
    i = _sigmoid(_blockdiag_dot(xc, wix_ref) + bix_ref[...])
    log_a = -LRU_C * r * _softplus(-lam_ref[...])
    a = jnp.exp(log_a)
    mult = jnp.sqrt(-jnp.tanh(log_a) * (1.0 + a * a))
    return a, mult * (i * xc)


def _rwkv_pre(p, prev, mu, wwa_ref, w0, a0, g2_ref, k_k, k_a, ones):
    px = p + (prev - p) * mu
    r = px[:, 0:D_RWKV]
    k = px[:, D_RWKV:2 * D_RWKV]
    v = px[:, 2 * D_RWKV:3 * D_RWKV]
    lo = px[:, 3 * D_RWKV:3 * D_RWKV + R_DECAY + R_AAA]
    g_lo = px[:, 3 * D_RWKV + R_DECAY + R_AAA:]
    lane = lax.broadcasted_iota(jnp.int32, lo.shape, 1)
    wa = _dot(jnp.where(lane < R_DECAY, jnp.tanh(lo), lo), wwa_ref[...])
    logw = -jnp.exp(-_softplus(-(w0 + wa[:, :D_RWKV])) - 0.5)
    a = _sigmoid(a0 + wa[:, D_RWKV:])
    g = _dot(_sigmoid(g_lo), g2_ref[...])
    kk = k * k_k
    kk = kk * lax.rsqrt(jnp.maximum(_segsum(kk * kk, ones), 1e-24))
    k2 = k * (1.0 + (a - 1.0) * k_a)
    return r, logw, k2, v, kk, a, g


def _rwkv_step(S, r, w, k, v, kk, a, eye):
    sa = -jnp.sum(S * kk, axis=1, keepdims=True)
    vcol = jnp.sum(eye * v, axis=1, keepdims=True)
    S = S * w + sa * (kk * a) + vcol * k
    ycol = jnp.sum(S * r, axis=1, keepdims=True)
    yrow = jnp.sum(eye * ycol, axis=0, keepdims=True)
    return S, yrow


def _rwkv_post(y, r, k2, v, g, r_k, ln_g, ln_b, ones):
    inv = 1.0 / RWKV_HEAD
    mean = _segsum(y, ones) * inv
    yc = y - mean
    var = _segsum(yc * yc, ones) * inv
    yn = yc * lax.rsqrt(var + GN_EPS) * ln_g + ln_b
    bonus = _segsum(r * k2 * r_k, ones) * v
    return (yn + bonus) * g


def _eye64():
    r = lax.broadcasted_iota(jnp.int32, (RWKV_HEAD, RWKV_HEAD), 0)
    c = lax.broadcasted_iota(jnp.int32, (RWKV_HEAD, RWKV_HEAD), 1)
    return jnp.where(r == c, 1.0, 0.0).astype(F32)


def _pool_project(d, wp_ref, scale):
    return _blockdiag_dot(d, wp_ref) * scale


def _gmlp_uz(z2, ln_g, ln_b):
    u = _gelu(z2[:, :D_GMLP])
    v = _gelu(z2[:, D_GMLP:])
    mu = jnp.mean(v, axis=-1, keepdims=True)
    vc = v - mu
    var = jnp.mean(vc * vc, axis=-1, keepdims=True)
    return u, vc * lax.rsqrt(var + 1e-5) * ln_g + ln_b


def _p_lru_body(x_ref, g_ref, w_ref, cw_ref, cb_ref, wra_ref, bra_ref, wix_ref, bix_ref, lam_ref,
                y_ref, convo_ref, ho_ref, hist_ref, h_ref):
    @pl.when(pl.program_id(1) == 0)
    def _():
        hist_ref[...] = jnp.zeros_like(hist_ref)
        h_ref[...] = jnp.zeros_like(h_ref)

    z = _dot(_rms(x_ref[...], g_ref[...]), w_ref[...])
    xa = z[:, :D_LRU]
    ga = z[:, D_LRU:]
    tm = xa.shape[0]
    hist = hist_ref[...]
    cw = cw_ref[...]
    xc = cb_ref[...] + cw[3:4] * xa
    for s in (1, 2, 3):
        xc = xc + cw[3 - s:4 - s] * _shift_rows(hist, xa, s)
    a, b = _lru_gates(xc, wra_ref, bra_ref, wix_ref, bix_ref, lam_ref)
    row = lax.broadcasted_iota(jnp.int32, a.shape, 0)
    s = 1
    while s < tm:
        m = row >= s
        b = jnp.where(m, a * pltpu.roll(b, s, 0) + b, b)
        a = jnp.where(m, a * pltpu.roll(a, s, 0), a)
        s *= 2
    h = a * h_ref[SUBLANES - 1:SUBLANES, :] + b
    y_ref[...] = (h * _gelu(ga)).astype(y_ref.dtype)
    hist_ref[...] = xa[tm - SUBLANES:]
    h_ref[...] = h[tm - SUBLANES:]
    convo_ref[...] = xa[tm - SUBLANES:]
    ho_ref[...] = h[tm - SUBLANES:]


def _p_rwkv_body(x_ref, g_ref, w_ref, mu_ref, wwa_ref, w0_ref, a0_ref, g2_ref, kk_ref, ka_ref,
                 rk_ref, lng_ref, lnb_ref,
                 y_ref, shifto_ref, so_ref,
                 hist_ref, st_ref, at_sc, rt_sc, bt_sc, kt_sc, v_sc, bh_sc, kh_sc, ge_sc, y_sc,
                 g_sc, q_sc, ry_sc, yc_sc, gd_sc):
    @pl.when(pl.program_id(1) == 0)
    def _():
        hist_ref[...] = jnp.zeros_like(hist_ref)
        st_ref[...] = jnp.zeros_like(st_ref)

    C = RWKV_CHUNK
    ones = _head_ones()
    p = _dot(_rms(x_ref[...], g_ref[...]), w_ref[...])
    tm = p.shape[0]
    nc = tm // C
    prev = _shift_rows(hist_ref[...], p, 1)
    r, logw, k2, v, kk, a, g = _rwkv_pre(p, prev, mu_ref[...], wwa_ref, w0_ref[...], a0_ref[...],
                                          g2_ref, kk_ref[...], ka_ref[...], ones)
    hist_ref[...] = p[tm - SUBLANES:]
    shifto_ref[...] = p[tm - SUBLANES:]
    rowc = lax.broadcasted_iota(jnp.int32, logw.shape, 0) % C
    lg = logw
    s = 1
    while s < C:
        lg = lg + jnp.where(rowc >= s, pltpu.roll(lg, s, 0), 0.0)
        s *= 2
    lg_end = jnp.concatenate(
        [jnp.broadcast_to(lg[(c + 1) * C - 1:(c + 1) * C, :], (C, D_RWKV)) for c in range(nc)], axis=0)
    inv_gam = jnp.exp(-lg)
    to_end = jnp.exp(lg_end - lg)
    b = kk * a
    for sc, val in ((at_sc, -kk * jnp.exp(lg - logw)), (rt_sc, r * jnp.exp(lg)), (bt_sc, b * inv_gam),
                    (kt_sc, k2 * inv_gam), (v_sc, v), (bh_sc, b * to_end), (kh_sc, k2 * to_end),
                    (ge_sc, jnp.exp(lg_end))):
        for h in range(RWKV_HEADS):
            sc[h] = val[:, h * RWKV_HEAD:(h + 1) * RWKV_HEAD]

    eye = _eye64()
    ri = lax.broadcasted_iota(jnp.int32, (C, C), 0)
    ci = lax.broadcasted_iota(jnp.int32, (C, C), 1)
    r2 = lax.broadcasted_iota(jnp.int32, (C, 2 * C), 0)
    c2 = lax.broadcasted_iota(jnp.int32, (C, 2 * C), 1) % C
    strict2 = c2 < r2
    incl2 = c2 <= r2
    pair = (ri // 2) == (ci // 2)
    levels = []
    nb = 2
    while nb < C:
        levels.append(((ri // (2 * nb)) == (ci // (2 * nb))) & ((ri // nb) != (ci // nb)))
        nb *= 2
    zeros_c = jnp.zeros((C, RWKV_HEAD), BF16)
    tn = (((0,), (0,)), ((), ()))
    nt_ = (((1,), (1,)), ((), ()))

    def mm(x, y):
        return jnp.dot(x, y, preferred_element_type=F32)

    H = range(RWKV_HEADS)

    def phase1(i, carry):
        cs = [i * RWKV_GROUP + d for d in range(RWKV_GROUP)]
        P = [(c, h) for c in cs for h in H]
        N = range(len(P))

        def rows(c, n=C):
            return pl.ds(pl.multiple_of(c * C, C), n)

        At, Rt, Bt, Kt, V, Bh, Kh = ([sc[h, rows(c), :] for c, h in P] for sc in
                                     (at_sc, rt_sc, bt_sc, kt_sc, v_sc, bh_sc, kh_sc))
        atb = [x.astype(BF16) for x in At]
        vb = [x.astype(BF16) for x in V]
        prod = [lax.dot_general(jnp.concatenate([atb[n], Rt[n].astype(BF16)], axis=0),
                                jnp.concatenate([Bt[n], Kt[n]], axis=0).astype(BF16), nt_,
                                preferred_element_type=F32) for n in N]
        top = [jnp.where(strict2, x[:C], 0.0) for x in prod]
        bot = [jnp.where(incl2, x[C:], 0.0).astype(BF16) for x in prod]
        lab = [x[:, :C] for x in top]
        lak_v = [mm(top[n].astype(BF16), jnp.concatenate([zeros_c, vb[n]], axis=0)) for n in N]
        T = [eye + jnp.where(pair, x, 0.0) for x in lab]
        for mk in levels:
            tb = [x.astype(BF16) for x in T]
            u = [mm(tb[n], jnp.where(mk, lab[n], 0.0).astype(BF16)).astype(BF16) for n in N]
            T = [T[n] + mm(u[n], tb[n]) for n in N]
        tb = [x.astype(BF16) for x in T]
        tab = [mm(tb[n], atb[n]).astype(BF16) for n in N]
        cv = [jnp.concatenate([mm(tb[n], lak_v[n].astype(BF16)).astype(BF16), vb[n]], axis=0) for n in N]
        bkh = [jnp.concatenate([Bh[n], Kh[n]], axis=0).astype(BF16) for n in N]
        for n, (c, h) in enumerate(P):
            g_sc[c, h] = lax.dot_general(bkh[n][:C], tab[n], tn, preferred_element_type=F32)
        for n, (c, h) in enumerate(P):
            q_sc[c, h] = lax.dot_general(bkh[n], cv[n], tn, preferred_element_type=F32)
        for n, (c, h) in enumerate(P):
            ry_sc[c, h] = Rt[n] + mm(bot[n][:, :C], tab[n])
        for n, (c, h) in enumerate(P):
            yc_sc[c, h] = mm(bot[n], cv[n])
        for c, h in P:
            gcol = jnp.sum(eye * ge_sc[h, rows(c, 1), :], axis=1, keepdims=True)
            gd_sc[c, h] = jnp.broadcast_to(gcol, (C, RWKV_HEAD))
        return carry

    lax.fori_loop(0, nc // RWKV_GROUP, phase1, 0)

    def phase2(c, carry):
        rows = pl.ds(pl.multiple_of(c * C, C), C)
        st = [st_ref[h] for h in H]
        sb = [x.astype(BF16) for x in st]
        gs = [mm(g_sc[c, h].astype(BF16), sb[h]) for h in H]
        ys = [mm(ry_sc[c, h].astype(BF16), sb[h]) for h in H]
        for h in H:
            st_ref[h] = st[h] * gd_sc[c, h] + gs[h] + q_sc[c, h]
            y_sc[h, rows, :] = ys[h] + yc_sc[c, h]
        return carry

    lax.fori_loop(0, nc, phase2, 0)
    y = jnp.concatenate([y_sc[h] for h in range(RWKV_HEADS)], axis=-1)
    yb = _rwkv_post(y, r, k2, v, g, rk_ref[...], lng_ref[...], lnb_ref[...], ones)
    y_ref[...] = yb.astype(y_ref.dtype)
    for h in range(RWKV_HEADS):
        so_ref[h] = st_ref[h].T


def _p_pool_body(x_ref, g_ref, w_ref, wp_ref, sc_ref, y_ref, poolo_ref, hist_ref):
    t = pl.program_id(1)

    @pl.when(t == 0)
    def _():
        hist_ref[...] = jnp.zeros_like(hist_ref)

    pc = _dot(_rms(x_ref[...], g_ref[...]), w_ref[...])
    tm = pc.shape[0]
    X = jnp.concatenate([hist_ref[...], pc], axis=0)
    pos = t * tm + lax.broadcasted_iota(jnp.int32, (tm, POOL_GW), 0)
    ds = []
    for gi, win in enumerate(POOL_WINDOWS):
        sl = slice(gi * POOL_GW, (gi + 1) * POOL_GW)
        s = X[:, sl]
        sh = 1
        while sh < win:
            s = s + pltpu.roll(s, sh, 0)
            sh *= 2
        cnt = jnp.minimum(pos + 1, win).astype(F32)
        ds.append(s[POOL_HIST:] / cnt - pc[:, sl])
    y = _pool_project(jnp.concatenate(ds, axis=-1), wp_ref, sc_ref[...])
    y_ref[...] = y.astype(y_ref.dtype)
    hist_ref[...] = pc[tm - POOL_HIST:]
    poolo_ref[...] = pc[tm - POOL_HIST:]


def _p_gmlp_body(x_ref, g_ref, w_ref, lng_ref, lnb_ref, ws_ref, bsb_ref, y_ref):
    z2 = _dot(_rms(x_ref[...], g_ref[...]), w_ref[...])
    tm = z2.shape[0]
    u, zv = _gmlp_uz(z2, lng_ref[...], lnb_ref[...])
    tril = (lax.broadcasted_iota(jnp.int32, (CHUNK, CHUNK), 0)
            >= lax.broadcasted_iota(jnp.int32, (CHUNK, CHUNK), 1))
    ws = [jnp.where(tril, ws_ref[gi], 0.0).astype(BF16) for gi in range(GMLP_GROUPS)]
    zb = zv.astype(BF16)
    rows = []
    for c in range(tm // CHUNK):
        cols = [jnp.dot(ws[gi], zb[c * CHUNK:(c + 1) * CHUNK, gi * LANES:(gi + 1) * LANES],
                        preferred_element_type=F32) for gi in range(GMLP_GROUPS)]
        rows.append(jnp.concatenate(cols, axis=-1) + bsb_ref[...])
    s = jnp.concatenate(rows, axis=0)
    y_ref[...] = (u * s).astype(y_ref.dtype)


def _merge_body(x_ref, g_ref, wg_ref, ya_ref, yb_ref, yc_ref, yd_ref, wp_ref, wo_ref, o_ref):
    x = x_ref[...]
    xn = _rms(x, g_ref[...]).astype(BF16)
    merged = None
    for i, y_ref in enumerate((ya_ref, yb_ref, yc_ref, yd_ref)):
        gate = _sigmoid(jnp.dot(xn, wg_ref[:, i * D_MODEL:(i + 1) * D_MODEL],
                                preferred_element_type=F32))
        term = gate * jnp.dot(y_ref[...], wp_ref[i], preferred_element_type=F32)
        merged = term if merged is None else merged + term
    o_ref[...] = x + _dot(merged, wo_ref[...])


def _softmax_rows(s):
    e = jnp.exp(s - jnp.max(s, axis=-1, keepdims=True))
    return e / jnp.sum(e, axis=-1, keepdims=True)


def _p_xattn_body(x_ref, g_ref, wq_ref, k_ref, v_ref, wo_ref, o_ref):
    x = x_ref[...]
    q = _dot(_rms(x, g_ref[...]), wq_ref[...]).astype(BF16)
    kb = k_ref[...].astype(BF16)
    vb = v_ref[...].astype(BF16)
    outs = []
    for h in range(X_HEADS):
        sl = slice(h * X_HEAD_DIM, (h + 1) * X_HEAD_DIM)
        s = lax.dot_general(q[:, sl], kb[:, sl], (((1,), (1,)), ((), ())),
                            preferred_element_type=F32) * (X_HEAD_DIM ** -0.5)
        outs.append(jnp.dot(_softmax_rows(s).astype(BF16), vb[:, sl], preferred_element_type=F32))
    o_ref[...] = x + _dot(jnp.concatenate(outs, axis=-1), wo_ref[...])


def _ffn_conv(hist, h, cw, cb):
    return cb + cw[2:3] * h + cw[1:2] * _shift_rows(hist, h, 1) + cw[0:1] * _shift_rows(hist, h, 2)


def _p_ffn_body(x_ref, g_ref, wug_ref, wuv_ref, cwg_ref, cwv_ref, cbg_ref, cbv_ref, wd_ref,
                o_ref, ffo_ref, hist_ref):
    @pl.when(pl.program_id(1) == 0)
    def _():
        hist_ref[...] = jnp.zeros_like(hist_ref)

    x = x_ref[...]
    tm = x.shape[0]
    xn = _rms(x, g_ref[...]).astype(BF16)
    acc = x
    for j in range(D_FF // FF_CW):
        cs = slice(j * FF_CW, (j + 1) * FF_CW)
        vs = slice(D_FF + j * FF_CW, D_FF + (j + 1) * FF_CW)
        hg = jnp.dot(xn, wug_ref[:, cs], preferred_element_type=F32)
        hv = jnp.dot(xn, wuv_ref[:, cs], preferred_element_type=F32)
        cg = _ffn_conv(hist_ref[:, cs], hg, cwg_ref[:, cs], cbg_ref[:, cs])
        cv = _ffn_conv(hist_ref[:, vs], hv, cwv_ref[:, cs], cbv_ref[:, cs])
        acc = acc + _dot(_gelu(cg) * cv, wd_ref[cs, :])
        hist_ref[:, cs] = hg[tm - SUBLANES:]
        hist_ref[:, vs] = hv[tm - SUBLANES:]
    o_ref[...] = acc
    ffo_ref[...] = hist_ref[...]


def _memkv_body(m_ref, g_ref, wk_ref, wv_ref, k_ref, v_ref):
    mn = _rms(m_ref[...], g_ref[...]).astype(BF16)
    k_ref[...] = jnp.dot(mn, wk_ref[...], preferred_element_type=F32)
    v_ref[...] = jnp.dot(mn, wv_ref[...], preferred_element_type=F32)


def _final_norm_body(x_ref, g_ref, o_ref):
    o_ref[...] = _rms(x_ref[...], g_ref[...])


def _s_mix_body(x_ref, g_ref, wa_ref, wb_ref, wc_ref, wd_ref,
                lconv_ref, lh_ref, cw_ref, cb_ref, wra_ref, bra_ref, wix_ref, bix_ref, lam_ref,
                shift_ref, mu_ref, wwa_ref, w0_ref, a0_ref, g2_ref, kk_ref, ka_ref,
                pool_ref, wp_ref, psc_ref,
                lng_ref, lnb_ref, wsd_ref, bsr_ref,
                ya_ref, xa_ref, h_ref, yc_ref, pc_ref, yd_ref, zv_ref, p_ref,
                r_ref, k2_ref, v_ref, gg_ref, rh_ref, wh_ref, kh_ref, vh_ref, kkh_ref, ah_ref):
    xn = _rms(x_ref[...], g_ref[...]).astype(BF16)
    z = jnp.dot(xn, wa_ref[...], preferred_element_type=F32)
    xa = z[:, :D_LRU]
    cw = cw_ref[...]
    xc = cb_ref[...] + cw[0:1] * lconv_ref[0] + cw[1:2] * lconv_ref[1] + cw[2:3] * lconv_ref[2] + cw[3:4] * xa
    a, b = _lru_gates(xc, wra_ref, bra_ref, wix_ref, bix_ref, lam_ref)
    h = a * lh_ref[...] + b
    ya_ref[...] = (h * _gelu(z[:, D_LRU:])).astype(ya_ref.dtype)
    xa_ref[...] = xa
    h_ref[...] = h
    ones = _head_ones()
    p = jnp.dot(xn, wb_ref[...], preferred_element_type=F32)
    r, logw, k2, v, kk, aa, g = _rwkv_pre(p, shift_ref[...], mu_ref[...], wwa_ref, w0_ref[...],
                                           a0_ref[...], g2_ref, kk_ref[...], ka_ref[...], ones)
    decay = jnp.exp(logw)
    p_ref[...] = p
    r_ref[...] = r
    k2_ref[...] = k2
    v_ref[...] = v
    gg_ref[...] = g
    for ref, val in ((rh_ref, r), (wh_ref, decay), (kh_ref, k2), (vh_ref, v), (kkh_ref, kk), (ah_ref, aa)):
        for hd in range(RWKV_HEADS):
            ref[hd] = val[:, hd * RWKV_HEAD:(hd + 1) * RWKV_HEAD]
    pc = jnp.dot(xn, wc_ref[...], preferred_element_type=F32)
    ds = []
    for gi, win in enumerate(POOL_WINDOWS):
        sl = slice(gi * POOL_GW, (gi + 1) * POOL_GW)
        s = pc[:, sl]
        for j in range(POOL_BUF - (win - 1), POOL_BUF):
            s = s + pool_ref[j][:, sl]
        ds.append(s / float(min(PAST_LEN + 1, win)) - pc[:, sl])
    yc_ref[...] = _pool_project(jnp.concatenate(ds, axis=-1), wp_ref, psc_ref[...]).astype(yc_ref.dtype)
    pc_ref[...] = pc
    u, zv = _gmlp_uz(jnp.dot(xn, wd_ref[...], preferred_element_type=F32), lng_ref[...], lnb_ref[...])
    yd_ref[...] = (u * (wsd_ref[...] * zv + bsr_ref[...])).astype(yd_ref.dtype)
    zv_ref[...] = zv


def _s_rwkv_body(s_ref, rh_ref, wh_ref, kh_ref, vh_ref, kkh_ref, ah_ref,
                 r_ref, k2_ref, v_ref, g_ref, rk_ref, lng_ref, lnb_ref,
                 y_ref, so_ref, y_sc):
    eye = _eye64()
    bb = s_ref.shape[0]

    def body(b, carry):
        for h in range(RWKV_HEADS):
            rows = [ref[h, pl.ds(b, 1), :] for ref in (rh_ref, wh_ref, kh_ref, vh_ref, kkh_ref, ah_ref)]
            S, yrow = _rwkv_step(s_ref[b, h], *rows, eye)
            so_ref[b, h] = S
            y_sc[h, pl.ds(b, 1), :] = yrow
        return carry

    lax.fori_loop(0, bb, body, 0)
    y = jnp.concatenate([y_sc[h] for h in range(RWKV_HEADS)], axis=-1)
    yb = _rwkv_post(y, r_ref[...], k2_ref[...], v_ref[...], g_ref[...], rk_ref[...], lng_ref[...],
                    lnb_ref[...], _head_ones())
    y_ref[...] = yb.astype(y_ref.dtype)


def _s_xattn_body(x_ref, g_ref, wq_ref, k_ref, v_ref, wo_ref, o_ref, q_sc, a_sc):
    i = pl.program_id(0)
    bb = k_ref.shape[0]

    @pl.when(i == 0)
    def _():
        q_sc[...] = _dot(_rms(x_ref[...], g_ref[...]), wq_ref[...])

    halves = X_HEAD_DIM // LANES

    def q_lanes(c, h):
        return slice(h * X_HEAD_DIM + c * LANES, h * X_HEAD_DIM + (c + 1) * LANES)

    for j in range(bb):
        row = pl.ds(i * bb + j, 1)
        qrow = q_sc[row, :]
        q8 = jnp.concatenate([qrow[:, q_lanes(c, h)] for c in range(halves) for h in range(X_HEADS)], axis=0)
        k3 = k_ref[j].reshape(N_MEM, SUBLANES, LANES)
        part = jnp.sum(k3 * q8[None], axis=-1, keepdims=True)
        part = jnp.broadcast_to(part, k3.shape)
        s = (part + pltpu.roll(part, X_HEADS, 1)) * (X_HEAD_DIM ** -0.5)
        e = jnp.exp(s - jnp.max(s, axis=0, keepdims=True))
        pr = e / jnp.sum(e, axis=0, keepdims=True)
        o8 = jnp.sum(pr * v_ref[j].reshape(N_MEM, SUBLANES, LANES), axis=0)
        a_sc[row, :] = jnp.concatenate([o8[c * X_HEADS + h:c * X_HEADS + h + 1, :]
                                        for h in range(X_HEADS) for c in range(halves)], axis=1)

    @pl.when(i == pl.num_programs(0) - 1)
    def _():
        o_ref[...] = x_ref[...] + _dot(a_sc[...], wo_ref[...])


def _s_ffn_body(x_ref, g_ref, wug_ref, wuv_ref, sg_ref, sv_ref, cwg_ref, cwv_ref, cbg_ref, cbv_ref,
                wd_ref, o_ref, hg_ref, hv_ref, acc_ref):
    j = pl.program_id(0)

    @pl.when(j == 0)
    def _():
        acc_ref[...] = x_ref[...]

    xn = _rms(x_ref[...], g_ref[...]).astype(BF16)
    hg = jnp.dot(xn, wug_ref[...], preferred_element_type=F32)
    hv = jnp.dot(xn, wuv_ref[...], preferred_element_type=F32)
    cwg = cwg_ref[...]
    cwv = cwv_ref[...]
    cg = cbg_ref[...] + cwg[0:1] * sg_ref[0] + cwg[1:2] * sg_ref[1] + cwg[2:3] * hg
    cv = cbv_ref[...] + cwv[0:1] * sv_ref[0] + cwv[1:2] * sv_ref[1] + cwv[2:3] * hv
    acc_ref[...] += _dot(_gelu(cg) * cv, wd_ref[...])
    hg_ref[...] = hg
    hv_ref[...] = hv

    @pl.when(j == pl.num_programs(0) - 1)
    def _():
        o_ref[...] = acc_ref[...]


def _params(n_grid, vmem_mb):
    return pltpu.CompilerParams(dimension_semantics=("arbitrary",) * n_grid,
                                vmem_limit_bytes=vmem_mb << 20)


def _whole(shape):
    return pl.BlockSpec(tuple(shape), lambda *_: (0,) * len(shape))


def _layer(arr, l):
    shape = arr.shape[1:]
    return pl.BlockSpec((None,) + tuple(shape), lambda *_: (l,) + (0,) * len(shape))


def _sds(shape, dtype=F32):
    return jax.ShapeDtypeStruct(tuple(shape), dtype)


def _pair_blockdiag(w):
    L = w.shape[0]
    w = w.reshape(L, 4, 2, RWKV_HEAD, RWKV_HEAD)
    z = jnp.zeros_like(w[:, :, 0])
    top = jnp.concatenate([w[:, :, 0], z], axis=-1)
    bot = jnp.concatenate([z, w[:, :, 1]], axis=-1)
    return jnp.concatenate([top, bot], axis=-2)


def _vec(a):
    return a.reshape(a.shape[0], 1, -1)


def kernel(x_prompt, x_sample, state_lru_conv, state_lru_h, state_rwkv_shift, state_rwkv_S, state_pool, state_ffn_conv, cache_mem_k, cache_mem_v, mem_prompt, g_mix, w_in, lru_conv_w, lru_conv_b, lru_w_ra, lru_b_ra, lru_w_ix, lru_b_ix, lru_lambda, rwkv_mu, rwkv_w0, rwkv_w2, rwkv_a0, rwkv_a2, rwkv_g2, rwkv_k_k, rwkv_k_a, rwkv_r_k, rwkv_ln_g, rwkv_ln_b, pool_w, pool_scale, gmlp_ln_g, gmlp_ln_b, gmlp_w_s, gmlp_b_s, w_pa, w_pb, w_pc, w_pd, w_o, g_xattn, g_mem, w_xq, w_xk, w_xv, w_xo, g_ffn, w_up, ffn_conv_w, ffn_conv_b, w_down, g_final):
    B, T, D = x_prompt.shape
    SB = x_sample.shape[0]
    assert D == D_MODEL and w_in.shape == (DEPTH, D_MODEL, D_IN) and x_sample.shape[1] == 1
    assert T % TM_MIX == 0 and T % TM_RWKV == 0 and SB % S_RWKV_BB == 0 and SB % S_ATT_BB == 0

    wA = w_in[:, :, O_A:O_B].astype(BF16)
    wB = w_in[:, :, O_B:O_C].astype(BF16)
    wC = w_in[:, :, O_C:O_D].astype(BF16)
    wD = w_in[:, :, O_D:O_G].astype(BF16)
    wG = w_in[:, :, O_G:].astype(BF16)
    wra = _pair_blockdiag(lru_w_ra).astype(BF16)
    wix = _pair_blockdiag(lru_w_ix).astype(BF16)
    zer = jnp.zeros((DEPTH, R_DECAY, D_RWKV), F32)
    wwa = jnp.concatenate([jnp.concatenate([rwkv_w2, zer], axis=-1),
                           jnp.concatenate([zer, rwkv_a2], axis=-1)], axis=1).astype(BF16)
    g2 = rwkv_g2.astype(BF16)
    wpool = pool_w.astype(BF16)
    bsb = jnp.repeat(gmlp_b_s, LANES, axis=-1)
    off = PAST_LEN % CHUNK
    wsd = jnp.repeat(gmlp_w_s[:, :, off, off], LANES, axis=-1)[:, None, :]
    bsr = jnp.repeat(gmlp_b_s[:, off, :], LANES, axis=-1)[:, None, :]
    wP = jnp.stack([w_pa, w_pb, w_pc, w_pd], axis=1).astype(BF16)
    wO = w_o.astype(BF16)
    wXq, wXk, wXv, wXo = (w.astype(BF16) for w in (w_xq, w_xk, w_xv, w_xo))
    wUg = w_up[:, :, :D_FF].astype(BF16)
    wUv = w_up[:, :, D_FF:].astype(BF16)
    wDn = w_down.astype(BF16)
    cwg, cwv = ffn_conv_w[:, :, :D_FF], ffn_conv_w[:, :, D_FF:]
    cbg, cbv = _vec(ffn_conv_b[:, :D_FF]), _vec(ffn_conv_b[:, D_FF:])
    gmix, gxat, gffn, gmem = _vec(g_mix), _vec(g_xattn), _vec(g_ffn), _vec(g_mem)
    lcb, bra, bix, lam = _vec(lru_conv_b), _vec(lru_b_ra), _vec(lru_b_ix), _vec(lru_lambda)
    mu, w0, a0, kkw, kaw = _vec(rwkv_mu), _vec(rwkv_w0), _vec(rwkv_a0), _vec(rwkv_k_k), _vec(rwkv_k_a)
    rk = rwkv_r_k.reshape(DEPTH, 1, D_RWKV)
    rlg, rlb = _vec(rwkv_ln_g), _vec(rwkv_ln_b)
    psc, glg, glb = _vec(pool_scale), _vec(gmlp_ln_g), _vec(gmlp_ln_b)
    gfin = g_final.reshape(1, D_MODEL)

    def cache_rows(c):
        c = c.reshape(DEPTH, SB, N_MEM, X_HEADS, X_HEAD_DIM // LANES, LANES)
        return jnp.swapaxes(c, 3, 4).reshape(DEPTH, SB, N_MEM * SUBLANES, LANES)

    cache_k, cache_v = cache_rows(cache_mem_k), cache_rows(cache_mem_v)

    kv_shape = _sds((DEPTH, B, N_MEM, D_MODEL))
    kv_spec = pl.BlockSpec((None, None, N_MEM, D_MODEL), lambda l, b: (l, b, 0, 0))
    wl_spec = pl.BlockSpec((None, D_MODEL, D_MODEL), lambda l, b: (l, 0, 0))
    p_mk, p_mv = pl.pallas_call(
        _memkv_body, grid=(DEPTH, B),
        in_specs=[pl.BlockSpec((None, N_MEM, D_MODEL), lambda l, b: (b, 0, 0)),
                  pl.BlockSpec((None, 1, D_MODEL), lambda l, b: (l, 0, 0)), wl_spec, wl_spec],
        out_specs=[kv_spec, kv_spec], out_shape=[kv_shape, kv_shape],
        compiler_params=_params(2, 32), name="memkv")(mem_prompt, gmem, wXk, wXv)

    nt = T // TM_MIX
    ntr = T // TM_RWKV
    xt_spec = pl.BlockSpec((None, TM_MIX, D_MODEL), lambda b, t: (b, t, 0))
    yt_spec = pl.BlockSpec((None, TM_MIX, D_LRU), lambda b, t: (b, t, 0))

    def tail_spec(rows, width):
        return pl.BlockSpec((None, rows, width), lambda b, t: (b, 0, 0))

    hp = x_prompt
    hs = x_sample.reshape(SB, D_MODEL)
    outs = {k: [] for k in ("p_conv", "s_conv", "p_h", "s_h", "p_shift", "s_shift", "p_S", "s_S",
                            "p_pool", "s_pool", "p_ffn", "s_ffn", "s_cv")}

    for l in range(DEPTH):
        L = functools.partial(_layer, l=l)

        yA, p_conv8, p_h8 = pl.pallas_call(
            _p_lru_body, grid=(B, nt),
            in_specs=[xt_spec, L(gmix), L(wA), L(lru_conv_w), L(lcb), L(wra), L(bra), L(wix), L(bix), L(lam)],
            out_specs=[yt_spec, tail_spec(SUBLANES, D_LRU), tail_spec(SUBLANES, D_LRU)],
            out_shape=[_sds((B, T, D_LRU), BF16), _sds((B, SUBLANES, D_LRU)), _sds((B, SUBLANES, D_LRU))],
            scratch_shapes=[pltpu.VMEM((SUBLANES, D_LRU), F32), pltpu.VMEM((SUBLANES, D_LRU), F32)],
            compiler_params=_params(2, 48), name="p_lru")(
                hp, gmix, wA, lru_conv_w, lcb, wra, bra, wix, bix, lam)

        head_sc = pltpu.VMEM((RWKV_HEADS, TM_RWKV, RWKV_HEAD), F32)
        chunk_sc = pltpu.VMEM((TM_RWKV // RWKV_CHUNK, RWKV_HEADS, RWKV_CHUNK, RWKV_HEAD), F32)
        yB, p_shift8, p_S = pl.pallas_call(
            _p_rwkv_body, grid=(B, ntr),
            in_specs=[pl.BlockSpec((None, TM_RWKV, D_MODEL), lambda b, t: (b, t, 0)),
                      L(gmix), L(wB), L(mu), L(wwa), L(w0), L(a0), L(g2), L(kkw), L(kaw), L(rk), L(rlg), L(rlb)],
            out_specs=[pl.BlockSpec((None, TM_RWKV, D_RWKV), lambda b, t: (b, t, 0)),
                       tail_spec(SUBLANES, D_RWKV_IN),
                       pl.BlockSpec((None, RWKV_HEADS, RWKV_HEAD, RWKV_HEAD), lambda b, t: (b, 0, 0, 0))],
            out_shape=[_sds((B, T, D_RWKV), BF16), _sds((B, SUBLANES, D_RWKV_IN)),
                       _sds((B, RWKV_HEADS, RWKV_HEAD, RWKV_HEAD))],
            scratch_shapes=[pltpu.VMEM((SUBLANES, D_RWKV_IN), F32),
                            pltpu.VMEM((RWKV_HEADS, RWKV_HEAD, RWKV_HEAD), F32)]
                           + [head_sc] * 9 + [chunk_sc] * 5,
            compiler_params=_params(2, 48), name="p_rwkv")(
                hp, gmix, wB, mu, wwa, w0, a0, g2, kkw, kaw, rk, rlg, rlb)

        yC, p_pool16 = pl.pallas_call(
            _p_pool_body, grid=(B, nt),
            in_specs=[xt_spec, L(gmix), L(wC), L(wpool), L(psc)],
            out_specs=[yt_spec, tail_spec(POOL_HIST, D_POOL)],
            out_shape=[_sds((B, T, D_POOL), BF16), _sds((B, POOL_HIST, D_POOL))],
            scratch_shapes=[pltpu.VMEM((POOL_HIST, D_POOL), F32)],
            compiler_params=_params(2, 48), name="p_pool")(hp, gmix, wC, wpool, psc)

        yD = pl.pallas_call(
            _p_gmlp_body, grid=(B, nt),
            in_specs=[xt_spec, L(gmix), L(wD), L(glg), L(glb), L(gmlp_w_s), L(bsb)],
            out_specs=yt_spec, out_shape=_sds((B, T, D_GMLP), BF16),
            compiler_params=_params(2, 48), name="p_gmlp")(hp, gmix, wD, glg, glb, gmlp_w_s, bsb)

        def merge(x2d, ys, tm):
            rows = x2d.shape[0]
            xs = pl.BlockSpec((tm, D_MODEL), lambda i: (i, 0))
            ysp = pl.BlockSpec((tm, D_LRU), lambda i: (i, 0))
            return pl.pallas_call(
                _merge_body, grid=(rows // tm,),
                in_specs=[xs, L(gmix), L(wG), ysp, ysp, ysp, ysp, L(wP), L(wO)],
                out_specs=xs, out_shape=_sds((rows, D_MODEL)),
                compiler_params=_params(1, 56), name="merge")(x2d, gmix, wG, *ys, wP, wO)

        hp = merge(hp.reshape(B * T, D_MODEL), [y.reshape(B * T, -1) for y in (yA, yB, yC, yD)],
                   TM_MIX).reshape(B, T, D_MODEL)

        kvb_spec = pl.BlockSpec((None, None, N_MEM, D_MODEL), lambda b, t: (l, b, 0, 0))
        hp = pl.pallas_call(
            _p_xattn_body, grid=(B, nt),
            in_specs=[xt_spec, L(gxat), L(wXq), kvb_spec, kvb_spec, L(wXo)],
            out_specs=xt_spec, out_shape=_sds((B, T, D_MODEL)),
            compiler_params=_params(2, 48), name="p_xattn")(hp, gxat, wXq, p_mk, p_mv, wXo)

        hp, p_ffn8 = pl.pallas_call(
            _p_ffn_body, grid=(B, nt),
            in_specs=[xt_spec, L(gffn), L(wUg), L(wUv), L(cwg), L(cwv), L(cbg), L(cbv), L(wDn)],
            out_specs=[xt_spec, tail_spec(SUBLANES, 2 * D_FF)],
            out_shape=[_sds((B, T, D_MODEL)), _sds((B, SUBLANES, 2 * D_FF))],
            scratch_shapes=[pltpu.VMEM((SUBLANES, 2 * D_FF), F32)],
            compiler_params=_params(2, 56), name="p_ffn")(hp, gffn, wUg, wUv, cwg, cwv, cbg, cbv, wDn)

        outs["p_conv"].append(p_conv8[:, SUBLANES - 3:])
        outs["p_h"].append(p_h8[:, SUBLANES - 1])
        outs["p_shift"].append(p_shift8[:, SUBLANES - 1:])
        outs["p_S"].append(p_S)
        outs["p_pool"].append(p_pool16[:, POOL_HIST - POOL_BUF:])
        outs["p_ffn"].append(p_ffn8[:, SUBLANES - 2:])

        lconv = jnp.swapaxes(state_lru_conv[l], 0, 1)
        spool = jnp.swapaxes(state_pool[l], 0, 1)
        sffn = jnp.swapaxes(state_ffn_conv[l], 0, 1)
        shift = state_rwkv_shift[l].reshape(SB, D_RWKV_IN)
        row512 = _sds((SB, D_LRU))
        heads = _sds((RWKV_HEADS, SB, RWKV_HEAD))
        mix_in = [hs, gmix[l], wA[l], wB[l], wC[l], wD[l],
                  lconv, state_lru_h[l], lru_conv_w[l], lcb[l], wra[l], bra[l], wix[l], bix[l], lam[l],
                  shift, mu[l], wwa[l], w0[l], a0[l], g2[l], kkw[l], kaw[l],
                  spool, wpool[l], psc[l],
                  glg[l], glb[l], wsd[l], bsr[l]]
        mix_out = [_sds((SB, D_LRU), BF16), row512, row512, _sds((SB, D_POOL), BF16), row512,
                   _sds((SB, D_GMLP), BF16), row512, _sds((SB, D_RWKV_IN)),
                   row512, row512, row512, row512, heads, heads, heads, heads, heads, heads]
        (yA, s_xa, s_h, yC, s_pc, yD, s_zv, s_p, s_r, s_k2, s_v, s_g,
         rh, wh, kh, vh, kkh, ah) = pl.pallas_call(
            _s_mix_body, grid=(1,),
            in_specs=[_whole(a.shape) for a in mix_in],
            out_specs=[_whole(o.shape) for o in mix_out], out_shape=mix_out,
            compiler_params=_params(1, 56), name="s_mix")(*mix_in)

        hd_spec = pl.BlockSpec((RWKV_HEADS, S_RWKV_BB, RWKV_HEAD), lambda i: (0, i, 0))
        rb_spec = pl.BlockSpec((S_RWKV_BB, D_RWKV), lambda i: (i, 0))
        st_spec = pl.BlockSpec((None, S_RWKV_BB, RWKV_HEADS, RWKV_HEAD, RWKV_HEAD), lambda i: (l, i, 0, 0, 0))
        so_spec = pl.BlockSpec((S_RWKV_BB, RWKV_HEADS, RWKV_HEAD, RWKV_HEAD), lambda i: (i, 0, 0, 0))
        yB, s_S = pl.pallas_call(
            _s_rwkv_body, grid=(SB // S_RWKV_BB,),
            in_specs=[st_spec] + [hd_spec] * 6 + [rb_spec] * 4 + [L(rk), L(rlg), L(rlb)],
            out_specs=[rb_spec, so_spec],
            out_shape=[_sds((SB, D_RWKV), BF16), _sds((SB, RWKV_HEADS, RWKV_HEAD, RWKV_HEAD))],
            scratch_shapes=[pltpu.VMEM((RWKV_HEADS, S_RWKV_BB, RWKV_HEAD), F32)],
            compiler_params=_params(1, 48), name="s_rwkv")(
                state_rwkv_S, rh, wh, kh, vh, kkh, ah, s_r, s_k2, s_v, s_g, rk, rlg, rlb)

        hs = merge(hs, [yA, yB, yC, yD], SB)

        kc_spec = pl.BlockSpec((None, S_ATT_BB, N_MEM * SUBLANES, LANES), lambda i: (l, i, 0, 0))
        xs_spec = _whole((SB, D_MODEL))
        hs = pl.pallas_call(
            _s_xattn_body, grid=(SB // S_ATT_BB,),
            in_specs=[xs_spec, L(gxat), L(wXq), kc_spec, kc_spec, L(wXo)],
            out_specs=xs_spec, out_shape=_sds((SB, D_MODEL)),
            scratch_shapes=[pltpu.VMEM((SB, D_MODEL), F32), pltpu.VMEM((SB, D_MODEL), F32)],
            compiler_params=_params(1, 56), name="s_xattn")(
                hs, gxat, wXq, cache_k, cache_v, wXo)

        ncf = D_FF // FF_CW
        wu_spec = pl.BlockSpec((None, D_MODEL, FF_CW), lambda j: (l, 0, j))
        sg_spec = pl.BlockSpec((2, SB, FF_CW), lambda j: (0, 0, j))
        sv_spec = pl.BlockSpec((2, SB, FF_CW), lambda j: (0, 0, j + ncf))
        cw_spec = pl.BlockSpec((None, 3, FF_CW), lambda j: (l, 0, j))
        cb_spec = pl.BlockSpec((None, 1, FF_CW), lambda j: (l, 0, j))
        hh_spec = pl.BlockSpec((SB, FF_CW), lambda j: (0, j))
        hs, s_hg, s_hv = pl.pallas_call(
            _s_ffn_body, grid=(ncf,),
            in_specs=[xs_spec, L(gffn), wu_spec, wu_spec, sg_spec, sv_spec, cw_spec, cw_spec, cb_spec, cb_spec,
                      pl.BlockSpec((None, FF_CW, D_MODEL), lambda j: (l, j, 0))],
            out_specs=[xs_spec, hh_spec, hh_spec],
            out_shape=[_sds((SB, D_MODEL)), _sds((SB, D_FF)), _sds((SB, D_FF))],
            scratch_shapes=[pltpu.VMEM((SB, D_MODEL), F32)],
            compiler_params=_params(1, 48), name="s_ffn")(
                hs, gffn, wUg, wUv, sffn, sffn, cwg, cwv, cbg, cbv, wDn)

        outs["s_conv"].append(jnp.concatenate([state_lru_conv[l][:, 1:], s_xa[:, None]], axis=1))
        outs["s_h"].append(s_h)
        outs["s_shift"].append(s_p[:, None])
        outs["s_S"].append(s_S)
        outs["s_pool"].append(jnp.concatenate([state_pool[l][:, 1:], s_pc[:, None]], axis=1))
        outs["s_ffn"].append(jnp.concatenate(
            [state_ffn_conv[l][:, 1:], jnp.concatenate([s_hg, s_hv], axis=-1)[:, None]], axis=1))
        outs["s_cv"].append(s_zv[:, None])

    def final_norm(x2d, tm):
        rows = x2d.shape[0]
        spec = pl.BlockSpec((tm, D_MODEL), lambda i: (i, 0))
        return pl.pallas_call(_final_norm_body, grid=(rows // tm,),
                              in_specs=[spec, _whole((1, D_MODEL))], out_specs=spec,
                              out_shape=_sds((rows, D_MODEL)),
                              compiler_params=_params(1, 32), name="final_norm")(x2d, gfin)

    y_prompt = final_norm(hp.reshape(B * T, D_MODEL), 1024).reshape(B, T, D_MODEL)
    y_sample = final_norm(hs, SB).reshape(SB, 1, D_MODEL)
    stk = {k: jnp.stack(v, axis=0) for k, v in outs.items()}
    mem_shape = (DEPTH, B, N_MEM, X_HEADS, X_HEAD_DIM)
    return (y_prompt, y_sample,
            stk["p_conv"], stk["s_conv"], stk["p_h"], stk["s_h"],
            stk["p_shift"], stk["s_shift"], stk["p_S"], stk["s_S"],
            stk["p_pool"], stk["s_pool"], stk["p_ffn"], stk["s_ffn"],
            p_mk.reshape(mem_shape), p_mv.reshape(mem_shape), stk["s_cv"])
```

```python
import functools
import math

import jax
import jax.numpy as jnp
from jax import lax
from jax.experimental import pallas as pl
from jax.experimental.pallas import tpu as pltpu

F32 = jnp.float32
BF16 = jnp.bfloat16

SUBLANES = 8
LANES = 128

D_MODEL = 1024
DEPTH = 4
N_MEM = 256
D_LRU = 512
LRU_C = 8.0
D_RWKV = 512
RWKV_HEAD = 64
RWKV_HEADS = D_RWKV // RWKV_HEAD
R_DECAY = 64
R_AAA = 64
R_GATE = 128
D_RWKV_IN = 3 * D_RWKV + R_DECAY + R_AAA + R_GATE
GN_EPS = 64e-5
D_POOL = 512
POOL_WINDOWS = (2, 4, 8, 16)
POOL_GW = D_POOL // len(POOL_WINDOWS)
POOL_BUF = max(POOL_WINDOWS) - 1
POOL_HIST = 16
D_GMLP = 512
GMLP_GROUPS = 4
CHUNK = 128
N_BRANCH = 4
X_HEADS = 4
X_HEAD_DIM = D_MODEL // X_HEADS
D_FF = 3 * D_MODEL
EPS = 1e-6
PAST_LEN = 16384

O_A = 0
O_B = 2 * D_LRU
O_C = O_B + D_RWKV_IN
O_D = O_C + D_POOL
O_G = O_D + 2 * D_GMLP
D_IN = O_G + N_BRANCH * D_MODEL

TM_MIX = 512
TM_RWKV = 256
FF_CW = 768
RWKV_CHUNK = 64
S_RWKV_BB = 16
S_ATT_BB = 8


def _dot(a, b):
    return jnp.dot(a.astype(BF16), b.astype(BF16), preferred_element_type=F32)


def _rms(x, g):
    return x * lax.rsqrt(jnp.mean(x * x, axis=-1, keepdims=True) + EPS) * g


def _gelu(x):
    c = math.sqrt(2.0 / math.pi)
    return 0.5 * x * (1.0 + jnp.tanh(c * (x + 0.044715 * (x * x * x))))


def _softplus(x):
    return jnp.maximum(x, 0.0) + jnp.log1p(jnp.exp(-jnp.abs(x)))


def _sigmoid(x):
    return jax.nn.sigmoid(x)


def _shift_rows(hist, cur, s):
    n = hist.shape[0]
    return pltpu.roll(jnp.concatenate([hist, cur], axis=0), s, 0)[n:]


def _head_ones():
    r = lax.broadcasted_iota(jnp.int32, (LANES, LANES), 0) // RWKV_HEAD
    c = lax.broadcasted_iota(jnp.int32, (LANES, LANES), 1) // RWKV_HEAD
    return jnp.where(r == c, 1.0, 0.0).astype(BF16)


def _segsum(x, ones):
    xb = x.astype(BF16)
    return jnp.concatenate([jnp.dot(xb[:, p * LANES:(p + 1) * LANES], ones, preferred_element_type=F32)
                            for p in range(x.shape[1] // LANES)], axis=-1)


def _blockdiag_dot(x, w_ref):
    outs = []
    for p in range(x.shape[1] // LANES):
        outs.append(jnp.dot(x[:, p * LANES:(p + 1) * LANES].astype(BF16), w_ref[p],
                            preferred_element_type=F32))
    return jnp.concatenate(outs, axis=-1)


def _lru_gates(xc, wra_ref, bra_ref, wix_ref, bix_ref, lam_ref):
    r = _sigmoid(_blockdiag_dot(xc, wra_ref) + bra_ref[...])
    i = _sigmoid(_blockdiag_dot(xc, wix_ref) + bix_ref[...])
    log_a = -LRU_C * r * _softplus(-lam_ref[...])
    a = jnp.exp(log_a)
    mult = jnp.sqrt(-jnp.tanh(log_a) * (1.0 + a * a))
    return a, mult * (i * xc)


def _rwkv_pre(p, prev, mu, wwa_ref, w0, a0, g2_ref, k_k, k_a, ones):
    px = p + (prev - p) * mu
    r = px[:, 0:D_RWKV]
    k = px[:, D_RWKV:2 * D_RWKV]
    v = px[:, 2 * D_RWKV:3 * D_RWKV]
    lo = px[:, 3 * D_RWKV:3 * D_RWKV + R_DECAY + R_AAA]
    g_lo = px[:, 3 * D_RWKV + R_DECAY + R_AAA:]
    lane = lax.broadcasted_iota(jnp.int32, lo.shape, 1)
    wa = _dot(jnp.where(lane < R_DECAY, jnp.tanh(lo), lo), wwa_ref[...])
    logw = -math.exp(-0.5) * _sigmoid(w0 + wa[:, :D_RWKV])
    a = _sigmoid(a0 + wa[:, D_RWKV:])
    g = _dot(_sigmoid(g_lo), g2_ref[...])
    kk = k * k_k
    kk = kk * lax.rsqrt(jnp.maximum(_segsum(kk * kk, ones), 1e-24))
    k2 = k * (1.0 + (a - 1.0) * k_a)
    return r, logw, k2, v, kk, a, g


def _rwkv_step(S, r, w, k, v, kk, a, eye):
    sa = -jnp.sum(S * kk, axis=1, keepdims=True)
    vcol = jnp.sum(eye * v, axis=1, keepdims=True)
    S = S * w + sa * (kk * a) + vcol * k
    ycol = jnp.sum(S * r, axis=1, keepdims=True)
    yrow = jnp.sum(eye * ycol, axis=0, keepdims=True)
    return S, yrow


def _rwkv_post(y, r, k2, v, g, r_k, ln_g, ln_b, ones):
    inv = 1.0 / RWKV_HEAD
    mean = _segsum(y, ones) * inv
    yc = y - mean
    var = _segsum(yc * yc, ones) * inv
    yn = yc * lax.rsqrt(var + GN_EPS) * ln_g + ln_b
    bonus = _segsum(r * k2 * r_k, ones) * v
    return (yn + bonus) * g


def _eye64():
    r = lax.broadcasted_iota(jnp.int32, (RWKV_HEAD, RWKV_HEAD), 0)
    c = lax.broadcasted_iota(jnp.int32, (RWKV_HEAD, RWKV_HEAD), 1)
    return jnp.where(r == c, 1.0, 0.0).astype(F32)


def _pool_project(d, wp_ref, scale):
    return _blockdiag_dot(d, wp_ref) * scale


def _gmlp_uz(z2, ln_g, ln_b):
    u = _gelu(z2[:, :D_GMLP])
    v = _gelu(z2[:, D_GMLP:])
    mu = jnp.mean(v, axis=-1, keepdims=True)
    vc = v - mu
    var = jnp.mean(vc * vc, axis=-1, keepdims=True)
    return u, vc * lax.rsqrt(var + 1e-5) * ln_g + ln_b


def _p_lru_body(x_ref, g_ref, w_ref, cw_ref, cb_ref, wra_ref, bra_ref, wix_ref, bix_ref, lam_ref,
                y_ref, convo_ref, ho_ref, hist_ref, h_ref):
    @pl.when(pl.program_id(1) == 0)
    def _():
        hist_ref[...] = jnp.zeros_like(hist_ref)
        h_ref[...] = jnp.zeros_like(h_ref)

    z = _dot(_rms(x_ref[...], g_ref[...]), w_ref[...])
    xa = z[:, :D_LRU]
    ga = z[:, D_LRU:]
    tm = xa.shape[0]
    hist = hist_ref[...]
    cw = cw_ref[...]
    xc = cb_ref[...] + cw[3:4] * xa
    for s in (1, 2, 3):
        xc = xc + cw[3 - s:4 - s] * _shift_rows(hist, xa, s)
    a, b = _lru_gates(xc, wra_ref, bra_ref, wix_ref, bix_ref, lam_ref)
    row = lax.broadcasted_iota(jnp.int32, a.shape, 0)
    s = 1
    while s < tm:
        m = row >= s
        b = jnp.where(m, a * pltpu.roll(b, s, 0) + b, b)
        a = jnp.where(m, a * pltpu.roll(a, s, 0), a)
        s *= 2
    h = a * h_ref[SUBLANES - 1:SUBLANES, :] + b
    y_ref[...] = (h * _gelu(ga)).astype(y_ref.dtype)
    hist_ref[...] = xa[tm - SUBLANES:]
    h_ref[...] = h[tm - SUBLANES:]
    convo_ref[...] = xa[tm - SUBLANES:]
    ho_ref[...] = h[tm - SUBLANES:]


def _pair_blockdiag_rows(x):
    lo = lax.broadcasted_iota(jnp.int32, x.shape, 1) < RWKV_HEAD
    z = jnp.zeros_like(x)
    return jnp.concatenate([jnp.where(lo, x, z), jnp.where(lo, z, x)], axis=0)


def _p_rwkv_body(x_ref, g_ref, w_ref, mu_ref, wwa_ref, w0_ref, a0_ref, g2_ref, kk_ref, ka_ref,
                 rk_ref, lng_ref, lnb_ref,
                 y_ref, shifto_ref, so_ref,
                 hist_ref, st_ref):
    @pl.when(pl.program_id(1) == 0)
    def _():
        hist_ref[...] = jnp.zeros_like(hist_ref)
        st_ref[...] = jnp.zeros_like(st_ref)

    C = RWKV_CHUNK
    ones = _head_ones()
    p = _dot(_rms(x_ref[...], g_ref[...]), w_ref[...])
    tm = p.shape[0]
    nc = tm // C
    npair = D_RWKV // LANES
    prev = _shift_rows(hist_ref[...], p, 1)
    r, logw, k2, v, kk, a, g = _rwkv_pre(p, prev, mu_ref[...], wwa_ref, w0_ref[...], a0_ref[...],
                                          g2_ref, kk_ref[...], ka_ref[...], ones)
    hist_ref[...] = p[tm - SUBLANES:]
    shifto_ref[...] = p[tm - SUBLANES:]
    rowc = lax.broadcasted_iota(jnp.int32, logw.shape, 0) % C
    lg = logw
    s = 1
    while s < C:
        lg = lg + jnp.where(rowc >= s, pltpu.roll(lg, s, 0), 0.0)
        s *= 2
    lg_last = [lg[(c + 1) * C - 1:(c + 1) * C, :] for c in range(nc)]
    lg_end = jnp.concatenate([jnp.broadcast_to(x, (C, D_RWKV)) for x in lg_last], axis=0)
    inv_gam = jnp.exp(-lg)
    to_end = jnp.exp(lg_end - lg)
    b = kk * a
    rt_f = r * jnp.exp(lg)
    at_b, rt_b, bt_b, kt_b, v_b, bh_b, kh_b = (x.astype(BF16) for x in (
        -kk * jnp.exp(lg - logw), rt_f, b * inv_gam, k2 * inv_gam, v, b * to_end, k2 * to_end))

    r2 = lax.broadcasted_iota(jnp.int32, (C, LANES), 0)
    c2 = lax.broadcasted_iota(jnp.int32, (C, LANES), 1) % C
    strict2 = c2 < r2
    incl2 = c2 <= r2
    eye2 = jnp.where(r2 == c2, 1.0, 0.0).astype(F32)
    pair2 = (r2 // 2) == (c2 // 2)
    levels = []
    nb = 2
    while nb < C:
        levels.append(((r2 // (2 * nb)) == (c2 // (2 * nb))) & ((r2 // nb) != (c2 // nb)))
        nb *= 2
    rr = lax.broadcasted_iota(jnp.int32, (LANES, LANES), 0)
    cc = lax.broadcasted_iota(jnp.int32, (LANES, LANES), 1)
    same_head = (rr // RWKV_HEAD) == (cc // RWKV_HEAD)
    eye128 = jnp.where(rr == cc, 1.0, 0.0).astype(F32)
    tn = (((0,), (0,)), ((), ()))
    nt_ = (((1,), (1,)), ((), ()))
    bd = _pair_blockdiag_rows

    def mm(x, y):
        return jnp.dot(x, y, preferred_element_type=F32)

    def blk(x, c, q):
        return x[c * C:(c + 1) * C, q * LANES:(q + 1) * LANES]

    P = [(c, q) for c in range(nc) for q in range(npair)]
    N = range(len(P))
    at, rt, bt, kt, vv, bh, kh = ([blk(x, c, q) for c, q in P] for x in
                                  (at_b, rt_b, bt_b, kt_b, v_b, bh_b, kh_b))
    ar = [jnp.concatenate([at[n], rt[n]], axis=0) for n in N]
    pb = [lax.dot_general(ar[n], bd(bt[n]), nt_, preferred_element_type=F32) for n in N]
    pk = [lax.dot_general(ar[n], bd(kt[n]), nt_, preferred_element_type=F32) for n in N]
    lab = [jnp.where(strict2, x[:C], 0.0) for x in pb]
    mrb = [jnp.where(incl2, x[C:], 0.0).astype(BF16) for x in pb]
    lak = [jnp.where(strict2, x[:C], 0.0).astype(BF16) for x in pk]
    mrk = [jnp.where(incl2, x[C:], 0.0).astype(BF16) for x in pk]
    v_bd = [bd(x) for x in vv]
    lak_v = [mm(lak[n], v_bd[n]) for n in N]
    T = [eye2 + jnp.where(pair2, x, 0.0) for x in lab]
    for mk in levels:
        tb = [x.astype(BF16) for x in T]
        u = [mm(tb[n], bd(jnp.where(mk, lab[n], 0.0).astype(BF16))).astype(BF16) for n in N]
        T = [T[n] + mm(u[n], bd(tb[n])) for n in N]
    tb = [x.astype(BF16) for x in T]
    tab = [mm(tb[n], bd(at[n])).astype(BF16) for n in N]
    cv = [mm(tb[n], bd(lak_v[n].astype(BF16))).astype(BF16) for n in N]
    g_bd = [jnp.where(same_head, lax.dot_general(bh[n], tab[n], tn, preferred_element_type=F32), 0.0)
            .astype(BF16) for n in N]
    q_bd = [jnp.where(same_head, lax.dot_general(jnp.concatenate([bh[n], kh[n]], axis=0),
                                                 jnp.concatenate([cv[n], vv[n]], axis=0), tn,
                                                 preferred_element_type=F32), 0.0) for n in N]
    ry = [(blk(rt_f, c, q) + mm(mrb[n], bd(tab[n]))).astype(BF16) for n, (c, q) in enumerate(P)]
    yc = [mm(jnp.concatenate([mrb[n], mrk[n]], axis=1),
             jnp.concatenate([bd(cv[n]), v_bd[n]], axis=0)) for n in N]
    gcol = [jnp.broadcast_to(jnp.sum(eye128 * jnp.exp(lg_last[c][:, q * LANES:(q + 1) * LANES]),
                                     axis=1, keepdims=True), (LANES, LANES)) for c, q in P]

    st = [st_ref[q] for q in range(npair)]
    y_rows = []
    for c in range(nc):
        sb = [x.astype(BF16) for x in st]
        ns = [c * npair + q for q in range(npair)]
        gs = [mm(g_bd[n], sb[q]) for q, n in enumerate(ns)]
        ys = [mm(ry[n], sb[q]) for q, n in enumerate(ns)]
        st = [st[q] * gcol[n] + gs[q] + q_bd[n] for q, n in enumerate(ns)]
        y_rows.append(jnp.concatenate([ys[q] + yc[n] for q, n in enumerate(ns)], axis=1))
    y = jnp.concatenate(y_rows, axis=0)
    yb = _rwkv_post(y, r, k2, v, g, rk_ref[...], lng_ref[...], lnb_ref[...], ones)
    y_ref[...] = yb.astype(y_ref.dtype)
    for q in range(npair):
        st_ref[q] = st[q]
        so_ref[2 * q] = st[q][:RWKV_HEAD, :RWKV_HEAD].T
        so_ref[2 * q + 1] = st[q][RWKV_HEAD:, RWKV_HEAD:].T


def _p_pool_body(x_ref, g_ref, w_ref, wp_ref, sc_ref, y_ref, poolo_ref, hist_ref):
    t = pl.program_id(1)

    @pl.when(t == 0)
    def _():
        hist_ref[...] = jnp.zeros_like(hist_ref)

    pc = _dot(_rms(x_ref[...], g_ref[...]), w_ref[...])
    tm = pc.shape[0]
    X = jnp.concatenate([hist_ref[...], pc], axis=0)
    pos = t * tm + lax.broadcasted_iota(jnp.int32, (tm, POOL_GW), 0)
    ds = []
    for gi, win in enumerate(POOL_WINDOWS):
        sl = slice(gi * POOL_GW, (gi + 1) * POOL_GW)
        s = X[:, sl]
        sh = 1
        while sh < win:
            s = s + pltpu.roll(s, sh, 0)
            sh *= 2
        cnt = jnp.minimum(pos + 1, win).astype(F32)
        ds.append(s[POOL_HIST:] / cnt - pc[:, sl])
    y = _pool_project(jnp.concatenate(ds, axis=-1), wp_ref, sc_ref[...])
    y_ref[...] = y.astype(y_ref.dtype)
    hist_ref[...] = pc[tm - POOL_HIST:]
    poolo_ref[...] = pc[tm - POOL_HIST:]


def _p_gmlp_body(x_ref, g_ref, w_ref, lng_ref, lnb_ref, ws_ref, bsb_ref, y_ref):
    z2 = _dot(_rms(x_ref[...], g_ref[...]), w_ref[...])
    tm = z2.shape[0]
    u, zv = _gmlp_uz(z2, lng_ref[...], lnb_ref[...])
    tril = (lax.broadcasted_iota(jnp.int32, (CHUNK, CHUNK), 0)
            >= lax.broadcasted_iota(jnp.int32, (CHUNK, CHUNK), 1))
    ws = [jnp.where(tril, ws_ref[gi], 0.0).astype(BF16) for gi in range(GMLP_GROUPS)]
    zb = zv.astype(BF16)
    rows = []
    for c in range(tm // CHUNK):
        cols = [jnp.dot(ws[gi], zb[c * CHUNK:(c + 1) * CHUNK, gi * LANES:(gi + 1) * LANES],
                        preferred_element_type=F32) for gi in range(GMLP_GROUPS)]
        rows.append(jnp.concatenate(cols, axis=-1) + bsb_ref[...])
    s = jnp.concatenate(rows, axis=0)
    y_ref[...] = (u * s).astype(y_ref.dtype)


def _merge_body(x_ref, g_ref, wg_ref, ya_ref, yb_ref, yc_ref, yd_ref, wp_ref, wo_ref, o_ref):
    x = x_ref[...]
    xn = _rms(x, g_ref[...]).astype(BF16)
    merged = None
    for i, y_ref in enumerate((ya_ref, yb_ref, yc_ref, yd_ref)):
        gate = _sigmoid(jnp.dot(xn, wg_ref[:, i * D_MODEL:(i + 1) * D_MODEL],
                                preferred_element_type=F32))
        term = gate * jnp.dot(y_ref[...], wp_ref[i], preferred_element_type=F32)
        merged = term if merged is None else merged + term
    o_ref[...] = x + _dot(merged, wo_ref[...])


def _softmax_rows(s):
    e = jnp.exp(s - jnp.max(s, axis=-1, keepdims=True))
    return e / jnp.sum(e, axis=-1, keepdims=True)


def _p_xattn_body(x_ref, g_ref, wq_ref, k_ref, v_ref, wo_ref, o_ref):
    x = x_ref[...]
    q = _dot(_rms(x, g_ref[...]), wq_ref[...]).astype(BF16)
    kb = k_ref[...].astype(BF16)
    vb = v_ref[...].astype(BF16)
    outs = []
    for h in range(X_HEADS):
        sl = slice(h * X_HEAD_DIM, (h + 1) * X_HEAD_DIM)
        s = lax.dot_general(q[:, sl], kb[:, sl], (((1,), (1,)), ((), ())),
                            preferred_element_type=F32) * (X_HEAD_DIM ** -0.5)
        outs.append(jnp.dot(_softmax_rows(s).astype(BF16), vb[:, sl], preferred_element_type=F32))
    o_ref[...] = x + _dot(jnp.concatenate(outs, axis=-1), wo_ref[...])


def _ffn_conv(hist, h, cw, cb):
    return cb + cw[2:3] * h + cw[1:2] * _shift_rows(hist, h, 1) + cw[0:1] * _shift_rows(hist, h, 2)


def _p_ffn_body(x_ref, g_ref, wug_ref, wuv_ref, cwg_ref, cwv_ref, cbg_ref, cbv_ref, wd_ref,
                o_ref, ffo_ref, hist_ref):
    @pl.when(pl.program_id(1) == 0)
    def _():
        hist_ref[...] = jnp.zeros_like(hist_ref)

    x = x_ref[...]
    tm = x.shape[0]
    xn = _rms(x, g_ref[...]).astype(BF16)
    acc = x
    for j in range(D_FF // FF_CW):
        cs = slice(j * FF_CW, (j + 1) * FF_CW)
        vs = slice(D_FF + j * FF_CW, D_FF + (j + 1) * FF_CW)
        hg = jnp.dot(xn, wug_ref[:, cs], preferred_element_type=F32)
        hv = jnp.dot(xn, wuv_ref[:, cs], preferred_element_type=F32)
        cg = _ffn_conv(hist_ref[:, cs], hg, cwg_ref[:, cs], cbg_ref[:, cs])
        cv = _ffn_conv(hist_ref[:, vs], hv, cwv_ref[:, cs], cbv_ref[:, cs])
        acc = acc + _dot(_gelu(cg) * cv, wd_ref[cs, :])
        hist_ref[:, cs] = hg[tm - SUBLANES:]
        hist_ref[:, vs] = hv[tm - SUBLANES:]
    o_ref[...] = acc
    ffo_ref[...] = hist_ref[...]


def _memkv_body(m_ref, g_ref, wk_ref, wv_ref, k_ref, v_ref):
    mn = _rms(m_ref[...], g_ref[...]).astype(BF16)
    k_ref[...] = jnp.dot(mn, wk_ref[...], preferred_element_type=F32)
    v_ref[...] = jnp.dot(mn, wv_ref[...], preferred_element_type=F32)


def _final_norm_body(x_ref, g_ref, o_ref):
    o_ref[...] = _rms(x_ref[...], g_ref[...])


def _s_mix_body(x_ref, g_ref, wa_ref, wb_ref, wc_ref, wd_ref,
                lconv_ref, lh_ref, cw_ref, cb_ref, wra_ref, bra_ref, wix_ref, bix_ref, lam_ref,
                shift_ref, mu_ref, wwa_ref, w0_ref, a0_ref, g2_ref, kk_ref, ka_ref,
                pool_ref, wp_ref, psc_ref,
                lng_ref, lnb_ref, wsd_ref, bsr_ref,
                ya_ref, xa_ref, h_ref, yc_ref, pc_ref, yd_ref, zv_ref, p_ref,
                r_ref, k2_ref, v_ref, gg_ref, rh_ref, wh_ref, kh_ref, vh_ref, kkh_ref, ah_ref):
    xn = _rms(x_ref[...], g_ref[...]).astype(BF16)
    z = jnp.dot(xn, wa_ref[...], preferred_element_type=F32)
    xa = z[:, :D_LRU]
    cw = cw_ref[...]
    xc = cb_ref[...] + cw[0:1] * lconv_ref[0] + cw[1:2] * lconv_ref[1] + cw[2:3] * lconv_ref[2] + cw[3:4] * xa
    a, b = _lru_gates(xc, wra_ref, bra_ref, wix_ref, bix_ref, lam_ref)
    h = a * lh_ref[...] + b
    ya_ref[...] = (h * _gelu(z[:, D_LRU:])).astype(ya_ref.dtype)
    xa_ref[...] = xa
    h_ref[...] = h
    ones = _head_ones()
    p = jnp.dot(xn, wb_ref[...], preferred_element_type=F32)
    r, logw, k2, v, kk, aa, g = _rwkv_pre(p, shift_ref[...], mu_ref[...], wwa_ref, w0_ref[...],
                                           a0_ref[...], g2_ref, kk_ref[...], ka_ref[...], ones)
    decay = jnp.exp(logw)
    p_ref[...] = p
    r_ref[...] = r
    k2_ref[...] = k2
    v_ref[...] = v
    gg_ref[...] = g
    for ref, val in ((rh_ref, r), (wh_ref, decay), (kh_ref, k2), (vh_ref, v), (kkh_ref, kk), (ah_ref, aa)):
        for hd in range(RWKV_HEADS):
            ref[hd] = val[:, hd * RWKV_HEAD:(hd + 1) * RWKV_HEAD]
    pc = jnp.dot(xn, wc_ref[...], preferred_element_type=F32)
    ds = []
    for gi, win in enumerate(POOL_WINDOWS):
        sl = slice(gi * POOL_GW, (gi + 1) * POOL_GW)
        s = pc[:, sl]
        for j in range(POOL_BUF - (win - 1), POOL_BUF):
            s = s + pool_ref[j][:, sl]
        ds.append(s / float(min(PAST_LEN + 1, win)) - pc[:, sl])
    yc_ref[...] = _pool_project(jnp.concatenate(ds, axis=-1), wp_ref, psc_ref[...]).astype(yc_ref.dtype)
    pc_ref[...] = pc
    u, zv = _gmlp_uz(jnp.dot(xn, wd_ref[...], preferred_element_type=F32), lng_ref[...], lnb_ref[...])
    yd_ref[...] = (u * (wsd_ref[...] * zv + bsr_ref[...])).astype(yd_ref.dtype)
    zv_ref[...] = zv


def _s_rwkv_body(s_ref, rh_ref, wh_ref, kh_ref, vh_ref, kkh_ref, ah_ref,
                 r_ref, k2_ref, v_ref, g_ref, rk_ref, lng_ref, lnb_ref,
                 y_ref, so_ref, y_sc):
    eye = _eye64()
    bb = s_ref.shape[0]

    def body(b, carry):
        for h in range(RWKV_HEADS):
            rows = [ref[h, pl.ds(b, 1), :] for ref in (rh_ref, wh_ref, kh_ref, vh_ref, kkh_ref, ah_ref)]
            S, yrow = _rwkv_step(s_ref[b, h], *rows, eye)
            so_ref[b, h] = S
            y_sc[h, pl.ds(b, 1), :] = yrow
        return carry

    lax.fori_loop(0, bb, body, 0)
    y = jnp.concatenate([y_sc[h] for h in range(RWKV_HEADS)], axis=-1)
    yb = _rwkv_post(y, r_ref[...], k2_ref[...], v_ref[...], g_ref[...], rk_ref[...], lng_ref[...],
                    lnb_ref[...], _head_ones())
    y_ref[...] = yb.astype(y_ref.dtype)


def _s_xattn_body(x_ref, g_ref, wq_ref, k_ref, v_ref, wo_ref, o_ref, q_sc, a_sc):
    i = pl.program_id(0)
    bb = k_ref.shape[0]

    @pl.when(i == 0)
    def _():
        q_sc[...] = _dot(_rms(x_ref[...], g_ref[...]), wq_ref[...])

    halves = X_HEAD_DIM // LANES

    def q_lanes(c, h):
        return slice(h * X_HEAD_DIM + c * LANES, h * X_HEAD_DIM + (c + 1) * LANES)

    for j in range(bb):
        row = pl.ds(i * bb + j, 1)
        qrow = q_sc[row, :]
        q8 = jnp.concatenate([qrow[:, q_lanes(c, h)] for c in range(halves) for h in range(X_HEADS)], axis=0)
        k3 = k_ref[j].reshape(N_MEM, SUBLANES, LANES)
        part = jnp.sum(k3 * q8[None], axis=-1, keepdims=True)
        part = jnp.broadcast_to(part, k3.shape)
        s = (part + pltpu.roll(part, X_HEADS, 1)) * (X_HEAD_DIM ** -0.5)
        e = jnp.exp(s - jnp.max(s, axis=0, keepdims=True))
        pr = e / jnp.sum(e, axis=0, keepdims=True)
        o8 = jnp.sum(pr * v_ref[j].reshape(N_MEM, SUBLANES, LANES), axis=0)
        a_sc[row, :] = jnp.concatenate([o8[c * X_HEADS + h:c * X_HEADS + h + 1, :]
                                        for h in range(X_HEADS) for c in range(halves)], axis=1)

    @pl.when(i == pl.num_programs(0) - 1)
    def _():
        o_ref[...] = x_ref[...] + _dot(a_sc[...], wo_ref[...])


def _s_ffn_body(x_ref, g_ref, wug_ref, wuv_ref, sg_ref, sv_ref, cwg_ref, cwv_ref, cbg_ref, cbv_ref,
                wd_ref, o_ref, hg_ref, hv_ref, acc_ref):
    j = pl.program_id(0)

    @pl.when(j == 0)
    def _():
        acc_ref[...] = x_ref[...]

    xn = _rms(x_ref[...], g_ref[...]).astype(BF16)
    hg = jnp.dot(xn, wug_ref[...], preferred_element_type=F32)
    hv = jnp.dot(xn, wuv_ref[...], preferred_element_type=F32)
    cwg = cwg_ref[...]
    cwv = cwv_ref[...]
    cg = cbg_ref[...] + cwg[0:1] * sg_ref[0] + cwg[1:2] * sg_ref[1] + cwg[2:3] * hg
    cv = cbv_ref[...] + cwv[0:1] * sv_ref[0] + cwv[1:2] * sv_ref[1] + cwv[2:3] * hv
    acc_ref[...] += _dot(_gelu(cg) * cv, wd_ref[...])
    hg_ref[...] = hg
    hv_ref[...] = hv

    @pl.when(j == pl.num_programs(0) - 1)
    def _():
        o_ref[...] = acc_ref[...]


def _params(n_grid, vmem_mb):
    return pltpu.CompilerParams(dimension_semantics=("arbitrary",) * n_grid,
                                vmem_limit_bytes=vmem_mb << 20)


def _whole(shape):
    return pl.BlockSpec(tuple(shape), lambda *_: (0,) * len(shape))


def _layer(arr, l):
    shape = arr.shape[1:]
    return pl.BlockSpec((None,) + tuple(shape), lambda *_: (l,) + (0,) * len(shape))


def _sds(shape, dtype=F32):
    return jax.ShapeDtypeStruct(tuple(shape), dtype)


def _pair_blockdiag(w):
    L = w.shape[0]
    w = w.reshape(L, 4, 2, RWKV_HEAD, RWKV_HEAD)
    z = jnp.zeros_like(w[:, :, 0])
    top = jnp.concatenate([w[:, :, 0], z], axis=-1)
    bot = jnp.concatenate([z, w[:, :, 1]], axis=-1)
    return jnp.concatenate([top, bot], axis=-2)


def _vec(a):
    return a.reshape(a.shape[0], 1, -1)


def kernel(x_prompt, x_sample, state_lru_conv, state_lru_h, state_rwkv_shift, state_rwkv_S, state_pool, state_ffn_conv, cache_mem_k, cache_mem_v, mem_prompt, g_mix, w_in, lru_conv_w, lru_conv_b, lru_w_ra, lru_b_ra, lru_w_ix, lru_b_ix, lru_lambda, rwkv_mu, rwkv_w0, rwkv_w2, rwkv_a0, rwkv_a2, rwkv_g2, rwkv_k_k, rwkv_k_a, rwkv_r_k, rwkv_ln_g, rwkv_ln_b, pool_w, pool_scale, gmlp_ln_g, gmlp_ln_b, gmlp_w_s, gmlp_b_s, w_pa, w_pb, w_pc, w_pd, w_o, g_xattn, g_mem, w_xq, w_xk, w_xv, w_xo, g_ffn, w_up, ffn_conv_w, ffn_conv_b, w_down, g_final):
    B, T, D = x_prompt.shape
    SB = x_sample.shape[0]
    assert D == D_MODEL and w_in.shape == (DEPTH, D_MODEL, D_IN) and x_sample.shape[1] == 1
    assert T % TM_MIX == 0 and T % TM_RWKV == 0 and SB % S_RWKV_BB == 0 and SB % S_ATT_BB == 0

    wA = w_in[:, :, O_A:O_B].astype(BF16)
    wB = w_in[:, :, O_B:O_C].astype(BF16)
    wC = w_in[:, :, O_C:O_D].astype(BF16)
    wD = w_in[:, :, O_D:O_G].astype(BF16)
    wG = w_in[:, :, O_G:].astype(BF16)
    wra = _pair_blockdiag(lru_w_ra).astype(BF16)
    wix = _pair_blockdiag(lru_w_ix).astype(BF16)
    zer = jnp.zeros((DEPTH, R_DECAY, D_RWKV), F32)
    wwa = jnp.concatenate([jnp.concatenate([rwkv_w2, zer], axis=-1),
                           jnp.concatenate([zer, rwkv_a2], axis=-1)], axis=1).astype(BF16)
    g2 = rwkv_g2.astype(BF16)
    wpool = pool_w.astype(BF16)
    bsb = jnp.repeat(gmlp_b_s, LANES, axis=-1)
    off = PAST_LEN % CHUNK
    wsd = jnp.repeat(gmlp_w_s[:, :, off, off], LANES, axis=-1)[:, None, :]
    bsr = jnp.repeat(gmlp_b_s[:, off, :], LANES, axis=-1)[:, None, :]
    wP = jnp.stack([w_pa, w_pb, w_pc, w_pd], axis=1).astype(BF16)
    wO = w_o.astype(BF16)
    wXq, wXk, wXv, wXo = (w.astype(BF16) for w in (w_xq, w_xk, w_xv, w_xo))
    wUg = w_up[:, :, :D_FF].astype(BF16)
    wUv = w_up[:, :, D_FF:].astype(BF16)
    wDn = w_down.astype(BF16)
    cwg, cwv = ffn_conv_w[:, :, :D_FF], ffn_conv_w[:, :, D_FF:]
    cbg, cbv = _vec(ffn_conv_b[:, :D_FF]), _vec(ffn_conv_b[:, D_FF:])
    gmix, gxat, gffn, gmem = _vec(g_mix), _vec(g_xattn), _vec(g_ffn), _vec(g_mem)
    lcb, bra, bix, lam = _vec(lru_conv_b), _vec(lru_b_ra), _vec(lru_b_ix), _vec(lru_lambda)
    mu, w0, a0, kkw, kaw = _vec(rwkv_mu), _vec(rwkv_w0), _vec(rwkv_a0), _vec(rwkv_k_k), _vec(rwkv_k_a)
    rk = rwkv_r_k.reshape(DEPTH, 1, D_RWKV)
    rlg, rlb = _vec(rwkv_ln_g), _vec(rwkv_ln_b)
    psc, glg, glb = _vec(pool_scale), _vec(gmlp_ln_g), _vec(gmlp_ln_b)
    gfin = g_final.reshape(1, D_MODEL)

    def cache_rows(c):
        c = c.reshape(DEPTH, SB, N_MEM, X_HEADS, X_HEAD_DIM // LANES, LANES)
        return jnp.swapaxes(c, 3, 4).reshape(DEPTH, SB, N_MEM * SUBLANES, LANES)

    cache_k, cache_v = cache_rows(cache_mem_k), cache_rows(cache_mem_v)

    kv_shape = _sds((DEPTH, B, N_MEM, D_MODEL))
    kv_spec = pl.BlockSpec((None, None, N_MEM, D_MODEL), lambda l, b: (l, b, 0, 0))
    wl_spec = pl.BlockSpec((None, D_MODEL, D_MODEL), lambda l, b: (l, 0, 0))
    p_mk, p_mv = pl.pallas_call(
        _memkv_body, grid=(DEPTH, B),
        in_specs=[pl.BlockSpec((None, N_MEM, D_MODEL), lambda l, b: (b, 0, 0)),
                  pl.BlockSpec((None, 1, D_MODEL), lambda l, b: (l, 0, 0)), wl_spec, wl_spec],
        out_specs=[kv_spec, kv_spec], out_shape=[kv_shape, kv_shape],
        compiler_params=_params(2, 32), name="memkv")(mem_prompt, gmem, wXk, wXv)

    nt = T // TM_MIX
    ntr = T // TM_RWKV
    xt_spec = pl.BlockSpec((None, TM_MIX, D_MODEL), lambda b, t: (b, t, 0))
    yt_spec = pl.BlockSpec((None, TM_MIX, D_LRU), lambda b, t: (b, t, 0))

    def tail_spec(rows, width):
        return pl.BlockSpec((None, rows, width), lambda b, t: (b, 0, 0))

    hp = x_prompt
    hs = x_sample.reshape(SB, D_MODEL)
    outs = {k: [] for k in ("p_conv", "s_conv", "p_h", "s_h", "p_shift", "s_shift", "p_S", "s_S",
                            "p_pool", "s_pool", "p_ffn", "s_ffn", "s_cv")}

    for l in range(DEPTH):
        L = functools.partial(_layer, l=l)

        yA, p_conv8, p_h8 = pl.pallas_call(
            _p_lru_body, grid=(B, nt),
            in_specs=[xt_spec, L(gmix), L(wA), L(lru_conv_w), L(lcb), L(wra), L(bra), L(wix), L(bix), L(lam)],
            out_specs=[yt_spec, tail_spec(SUBLANES, D_LRU), tail_spec(SUBLANES, D_LRU)],
            out_shape=[_sds((B, T, D_LRU), BF16), _sds((B, SUBLANES, D_LRU)), _sds((B, SUBLANES, D_LRU))],
            scratch_shapes=[pltpu.VMEM((SUBLANES, D_LRU), F32), pltpu.VMEM((SUBLANES, D_LRU), F32)],
            compiler_params=_params(2, 48), name="p_lru")(
                hp, gmix, wA, lru_conv_w, lcb, wra, bra, wix, bix, lam)

        yB, p_shift8, p_S = pl.pallas_call(
            _p_rwkv_body, grid=(B, ntr),
            in_specs=[pl.BlockSpec((None, TM_RWKV, D_MODEL), lambda b, t: (b, t, 0)),
                      L(gmix), L(wB), L(mu), L(wwa), L(w0), L(a0), L(g2), L(kkw), L(kaw), L(rk), L(rlg), L(rlb)],
            out_specs=[pl.BlockSpec((None, TM_RWKV, D_RWKV), lambda b, t: (b, t, 0)),
                       tail_spec(SUBLANES, D_RWKV_IN),
                       pl.BlockSpec((None, RWKV_HEADS, RWKV_HEAD, RWKV_HEAD), lambda b, t: (b, 0, 0, 0))],
            out_shape=[_sds((B, T, D_RWKV), BF16), _sds((B, SUBLANES, D_RWKV_IN)),
                       _sds((B, RWKV_HEADS, RWKV_HEAD, RWKV_HEAD))],
            scratch_shapes=[pltpu.VMEM((SUBLANES, D_RWKV_IN), F32),
                            pltpu.VMEM((D_RWKV // LANES, LANES, LANES), F32)],
            compiler_params=_params(2, 48), name="p_rwkv")(
                hp, gmix, wB, mu, wwa, w0, a0, g2, kkw, kaw, rk, rlg, rlb)

        yC, p_pool16 = pl.pallas_call(
            _p_pool_body, grid=(B, nt),
            in_specs=[xt_spec, L(gmix), L(wC), L(wpool), L(psc)],
            out_specs=[yt_spec, tail_spec(POOL_HIST, D_POOL)],
            out_shape=[_sds((B, T, D_POOL), BF16), _sds((B, POOL_HIST, D_POOL))],
            scratch_shapes=[pltpu.VMEM((POOL_HIST, D_POOL), F32)],
            compiler_params=_params(2, 48), name="p_pool")(hp, gmix, wC, wpool, psc)

        yD = pl.pallas_call(
            _p_gmlp_body, grid=(B, nt),
            in_specs=[xt_spec, L(gmix), L(wD), L(glg), L(glb), L(gmlp_w_s), L(bsb)],
            out_specs=yt_spec, out_shape=_sds((B, T, D_GMLP), BF16),
            compiler_params=_params(2, 48), name="p_gmlp")(hp, gmix, wD, glg, glb, gmlp_w_s, bsb)

        def merge(x2d, ys, tm):
            rows = x2d.shape[0]
            xs = pl.BlockSpec((tm, D_MODEL), lambda i: (i, 0))
            ysp = pl.BlockSpec((tm, D_LRU), lambda i: (i, 0))
            return pl.pallas_call(
                _merge_body, grid=(rows // tm,),
                in_specs=[xs, L(gmix), L(wG), ysp, ysp, ysp, ysp, L(wP), L(wO)],
                out_specs=xs, out_shape=_sds((rows, D_MODEL)),
                compiler_params=_params(1, 56), name="merge")(x2d, gmix, wG, *ys, wP, wO)

        hp = merge(hp.reshape(B * T, D_MODEL), [y.reshape(B * T, -1) for y in (yA, yB, yC, yD)],
                   TM_MIX).reshape(B, T, D_MODEL)

        kvb_spec = pl.BlockSpec((None, None, N_MEM, D_MODEL), lambda b, t: (l, b, 0, 0))
        hp = pl.pallas_call(
            _p_xattn_body, grid=(B, nt),
            in_specs=[xt_spec, L(gxat), L(wXq), kvb_spec, kvb_spec, L(wXo)],
            out_specs=xt_spec, out_shape=_sds((B, T, D_MODEL)),
            compiler_params=_params(2, 48), name="p_xattn")(hp, gxat, wXq, p_mk, p_mv, wXo)

        hp, p_ffn8 = pl.pallas_call(
            _p_ffn_body, grid=(B, nt),
            in_specs=[xt_spec, L(gffn), L(wUg), L(wUv), L(cwg), L(cwv), L(cbg), L(cbv), L(wDn)],
            out_specs=[xt_spec, tail_spec(SUBLANES, 2 * D_FF)],
            out_shape=[_sds((B, T, D_MODEL)), _sds((B, SUBLANES, 2 * D_FF))],
            scratch_shapes=[pltpu.VMEM((SUBLANES, 2 * D_FF), F32)],
            compiler_params=_params(2, 56), name="p_ffn")(hp, gffn, wUg, wUv, cwg, cwv, cbg, cbv, wDn)

        outs["p_conv"].append(p_conv8[:, SUBLANES - 3:])
        outs["p_h"].append(p_h8[:, SUBLANES - 1])
        outs["p_shift"].append(p_shift8[:, SUBLANES - 1:])
        outs["p_S"].append(p_S)
        outs["p_pool"].append(p_pool16[:, POOL_HIST - POOL_BUF:])
        outs["p_ffn"].append(p_ffn8[:, SUBLANES - 2:])

        lconv = jnp.swapaxes(state_lru_conv[l], 0, 1)
        spool = jnp.swapaxes(state_pool[l], 0, 1)
        sffn = jnp.swapaxes(state_ffn_conv[l], 0, 1)
        shift = state_rwkv_shift[l].reshape(SB, D_RWKV_IN)
        row512 = _sds((SB, D_LRU))
        heads = _sds((RWKV_HEADS, SB, RWKV_HEAD))
        mix_in = [hs, gmix[l], wA[l], wB[l], wC[l], wD[l],
                  lconv, state_lru_h[l], lru_conv_w[l], lcb[l], wra[l], bra[l], wix[l], bix[l], lam[l],
                  shift, mu[l], wwa[l], w0[l], a0[l], g2[l], kkw[l], kaw[l],
                  spool, wpool[l], psc[l],
                  glg[l], glb[l], wsd[l], bsr[l]]
        mix_out = [_sds((SB, D_LRU), BF16), row512, row512, _sds((SB, D_POOL), BF16), row512,
                   _sds((SB, D_GMLP), BF16), row512, _sds((SB, D_RWKV_IN)),
                   row512, row512, row512, row512, heads, heads, heads, heads, heads, heads]
        (yA, s_xa, s_h, yC, s_pc, yD, s_zv, s_p, s_r, s_k2, s_v, s_g,
         rh, wh, kh, vh, kkh, ah) = pl.pallas_call(
            _s_mix_body, grid=(1,),
            in_specs=[_whole(a.shape) for a in mix_in],
            out_specs=[_whole(o.shape) for o in mix_out], out_shape=mix_out,
            compiler_params=_params(1, 56), name="s_mix")(*mix_in)

        hd_spec = pl.BlockSpec((RWKV_HEADS, S_RWKV_BB, RWKV_HEAD), lambda i: (0, i, 0))
        rb_spec = pl.BlockSpec((S_RWKV_BB, D_RWKV), lambda i: (i, 0))
        st_spec = pl.BlockSpec((None, S_RWKV_BB, RWKV_HEADS, RWKV_HEAD, RWKV_HEAD), lambda i: (l, i, 0, 0, 0))
        so_spec = pl.BlockSpec((S_RWKV_BB, RWKV_HEADS, RWKV_HEAD, RWKV_HEAD), lambda i: (i, 0, 0, 0))
        yB, s_S = pl.pallas_call(
            _s_rwkv_body, grid=(SB // S_RWKV_BB,),
            in_specs=[st_spec] + [hd_spec] * 6 + [rb_spec] * 4 + [L(rk), L(rlg), L(rlb)],
            out_specs=[rb_spec, so_spec],
            out_shape=[_sds((SB, D_RWKV), BF16), _sds((SB, RWKV_HEADS, RWKV_HEAD, RWKV_HEAD))],
            scratch_shapes=[pltpu.VMEM((RWKV_HEADS, S_RWKV_BB, RWKV_HEAD), F32)],
            compiler_params=_params(1, 48), name="s_rwkv")(
                state_rwkv_S, rh, wh, kh, vh, kkh, ah, s_r, s_k2, s_v, s_g, rk, rlg, rlb)

        hs = merge(hs, [yA, yB, yC, yD], SB)

        kc_spec = pl.BlockSpec((None, S_ATT_BB, N_MEM * SUBLANES, LANES), lambda i: (l, i, 0, 0))
        xs_spec = _whole((SB, D_MODEL))
        hs = pl.pallas_call(
            _s_xattn_body, grid=(SB // S_ATT_BB,),
            in_specs=[xs_spec, L(gxat), L(wXq), kc_spec, kc_spec, L(wXo)],
            out_specs=xs_spec, out_shape=_sds((SB, D_MODEL)),
            scratch_shapes=[pltpu.VMEM((SB, D_MODEL), F32), pltpu.VMEM((SB, D_MODEL), F32)],
            compiler_params=_params(1, 56), name="s_xattn")(
                hs, gxat, wXq, cache_k, cache_v, wXo)

        ncf = D_FF // FF_CW
        wu_spec = pl.BlockSpec((None, D_MODEL, FF_CW), lambda j: (l, 0, j))
        sg_spec = pl.BlockSpec((2, SB, FF_CW), lambda j: (0, 0, j))
        sv_spec = pl.BlockSpec((2, SB, FF_CW), lambda j: (0, 0, j + ncf))
        cw_spec = pl.BlockSpec((None, 3, FF_CW), lambda j: (l, 0, j))
        cb_spec = pl.BlockSpec((None, 1, FF_CW), lambda j: (l, 0, j))
        hh_spec = pl.BlockSpec((SB, FF_CW), lambda j: (0, j))
        hs, s_hg, s_hv = pl.pallas_call(
            _s_ffn_body, grid=(ncf,),
            in_specs=[xs_spec, L(gffn), wu_spec, wu_spec, sg_spec, sv_spec, cw_spec, cw_spec, cb_spec, cb_spec,
                      pl.BlockSpec((None, FF_CW, D_MODEL), lambda j: (l, j, 0))],
            out_specs=[xs_spec, hh_spec, hh_spec],
            out_shape=[_sds((SB, D_MODEL)), _sds((SB, D_FF)), _sds((SB, D_FF))],
            scratch_shapes=[pltpu.VMEM((SB, D_MODEL), F32)],
            compiler_params=_params(1, 48), name="s_ffn")(
                hs, gffn, wUg, wUv, sffn, sffn, cwg, cwv, cbg, cbv, wDn)

        outs["s_conv"].append(jnp.concatenate([state_lru_conv[l][:, 1:], s_xa[:, None]], axis=1))
        outs["s_h"].append(s_h)
        outs["s_shift"].append(s_p[:, None])
        outs["s_S"].append(s_S)
        outs["s_pool"].append(jnp.concatenate([state_pool[l][:, 1:], s_pc[:, None]], axis=1))
        outs["s_ffn"].append(jnp.concatenate(
            [state_ffn_conv[l][:, 1:], jnp.concatenate([s_hg, s_hv], axis=-1)[:, None]], axis=1))
        outs["s_cv"].append(s_zv[:, None])

    def final_norm(x2d, tm):
        rows = x2d.shape[0]
        spec = pl.BlockSpec((tm, D_MODEL), lambda i: (i, 0))
        return pl.pallas_call(_final_norm_body, grid=(rows // tm,),
                              in_specs=[spec, _whole((1, D_MODEL))], out_specs=spec,
                              out_shape=_sds((rows, D_MODEL)),
                              compiler_params=_params(1, 32), name="final_norm")(x2d, gfin)

    y_prompt = final_norm(hp.reshape(B * T, D_MODEL), 1024).reshape(B, T, D_MODEL)
    y_sample = final_norm(hs, SB).reshape(SB, 1, D_MODEL)
    stk = {k: jnp.stack(v, axis=0) for k, v in outs.items()}
    mem_shape = (DEPTH, B, N_MEM, X_HEADS, X_HEAD_DIM)
    return (y_prompt, y_sample,
            stk["p_conv"], stk["s_conv"], stk["p_h"], stk["s_h"],
            stk["p_shift"], stk["s_shift"], stk["p_S"], stk["s_S"],
            stk["p_pool"], stk["s_pool"], stk["p_ffn"], stk["s_ffn"],
            p_mk.reshape(mem_shape), p_mv.reshape(mem_shape), stk["s_cv"])
```

```python
import functools
import math

import jax
import jax.numpy as jnp
from jax import lax
from jax.experimental import pallas as pl
from jax.experimental.pallas import tpu as pltpu

F32 = jnp.float32
BF16 = jnp.bfloat16

SUBLANES = 8
LANES = 128

D_MODEL = 1024
DEPTH = 4
N_MEM = 256
D_LRU = 512
LRU_C = 8.0
D_RWKV = 512
RWKV_HEAD = 64
RWKV_HEADS = D_RWKV // RWKV_HEAD
R_DECAY = 64
R_AAA = 64
R_GATE = 128
D_RWKV_IN = 3 * D_RWKV + R_DECAY + R_AAA + R_GATE
GN_EPS = 64e-5
D_POOL = 512
POOL_WINDOWS = (2, 4, 8, 16)
POOL_GW = D_POOL // len(POOL_WINDOWS)
POOL_BUF = max(POOL_WINDOWS) - 1
POOL_HIST = 16
D_GMLP = 512
GMLP_GROUPS = 4
CHUNK = 128
N_BRANCH = 4
X_HEADS = 4
X_HEAD_DIM = D_MODEL // X_HEADS
D_FF = 3 * D_MODEL
EPS = 1e-6
PAST_LEN = 16384

O_A = 0
O_B = 2 * D_LRU
O_C = O_B + D_RWKV_IN
O_D = O_C + D_POOL
O_G = O_D + 2 * D_GMLP
D_IN = O_G + N_BRANCH * D_MODEL

TM_MIX = 512
TM_RWKV = 256
FF_CW = 768
RWKV_CHUNK = 64
S_ATT_BB = 8


def _dot(a, b):
    return jnp.dot(a.astype(BF16), b.astype(BF16), preferred_element_type=F32)


def _rms(x, g):
    return x * lax.rsqrt(jnp.mean(x * x, axis=-1, keepdims=True) + EPS) * g


def _gelu(x):
    c = math.sqrt(2.0 / math.pi)
    return 0.5 * x * (1.0 + jnp.tanh(c * (x + 0.044715 * (x * x * x))))


def _softplus(x):
    return jnp.maximum(x, 0.0) + jnp.log1p(jnp.exp(-jnp.abs(x)))


def _sigmoid(x):
    return jax.nn.sigmoid(x)


def _shift_rows(hist, cur, s):
    n = hist.shape[0]
    return pltpu.roll(jnp.concatenate([hist, cur], axis=0), s, 0)[n:]


def _head_ones():
    r = lax.broadcasted_iota(jnp.int32, (LANES, LANES), 0) // RWKV_HEAD
    c = lax.broadcasted_iota(jnp.int32, (LANES, LANES), 1) // RWKV_HEAD
    return jnp.where(r == c, 1.0, 0.0).astype(BF16)


def _segsum(x, ones):
    xb = x.astype(BF16)
    return jnp.concatenate([jnp.dot(xb[:, p * LANES:(p + 1) * LANES], ones, preferred_element_type=F32)
                            for p in range(x.shape[1] // LANES)], axis=-1)


def _blockdiag_dot(x, w_ref):
    outs = []
    for p in range(x.shape[1] // LANES):
        outs.append(jnp.dot(x[:, p * LANES:(p + 1) * LANES].astype(BF16), w_ref[p],
                            preferred_element_type=F32))
    return jnp.concatenate(outs, axis=-1)


def _lru_gates(xc, wra_ref, bra_ref, wix_ref, bix_ref, lam_ref):
    r = _sigmoid(_blockdiag_dot(xc, wra_ref) + bra_ref[...])
    i = _sigmoid(_blockdiag_dot(xc, wix_ref) + bix_ref[...])
    log_a = -LRU_C * r * _softplus(-lam_ref[...])
    a = jnp.exp(log_a)
    mult = jnp.sqrt(-jnp.tanh(log_a) * (1.0 + a * a))
    return a, mult * (i * xc)


def _rwkv_pre(p, prev, mu, wwa_ref, w0, a0, g2_ref, k_k, k_a, ones):
    px = p + (prev - p) * mu
    r = px[:, 0:D_RWKV]
    k = px[:, D_RWKV:2 * D_RWKV]
    v = px[:, 2 * D_RWKV:3 * D_RWKV]
    lo = px[:, 3 * D_RWKV:3 * D_RWKV + R_DECAY + R_AAA]
    g_lo = px[:, 3 * D_RWKV + R_DECAY + R_AAA:]
    lane = lax.broadcasted_iota(jnp.int32, lo.shape, 1)
    wa = _dot(jnp.where(lane < R_DECAY, jnp.tanh(lo), lo), wwa_ref[...])
    logw = -math.exp(-0.5) * _sigmoid(w0 + wa[:, :D_RWKV])
    a = _sigmoid(a0 + wa[:, D_RWKV:])
    g = _dot(_sigmoid(g_lo), g2_ref[...])
    kk = k * k_k
    kk = kk * lax.rsqrt(jnp.maximum(_segsum(kk * kk, ones), 1e-24))
    k2 = k * (1.0 + (a - 1.0) * k_a)
    return r, logw, k2, v, kk, a, g


def _rwkv_post(y, r, k2, v, g, r_k, ln_g, ln_b, ones):
    inv = 1.0 / RWKV_HEAD
    mean = _segsum(y, ones) * inv
    yc = y - mean
    var = _segsum(yc * yc, ones) * inv
    yn = yc * lax.rsqrt(var + GN_EPS) * ln_g + ln_b
    bonus = _segsum(r * k2 * r_k, ones) * v
    return (yn + bonus) * g


def _pool_project(d, wp_ref, scale):
    return _blockdiag_dot(d, wp_ref) * scale


def _gmlp_uz(z2, ln_g, ln_b):
    u = _gelu(z2[:, :D_GMLP])
    v = _gelu(z2[:, D_GMLP:])
    mu = jnp.mean(v, axis=-1, keepdims=True)
    vc = v - mu
    var = jnp.mean(vc * vc, axis=-1, keepdims=True)
    return u, vc * lax.rsqrt(var + 1e-5) * ln_g + ln_b


def _p_lru_body(x_ref, g_ref, w_ref, cw_ref, cb_ref, wra_ref, bra_ref, wix_ref, bix_ref, lam_ref,
                y_ref, convo_ref, ho_ref, hist_ref, h_ref):
    @pl.when(pl.program_id(1) == 0)
    def _():
        hist_ref[...] = jnp.zeros_like(hist_ref)
        h_ref[...] = jnp.zeros_like(h_ref)

    z = _dot(_rms(x_ref[...], g_ref[...]), w_ref[...])
    xa = z[:, :D_LRU]
    ga = z[:, D_LRU:]
    tm = xa.shape[0]
    hist = hist_ref[...]
    cw = cw_ref[...]
    xc = cb_ref[...] + cw[3:4] * xa
    for s in (1, 2, 3):
        xc = xc + cw[3 - s:4 - s] * _shift_rows(hist, xa, s)
    a, b = _lru_gates(xc, wra_ref, bra_ref, wix_ref, bix_ref, lam_ref)
    row = lax.broadcasted_iota(jnp.int32, a.shape, 0)
    s = 1
    while s < tm:
        m = row >= s
        b = jnp.where(m, a * pltpu.roll(b, s, 0) + b, b)
        a = jnp.where(m, a * pltpu.roll(a, s, 0), a)
        s *= 2
    h = a * h_ref[SUBLANES - 1:SUBLANES, :] + b
    y_ref[...] = (h * _gelu(ga)).astype(y_ref.dtype)
    hist_ref[...] = xa[tm - SUBLANES:]
    h_ref[...] = h[tm - SUBLANES:]
    convo_ref[...] = xa[tm - SUBLANES:]
    ho_ref[...] = h[tm - SUBLANES:]


def _pair_blockdiag_rows(x):
    lo = lax.broadcasted_iota(jnp.int32, x.shape, 1) < RWKV_HEAD
    z = jnp.zeros_like(x)
    return jnp.concatenate([jnp.where(lo, x, z), jnp.where(lo, z, x)], axis=0)


def _p_rwkv_body(x_ref, g_ref, w_ref, mu_ref, wwa_ref, w0_ref, a0_ref, g2_ref, kk_ref, ka_ref,
                 rk_ref, lng_ref, lnb_ref,
                 y_ref, shifto_ref, so_ref,
                 hist_ref, st_ref):
    @pl.when(pl.program_id(1) == 0)
    def _():
        hist_ref[...] = jnp.zeros_like(hist_ref)
        st_ref[...] = jnp.zeros_like(st_ref)

    C = RWKV_CHUNK
    ones = _head_ones()
    p = _dot(_rms(x_ref[...], g_ref[...]), w_ref[...])
    tm = p.shape[0]
    nc = tm // C
    npair = D_RWKV // LANES
    prev = _shift_rows(hist_ref[...], p, 1)
    r, logw, k2, v, kk, a, g = _rwkv_pre(p, prev, mu_ref[...], wwa_ref, w0_ref[...], a0_ref[...],
                                          g2_ref, kk_ref[...], ka_ref[...], ones)
    hist_ref[...] = p[tm - SUBLANES:]
    shifto_ref[...] = p[tm - SUBLANES:]
    rowc = lax.broadcasted_iota(jnp.int32, logw.shape, 0) % C
    lg = logw
    s = 1
    while s < C:
        lg = lg + jnp.where(rowc >= s, pltpu.roll(lg, s, 0), 0.0)
        s *= 2
    lg_last = [lg[(c + 1) * C - 1:(c + 1) * C, :] for c in range(nc)]
    lg_end = jnp.concatenate([jnp.broadcast_to(x, (C, D_RWKV)) for x in lg_last], axis=0)
    inv_gam = jnp.exp(-lg)
    to_end = jnp.exp(lg_end - lg)
    b = kk * a
    rt_f = r * jnp.exp(lg)
    at_b, rt_b, bt_b, kt_b, v_b, bh_b, kh_b = (x.astype(BF16) for x in (
        -kk * jnp.exp(lg - logw), rt_f, b * inv_gam, k2 * inv_gam, v, b * to_end, k2 * to_end))

    r2 = lax.broadcasted_iota(jnp.int32, (C, LANES), 0)
    c2 = lax.broadcasted_iota(jnp.int32, (C, LANES), 1) % C
    strict2 = c2 < r2
    incl2 = c2 <= r2
    eye2 = jnp.where(r2 == c2, 1.0, 0.0).astype(F32)
    pair2 = (r2 // 2) == (c2 // 2)
    levels = []
    nb = 2
    while nb < C:
        levels.append(((r2 // (2 * nb)) == (c2 // (2 * nb))) & ((r2 // nb) != (c2 // nb)))
        nb *= 2
    rr = lax.broadcasted_iota(jnp.int32, (LANES, LANES), 0)
    cc = lax.broadcasted_iota(jnp.int32, (LANES, LANES), 1)
    same_head = (rr // RWKV_HEAD) == (cc // RWKV_HEAD)
    eye128 = jnp.where(rr == cc, 1.0, 0.0).astype(F32)
    tn = (((0,), (0,)), ((), ()))
    nt_ = (((1,), (1,)), ((), ()))
    bd = _pair_blockdiag_rows

    def mm(x, y):
        return jnp.dot(x, y, preferred_element_type=F32)

    def blk(x, c, q):
        return x[c * C:(c + 1) * C, q * LANES:(q + 1) * LANES]

    P = [(c, q) for c in range(nc) for q in range(npair)]
    N = range(len(P))
    at, rt, bt, kt, vv, bh, kh = ([blk(x, c, q) for c, q in P] for x in
                                  (at_b, rt_b, bt_b, kt_b, v_b, bh_b, kh_b))
    ar = [jnp.concatenate([at[n], rt[n]], axis=0) for n in N]
    pb = [lax.dot_general(ar[n], bd(bt[n]), nt_, preferred_element_type=F32) for n in N]
    pk = [lax.dot_general(ar[n], bd(kt[n]), nt_, preferred_element_type=F32) for n in N]
    lab = [jnp.where(strict2, x[:C], 0.0) for x in pb]
    mrb = [jnp.where(incl2, x[C:], 0.0).astype(BF16) for x in pb]
    lak = [jnp.where(strict2, x[:C], 0.0).astype(BF16) for x in pk]
    mrk = [jnp.where(incl2, x[C:], 0.0).astype(BF16) for x in pk]
    v_bd = [bd(x) for x in vv]
    lak_v = [mm(lak[n], v_bd[n]) for n in N]
    T = [eye2 + jnp.where(pair2, x, 0.0) for x in lab]
    for mk in levels:
        tb = [x.astype(BF16) for x in T]
        u = [mm(tb[n], bd(jnp.where(mk, lab[n], 0.0).astype(BF16))).astype(BF16) for n in N]
        T = [T[n] + mm(u[n], bd(tb[n])) for n in N]
    tb = [x.astype(BF16) for x in T]
    tab = [mm(tb[n], bd(at[n])).astype(BF16) for n in N]
    cv = [mm(tb[n], bd(lak_v[n].astype(BF16))).astype(BF16) for n in N]
    g_bd = [jnp.where(same_head, lax.dot_general(bh[n], tab[n], tn, preferred_element_type=F32), 0.0)
            .astype(BF16) for n in N]
    q_bd = [jnp.where(same_head, lax.dot_general(jnp.concatenate([bh[n], kh[n]], axis=0),
                                                 jnp.concatenate([cv[n], vv[n]], axis=0), tn,
                                                 preferred_element_type=F32), 0.0) for n in N]
    ry = [(blk(rt_f, c, q) + mm(mrb[n], bd(tab[n]))).astype(BF16) for n, (c, q) in enumerate(P)]
    yc = [mm(jnp.concatenate([mrb[n], mrk[n]], axis=1),
             jnp.concatenate([bd(cv[n]), v_bd[n]], axis=0)) for n in N]
    gcol = [jnp.broadcast_to(jnp.sum(eye128 * jnp.exp(lg_last[c][:, q * LANES:(q + 1) * LANES]),
                                     axis=1, keepdims=True), (LANES, LANES)) for c, q in P]

    st = [st_ref[q] for q in range(npair)]
    y_rows = []
    for c in range(nc):
        sb = [x.astype(BF16) for x in st]
        ns = [c * npair + q for q in range(npair)]
        gs = [mm(g_bd[n], sb[q]) for q, n in enumerate(ns)]
        ys = [mm(ry[n], sb[q]) for q, n in enumerate(ns)]
        st = [st[q] * gcol[n] + gs[q] + q_bd[n] for q, n in enumerate(ns)]
        y_rows.append(jnp.concatenate([ys[q] + yc[n] for q, n in enumerate(ns)], axis=1))
    y = jnp.concatenate(y_rows, axis=0)
    yb = _rwkv_post(y, r, k2, v, g, rk_ref[...], lng_ref[...], lnb_ref[...], ones)
    y_ref[...] = yb.astype(y_ref.dtype)
    for q in range(npair):
        st_ref[q] = st[q]
        so_ref[2 * q] = st[q][:RWKV_HEAD, :RWKV_HEAD].T
        so_ref[2 * q + 1] = st[q][RWKV_HEAD:, RWKV_HEAD:].T


def _p_pool_body(x_ref, g_ref, w_ref, wp_ref, sc_ref, y_ref, poolo_ref, hist_ref):
    t = pl.program_id(1)

    @pl.when(t == 0)
    def _():
        hist_ref[...] = jnp.zeros_like(hist_ref)

    pc = _dot(_rms(x_ref[...], g_ref[...]), w_ref[...])
    tm = pc.shape[0]
    X = jnp.concatenate([hist_ref[...], pc], axis=0)
    pos = t * tm + lax.broadcasted_iota(jnp.int32, (tm, POOL_GW), 0)
    ds = []
    for gi, win in enumerate(POOL_WINDOWS):
        sl = slice(gi * POOL_GW, (gi + 1) * POOL_GW)
        s = X[:, sl]
        sh = 1
        while sh < win:
            s = s + pltpu.roll(s, sh, 0)
            sh *= 2
        cnt = jnp.minimum(pos + 1, win).astype(F32)
        ds.append(s[POOL_HIST:] / cnt - pc[:, sl])
    y = _pool_project(jnp.concatenate(ds, axis=-1), wp_ref, sc_ref[...])
    y_ref[...] = y.astype(y_ref.dtype)
    hist_ref[...] = pc[tm - POOL_HIST:]
    poolo_ref[...] = pc[tm - POOL_HIST:]


def _p_gmlp_body(x_ref, g_ref, w_ref, lng_ref, lnb_ref, ws_ref, bsb_ref, y_ref):
    z2 = _dot(_rms(x_ref[...], g_ref[...]), w_ref[...])
    tm = z2.shape[0]
    u, zv = _gmlp_uz(z2, lng_ref[...], lnb_ref[...])
    tril = (lax.broadcasted_iota(jnp.int32, (CHUNK, CHUNK), 0)
            >= lax.broadcasted_iota(jnp.int32, (CHUNK, CHUNK), 1))
    ws = [jnp.where(tril, ws_ref[gi], 0.0).astype(BF16) for gi in range(GMLP_GROUPS)]
    zb = zv.astype(BF16)
    rows = []
    for c in range(tm // CHUNK):
        cols = [jnp.dot(ws[gi], zb[c * CHUNK:(c + 1) * CHUNK, gi * LANES:(gi + 1) * LANES],
                        preferred_element_type=F32) for gi in range(GMLP_GROUPS)]
        rows.append(jnp.concatenate(cols, axis=-1) + bsb_ref[...])
    s = jnp.concatenate(rows, axis=0)
    y_ref[...] = (u * s).astype(y_ref.dtype)


def _merge_body(x_ref, g_ref, wg_ref, ya_ref, yb_ref, yc_ref, yd_ref, wp_ref, wo_ref, o_ref):
    x = x_ref[...]
    xn = _rms(x, g_ref[...]).astype(BF16)
    merged = None
    for i, y_ref in enumerate((ya_ref, yb_ref, yc_ref, yd_ref)):
        gate = _sigmoid(jnp.dot(xn, wg_ref[:, i * D_MODEL:(i + 1) * D_MODEL],
                                preferred_element_type=F32))
        term = gate * jnp.dot(y_ref[...], wp_ref[i], preferred_element_type=F32)
        merged = term if merged is None else merged + term
    o_ref[...] = x + _dot(merged, wo_ref[...])


def _softmax_rows(s):
    e = jnp.exp(s - jnp.max(s, axis=-1, keepdims=True))
    return e / jnp.sum(e, axis=-1, keepdims=True)


def _p_xattn_body(x_ref, g_ref, wq_ref, k_ref, v_ref, wo_ref, o_ref):
    x = x_ref[...]
    q = _dot(_rms(x, g_ref[...]), wq_ref[...]).astype(BF16)
    kb = k_ref[...].astype(BF16)
    vb = v_ref[...].astype(BF16)
    outs = []
    for h in range(X_HEADS):
        sl = slice(h * X_HEAD_DIM, (h + 1) * X_HEAD_DIM)
        s = lax.dot_general(q[:, sl], kb[:, sl], (((1,), (1,)), ((), ())),
                            preferred_element_type=F32) * (X_HEAD_DIM ** -0.5)
        outs.append(jnp.dot(_softmax_rows(s).astype(BF16), vb[:, sl], preferred_element_type=F32))
    o_ref[...] = x + _dot(jnp.concatenate(outs, axis=-1), wo_ref[...])


def _ffn_conv(hist, h, cw, cb):
    return cb + cw[2:3] * h + cw[1:2] * _shift_rows(hist, h, 1) + cw[0:1] * _shift_rows(hist, h, 2)


def _p_ffn_body(x_ref, g_ref, wug_ref, wuv_ref, cwg_ref, cwv_ref, cbg_ref, cbv_ref, wd_ref,
                o_ref, ffo_ref, hist_ref):
    @pl.when(pl.program_id(1) == 0)
    def _():
        hist_ref[...] = jnp.zeros_like(hist_ref)

    x = x_ref[...]
    tm = x.shape[0]
    xn = _rms(x, g_ref[...]).astype(BF16)
    acc = x
    for j in range(D_FF // FF_CW):
        cs = slice(j * FF_CW, (j + 1) * FF_CW)
        vs = slice(D_FF + j * FF_CW, D_FF + (j + 1) * FF_CW)
        hg = jnp.dot(xn, wug_ref[:, cs], preferred_element_type=F32)
        hv = jnp.dot(xn, wuv_ref[:, cs], preferred_element_type=F32)
        cg = _ffn_conv(hist_ref[:, cs], hg, cwg_ref[:, cs], cbg_ref[:, cs])
        cv = _ffn_conv(hist_ref[:, vs], hv, cwv_ref[:, cs], cbv_ref[:, cs])
        acc = acc + _dot(_gelu(cg) * cv, wd_ref[cs, :])
        hist_ref[:, cs] = hg[tm - SUBLANES:]
        hist_ref[:, vs] = hv[tm - SUBLANES:]
    o_ref[...] = acc
    ffo_ref[...] = hist_ref[...]


def _memkv_body(m_ref, g_ref, wk_ref, wv_ref, k_ref, v_ref):
    mn = _rms(m_ref[...], g_ref[...]).astype(BF16)
    k_ref[...] = jnp.dot(mn, wk_ref[...], preferred_element_type=F32)
    v_ref[...] = jnp.dot(mn, wv_ref[...], preferred_element_type=F32)


def _final_norm_body(x_ref, g_ref, o_ref):
    o_ref[...] = _rms(x_ref[...], g_ref[...])


def _s_mix_body(x_ref, g_ref, wa_ref, wb_ref, wc_ref, wd_ref,
                lconv_ref, lh_ref, cw_ref, cb_ref, wra_ref, bra_ref, wix_ref, bix_ref, lam_ref,
                shift_ref, mu_ref, wwa_ref, w0_ref, a0_ref, g2_ref, kk_ref, ka_ref,
                pool_ref, wp_ref, psc_ref,
                lng_ref, lnb_ref, wsd_ref, bsr_ref,
                ya_ref, xa_ref, h_ref, yc_ref, pc_ref, yd_ref, zv_ref, p_ref,
                rt_ref, wt_ref, kt_ref, vt_ref, kkt_ref, at_ref, gt_ref):
    xn = _rms(x_ref[...], g_ref[...]).astype(BF16)
    z = jnp.dot(xn, wa_ref[...], preferred_element_type=F32)
    xa = z[:, :D_LRU]
    cw = cw_ref[...]
    xc = cb_ref[...] + cw[0:1] * lconv_ref[0] + cw[1:2] * lconv_ref[1] + cw[2:3] * lconv_ref[2] + cw[3:4] * xa
    a, b = _lru_gates(xc, wra_ref, bra_ref, wix_ref, bix_ref, lam_ref)
    h = a * lh_ref[...] + b
    ya_ref[...] = (h * _gelu(z[:, D_LRU:])).astype(ya_ref.dtype)
    xa_ref[...] = xa
    h_ref[...] = h
    ones = _head_ones()
    p = jnp.dot(xn, wb_ref[...], preferred_element_type=F32)
    r, logw, k2, v, kk, aa, g = _rwkv_pre(p, shift_ref[...], mu_ref[...], wwa_ref, w0_ref[...],
                                           a0_ref[...], g2_ref, kk_ref[...], ka_ref[...], ones)
    p_ref[...] = p
    for ref, val in ((rt_ref, r), (wt_ref, jnp.exp(logw)), (kt_ref, k2), (vt_ref, v), (kkt_ref, kk),
                     (at_ref, aa), (gt_ref, g)):
        ref[...] = val.T
    pc = jnp.dot(xn, wc_ref[...], preferred_element_type=F32)
    ds = []
    for gi, win in enumerate(POOL_WINDOWS):
        sl = slice(gi * POOL_GW, (gi + 1) * POOL_GW)
        s = pc[:, sl]
        for j in range(POOL_BUF - (win - 1), POOL_BUF):
            s = s + pool_ref[j][:, sl]
        ds.append(s / float(min(PAST_LEN + 1, win)) - pc[:, sl])
    yc_ref[...] = _pool_project(jnp.concatenate(ds, axis=-1), wp_ref, psc_ref[...]).astype(yc_ref.dtype)
    pc_ref[...] = pc
    u, zv = _gmlp_uz(jnp.dot(xn, wd_ref[...], preferred_element_type=F32), lng_ref[...], lnb_ref[...])
    yd_ref[...] = (u * (wsd_ref[...] * zv + bsr_ref[...])).astype(yd_ref.dtype)
    zv_ref[...] = zv


def _s_rwkv_body(has_prev, s_ref, r_ref, w_ref, k_ref, v_ref, kk_ref, a_ref, g_ref, rk_ref, lng_ref,
                 lnb_ref, *rest):
    y_ref, so_ref, yt_sc = rest[1:] if has_prev else rest
    h = pl.program_id(0)
    kk = kk_ref[...]
    w = w_ref[...]
    k = k_ref[...]
    r = r_ref[...]
    bk = kk * a_ref[...]

    def body(i, carry):
        si = s_ref[i]
        sa = -jnp.sum(si * kk, axis=0, keepdims=True)
        s2 = si * w + sa * bk + v_ref[pl.ds(i, 1), :] * k
        so_ref[i] = s2
        yt_sc[pl.ds(h * RWKV_HEAD + i, 1), :] = jnp.sum(s2 * r, axis=0, keepdims=True)
        return carry

    lax.fori_loop(0, RWKV_HEAD, body, 0, unroll=4)
    rows = pl.ds(pl.multiple_of(h * RWKV_HEAD, RWKV_HEAD), RWKV_HEAD)
    y = yt_sc[rows, :]
    yc = y - jnp.mean(y, axis=0, keepdims=True)
    var = jnp.mean(yc * yc, axis=0, keepdims=True)
    yn = yc * lax.rsqrt(var + GN_EPS) * lng_ref[...] + lnb_ref[...]
    bonus = jnp.sum(r * k * rk_ref[...], axis=0, keepdims=True) * v_ref[...]
    yt_sc[rows, :] = (yn + bonus) * g_ref[...]

    @pl.when(h == pl.num_programs(0) - 1)
    def _():
        y_ref[...] = yt_sc[...].T.astype(y_ref.dtype)


def _s_xattn_body(x_ref, g_ref, wq_ref, k_ref, v_ref, wo_ref, o_ref, q_sc, a_sc):
    i = pl.program_id(0)
    bb = k_ref.shape[0]

    @pl.when(i == 0)
    def _():
        q_sc[...] = _dot(_rms(x_ref[...], g_ref[...]), wq_ref[...])

    halves = X_HEAD_DIM // LANES

    def q_lanes(c, h):
        return slice(h * X_HEAD_DIM + c * LANES, h * X_HEAD_DIM + (c + 1) * LANES)

    for j in range(bb):
        row = pl.ds(i * bb + j, 1)
        qrow = q_sc[row, :]
        q8 = jnp.concatenate([qrow[:, q_lanes(c, h)] for c in range(halves) for h in range(X_HEADS)], axis=0)
        k3 = k_ref[j].reshape(N_MEM, SUBLANES, LANES)
        part = jnp.sum(k3 * q8[None], axis=-1, keepdims=True)
        part = jnp.broadcast_to(part, k3.shape)
        s = (part + pltpu.roll(part, X_HEADS, 1)) * (X_HEAD_DIM ** -0.5)
        e = jnp.exp(s - jnp.max(s, axis=0, keepdims=True))
        pr = e / jnp.sum(e, axis=0, keepdims=True)
        o8 = jnp.sum(pr * v_ref[j].reshape(N_MEM, SUBLANES, LANES), axis=0)
        a_sc[row, :] = jnp.concatenate([o8[c * X_HEADS + h:c * X_HEADS + h + 1, :]
                                        for h in range(X_HEADS) for c in range(halves)], axis=1)

    @pl.when(i == pl.num_programs(0) - 1)
    def _():
        o_ref[...] = x_ref[...] + _dot(a_sc[...], wo_ref[...])


def _s_ffn_body(x_ref, g_ref, wug_ref, wuv_ref, sg_ref, sv_ref, cwg_ref, cwv_ref, cbg_ref, cbv_ref,
                wd_ref, o_ref, hg_ref, hv_ref, acc_ref):
    j = pl.program_id(0)

    @pl.when(j == 0)
    def _():
        acc_ref[...] = x_ref[...]

    xn = _rms(x_ref[...], g_ref[...]).astype(BF16)
    hg = jnp.dot(xn, wug_ref[...], preferred_element_type=F32)
    hv = jnp.dot(xn, wuv_ref[...], preferred_element_type=F32)
    cwg = cwg_ref[...]
    cwv = cwv_ref[...]
    cg = cbg_ref[...] + cwg[0:1] * sg_ref[0] + cwg[1:2] * sg_ref[1] + cwg[2:3] * hg
    cv = cbv_ref[...] + cwv[0:1] * sv_ref[0] + cwv[1:2] * sv_ref[1] + cwv[2:3] * hv
    acc_ref[...] += _dot(_gelu(cg) * cv, wd_ref[...])
    hg_ref[...] = hg
    hv_ref[...] = hv

    @pl.when(j == pl.num_programs(0) - 1)
    def _():
        o_ref[...] = acc_ref[...]


def _params(n_grid, vmem_mb):
    return pltpu.CompilerParams(dimension_semantics=("arbitrary",) * n_grid,
                                vmem_limit_bytes=vmem_mb << 20)


def _whole(shape):
    return pl.BlockSpec(tuple(shape), lambda *_: (0,) * len(shape))


def _layer(arr, l):
    shape = arr.shape[1:]
    return pl.BlockSpec((None,) + tuple(shape), lambda *_: (l,) + (0,) * len(shape))


def _sds(shape, dtype=F32):
    return jax.ShapeDtypeStruct(tuple(shape), dtype)


def _pair_blockdiag(w):
    L = w.shape[0]
    w = w.reshape(L, 4, 2, RWKV_HEAD, RWKV_HEAD)
    z = jnp.zeros_like(w[:, :, 0])
    top = jnp.concatenate([w[:, :, 0], z], axis=-1)
    bot = jnp.concatenate([z, w[:, :, 1]], axis=-1)
    return jnp.concatenate([top, bot], axis=-2)


def _vec(a):
    return a.reshape(a.shape[0], 1, -1)


def kernel(x_prompt, x_sample, state_lru_conv, state_lru_h, state_rwkv_shift, state_rwkv_S, state_pool, state_ffn_conv, cache_mem_k, cache_mem_v, mem_prompt, g_mix, w_in, lru_conv_w, lru_conv_b, lru_w_ra, lru_b_ra, lru_w_ix, lru_b_ix, lru_lambda, rwkv_mu, rwkv_w0, rwkv_w2, rwkv_a0, rwkv_a2, rwkv_g2, rwkv_k_k, rwkv_k_a, rwkv_r_k, rwkv_ln_g, rwkv_ln_b, pool_w, pool_scale, gmlp_ln_g, gmlp_ln_b, gmlp_w_s, gmlp_b_s, w_pa, w_pb, w_pc, w_pd, w_o, g_xattn, g_mem, w_xq, w_xk, w_xv, w_xo, g_ffn, w_up, ffn_conv_w, ffn_conv_b, w_down, g_final):
    B, T, D = x_prompt.shape
    SB = x_sample.shape[0]
    assert D == D_MODEL and w_in.shape == (DEPTH, D_MODEL, D_IN) and x_sample.shape[1] == 1
    assert T % TM_MIX == 0 and T % TM_RWKV == 0 and TM_RWKV % RWKV_CHUNK == 0 and SB % S_ATT_BB == 0
    assert SB == LANES and X_HEADS * (X_HEAD_DIM // LANES) == SUBLANES

    wA = w_in[:, :, O_A:O_B].astype(BF16)
    wB = w_in[:, :, O_B:O_C].astype(BF16)
    wC = w_in[:, :, O_C:O_D].astype(BF16)
    wD = w_in[:, :, O_D:O_G].astype(BF16)
    wG = w_in[:, :, O_G:].astype(BF16)
    wra = _pair_blockdiag(lru_w_ra).astype(BF16)
    wix = _pair_blockdiag(lru_w_ix).astype(BF16)
    zer = jnp.zeros((DEPTH, R_DECAY, D_RWKV), F32)
    wwa = jnp.concatenate([jnp.concatenate([rwkv_w2, zer], axis=-1),
                           jnp.concatenate([zer, rwkv_a2], axis=-1)], axis=1).astype(BF16)
    g2 = rwkv_g2.astype(BF16)
    wpool = pool_w.astype(BF16)
    bsb = jnp.repeat(gmlp_b_s, LANES, axis=-1)
    off = PAST_LEN % CHUNK
    wsd = jnp.repeat(gmlp_w_s[:, :, off, off], LANES, axis=-1)[:, None, :]
    bsr = jnp.repeat(gmlp_b_s[:, off, :], LANES, axis=-1)[:, None, :]
    wP = jnp.stack([w_pa, w_pb, w_pc, w_pd], axis=1).astype(BF16)
    wO = w_o.astype(BF16)
    wXq, wXk, wXv, wXo = (w.astype(BF16) for w in (w_xq, w_xk, w_xv, w_xo))
    wUg = w_up[:, :, :D_FF].astype(BF16)
    wUv = w_up[:, :, D_FF:].astype(BF16)
    wDn = w_down.astype(BF16)
    cwg, cwv = ffn_conv_w[:, :, :D_FF], ffn_conv_w[:, :, D_FF:]
    cbg, cbv = _vec(ffn_conv_b[:, :D_FF]), _vec(ffn_conv_b[:, D_FF:])
    gmix, gxat, gffn, gmem = _vec(g_mix), _vec(g_xattn), _vec(g_ffn), _vec(g_mem)
    lcb, bra, bix, lam = _vec(lru_conv_b), _vec(lru_b_ra), _vec(lru_b_ix), _vec(lru_lambda)
    mu, w0, a0, kkw, kaw = _vec(rwkv_mu), _vec(rwkv_w0), _vec(rwkv_a0), _vec(rwkv_k_k), _vec(rwkv_k_a)
    rk = rwkv_r_k.reshape(DEPTH, 1, D_RWKV)
    rlg, rlb = _vec(rwkv_ln_g), _vec(rwkv_ln_b)
    psc, glg, glb = _vec(pool_scale), _vec(gmlp_ln_g), _vec(gmlp_ln_b)
    gfin = g_final.reshape(1, D_MODEL)
    rk_col, rlg_col, rlb_col = (a.reshape(DEPTH, D_RWKV, 1) for a in (rwkv_r_k, rwkv_ln_g, rwkv_ln_b))
    state_S5 = jnp.transpose(state_rwkv_S, (0, 2, 3, 4, 1))
    s_S5 = None

    def cache_rows(c):
        c = c.reshape(DEPTH, SB, N_MEM, X_HEADS, X_HEAD_DIM // LANES, LANES)
        return jnp.swapaxes(c, 3, 4).reshape(DEPTH, SB, N_MEM * SUBLANES, LANES)

    cache_k, cache_v = cache_rows(cache_mem_k), cache_rows(cache_mem_v)

    kv_shape = _sds((DEPTH, B, N_MEM, D_MODEL))
    kv_spec = pl.BlockSpec((None, None, N_MEM, D_MODEL), lambda l, b: (l, b, 0, 0))
    wl_spec = pl.BlockSpec((None, D_MODEL, D_MODEL), lambda l, b: (l, 0, 0))
    p_mk, p_mv = pl.pallas_call(
        _memkv_body, grid=(DEPTH, B),
        in_specs=[pl.BlockSpec((None, N_MEM, D_MODEL), lambda l, b: (b, 0, 0)),
                  pl.BlockSpec((None, 1, D_MODEL), lambda l, b: (l, 0, 0)), wl_spec, wl_spec],
        out_specs=[kv_spec, kv_spec], out_shape=[kv_shape, kv_shape],
        compiler_params=_params(2, 32), name="memkv")(mem_prompt, gmem, wXk, wXv)

    nt = T // TM_MIX
    ntr = T // TM_RWKV
    xt_spec = pl.BlockSpec((None, TM_MIX, D_MODEL), lambda b, t: (b, t, 0))
    yt_spec = pl.BlockSpec((None, TM_MIX, D_LRU), lambda b, t: (b, t, 0))

    def tail_spec(rows, width):
        return pl.BlockSpec((None, rows, width), lambda b, t: (b, 0, 0))

    hp = x_prompt
    hs = x_sample.reshape(SB, D_MODEL)
    outs = {k: [] for k in ("p_conv", "s_conv", "p_h", "s_h", "p_shift", "s_shift", "p_S",
                            "p_pool", "s_pool", "p_ffn", "s_ffn", "s_cv")}

    for l in range(DEPTH):
        L = functools.partial(_layer, l=l)

        yA, p_conv8, p_h8 = pl.pallas_call(
            _p_lru_body, grid=(B, nt),
            in_specs=[xt_spec, L(gmix), L(wA), L(lru_conv_w), L(lcb), L(wra), L(bra), L(wix), L(bix), L(lam)],
            out_specs=[yt_spec, tail_spec(SUBLANES, D_LRU), tail_spec(SUBLANES, D_LRU)],
            out_shape=[_sds((B, T, D_LRU), BF16), _sds((B, SUBLANES, D_LRU)), _sds((B, SUBLANES, D_LRU))],
            scratch_shapes=[pltpu.VMEM((SUBLANES, D_LRU), F32), pltpu.VMEM((SUBLANES, D_LRU), F32)],
            compiler_params=_params(2, 48), name="p_lru")(
                hp, gmix, wA, lru_conv_w, lcb, wra, bra, wix, bix, lam)

        yB, p_shift8, p_S = pl.pallas_call(
            _p_rwkv_body, grid=(B, ntr),
            in_specs=[pl.BlockSpec((None, TM_RWKV, D_MODEL), lambda b, t: (b, t, 0)),
                      L(gmix), L(wB), L(mu), L(wwa), L(w0), L(a0), L(g2), L(kkw), L(kaw), L(rk), L(rlg), L(rlb)],
            out_specs=[pl.BlockSpec((None, TM_RWKV, D_RWKV), lambda b, t: (b, t, 0)),
                       tail_spec(SUBLANES, D_RWKV_IN),
                       pl.BlockSpec((None, RWKV_HEADS, RWKV_HEAD, RWKV_HEAD), lambda b, t: (b, 0, 0, 0))],
            out_shape=[_sds((B, T, D_RWKV), BF16), _sds((B, SUBLANES, D_RWKV_IN)),
                       _sds((B, RWKV_HEADS, RWKV_HEAD, RWKV_HEAD))],
            scratch_shapes=[pltpu.VMEM((SUBLANES, D_RWKV_IN), F32),
                            pltpu.VMEM((D_RWKV // LANES, LANES, LANES), F32)],
            compiler_params=_params(2, 48), name="p_rwkv")(
                hp, gmix, wB, mu, wwa, w0, a0, g2, kkw, kaw, rk, rlg, rlb)

        yC, p_pool16 = pl.pallas_call(
            _p_pool_body, grid=(B, nt),
            in_specs=[xt_spec, L(gmix), L(wC), L(wpool), L(psc)],
            out_specs=[yt_spec, tail_spec(POOL_HIST, D_POOL)],
            out_shape=[_sds((B, T, D_POOL), BF16), _sds((B, POOL_HIST, D_POOL))],
            scratch_shapes=[pltpu.VMEM((POOL_HIST, D_POOL), F32)],
            compiler_params=_params(2, 48), name="p_pool")(hp, gmix, wC, wpool, psc)

        yD = pl.pallas_call(
            _p_gmlp_body, grid=(B, nt),
            in_specs=[xt_spec, L(gmix), L(wD), L(glg), L(glb), L(gmlp_w_s), L(bsb)],
            out_specs=yt_spec, out_shape=_sds((B, T, D_GMLP), BF16),
            compiler_params=_params(2, 48), name="p_gmlp")(hp, gmix, wD, glg, glb, gmlp_w_s, bsb)

        def merge(x2d, ys, tm):
            rows = x2d.shape[0]
            xs = pl.BlockSpec((tm, D_MODEL), lambda i: (i, 0))
            ysp = pl.BlockSpec((tm, D_LRU), lambda i: (i, 0))
            return pl.pallas_call(
                _merge_body, grid=(rows // tm,),
                in_specs=[xs, L(gmix), L(wG), ysp, ysp, ysp, ysp, L(wP), L(wO)],
                out_specs=xs, out_shape=_sds((rows, D_MODEL)),
                compiler_params=_params(1, 56), name="merge")(x2d, gmix, wG, *ys, wP, wO)

        hp = merge(hp.reshape(B * T, D_MODEL), [y.reshape(B * T, -1) for y in (yA, yB, yC, yD)],
                   TM_MIX).reshape(B, T, D_MODEL)

        kvb_spec = pl.BlockSpec((None, None, N_MEM, D_MODEL), lambda b, t: (l, b, 0, 0))
        hp = pl.pallas_call(
            _p_xattn_body, grid=(B, nt),
            in_specs=[xt_spec, L(gxat), L(wXq), kvb_spec, kvb_spec, L(wXo)],
            out_specs=xt_spec, out_shape=_sds((B, T, D_MODEL)),
            compiler_params=_params(2, 48), name="p_xattn")(hp, gxat, wXq, p_mk, p_mv, wXo)

        hp, p_ffn8 = pl.pallas_call(
            _p_ffn_body, grid=(B, nt),
            in_specs=[xt_spec, L(gffn), L(wUg), L(wUv), L(cwg), L(cwv), L(cbg), L(cbv), L(wDn)],
            out_specs=[xt_spec, tail_spec(SUBLANES, 2 * D_FF)],
            out_shape=[_sds((B, T, D_MODEL)), _sds((B, SUBLANES, 2 * D_FF))],
            scratch_shapes=[pltpu.VMEM((SUBLANES, 2 * D_FF), F32)],
            compiler_params=_params(2, 56), name="p_ffn")(hp, gffn, wUg, wUv, cwg, cwv, cbg, cbv, wDn)

        outs["p_conv"].append(p_conv8[:, SUBLANES - 3:])
        outs["p_h"].append(p_h8[:, SUBLANES - 1])
        outs["p_shift"].append(p_shift8[:, SUBLANES - 1:])
        outs["p_S"].append(p_S)
        outs["p_pool"].append(p_pool16[:, POOL_HIST - POOL_BUF:])
        outs["p_ffn"].append(p_ffn8[:, SUBLANES - 2:])

        lconv = jnp.swapaxes(state_lru_conv[l], 0, 1)
        spool = jnp.swapaxes(state_pool[l], 0, 1)
        sffn = jnp.swapaxes(state_ffn_conv[l], 0, 1)
        shift = state_rwkv_shift[l].reshape(SB, D_RWKV_IN)
        row512 = _sds((SB, D_LRU))
        chan = _sds((D_RWKV, SB))
        mix_in = [hs, gmix[l], wA[l], wB[l], wC[l], wD[l],
                  lconv, state_lru_h[l], lru_conv_w[l], lcb[l], wra[l], bra[l], wix[l], bix[l], lam[l],
                  shift, mu[l], wwa[l], w0[l], a0[l], g2[l], kkw[l], kaw[l],
                  spool, wpool[l], psc[l],
                  glg[l], glb[l], wsd[l], bsr[l]]
        mix_out = [_sds((SB, D_LRU), BF16), row512, row512, _sds((SB, D_POOL), BF16), row512,
                   _sds((SB, D_GMLP), BF16), row512, _sds((SB, D_RWKV_IN)),
                   chan, chan, chan, chan, chan, chan, chan]
        (yA, s_xa, s_h, yC, s_pc, yD, s_zv, s_p, rT, wT, kT, vT, kkT, aT, gT) = pl.pallas_call(
            _s_mix_body, grid=(1,),
            in_specs=[_whole(a.shape) for a in mix_in],
            out_specs=[_whole(o.shape) for o in mix_out], out_shape=mix_out,
            compiler_params=_params(1, 56), name="s_mix")(*mix_in)

        hd_spec = pl.BlockSpec((RWKV_HEAD, SB), lambda h: (h, 0))
        col_spec = pl.BlockSpec((None, RWKV_HEAD, 1), lambda h: (l, h, 0))
        st_spec = pl.BlockSpec((None, None, RWKV_HEAD, RWKV_HEAD, SB), lambda h: (l, h, 0, 0, 0))
        has_prev = s_S5 is not None
        yB, s_S5 = pl.pallas_call(
            functools.partial(_s_rwkv_body, has_prev), grid=(RWKV_HEADS,),
            in_specs=[st_spec] + [hd_spec] * 7 + [col_spec] * 3
                     + ([pl.BlockSpec(memory_space=pl.ANY)] if has_prev else []),
            out_specs=[_whole((SB, D_RWKV)), st_spec],
            out_shape=[_sds((SB, D_RWKV), BF16), _sds(state_S5.shape)],
            scratch_shapes=[pltpu.VMEM((D_RWKV, SB), F32)],
            input_output_aliases={11: 1} if has_prev else {},
            compiler_params=_params(1, 48), name="s_rwkv")(
                state_S5, rT, wT, kT, vT, kkT, aT, gT, rk_col, rlg_col, rlb_col,
                *([s_S5] if has_prev else []))

        hs = merge(hs, [yA, yB, yC, yD], SB)

        kc_spec = pl.BlockSpec((None, S_ATT_BB, N_MEM * SUBLANES, LANES), lambda i: (l, i, 0, 0))
        xs_spec = _whole((SB, D_MODEL))
        hs = pl.pallas_call(
            _s_xattn_body, grid=(SB // S_ATT_BB,),
            in_specs=[xs_spec, L(gxat), L(wXq), kc_spec, kc_spec, L(wXo)],
            out_specs=xs_spec, out_shape=_sds((SB, D_MODEL)),
            scratch_shapes=[pltpu.VMEM((SB, D_MODEL), F32), pltpu.VMEM((SB, D_MODEL), F32)],
            compiler_params=_params(1, 56), name="s_xattn")(
                hs, gxat, wXq, cache_k, cache_v, wXo)

        ncf = D_FF // FF_CW
        wu_spec = pl.BlockSpec((None, D_MODEL, FF_CW), lambda j: (l, 0, j))
        sg_spec = pl.BlockSpec((2, SB, FF_CW), lambda j: (0, 0, j))
        sv_spec = pl.BlockSpec((2, SB, FF_CW), lambda j: (0, 0, j + ncf))
        cw_spec = pl.BlockSpec((None, 3, FF_CW), lambda j: (l, 0, j))
        cb_spec = pl.BlockSpec((None, 1, FF_CW), lambda j: (l, 0, j))
        hh_spec = pl.BlockSpec((SB, FF_CW), lambda j: (0, j))
        hs, s_hg, s_hv = pl.pallas_call(
            _s_ffn_body, grid=(ncf,),
            in_specs=[xs_spec, L(gffn), wu_spec, wu_spec, sg_spec, sv_spec, cw_spec, cw_spec, cb_spec, cb_spec,
                      pl.BlockSpec((None, FF_CW, D_MODEL), lambda j: (l, j, 0))],
            out_specs=[xs_spec, hh_spec, hh_spec],
            out_shape=[_sds((SB, D_MODEL)), _sds((SB, D_FF)), _sds((SB, D_FF))],
            scratch_shapes=[pltpu.VMEM((SB, D_MODEL), F32)],
            compiler_params=_params(1, 48), name="s_ffn")(
                hs, gffn, wUg, wUv, sffn, sffn, cwg, cwv, cbg, cbv, wDn)

        outs["s_conv"].append(jnp.concatenate([state_lru_conv[l][:, 1:], s_xa[:, None]], axis=1))
        outs["s_h"].append(s_h)
        outs["s_shift"].append(s_p[:, None])
        outs["s_pool"].append(jnp.concatenate([state_pool[l][:, 1:], s_pc[:, None]], axis=1))
        outs["s_ffn"].append(jnp.concatenate(
            [state_ffn_conv[l][:, 1:], jnp.concatenate([s_hg, s_hv], axis=-1)[:, None]], axis=1))
        outs["s_cv"].append(s_zv[:, None])

    def final_norm(x2d, tm):
        rows = x2d.shape[0]
        spec = pl.BlockSpec((tm, D_MODEL), lambda i: (i, 0))
        return pl.pallas_call(_final_norm_body, grid=(rows // tm,),
                              in_specs=[spec, _whole((1, D_MODEL))], out_specs=spec,
                              out_shape=_sds((rows, D_MODEL)),
                              compiler_params=_params(1, 32), name="final_norm")(x2d, gfin)

    y_prompt = final_norm(hp.reshape(B * T, D_MODEL), 1024).reshape(B, T, D_MODEL)
    y_sample = final_norm(hs, SB).reshape(SB, 1, D_MODEL)
    stk = {k: jnp.stack(v, axis=0) for k, v in outs.items()}
    mem_shape = (DEPTH, B, N_MEM, X_HEADS, X_HEAD_DIM)
    return (y_prompt, y_sample,
            stk["p_conv"], stk["s_conv"], stk["p_h"], stk["s_h"],
            stk["p_shift"], stk["s_shift"], stk["p_S"], jnp.transpose(s_S5, (0, 4, 1, 2, 3)),
            stk["p_pool"], stk["s_pool"], stk["p_ffn"], stk["s_ffn"],
            p_mk.reshape(mem_shape), p_mv.reshape(mem_shape), stk["s_cv"])
```

```python
import functools
import math

import jax
import jax.numpy as jnp
from jax import lax
from jax.experimental import pallas as pl
from jax.experimental.pallas import tpu as pltpu

F32 = jnp.float32
BF16 = jnp.bfloat16

SUBLANES = 8
LANES = 128

D_MODEL = 1024
DEPTH = 4
N_MEM = 256
D_LRU = 512
LRU_C = 8.0
D_RWKV = 512
RWKV_HEAD = 64
RWKV_HEADS = D_RWKV // RWKV_HEAD
R_DECAY = 64
R_AAA = 64
R_GATE = 128
D_RWKV_IN = 3 * D_RWKV + R_DECAY + R_AAA + R_GATE
GN_EPS = 64e-5
D_POOL = 512
POOL_WINDOWS = (2, 4, 8, 16)
POOL_GW = D_POOL // len(POOL_WINDOWS)
POOL_BUF = max(POOL_WINDOWS) - 1
POOL_HIST = 16
D_GMLP = 512
GMLP_GROUPS = 4
CHUNK = 128
N_BRANCH = 4
X_HEADS = 4
X_HEAD_DIM = D_MODEL // X_HEADS
D_FF = 3 * D_MODEL
EPS = 1e-6
PAST_LEN = 16384

O_A = 0
O_B = 2 * D_LRU
O_C = O_B + D_RWKV_IN
O_D = O_C + D_POOL
O_G = O_D + 2 * D_GMLP
D_IN = O_G + N_BRANCH * D_MODEL
P_G = 0
P_A = P_G + (D_IN - O_G)
P_D = P_A + (O_B - O_A)
P_C = P_D + (O_G - O_D)
P_B = -(-(P_C + D_POOL) // D_RWKV_IN) * D_RWKV_IN

TM_MIX = 512
TM_RWKV = 512
FF_CW = 768
RWKV_CHUNK = 64
S_ATT_BB = 8


def _dot(a, b):
    return jnp.dot(a.astype(BF16), b.astype(BF16), preferred_element_type=F32)


def _rms(x, g):
    return x * lax.rsqrt(jnp.mean(x * x, axis=-1, keepdims=True) + EPS) * g


def _gelu(x):
    c = math.sqrt(2.0 / math.pi)
    return 0.5 * x * (1.0 + jnp.tanh(c * (x + 0.044715 * (x * x * x))))


def _softplus(x):
    return jnp.maximum(x, 0.0) + jnp.log1p(jnp.exp(-jnp.abs(x)))


def _sigmoid(x):
    return jax.nn.sigmoid(x)


def _shift_rows(hist, cur, s):
    n = hist.shape[0]
    return pltpu.roll(jnp.concatenate([hist, cur], axis=0), s, 0)[n:]


def _head_ones():
    r = lax.broadcasted_iota(jnp.int32, (LANES, LANES), 0) // RWKV_HEAD
    c = lax.broadcasted_iota(jnp.int32, (LANES, LANES), 1) // RWKV_HEAD
    return jnp.where(r == c, 1.0, 0.0).astype(BF16)


def _segsum(x, ones):
    xb = x.astype(BF16)
    return jnp.concatenate([jnp.dot(xb[:, p * LANES:(p + 1) * LANES], ones, preferred_element_type=F32)
                            for p in range(x.shape[1] // LANES)], axis=-1)


def _blockdiag_dot(x, w_ref):
    outs = []
    for p in range(x.shape[1] // LANES):
        outs.append(jnp.dot(x[:, p * LANES:(p + 1) * LANES].astype(BF16), w_ref[p],
                            preferred_element_type=F32))
    return jnp.concatenate(outs, axis=-1)


def _lru_gates(xc, wra_ref, bra_ref, wix_ref, bix_ref, lam_ref):
    r = _sigmoid(_blockdiag_dot(xc, wra_ref) + bra_ref[...])
    i = _sigmoid(_blockdiag_dot(xc, wix_ref) + bix_ref[...])
    log_a = -LRU_C * r * _softplus(-lam_ref[...])
    a = jnp.exp(log_a)
    mult = jnp.sqrt(-jnp.tanh(log_a) * (1.0 + a * a))
    return a, mult * (i * xc)


def _rwkv_pre(p, prev, mu, wwa_ref, w0, a0, g2_ref, k_k, k_a, ones):
    px = p + (prev - p) * mu
    r = px[:, 0:D_RWKV]
    k = px[:, D_RWKV:2 * D_RWKV]
    v = px[:, 2 * D_RWKV:3 * D_RWKV]
    lo = px[:, 3 * D_RWKV:3 * D_RWKV + R_DECAY + R_AAA]
    g_lo = px[:, 3 * D_RWKV + R_DECAY + R_AAA:]
    lane = lax.broadcasted_iota(jnp.int32, lo.shape, 1)
    wa = _dot(jnp.where(lane < R_DECAY, jnp.tanh(lo), lo), wwa_ref[...])
    logw = -math.exp(-0.5) * _sigmoid(w0 + wa[:, :D_RWKV])
    a = _sigmoid(a0 + wa[:, D_RWKV:])
    g = _dot(_sigmoid(g_lo), g2_ref[...])
    kk = k * k_k
    kk = kk * lax.rsqrt(jnp.maximum(_segsum(kk * kk, ones), 1e-24))
    k2 = k * (1.0 + (a - 1.0) * k_a)
    return r, logw, k2, v, kk, a, g


def _rwkv_post(y, r, k2, v, g, r_k, ln_g, ln_b, ones):
    inv = 1.0 / RWKV_HEAD
    mean = _segsum(y, ones) * inv
    yc = y - mean
    var = _segsum(yc * yc, ones) * inv
    yn = yc * lax.rsqrt(var + GN_EPS) * ln_g + ln_b
    bonus = _segsum(r * k2 * r_k, ones) * v
    return (yn + bonus) * g


def _pool_project(d, wp_ref, scale):
    return _blockdiag_dot(d, wp_ref) * scale


def _gmlp_uz(z2, ln_g, ln_b):
    u = _gelu(z2[:, :D_GMLP])
    v = _gelu(z2[:, D_GMLP:])
    mu = jnp.mean(v, axis=-1, keepdims=True)
    vc = v - mu
    var = jnp.mean(vc * vc, axis=-1, keepdims=True)
    return u, vc * lax.rsqrt(var + 1e-5) * ln_g + ln_b


def _p_lru_body(x_ref, g_ref, w_ref, cw_ref, cb_ref, wra_ref, bra_ref, wix_ref, bix_ref, lam_ref,
                y_ref, convo_ref, ho_ref, hist_ref, h_ref):
    @pl.when(pl.program_id(1) == 0)
    def _():
        hist_ref[...] = jnp.zeros_like(hist_ref)
        h_ref[...] = jnp.zeros_like(h_ref)

    z = _dot(_rms(x_ref[...], g_ref[...]), w_ref[...])
    xa = z[:, :D_LRU]
    ga = z[:, D_LRU:]
    tm = xa.shape[0]
    hist = hist_ref[...]
    cw = cw_ref[...]
    xc = cb_ref[...] + cw[3:4] * xa
    for s in (1, 2, 3):
        xc = xc + cw[3 - s:4 - s] * _shift_rows(hist, xa, s)
    a, b = _lru_gates(xc, wra_ref, bra_ref, wix_ref, bix_ref, lam_ref)
    row = lax.broadcasted_iota(jnp.int32, a.shape, 0)
    s = 1
    while s < tm:
        m = row >= s
        b = jnp.where(m, a * pltpu.roll(b, s, 0) + b, b)
        a = jnp.where(m, a * pltpu.roll(a, s, 0), a)
        s *= 2
    h = a * h_ref[SUBLANES - 1:SUBLANES, :] + b
    y_ref[...] = (h * _gelu(ga)).astype(y_ref.dtype)
    hist_ref[...] = xa[tm - SUBLANES:]
    h_ref[...] = h[tm - SUBLANES:]
    convo_ref[...] = xa[tm - SUBLANES:]
    ho_ref[...] = h[tm - SUBLANES:]


def _pair_blockdiag_rows(x):
    lo = lax.broadcasted_iota(jnp.int32, x.shape, 1) < RWKV_HEAD
    z = jnp.zeros_like(x)
    return jnp.concatenate([jnp.where(lo, x, z), jnp.where(lo, z, x)], axis=0)


def _p_rwkv_body(x_ref, g_ref, w_ref, mu_ref, wwa_ref, w0_ref, a0_ref, g2_ref, kk_ref, ka_ref,
                 rk_ref, lng_ref, lnb_ref,
                 y_ref, shifto_ref, so_ref,
                 hist_ref, st_ref):
    @pl.when(pl.program_id(1) == 0)
    def _():
        hist_ref[...] = jnp.zeros_like(hist_ref)
        st_ref[...] = jnp.zeros_like(st_ref)

    C = RWKV_CHUNK
    ones = _head_ones()
    p = _dot(_rms(x_ref[...], g_ref[...]), w_ref[...])
    tm = p.shape[0]
    nc = tm // C
    npair = D_RWKV // LANES
    prev = _shift_rows(hist_ref[...], p, 1)
    r, logw, k2, v, kk, a, g = _rwkv_pre(p, prev, mu_ref[...], wwa_ref, w0_ref[...], a0_ref[...],
                                          g2_ref, kk_ref[...], ka_ref[...], ones)
    hist_ref[...] = p[tm - SUBLANES:]
    shifto_ref[...] = p[tm - SUBLANES:]
    rowc = lax.broadcasted_iota(jnp.int32, logw.shape, 0) % C
    lg = logw
    s = 1
    while s < C:
        lg = lg + jnp.where(rowc >= s, pltpu.roll(lg, s, 0), 0.0)
        s *= 2
    lg_last = [lg[(c + 1) * C - 1:(c + 1) * C, :] for c in range(nc)]
    lg_end = jnp.concatenate([jnp.broadcast_to(x, (C, D_RWKV)) for x in lg_last], axis=0)
    inv_gam = jnp.exp(-lg)
    to_end = jnp.exp(lg_end - lg)
    b = kk * a
    rt_f = r * jnp.exp(lg)
    at_b, rt_b, bt_b, kt_b, v_b, bh_b, kh_b = (x.astype(BF16) for x in (
        -kk * jnp.exp(lg - logw), rt_f, b * inv_gam, k2 * inv_gam, v, b * to_end, k2 * to_end))

    r2 = lax.broadcasted_iota(jnp.int32, (C, LANES), 0)
    c2 = lax.broadcasted_iota(jnp.int32, (C, LANES), 1) % C
    strict2 = c2 < r2
    incl2 = c2 <= r2
    eye2 = jnp.where(r2 == c2, 1.0, 0.0).astype(F32)
    pair2 = (r2 // 2) == (c2 // 2)
    levels = []
    nb = 2
    while nb < C:
        levels.append(((r2 // (2 * nb)) == (c2 // (2 * nb))) & ((r2 // nb) != (c2 // nb)))
        nb *= 2
    rr = lax.broadcasted_iota(jnp.int32, (LANES, LANES), 0)
    cc = lax.broadcasted_iota(jnp.int32, (LANES, LANES), 1)
    same_head = (rr // RWKV_HEAD) == (cc // RWKV_HEAD)
    eye128 = jnp.where(rr == cc, 1.0, 0.0).astype(F32)
    tn = (((0,), (0,)), ((), ()))
    nt_ = (((1,), (1,)), ((), ()))
    bd = _pair_blockdiag_rows

    def mm(x, y):
        return jnp.dot(x, y, preferred_element_type=F32)

    def blk(x, c, q):
        return x[c * C:(c + 1) * C, q * LANES:(q + 1) * LANES]

    P = [(c, q) for c in range(nc) for q in range(npair)]
    N = range(len(P))
    at, rt, bt, kt, vv, bh, kh = ([blk(x, c, q) for c, q in P] for x in
                                  (at_b, rt_b, bt_b, kt_b, v_b, bh_b, kh_b))
    ar = [jnp.concatenate([at[n], rt[n]], axis=0) for n in N]
    pb = [lax.dot_general(ar[n], bd(bt[n]), nt_, preferred_element_type=F32) for n in N]
    pk = [lax.dot_general(ar[n], bd(kt[n]), nt_, preferred_element_type=F32) for n in N]
    lab = [jnp.where(strict2, x[:C], 0.0) for x in pb]
    mrb = [jnp.where(incl2, x[C:], 0.0).astype(BF16) for x in pb]
    lak = [jnp.where(strict2, x[:C], 0.0).astype(BF16) for x in pk]
    mrk = [jnp.where(incl2, x[C:], 0.0).astype(BF16) for x in pk]
    v_bd = [bd(x) for x in vv]
    lak_v = [mm(lak[n], v_bd[n]) for n in N]
    T = [eye2 + jnp.where(pair2, x, 0.0) for x in lab]
    for mk in levels:
        tb = [x.astype(BF16) for x in T]
        u = [mm(tb[n], bd(jnp.where(mk, lab[n], 0.0).astype(BF16))).astype(BF16) for n in N]
        T = [T[n] + mm(u[n], bd(tb[n])) for n in N]
    tb = [x.astype(BF16) for x in T]
    tab = [mm(tb[n], bd(at[n])).astype(BF16) for n in N]
    cv = [mm(tb[n], bd(lak_v[n].astype(BF16))).astype(BF16) for n in N]
    g_bd = [jnp.where(same_head, lax.dot_general(bh[n], tab[n], tn, preferred_element_type=F32), 0.0)
            .astype(BF16) for n in N]
    q_bd = [jnp.where(same_head, lax.dot_general(jnp.concatenate([bh[n], kh[n]], axis=0),
                                                 jnp.concatenate([cv[n], vv[n]], axis=0), tn,
                                                 preferred_element_type=F32), 0.0) for n in N]
    ry = [(blk(rt_f, c, q) + mm(mrb[n], bd(tab[n]))).astype(BF16) for n, (c, q) in enumerate(P)]
    yc = [mm(jnp.concatenate([mrb[n], mrk[n]], axis=1),
             jnp.concatenate([bd(cv[n]), v_bd[n]], axis=0)) for n in N]
    gcol = [jnp.broadcast_to(jnp.sum(eye128 * jnp.exp(lg_last[c][:, q * LANES:(q + 1) * LANES]),
                                     axis=1, keepdims=True), (LANES, LANES)) for c, q in P]

    st = [st_ref[q] for q in range(npair)]
    y_rows = []
    for c in range(nc):
        sb = [x.astype(BF16) for x in st]
        ns = [c * npair + q for q in range(npair)]
        gs = [mm(g_bd[n], sb[q]) for q, n in enumerate(ns)]
        ys = [mm(ry[n], sb[q]) for q, n in enumerate(ns)]
        st = [st[q] * gcol[n] + gs[q] + q_bd[n] for q, n in enumerate(ns)]
        y_rows.append(jnp.concatenate([ys[q] + yc[n] for q, n in enumerate(ns)], axis=1))
    y = jnp.concatenate(y_rows, axis=0)
    yb = _rwkv_post(y, r, k2, v, g, rk_ref[...], lng_ref[...], lnb_ref[...], ones)
    y_ref[...] = yb.astype(y_ref.dtype)
    for q in range(npair):
        st_ref[q] = st[q]
        so_ref[2 * q] = st[q][:RWKV_HEAD, :RWKV_HEAD].T
        so_ref[2 * q + 1] = st[q][RWKV_HEAD:, RWKV_HEAD:].T


def _p_pool_body(x_ref, g_ref, w_ref, wp_ref, sc_ref, y_ref, poolo_ref, hist_ref):
    t = pl.program_id(1)

    @pl.when(t == 0)
    def _():
        hist_ref[...] = jnp.zeros_like(hist_ref)

    pc = _dot(_rms(x_ref[...], g_ref[...]), w_ref[...])
    tm = pc.shape[0]
    X = jnp.concatenate([hist_ref[...], pc], axis=0)
    pos = t * tm + lax.broadcasted_iota(jnp.int32, (tm, POOL_GW), 0)
    ds = []
    for gi, win in enumerate(POOL_WINDOWS):
        sl = slice(gi * POOL_GW, (gi + 1) * POOL_GW)
        s = X[:, sl]
        sh = 1
        while sh < win:
            s = s + pltpu.roll(s, sh, 0)
            sh *= 2
        cnt = jnp.minimum(pos + 1, win).astype(F32)
        ds.append(s[POOL_HIST:] / cnt - pc[:, sl])
    y = _pool_project(jnp.concatenate(ds, axis=-1), wp_ref, sc_ref[...])
    y_ref[...] = y.astype(y_ref.dtype)
    hist_ref[...] = pc[tm - POOL_HIST:]
    poolo_ref[...] = pc[tm - POOL_HIST:]


def _p_gmlp_body(x_ref, g_ref, w_ref, lng_ref, lnb_ref, ws_ref, bsb_ref, y_ref):
    z2 = _dot(_rms(x_ref[...], g_ref[...]), w_ref[...])
    tm = z2.shape[0]
    u, zv = _gmlp_uz(z2, lng_ref[...], lnb_ref[...])
    tril = (lax.broadcasted_iota(jnp.int32, (CHUNK, CHUNK), 0)
            >= lax.broadcasted_iota(jnp.int32, (CHUNK, CHUNK), 1))
    ws = [jnp.where(tril, ws_ref[gi], 0.0).astype(BF16) for gi in range(GMLP_GROUPS)]
    zb = zv.astype(BF16)
    rows = []
    for c in range(tm // CHUNK):
        cols = [jnp.dot(ws[gi], zb[c * CHUNK:(c + 1) * CHUNK, gi * LANES:(gi + 1) * LANES],
                        preferred_element_type=F32) for gi in range(GMLP_GROUPS)]
        rows.append(jnp.concatenate(cols, axis=-1) + bsb_ref[...])
    s = jnp.concatenate(rows, axis=0)
    y_ref[...] = (u * s).astype(y_ref.dtype)


def _merge_body(x_ref, g_ref, wg_ref, ya_ref, yb_ref, yc_ref, yd_ref, wp_ref, wo_ref, o_ref):
    x = x_ref[...]
    xn = _rms(x, g_ref[...]).astype(BF16)
    merged = None
    for i, y_ref in enumerate((ya_ref, yb_ref, yc_ref, yd_ref)):
        gate = _sigmoid(jnp.dot(xn, wg_ref[:, i * D_MODEL:(i + 1) * D_MODEL],
                                preferred_element_type=F32))
        term = gate * jnp.dot(y_ref[...], wp_ref[i], preferred_element_type=F32)
        merged = term if merged is None else merged + term
    o_ref[...] = x + _dot(merged, wo_ref[...])


def _softmax_rows(s):
    e = jnp.exp(s - jnp.max(s, axis=-1, keepdims=True))
    return e / jnp.sum(e, axis=-1, keepdims=True)


def _p_xattn_body(x_ref, g_ref, wq_ref, k_ref, v_ref, wo_ref, o_ref):
    x = x_ref[...]
    q = _dot(_rms(x, g_ref[...]), wq_ref[...]).astype(BF16)
    kb = k_ref[...].astype(BF16)
    vb = v_ref[...].astype(BF16)
    outs = []
    for h in range(X_HEADS):
        sl = slice(h * X_HEAD_DIM, (h + 1) * X_HEAD_DIM)
        s = lax.dot_general(q[:, sl], kb[:, sl], (((1,), (1,)), ((), ())),
                            preferred_element_type=F32) * (X_HEAD_DIM ** -0.5)
        outs.append(jnp.dot(_softmax_rows(s).astype(BF16), vb[:, sl], preferred_element_type=F32))
    o_ref[...] = x + _dot(jnp.concatenate(outs, axis=-1), wo_ref[...])


def _ffn_conv(hist, h, cw, cb):
    return cb + cw[2:3] * h + cw[1:2] * _shift_rows(hist, h, 1) + cw[0:1] * _shift_rows(hist, h, 2)


def _p_ffn_body(x_ref, g_ref, wug_ref, wuv_ref, cwg_ref, cwv_ref, cbg_ref, cbv_ref, wd_ref,
                o_ref, ffo_ref, hist_ref):
    @pl.when(pl.program_id(1) == 0)
    def _():
        hist_ref[...] = jnp.zeros_like(hist_ref)

    x = x_ref[...]
    tm = x.shape[0]
    xn = _rms(x, g_ref[...]).astype(BF16)
    acc = x
    for j in range(D_FF // FF_CW):
        cs = slice(j * FF_CW, (j + 1) * FF_CW)
        vs = slice(D_FF + j * FF_CW, D_FF + (j + 1) * FF_CW)
        hg = jnp.dot(xn, wug_ref[:, cs], preferred_element_type=F32)
        hv = jnp.dot(xn, wuv_ref[:, cs], preferred_element_type=F32)
        cg = _ffn_conv(hist_ref[:, cs], hg, cwg_ref[:, cs], cbg_ref[:, cs])
        cv = _ffn_conv(hist_ref[:, vs], hv, cwv_ref[:, cs], cbv_ref[:, cs])
        acc = acc + _dot(_gelu(cg) * cv, wd_ref[cs, :])
        hist_ref[:, cs] = hg[tm - SUBLANES:]
        hist_ref[:, vs] = hv[tm - SUBLANES:]
    o_ref[...] = acc
    ffo_ref[...] = hist_ref[...]


def _memkv_body(m_ref, g_ref, wk_ref, wv_ref, k_ref, v_ref):
    mn = _rms(m_ref[...], g_ref[...]).astype(BF16)
    k_ref[...] = jnp.dot(mn, wk_ref[...], preferred_element_type=F32)
    v_ref[...] = jnp.dot(mn, wv_ref[...], preferred_element_type=F32)


def _final_norm_body(x_ref, g_ref, o_ref):
    o_ref[...] = _rms(x_ref[...], g_ref[...])


def _s_mix_body(x_ref, g_ref, wa_ref, wb_ref, wc_ref, wd_ref,
                lconv_ref, lh_ref, cw_ref, cb_ref, wra_ref, bra_ref, wix_ref, bix_ref, lam_ref,
                shift_ref, mu_ref, wwa_ref, w0_ref, a0_ref, g2_ref, kk_ref, ka_ref,
                pool_ref, wp_ref, psc_ref,
                lng_ref, lnb_ref, wsd_ref, bsr_ref,
                ya_ref, xa_ref, h_ref, yc_ref, pc_ref, yd_ref, zv_ref, p_ref,
                rt_ref, wt_ref, kt_ref, vt_ref, kkt_ref, at_ref, gt_ref):
    xn = _rms(x_ref[...], g_ref[...]).astype(BF16)
    z = jnp.dot(xn, wa_ref[...], preferred_element_type=F32)
    xa = z[:, :D_LRU]
    cw = cw_ref[...]
    xc = cb_ref[...] + cw[0:1] * lconv_ref[0] + cw[1:2] * lconv_ref[1] + cw[2:3] * lconv_ref[2] + cw[3:4] * xa
    a, b = _lru_gates(xc, wra_ref, bra_ref, wix_ref, bix_ref, lam_ref)
    h = a * lh_ref[...] + b
    ya_ref[...] = (h * _gelu(z[:, D_LRU:])).astype(ya_ref.dtype)
    xa_ref[...] = xa
    h_ref[...] = h
    ones = _head_ones()
    p = jnp.dot(xn, wb_ref[...], preferred_element_type=F32)
    r, logw, k2, v, kk, aa, g = _rwkv_pre(p, shift_ref[...], mu_ref[...], wwa_ref, w0_ref[...],
                                           a0_ref[...], g2_ref, kk_ref[...], ka_ref[...], ones)
    p_ref[...] = p
    for ref, val in ((rt_ref, r), (wt_ref, jnp.exp(logw)), (kt_ref, k2), (vt_ref, v), (kkt_ref, kk),
                     (at_ref, aa), (gt_ref, g)):
        ref[...] = val.T
    pc = jnp.dot(xn, wc_ref[...], preferred_element_type=F32)
    ds = []
    for gi, win in enumerate(POOL_WINDOWS):
        sl = slice(gi * POOL_GW, (gi + 1) * POOL_GW)
        s = pc[:, sl]
        for j in range(POOL_BUF - (win - 1), POOL_BUF):
            s = s + pool_ref[j][:, sl]
        ds.append(s / float(min(PAST_LEN + 1, win)) - pc[:, sl])
    yc_ref[...] = _pool_project(jnp.concatenate(ds, axis=-1), wp_ref, psc_ref[...]).astype(yc_ref.dtype)
    pc_ref[...] = pc
    u, zv = _gmlp_uz(jnp.dot(xn, wd_ref[...], preferred_element_type=F32), lng_ref[...], lnb_ref[...])
    yd_ref[...] = (u * (wsd_ref[...] * zv + bsr_ref[...])).astype(yd_ref.dtype)
    zv_ref[...] = zv


def _s_rwkv_body(has_prev, s_ref, r_ref, w_ref, k_ref, v_ref, kk_ref, a_ref, g_ref, rk_ref, lng_ref,
                 lnb_ref, *rest):
    y_ref, so_ref, yt_sc = rest[1:] if has_prev else rest
    h = pl.program_id(0)
    kk = kk_ref[...]
    w = w_ref[...]
    k = k_ref[...]
    r = r_ref[...]
    bk = kk * a_ref[...]

    def body(i, carry):
        si = s_ref[i]
        sa = -jnp.sum(si * kk, axis=0, keepdims=True)
        s2 = si * w + sa * bk + v_ref[pl.ds(i, 1), :] * k
        so_ref[i] = s2
        yt_sc[pl.ds(h * RWKV_HEAD + i, 1), :] = jnp.sum(s2 * r, axis=0, keepdims=True)
        return carry

    lax.fori_loop(0, RWKV_HEAD, body, 0, unroll=4)
    rows = pl.ds(pl.multiple_of(h * RWKV_HEAD, RWKV_HEAD), RWKV_HEAD)
    y = yt_sc[rows, :]
    yc = y - jnp.mean(y, axis=0, keepdims=True)
    var = jnp.mean(yc * yc, axis=0, keepdims=True)
    yn = yc * lax.rsqrt(var + GN_EPS) * lng_ref[...] + lnb_ref[...]
    bonus = jnp.sum(r * k * rk_ref[...], axis=0, keepdims=True) * v_ref[...]
    yt_sc[rows, :] = (yn + bonus) * g_ref[...]

    @pl.when(h == pl.num_programs(0) - 1)
    def _():
        y_ref[...] = yt_sc[...].T.astype(y_ref.dtype)


def _s_xattn_body(x_ref, g_ref, wq_ref, k_ref, v_ref, wo_ref, o_ref, q_sc, a_sc):
    i = pl.program_id(0)
    bb = k_ref.shape[0]

    @pl.when(i == 0)
    def _():
        q_sc[...] = _dot(_rms(x_ref[...], g_ref[...]), wq_ref[...])

    halves = X_HEAD_DIM // LANES

    def q_lanes(c, h):
        return slice(h * X_HEAD_DIM + c * LANES, h * X_HEAD_DIM + (c + 1) * LANES)

    for j in range(bb):
        row = pl.ds(i * bb + j, 1)
        qrow = q_sc[row, :]
        q8 = jnp.concatenate([qrow[:, q_lanes(c, h)] for c in range(halves) for h in range(X_HEADS)], axis=0)
        k3 = k_ref[j].reshape(N_MEM, SUBLANES, LANES)
        part = jnp.sum(k3 * q8[None], axis=-1, keepdims=True)
        part = jnp.broadcast_to(part, k3.shape)
        s = (part + pltpu.roll(part, X_HEADS, 1)) * (X_HEAD_DIM ** -0.5)
        e = jnp.exp(s - jnp.max(s, axis=0, keepdims=True))
        pr = e / jnp.sum(e, axis=0, keepdims=True)
        o8 = jnp.sum(pr * v_ref[j].reshape(N_MEM, SUBLANES, LANES), axis=0)
        a_sc[row, :] = jnp.concatenate([o8[c * X_HEADS + h:c * X_HEADS + h + 1, :]
                                        for h in range(X_HEADS) for c in range(halves)], axis=1)

    @pl.when(i == pl.num_programs(0) - 1)
    def _():
        o_ref[...] = x_ref[...] + _dot(a_sc[...], wo_ref[...])


def _s_ffn_body(x_ref, g_ref, wug_ref, wuv_ref, sg_ref, sv_ref, cwg_ref, cwv_ref, cbg_ref, cbv_ref,
                wd_ref, o_ref, hg_ref, hv_ref, acc_ref):
    j = pl.program_id(0)

    @pl.when(j == 0)
    def _():
        acc_ref[...] = x_ref[...]

    xn = _rms(x_ref[...], g_ref[...]).astype(BF16)
    hg = jnp.dot(xn, wug_ref[...], preferred_element_type=F32)
    hv = jnp.dot(xn, wuv_ref[...], preferred_element_type=F32)
    cwg = cwg_ref[...]
    cwv = cwv_ref[...]
    cg = cbg_ref[...] + cwg[0:1] * sg_ref[0] + cwg[1:2] * sg_ref[1] + cwg[2:3] * hg
    cv = cbv_ref[...] + cwv[0:1] * sv_ref[0] + cwv[1:2] * sv_ref[1] + cwv[2:3] * hv
    acc_ref[...] += _dot(_gelu(cg) * cv, wd_ref[...])
    hg_ref[...] = hg
    hv_ref[...] = hv

    @pl.when(j == pl.num_programs(0) - 1)
    def _():
        o_ref[...] = acc_ref[...]


def _params(n_grid, vmem_mb):
    return pltpu.CompilerParams(dimension_semantics=("arbitrary",) * n_grid,
                                vmem_limit_bytes=vmem_mb << 20)


def _whole(shape):
    return pl.BlockSpec(tuple(shape), lambda *_: (0,) * len(shape))


def _layer(arr, l):
    shape = arr.shape[1:]
    return pl.BlockSpec((None,) + tuple(shape), lambda *_: (l,) + (0,) * len(shape))


def _sds(shape, dtype=F32):
    return jax.ShapeDtypeStruct(tuple(shape), dtype)


def _pair_blockdiag(w):
    L = w.shape[0]
    w = w.reshape(L, 4, 2, RWKV_HEAD, RWKV_HEAD)
    z = jnp.zeros_like(w[:, :, 0])
    top = jnp.concatenate([w[:, :, 0], z], axis=-1)
    bot = jnp.concatenate([z, w[:, :, 1]], axis=-1)
    return jnp.concatenate([top, bot], axis=-2)


def _vec(a):
    return a.reshape(a.shape[0], 1, -1)


def kernel(x_prompt, x_sample, state_lru_conv, state_lru_h, state_rwkv_shift, state_rwkv_S, state_pool, state_ffn_conv, cache_mem_k, cache_mem_v, mem_prompt, g_mix, w_in, lru_conv_w, lru_conv_b, lru_w_ra, lru_b_ra, lru_w_ix, lru_b_ix, lru_lambda, rwkv_mu, rwkv_w0, rwkv_w2, rwkv_a0, rwkv_a2, rwkv_g2, rwkv_k_k, rwkv_k_a, rwkv_r_k, rwkv_ln_g, rwkv_ln_b, pool_w, pool_scale, gmlp_ln_g, gmlp_ln_b, gmlp_w_s, gmlp_b_s, w_pa, w_pb, w_pc, w_pd, w_o, g_xattn, g_mem, w_xq, w_xk, w_xv, w_xo, g_ffn, w_up, ffn_conv_w, ffn_conv_b, w_down, g_final):
    B, T, D = x_prompt.shape
    SB = x_sample.shape[0]
    assert D == D_MODEL and w_in.shape == (DEPTH, D_MODEL, D_IN) and x_sample.shape[1] == 1
    assert T % TM_MIX == 0 and T % TM_RWKV == 0 and TM_RWKV % RWKV_CHUNK == 0 and SB % S_ATT_BB == 0
    assert SB == LANES and X_HEADS * (X_HEAD_DIM // LANES) == SUBLANES

    w_in_b = jnp.concatenate(
        [w_in[:, :, O_G:], w_in[:, :, O_A:O_B], w_in[:, :, O_D:O_G], w_in[:, :, O_C:O_D],
         jnp.zeros((DEPTH, D_MODEL, P_B - (P_C + D_POOL)), w_in.dtype), w_in[:, :, O_B:O_C]], axis=-1).astype(BF16)
    wra = _pair_blockdiag(lru_w_ra).astype(BF16)
    wix = _pair_blockdiag(lru_w_ix).astype(BF16)
    zer = jnp.zeros((DEPTH, R_DECAY, D_RWKV), F32)
    wwa = jnp.concatenate([jnp.concatenate([rwkv_w2, zer], axis=-1),
                           jnp.concatenate([zer, rwkv_a2], axis=-1)], axis=1).astype(BF16)
    g2 = rwkv_g2.astype(BF16)
    wpool = pool_w.astype(BF16)
    bsb = jnp.repeat(gmlp_b_s, LANES, axis=-1)
    off = PAST_LEN % CHUNK
    wsd = jnp.repeat(gmlp_w_s[:, :, off, off], LANES, axis=-1)[:, None, :]
    bsr = jnp.repeat(gmlp_b_s[:, off, :], LANES, axis=-1)[:, None, :]
    wP = jnp.stack([w_pa, w_pb, w_pc, w_pd], axis=1).astype(BF16)
    wO = w_o.astype(BF16)
    wXq, wXk, wXv, wXo = (w.astype(BF16) for w in (w_xq, w_xk, w_xv, w_xo))
    w_up_b = w_up.astype(BF16)
    wDn = w_down.astype(BF16)
    cwg, cwv = ffn_conv_w[:, :, :D_FF], ffn_conv_w[:, :, D_FF:]
    cbg, cbv = _vec(ffn_conv_b[:, :D_FF]), _vec(ffn_conv_b[:, D_FF:])
    gmix, gxat, gffn, gmem = _vec(g_mix), _vec(g_xattn), _vec(g_ffn), _vec(g_mem)
    lcb, bra, bix, lam = _vec(lru_conv_b), _vec(lru_b_ra), _vec(lru_b_ix), _vec(lru_lambda)
    mu, w0, a0, kkw, kaw = _vec(rwkv_mu), _vec(rwkv_w0), _vec(rwkv_a0), _vec(rwkv_k_k), _vec(rwkv_k_a)
    rk = rwkv_r_k.reshape(DEPTH, 1, D_RWKV)
    rlg, rlb = _vec(rwkv_ln_g), _vec(rwkv_ln_b)
    psc, glg, glb = _vec(pool_scale), _vec(gmlp_ln_g), _vec(gmlp_ln_b)
    gfin = g_final.reshape(1, D_MODEL)
    rk_col, rlg_col, rlb_col = (a.reshape(DEPTH, D_RWKV, 1) for a in (rwkv_r_k, rwkv_ln_g, rwkv_ln_b))
    state_S5 = jnp.transpose(state_rwkv_S, (0, 2, 3, 4, 1))
    s_S5 = None

    def cache_rows(c):
        c = c.reshape(DEPTH, SB, N_MEM, X_HEADS, X_HEAD_DIM // LANES, LANES)
        return jnp.swapaxes(c, 3, 4).reshape(DEPTH, SB, N_MEM * SUBLANES, LANES)

    cache_k, cache_v = cache_rows(cache_mem_k), cache_rows(cache_mem_v)

    kv_shape = _sds((DEPTH, B, N_MEM, D_MODEL))
    kv_spec = pl.BlockSpec((None, None, N_MEM, D_MODEL), lambda l, b: (l, b, 0, 0))
    wl_spec = pl.BlockSpec((None, D_MODEL, D_MODEL), lambda l, b: (l, 0, 0))
    p_mk, p_mv = pl.pallas_call(
        _memkv_body, grid=(DEPTH, B),
        in_specs=[pl.BlockSpec((None, N_MEM, D_MODEL), lambda l, b: (b, 0, 0)),
                  pl.BlockSpec((None, 1, D_MODEL), lambda l, b: (l, 0, 0)), wl_spec, wl_spec],
        out_specs=[kv_spec, kv_spec], out_shape=[kv_shape, kv_shape],
        compiler_params=_params(2, 32), name="memkv")(mem_prompt, gmem, wXk, wXv)

    nt = T // TM_MIX
    ntr = T // TM_RWKV
    xt_spec = pl.BlockSpec((None, TM_MIX, D_MODEL), lambda b, t: (b, t, 0))
    yt_spec = pl.BlockSpec((None, TM_MIX, D_LRU), lambda b, t: (b, t, 0))

    def tail_spec(rows, width):
        return pl.BlockSpec((None, rows, width), lambda b, t: (b, 0, 0))

    hp = x_prompt
    hs = x_sample.reshape(SB, D_MODEL)
    outs = {k: [] for k in ("p_conv", "s_conv", "p_h", "s_h", "p_shift", "s_shift", "p_S",
                            "p_pool", "s_pool", "p_ffn", "s_ffn", "s_cv")}

    for l in range(DEPTH):
        L = functools.partial(_layer, l=l)

        def win(start, width):
            assert start % width == 0
            return pl.BlockSpec((None, D_MODEL, width), lambda *_: (l, 0, start // width))

        win_a, win_b = win(P_A, O_B - O_A), win(P_B, O_C - O_B)
        win_c, win_d, win_g = win(P_C, O_D - O_C), win(P_D, O_G - O_D), win(P_G, D_IN - O_G)

        wup_g = pl.BlockSpec((None, D_MODEL, D_FF), lambda *_: (l, 0, 0))
        wup_v = pl.BlockSpec((None, D_MODEL, D_FF), lambda *_: (l, 0, 1))

        yA, p_conv8, p_h8 = pl.pallas_call(
            _p_lru_body, grid=(B, nt),
            in_specs=[xt_spec, L(gmix), win_a, L(lru_conv_w), L(lcb), L(wra), L(bra), L(wix), L(bix), L(lam)],
            out_specs=[yt_spec, tail_spec(SUBLANES, D_LRU), tail_spec(SUBLANES, D_LRU)],
            out_shape=[_sds((B, T, D_LRU), BF16), _sds((B, SUBLANES, D_LRU)), _sds((B, SUBLANES, D_LRU))],
            scratch_shapes=[pltpu.VMEM((SUBLANES, D_LRU), F32), pltpu.VMEM((SUBLANES, D_LRU), F32)],
            compiler_params=_params(2, 48), name="p_lru")(
                hp, gmix, w_in_b, lru_conv_w, lcb, wra, bra, wix, bix, lam)

        yB, p_shift8, p_S = pl.pallas_call(
            _p_rwkv_body, grid=(B, ntr),
            in_specs=[pl.BlockSpec((None, TM_RWKV, D_MODEL), lambda b, t: (b, t, 0)),
                      L(gmix), win_b, L(mu), L(wwa), L(w0), L(a0), L(g2), L(kkw), L(kaw), L(rk), L(rlg), L(rlb)],
            out_specs=[pl.BlockSpec((None, TM_RWKV, D_RWKV), lambda b, t: (b, t, 0)),
                       tail_spec(SUBLANES, D_RWKV_IN),
                       pl.BlockSpec((None, RWKV_HEADS, RWKV_HEAD, RWKV_HEAD), lambda b, t: (b, 0, 0, 0))],
            out_shape=[_sds((B, T, D_RWKV), BF16), _sds((B, SUBLANES, D_RWKV_IN)),
                       _sds((B, RWKV_HEADS, RWKV_HEAD, RWKV_HEAD))],
            scratch_shapes=[pltpu.VMEM((SUBLANES, D_RWKV_IN), F32),
                            pltpu.VMEM((D_RWKV // LANES, LANES, LANES), F32)],
            compiler_params=_params(2, 48), name="p_rwkv")(
                hp, gmix, w_in_b, mu, wwa, w0, a0, g2, kkw, kaw, rk, rlg, rlb)

        yC, p_pool16 = pl.pallas_call(
            _p_pool_body, grid=(B, nt),
            in_specs=[xt_spec, L(gmix), win_c, L(wpool), L(psc)],
            out_specs=[yt_spec, tail_spec(POOL_HIST, D_POOL)],
            out_shape=[_sds((B, T, D_POOL), BF16), _sds((B, POOL_HIST, D_POOL))],
            scratch_shapes=[pltpu.VMEM((POOL_HIST, D_POOL), F32)],
            compiler_params=_params(2, 48), name="p_pool")(hp, gmix, w_in_b, wpool, psc)

        yD = pl.pallas_call(
            _p_gmlp_body, grid=(B, nt),
            in_specs=[xt_spec, L(gmix), win_d, L(glg), L(glb), L(gmlp_w_s), L(bsb)],
            out_specs=yt_spec, out_shape=_sds((B, T, D_GMLP), BF16),
            compiler_params=_params(2, 48), name="p_gmlp")(hp, gmix, w_in_b, glg, glb, gmlp_w_s, bsb)

        def merge(x2d, ys, tm):
            rows = x2d.shape[0]
            xs = pl.BlockSpec((tm, D_MODEL), lambda i: (i, 0))
            ysp = pl.BlockSpec((tm, D_LRU), lambda i: (i, 0))
            return pl.pallas_call(
                _merge_body, grid=(rows // tm,),
                in_specs=[xs, L(gmix), win_g, ysp, ysp, ysp, ysp, L(wP), L(wO)],
                out_specs=xs, out_shape=_sds((rows, D_MODEL)),
                compiler_params=_params(1, 56), name="merge")(x2d, gmix, w_in_b, *ys, wP, wO)

        hp = merge(hp.reshape(B * T, D_MODEL), [y.reshape(B * T, -1) for y in (yA, yB, yC, yD)],
                   TM_MIX).reshape(B, T, D_MODEL)

        kvb_spec = pl.BlockSpec((None, None, N_MEM, D_MODEL), lambda b, t: (l, b, 0, 0))
        hp = pl.pallas_call(
            _p_xattn_body, grid=(B, nt),
            in_specs=[xt_spec, L(gxat), L(wXq), kvb_spec, kvb_spec, L(wXo)],
            out_specs=xt_spec, out_shape=_sds((B, T, D_MODEL)),
            compiler_params=_params(2, 48), name="p_xattn")(hp, gxat, wXq, p_mk, p_mv, wXo)

        hp, p_ffn8 = pl.pallas_call(
            _p_ffn_body, grid=(B, nt),
            in_specs=[xt_spec, L(gffn), wup_g, wup_v, L(cwg), L(cwv), L(cbg), L(cbv), L(wDn)],
            out_specs=[xt_spec, tail_spec(SUBLANES, 2 * D_FF)],
            out_shape=[_sds((B, T, D_MODEL)), _sds((B, SUBLANES, 2 * D_FF))],
            scratch_shapes=[pltpu.VMEM((SUBLANES, 2 * D_FF), F32)],
            compiler_params=_params(2, 56), name="p_ffn")(hp, gffn, w_up_b, w_up_b, cwg, cwv, cbg, cbv, wDn)

        outs["p_conv"].append(p_conv8[:, SUBLANES - 3:])
        outs["p_h"].append(p_h8[:, SUBLANES - 1])
        outs["p_shift"].append(p_shift8[:, SUBLANES - 1:])
        outs["p_S"].append(p_S)
        outs["p_pool"].append(p_pool16[:, POOL_HIST - POOL_BUF:])
        outs["p_ffn"].append(p_ffn8[:, SUBLANES - 2:])

        lconv = jnp.swapaxes(state_lru_conv[l], 0, 1)
        spool = jnp.swapaxes(state_pool[l], 0, 1)
        sffn = jnp.swapaxes(state_ffn_conv[l], 0, 1)
        shift = state_rwkv_shift[l].reshape(SB, D_RWKV_IN)
        row512 = _sds((SB, D_LRU))
        chan = _sds((D_RWKV, SB))
        mix_in = [hs, gmix[l], w_in_b, w_in_b, w_in_b, w_in_b,
                  lconv, state_lru_h[l], lru_conv_w[l], lcb[l], wra[l], bra[l], wix[l], bix[l], lam[l],
                  shift, mu[l], wwa[l], w0[l], a0[l], g2[l], kkw[l], kaw[l],
                  spool, wpool[l], psc[l],
                  glg[l], glb[l], wsd[l], bsr[l]]
        mix_out = [_sds((SB, D_LRU), BF16), row512, row512, _sds((SB, D_POOL), BF16), row512,
                   _sds((SB, D_GMLP), BF16), row512, _sds((SB, D_RWKV_IN)),
                   chan, chan, chan, chan, chan, chan, chan]
        (yA, s_xa, s_h, yC, s_pc, yD, s_zv, s_p, rT, wT, kT, vT, kkT, aT, gT) = pl.pallas_call(
            _s_mix_body, grid=(1,),
            in_specs=[_whole(mix_in[0].shape), _whole(mix_in[1].shape),
                      win_a, win_b, win_c, win_d]
                     + [_whole(a.shape) for a in mix_in[6:]],
            out_specs=[_whole(o.shape) for o in mix_out], out_shape=mix_out,
            compiler_params=_params(1, 56), name="s_mix")(*mix_in)

        hd_spec = pl.BlockSpec((RWKV_HEAD, SB), lambda h: (h, 0))
        col_spec = pl.BlockSpec((None, RWKV_HEAD, 1), lambda h: (l, h, 0))
        st_spec = pl.BlockSpec((None, None, RWKV_HEAD, RWKV_HEAD, SB), lambda h: (l, h, 0, 0, 0))
        has_prev = s_S5 is not None
        yB, s_S5 = pl.pallas_call(
            functools.partial(_s_rwkv_body, has_prev), grid=(RWKV_HEADS,),
            in_specs=[st_spec] + [hd_spec] * 7 + [col_spec] * 3
                     + ([pl.BlockSpec(memory_space=pl.ANY)] if has_prev else []),
            out_specs=[_whole((SB, D_RWKV)), st_spec],
            out_shape=[_sds((SB, D_RWKV), BF16), _sds(state_S5.shape)],
            scratch_shapes=[pltpu.VMEM((D_RWKV, SB), F32)],
            input_output_aliases={11: 1} if has_prev else {},
            compiler_params=_params(1, 48), name="s_rwkv")(
                state_S5, rT, wT, kT, vT, kkT, aT, gT, rk_col, rlg_col, rlb_col,
                *([s_S5] if has_prev else []))

        hs = merge(hs, [yA, yB, yC, yD], SB)

        kc_spec = pl.BlockSpec((None, S_ATT_BB, N_MEM * SUBLANES, LANES), lambda i: (l, i, 0, 0))
        xs_spec = _whole((SB, D_MODEL))
        hs = pl.pallas_call(
            _s_xattn_body, grid=(SB // S_ATT_BB,),
            in_specs=[xs_spec, L(gxat), L(wXq), kc_spec, kc_spec, L(wXo)],
            out_specs=xs_spec, out_shape=_sds((SB, D_MODEL)),
            scratch_shapes=[pltpu.VMEM((SB, D_MODEL), F32), pltpu.VMEM((SB, D_MODEL), F32)],
            compiler_params=_params(1, 56), name="s_xattn")(
                hs, gxat, wXq, cache_k, cache_v, wXo)

        ncf = D_FF // FF_CW
        wu_spec = pl.BlockSpec((None, D_MODEL, FF_CW), lambda j: (l, 0, j))
        wuv_spec = pl.BlockSpec((None, D_MODEL, FF_CW), lambda j: (l, 0, j + ncf))
        sg_spec = pl.BlockSpec((2, SB, FF_CW), lambda j: (0, 0, j))
        sv_spec = pl.BlockSpec((2, SB, FF_CW), lambda j: (0, 0, j + ncf))
        cw_spec = pl.BlockSpec((None, 3, FF_CW), lambda j: (l, 0, j))
        cb_spec = pl.BlockSpec((None, 1, FF_CW), lambda j: (l, 0, j))
        hh_spec = pl.BlockSpec((SB, FF_CW), lambda j: (0, j))
        hs, s_hg, s_hv = pl.pallas_call(
            _s_ffn_body, grid=(ncf,),
            in_specs=[xs_spec, L(gffn), wu_spec, wuv_spec, sg_spec, sv_spec, cw_spec, cw_spec, cb_spec, cb_spec,
                      pl.BlockSpec((None, FF_CW, D_MODEL), lambda j: (l, j, 0))],
            out_specs=[xs_spec, hh_spec, hh_spec],
            out_shape=[_sds((SB, D_MODEL)), _sds((SB, D_FF)), _sds((SB, D_FF))],
            scratch_shapes=[pltpu.VMEM((SB, D_MODEL), F32)],
            compiler_params=_params(1, 48), name="s_ffn")(
                hs, gffn, w_up_b, w_up_b, sffn, sffn, cwg, cwv, cbg, cbv, wDn)

        outs["s_conv"].append(jnp.concatenate([state_lru_conv[l][:, 1:], s_xa[:, None]], axis=1))
        outs["s_h"].append(s_h)
        outs["s_shift"].append(s_p[:, None])
        outs["s_pool"].append(jnp.concatenate([state_pool[l][:, 1:], s_pc[:, None]], axis=1))
        outs["s_ffn"].append(jnp.concatenate(
            [state_ffn_conv[l][:, 1:], jnp.concatenate([s_hg, s_hv], axis=-1)[:, None]], axis=1))
        outs["s_cv"].append(s_zv[:, None])

    def final_norm(x2d, tm):
        rows = x2d.shape[0]
        spec = pl.BlockSpec((tm, D_MODEL), lambda i: (i, 0))
        return pl.pallas_call(_final_norm_body, grid=(rows // tm,),
                              in_specs=[spec, _whole((1, D_MODEL))], out_specs=spec,
                              out_shape=_sds((rows, D_MODEL)),
                              compiler_params=_params(1, 32), name="final_norm")(x2d, gfin)

    y_prompt = final_norm(hp.reshape(B * T, D_MODEL), 1024).reshape(B, T, D_MODEL)
    y_sample = final_norm(hs, SB).reshape(SB, 1, D_MODEL)
    stk = {k: jnp.stack(v, axis=0) for k, v in outs.items()}
    mem_shape = (DEPTH, B, N_MEM, X_HEADS, X_HEAD_DIM)
    return (y_prompt, y_sample,
            stk["p_conv"], stk["s_conv"], stk["p_h"], stk["s_h"],
            stk["p_shift"], stk["s_shift"], stk["p_S"], jnp.transpose(s_S5, (0, 4, 1, 2, 3)),
            stk["p_pool"], stk["s_pool"], stk["p_ffn"], stk["s_ffn"],
            p_mk.reshape(mem_shape), p_mv.reshape(mem_shape), stk["s_cv"])
```

```python
import functools
import math

import jax
import jax.numpy as jnp
from jax import lax
from jax.experimental import pallas as pl
from jax.experimental.pallas import tpu as pltpu

F32 = jnp.float32
BF16 = jnp.bfloat16

SUBLANES = 8
LANES = 128

D_MODEL = 1024
DEPTH = 4
N_MEM = 256
D_LRU = 512
LRU_C = 8.0
D_RWKV = 512
RWKV_HEAD = 64
RWKV_HEADS = D_RWKV // RWKV_HEAD
R_DECAY = 64
R_AAA = 64
R_GATE = 128
D_RWKV_IN = 3 * D_RWKV + R_DECAY + R_AAA + R_GATE
GN_EPS = 64e-5
D_POOL = 512
POOL_WINDOWS = (2, 4, 8, 16)
POOL_GW = D_POOL // len(POOL_WINDOWS)
POOL_BUF = max(POOL_WINDOWS) - 1
POOL_HIST = 16
D_GMLP = 512
GMLP_GROUPS = 4
CHUNK = 128
N_BRANCH = 4
X_HEADS = 4
X_HEAD_DIM = D_MODEL // X_HEADS
D_FF = 3 * D_MODEL
EPS = 1e-6
PAST_LEN = 16384

O_A = 0
O_B = 2 * D_LRU
O_C = O_B + D_RWKV_IN
O_D = O_C + D_POOL
O_G = O_D + 2 * D_GMLP
D_IN = O_G + N_BRANCH * D_MODEL
P_G = 0
P_A = P_G + (D_IN - O_G)
P_D = P_A + (O_B - O_A)
P_C = P_D + (O_G - O_D)
P_B = -(-(P_C + D_POOL) // D_RWKV_IN) * D_RWKV_IN

TM_MIX = 512
TM_RWKV = 512
FF_CW = 1536
RWKV_CHUNK = 64
S_ATT_BB = 8


def _dot(a, b):
    return jnp.dot(a.astype(BF16), b.astype(BF16), preferred_element_type=F32)


def _rms(x, g):
    return x * lax.rsqrt(jnp.mean(x * x, axis=-1, keepdims=True) + EPS) * g


def _gelu(x):
    c = math.sqrt(2.0 / math.pi)
    return 0.5 * x * (1.0 + jnp.tanh(c * (x + 0.044715 * (x * x * x))))


def _softplus(x):
    return jnp.maximum(x, 0.0) + jnp.log1p(jnp.exp(-jnp.abs(x)))


def _sigmoid(x):
    return jax.nn.sigmoid(x)


def _shift_rows(hist, cur, s):
    n = hist.shape[0]
    return pltpu.roll(jnp.concatenate([hist, cur], axis=0), s, 0)[n:]


def _head_ones():
    r = lax.broadcasted_iota(jnp.int32, (LANES, LANES), 0) // RWKV_HEAD
    c = lax.broadcasted_iota(jnp.int32, (LANES, LANES), 1) // RWKV_HEAD
    return jnp.where(r == c, 1.0, 0.0).astype(BF16)


def _segsum(x, ones):
    xb = x.astype(BF16)
    return jnp.concatenate([jnp.dot(xb[:, p * LANES:(p + 1) * LANES], ones, preferred_element_type=F32)
                            for p in range(x.shape[1] // LANES)], axis=-1)


def _blockdiag_dot(x, w_ref):
    outs = []
    for p in range(x.shape[1] // LANES):
        outs.append(jnp.dot(x[:, p * LANES:(p + 1) * LANES].astype(BF16), w_ref[p],
                            preferred_element_type=F32))
    return jnp.concatenate(outs, axis=-1)


def _lru_gates(xc, wra_ref, bra_ref, wix_ref, bix_ref, lam_ref):
    r = _sigmoid(_blockdiag_dot(xc, wra_ref) + bra_ref[...])
    i = _sigmoid(_blockdiag_dot(xc, wix_ref) + bix_ref[...])
    log_a = -LRU_C * r * _softplus(-lam_ref[...])
    a = jnp.exp(log_a)
    mult = jnp.sqrt(-jnp.tanh(log_a) * (1.0 + a * a))
    return a, mult * (i * xc)


def _rwkv_pre(p, prev, mu, wwa_ref, w0, a0, g2_ref, k_k, k_a, ones):
    px = p + (prev - p) * mu
    r = px[:, 0:D_RWKV]
    k = px[:, D_RWKV:2 * D_RWKV]
    v = px[:, 2 * D_RWKV:3 * D_RWKV]
    lo = px[:, 3 * D_RWKV:3 * D_RWKV + R_DECAY + R_AAA]
    g_lo = px[:, 3 * D_RWKV + R_DECAY + R_AAA:]
    lane = lax.broadcasted_iota(jnp.int32, lo.shape, 1)
    wa = _dot(jnp.where(lane < R_DECAY, jnp.tanh(lo), lo), wwa_ref[...])
    logw = -math.exp(-0.5) * _sigmoid(w0 + wa[:, :D_RWKV])
    a = _sigmoid(a0 + wa[:, D_RWKV:])
    g = _dot(_sigmoid(g_lo), g2_ref[...])
    kk = k * k_k
    kk = kk * lax.rsqrt(jnp.maximum(_segsum(kk * kk, ones), 1e-24))
    k2 = k * (1.0 + (a - 1.0) * k_a)
    return r, logw, k2, v, kk, a, g


def _rwkv_post(y, r, k2, v, g, r_k, ln_g, ln_b, ones):
    inv = 1.0 / RWKV_HEAD
    mean = _segsum(y, ones) * inv
    yc = y - mean
    var = _segsum(yc * yc, ones) * inv
    yn = yc * lax.rsqrt(var + GN_EPS) * ln_g + ln_b
    bonus = _segsum(r * k2 * r_k, ones) * v
    return (yn + bonus) * g


def _pool_project(d, wp_ref, scale):
    return _blockdiag_dot(d, wp_ref) * scale


def _gmlp_uz(z2, ln_g, ln_b):
    u = _gelu(z2[:, :D_GMLP])
    v = _gelu(z2[:, D_GMLP:])
    mu = jnp.mean(v, axis=-1, keepdims=True)
    vc = v - mu
    var = jnp.mean(vc * vc, axis=-1, keepdims=True)
    return u, vc * lax.rsqrt(var + 1e-5) * ln_g + ln_b


def _p_lru_body(x_ref, g_ref, w_ref, cw_ref, cb_ref, wra_ref, bra_ref, wix_ref, bix_ref, lam_ref,
                y_ref, convo_ref, ho_ref, hist_ref, h_ref):
    @pl.when(pl.program_id(1) == 0)
    def _():
        hist_ref[...] = jnp.zeros_like(hist_ref)
        h_ref[...] = jnp.zeros_like(h_ref)

    z = _dot(_rms(x_ref[...], g_ref[...]), w_ref[...])
    xa = z[:, :D_LRU]
    ga = z[:, D_LRU:]
    tm = xa.shape[0]
    hist = hist_ref[...]
    cw = cw_ref[...]
    xc = cb_ref[...] + cw[3:4] * xa
    for s in (1, 2, 3):
        xc = xc + cw[3 - s:4 - s] * _shift_rows(hist, xa, s)
    a, b = _lru_gates(xc, wra_ref, bra_ref, wix_ref, bix_ref, lam_ref)
    row = lax.broadcasted_iota(jnp.int32, a.shape, 0)
    s = 1
    while s < tm:
        m = row >= s
        b = jnp.where(m, a * pltpu.roll(b, s, 0) + b, b)
        a = jnp.where(m, a * pltpu.roll(a, s, 0), a)
        s *= 2
    h = a * h_ref[SUBLANES - 1:SUBLANES, :] + b
    y_ref[...] = (h * _gelu(ga)).astype(y_ref.dtype)
    hist_ref[...] = xa[tm - SUBLANES:]
    h_ref[...] = h[tm - SUBLANES:]
    convo_ref[...] = xa[tm - SUBLANES:]
    ho_ref[...] = h[tm - SUBLANES:]


def _pair_blockdiag_rows(x):
    lo = lax.broadcasted_iota(jnp.int32, x.shape, 1) < RWKV_HEAD
    z = jnp.zeros_like(x)
    return jnp.concatenate([jnp.where(lo, x, z), jnp.where(lo, z, x)], axis=0)


def _p_rwkv_body(x_ref, g_ref, w_ref, mu_ref, wwa_ref, w0_ref, a0_ref, g2_ref, kk_ref, ka_ref,
                 rk_ref, lng_ref, lnb_ref,
                 y_ref, shifto_ref, so_ref,
                 hist_ref, st_ref):
    @pl.when(pl.program_id(1) == 0)
    def _():
        hist_ref[...] = jnp.zeros_like(hist_ref)
        st_ref[...] = jnp.zeros_like(st_ref)

    C = RWKV_CHUNK
    ones = _head_ones()
    p = _dot(_rms(x_ref[...], g_ref[...]), w_ref[...])
    tm = p.shape[0]
    nc = tm // C
    npair = D_RWKV // LANES
    prev = _shift_rows(hist_ref[...], p, 1)
    r, logw, k2, v, kk, a, g = _rwkv_pre(p, prev, mu_ref[...], wwa_ref, w0_ref[...], a0_ref[...],
                                          g2_ref, kk_ref[...], ka_ref[...], ones)
    hist_ref[...] = p[tm - SUBLANES:]
    shifto_ref[...] = p[tm - SUBLANES:]
    rowc = lax.broadcasted_iota(jnp.int32, logw.shape, 0) % C
    lg = logw
    s = 1
    while s < C:
        lg = lg + jnp.where(rowc >= s, pltpu.roll(lg, s, 0), 0.0)
        s *= 2
    lg_last = [lg[(c + 1) * C - 1:(c + 1) * C, :] for c in range(nc)]
    lg_end = jnp.concatenate([jnp.broadcast_to(x, (C, D_RWKV)) for x in lg_last], axis=0)
    inv_gam = jnp.exp(-lg)
    to_end = jnp.exp(lg_end - lg)
    b = kk * a
    rt_f = r * jnp.exp(lg)
    at_b, rt_b, bt_b, kt_b, v_b, bh_b, kh_b = (x.astype(BF16) for x in (
        -kk * jnp.exp(lg - logw), rt_f, b * inv_gam, k2 * inv_gam, v, b * to_end, k2 * to_end))

    r2 = lax.broadcasted_iota(jnp.int32, (C, LANES), 0)
    c2 = lax.broadcasted_iota(jnp.int32, (C, LANES), 1) % C
    strict2 = c2 < r2
    incl2 = c2 <= r2
    eye2 = jnp.where(r2 == c2, 1.0, 0.0).astype(F32)
    pair2 = (r2 // 2) == (c2 // 2)
    levels = []
    nb = 2
    while nb < C:
        levels.append(((r2 // (2 * nb)) == (c2 // (2 * nb))) & ((r2 // nb) != (c2 // nb)))
        nb *= 2
    rr = lax.broadcasted_iota(jnp.int32, (LANES, LANES), 0)
    cc = lax.broadcasted_iota(jnp.int32, (LANES, LANES), 1)
    same_head = (rr // RWKV_HEAD) == (cc // RWKV_HEAD)
    eye128 = jnp.where(rr == cc, 1.0, 0.0).astype(F32)
    tn = (((0,), (0,)), ((), ()))
    nt_ = (((1,), (1,)), ((), ()))
    bd = _pair_blockdiag_rows

    def mm(x, y):
        return jnp.dot(x, y, preferred_element_type=F32)

    def blk(x, c, q):
        return x[c * C:(c + 1) * C, q * LANES:(q + 1) * LANES]

    P = [(c, q) for c in range(nc) for q in range(npair)]
    N = range(len(P))
    at, rt, bt, kt, vv, bh, kh = ([blk(x, c, q) for c, q in P] for x in
                                  (at_b, rt_b, bt_b, kt_b, v_b, bh_b, kh_b))
    ar = [jnp.concatenate([at[n], rt[n]], axis=0) for n in N]
    pb = [lax.dot_general(ar[n], bd(bt[n]), nt_, preferred_element_type=F32) for n in N]
    pk = [lax.dot_general(ar[n], bd(kt[n]), nt_, preferred_element_type=F32) for n in N]
    lab = [jnp.where(strict2, x[:C], 0.0) for x in pb]
    mrb = [jnp.where(incl2, x[C:], 0.0).astype(BF16) for x in pb]
    lak = [jnp.where(strict2, x[:C], 0.0).astype(BF16) for x in pk]
    mrk = [jnp.where(incl2, x[C:], 0.0).astype(BF16) for x in pk]
    v_bd = [bd(x) for x in vv]
    lak_v = [mm(lak[n], v_bd[n]) for n in N]
    T = [eye2 + jnp.where(pair2, x, 0.0) for x in lab]
    for mk in levels:
        tb = [x.astype(BF16) for x in T]
        u = [mm(tb[n], bd(jnp.where(mk, lab[n], 0.0).astype(BF16))).astype(BF16) for n in N]
        T = [T[n] + mm(u[n], bd(tb[n])) for n in N]
    tb = [x.astype(BF16) for x in T]
    tab = [mm(tb[n], bd(at[n])).astype(BF16) for n in N]
    cv = [mm(tb[n], bd(lak_v[n].astype(BF16))).astype(BF16) for n in N]
    g_bd = [jnp.where(same_head, lax.dot_general(bh[n], tab[n], tn, preferred_element_type=F32), 0.0)
            .astype(BF16) for n in N]
    q_bd = [jnp.where(same_head, lax.dot_general(jnp.concatenate([bh[n], kh[n]], axis=0),
                                                 jnp.concatenate([cv[n], vv[n]], axis=0), tn,
                                                 preferred_element_type=F32), 0.0) for n in N]
    ry = [(blk(rt_f, c, q) + mm(mrb[n], bd(tab[n]))).astype(BF16) for n, (c, q) in enumerate(P)]
    yc = [mm(jnp.concatenate([mrb[n], mrk[n]], axis=1),
             jnp.concatenate([bd(cv[n]), v_bd[n]], axis=0)) for n in N]
    gcol = [jnp.broadcast_to(jnp.sum(eye128 * jnp.exp(lg_last[c][:, q * LANES:(q + 1) * LANES]),
                                     axis=1, keepdims=True), (LANES, LANES)) for c, q in P]

    st = [st_ref[q] for q in range(npair)]
    y_rows = []
    for c in range(nc):
        sb = [x.astype(BF16) for x in st]
        ns = [c * npair + q for q in range(npair)]
        gs = [mm(g_bd[n], sb[q]) for q, n in enumerate(ns)]
        ys = [mm(ry[n], sb[q]) for q, n in enumerate(ns)]
        st = [st[q] * gcol[n] + gs[q] + q_bd[n] for q, n in enumerate(ns)]
        y_rows.append(jnp.concatenate([ys[q] + yc[n] for q, n in enumerate(ns)], axis=1))
    y = jnp.concatenate(y_rows, axis=0)
    yb = _rwkv_post(y, r, k2, v, g, rk_ref[...], lng_ref[...], lnb_ref[...], ones)
    y_ref[...] = yb.astype(y_ref.dtype)
    for q in range(npair):
        st_ref[q] = st[q]
        so_ref[2 * q] = st[q][:RWKV_HEAD, :RWKV_HEAD].T
        so_ref[2 * q + 1] = st[q][RWKV_HEAD:, RWKV_HEAD:].T


def _p_pool_body(x_ref, g_ref, w_ref, wp_ref, sc_ref, y_ref, poolo_ref, hist_ref):
    t = pl.program_id(1)

    @pl.when(t == 0)
    def _():
        hist_ref[...] = jnp.zeros_like(hist_ref)

    pc = _dot(_rms(x_ref[...], g_ref[...]), w_ref[...])
    tm = pc.shape[0]
    X = jnp.concatenate([hist_ref[...], pc], axis=0)
    pos = t * tm + lax.broadcasted_iota(jnp.int32, (tm, POOL_GW), 0)
    ds = []
    for gi, win in enumerate(POOL_WINDOWS):
        sl = slice(gi * POOL_GW, (gi + 1) * POOL_GW)
        s = X[:, sl]
        sh = 1
        while sh < win:
            s = s + pltpu.roll(s, sh, 0)
            sh *= 2
        cnt = jnp.minimum(pos + 1, win).astype(F32)
        ds.append(s[POOL_HIST:] / cnt - pc[:, sl])
    y = _pool_project(jnp.concatenate(ds, axis=-1), wp_ref, sc_ref[...])
    y_ref[...] = y.astype(y_ref.dtype)
    hist_ref[...] = pc[tm - POOL_HIST:]
    poolo_ref[...] = pc[tm - POOL_HIST:]


def _p_gmlp_body(x_ref, g_ref, w_ref, lng_ref, lnb_ref, ws_ref, bsb_ref, y_ref):
    z2 = _dot(_rms(x_ref[...], g_ref[...]), w_ref[...])
    tm = z2.shape[0]
    u, zv = _gmlp_uz(z2, lng_ref[...], lnb_ref[...])
    tril = (lax.broadcasted_iota(jnp.int32, (CHUNK, CHUNK), 0)
            >= lax.broadcasted_iota(jnp.int32, (CHUNK, CHUNK), 1))
    ws = [jnp.where(tril, ws_ref[gi], 0.0).astype(BF16) for gi in range(GMLP_GROUPS)]
    zb = zv.astype(BF16)
    rows = []
    for c in range(tm // CHUNK):
        cols = [jnp.dot(ws[gi], zb[c * CHUNK:(c + 1) * CHUNK, gi * LANES:(gi + 1) * LANES],
                        preferred_element_type=F32) for gi in range(GMLP_GROUPS)]
        rows.append(jnp.concatenate(cols, axis=-1) + bsb_ref[...])
    s = jnp.concatenate(rows, axis=0)
    y_ref[...] = (u * s).astype(y_ref.dtype)


def _merge_body(x_ref, g_ref, wg_ref, ya_ref, yb_ref, yc_ref, yd_ref, wp_ref, wo_ref, o_ref):
    x = x_ref[...]
    xn = _rms(x, g_ref[...]).astype(BF16)
    merged = None
    for i, y_ref in enumerate((ya_ref, yb_ref, yc_ref, yd_ref)):
        gate = _sigmoid(jnp.dot(xn, wg_ref[:, i * D_MODEL:(i + 1) * D_MODEL],
                                preferred_element_type=F32))
        term = gate * jnp.dot(y_ref[...], wp_ref[i], preferred_element_type=F32)
        merged = term if merged is None else merged + term
    o_ref[...] = x + _dot(merged, wo_ref[...])


def _softmax_rows(s):
    e = jnp.exp(s - jnp.max(s, axis=-1, keepdims=True))
    return e / jnp.sum(e, axis=-1, keepdims=True)


def _kv_rows(half, h):
    return pl.ds(half * X_HEADS + h, N_MEM, stride=SUBLANES)


def _p_xattn_body(x_ref, g_ref, wq_ref, k_ref, v_ref, wo_ref, o_ref):
    x = x_ref[...]
    q = _dot(_rms(x, g_ref[...]), wq_ref[...]).astype(BF16)
    nt_ = (((1,), (1,)), ((), ()))
    outs = []
    for h in range(X_HEADS):
        s = None
        for half in range(X_HEAD_DIM // LANES):
            lo = h * X_HEAD_DIM + half * LANES
            part = lax.dot_general(q[:, lo:lo + LANES], k_ref[_kv_rows(half, h), :].astype(BF16), nt_,
                                   preferred_element_type=F32)
            s = part if s is None else s + part
        pr = _softmax_rows(s * (X_HEAD_DIM ** -0.5)).astype(BF16)
        for half in range(X_HEAD_DIM // LANES):
            outs.append(jnp.dot(pr, v_ref[_kv_rows(half, h), :].astype(BF16), preferred_element_type=F32))
    o_ref[...] = x + _dot(jnp.concatenate(outs, axis=-1), wo_ref[...])


def _ffn_conv(hist, h, cw, cb):
    return cb + cw[2:3] * h + cw[1:2] * _shift_rows(hist, h, 1) + cw[0:1] * _shift_rows(hist, h, 2)


def _p_ffn_body(x_ref, g_ref, wug_ref, wuv_ref, cwg_ref, cwv_ref, cbg_ref, cbv_ref, wd_ref,
                o_ref, ffo_ref, hist_ref):
    @pl.when(pl.program_id(1) == 0)
    def _():
        hist_ref[...] = jnp.zeros_like(hist_ref)

    x = x_ref[...]
    tm = x.shape[0]
    xn = _rms(x, g_ref[...]).astype(BF16)
    acc = x
    for j in range(D_FF // FF_CW):
        cs = slice(j * FF_CW, (j + 1) * FF_CW)
        vs = slice(D_FF + j * FF_CW, D_FF + (j + 1) * FF_CW)
        hg = jnp.dot(xn, wug_ref[:, cs], preferred_element_type=F32)
        hv = jnp.dot(xn, wuv_ref[:, cs], preferred_element_type=F32)
        cg = _ffn_conv(hist_ref[:, cs], hg, cwg_ref[:, cs], cbg_ref[:, cs])
        cv = _ffn_conv(hist_ref[:, vs], hv, cwv_ref[:, cs], cbv_ref[:, cs])
        acc = acc + _dot(_gelu(cg) * cv, wd_ref[cs, :])
        hist_ref[:, cs] = hg[tm - SUBLANES:]
        hist_ref[:, vs] = hv[tm - SUBLANES:]
    o_ref[...] = acc
    ffo_ref[...] = hist_ref[...]


def _memkv_body(m_ref, g_ref, wk_ref, wv_ref, k_ref, v_ref):
    mn = _rms(m_ref[...], g_ref[...]).astype(BF16)
    for w_ref, o_ref in ((wk_ref, k_ref), (wv_ref, v_ref)):
        kv = jnp.dot(mn, w_ref[...], preferred_element_type=F32)
        for h in range(X_HEADS):
            for half in range(X_HEAD_DIM // LANES):
                lo = h * X_HEAD_DIM + half * LANES
                o_ref[_kv_rows(half, h), :] = kv[:, lo:lo + LANES]


def _final_norm_body(x_ref, g_ref, o_ref):
    o_ref[...] = _rms(x_ref[...], g_ref[...])


def _s_mix_body(x_ref, g_ref, wa_ref, wb_ref, wc_ref, wd_ref,
                lconv_ref, lh_ref, cw_ref, cb_ref, wra_ref, bra_ref, wix_ref, bix_ref, lam_ref,
                shift_ref, mu_ref, wwa_ref, w0_ref, a0_ref, g2_ref, kk_ref, ka_ref,
                pool_ref, wp_ref, psc_ref,
                lng_ref, lnb_ref, wsd_ref, bsr_ref,
                ya_ref, xa_ref, h_ref, yc_ref, pc_ref, yd_ref, zv_ref, p_ref,
                rt_ref, wt_ref, kt_ref, vt_ref, kkt_ref, at_ref, gt_ref):
    xn = _rms(x_ref[...], g_ref[...]).astype(BF16)
    z = jnp.dot(xn, wa_ref[...], preferred_element_type=F32)
    xa = z[:, :D_LRU]
    cw = cw_ref[...]
    xc = cb_ref[...] + cw[0:1] * lconv_ref[0] + cw[1:2] * lconv_ref[1] + cw[2:3] * lconv_ref[2] + cw[3:4] * xa
    a, b = _lru_gates(xc, wra_ref, bra_ref, wix_ref, bix_ref, lam_ref)
    h = a * lh_ref[...] + b
    ya_ref[...] = (h * _gelu(z[:, D_LRU:])).astype(ya_ref.dtype)
    xa_ref[...] = xa
    h_ref[...] = h
    ones = _head_ones()
    p = jnp.dot(xn, wb_ref[...], preferred_element_type=F32)
    r, logw, k2, v, kk, aa, g = _rwkv_pre(p, shift_ref[...], mu_ref[...], wwa_ref, w0_ref[...],
                                           a0_ref[...], g2_ref, kk_ref[...], ka_ref[...], ones)
    p_ref[...] = p
    for ref, val in ((rt_ref, r), (wt_ref, jnp.exp(logw)), (kt_ref, k2), (vt_ref, v), (kkt_ref, kk),
                     (at_ref, aa), (gt_ref, g)):
        ref[...] = val.T
    pc = jnp.dot(xn, wc_ref[...], preferred_element_type=F32)
    ds = []
    for gi, win in enumerate(POOL_WINDOWS):
        sl = slice(gi * POOL_GW, (gi + 1) * POOL_GW)
        s = pc[:, sl]
        for j in range(POOL_BUF - (win - 1), POOL_BUF):
            s = s + pool_ref[j][:, sl]
        ds.append(s / float(min(PAST_LEN + 1, win)) - pc[:, sl])
    yc_ref[...] = _pool_project(jnp.concatenate(ds, axis=-1), wp_ref, psc_ref[...]).astype(yc_ref.dtype)
    pc_ref[...] = pc
    u, zv = _gmlp_uz(jnp.dot(xn, wd_ref[...], preferred_element_type=F32), lng_ref[...], lnb_ref[...])
    yd_ref[...] = (u * (wsd_ref[...] * zv + bsr_ref[...])).astype(yd_ref.dtype)
    zv_ref[...] = zv


def _s_rwkv_body(has_prev, s_ref, r_ref, w_ref, k_ref, v_ref, kk_ref, a_ref, g_ref, rk_ref, lng_ref,
                 lnb_ref, *rest):
    y_ref, so_ref, yt_sc = rest[1:] if has_prev else rest
    h = pl.program_id(0)
    kk = kk_ref[...]
    w = w_ref[...]
    k = k_ref[...]
    r = r_ref[...]
    bk = kk * a_ref[...]

    def body(i, carry):
        si = s_ref[i]
        sa = -jnp.sum(si * kk, axis=0, keepdims=True)
        s2 = si * w + sa * bk + v_ref[pl.ds(i, 1), :] * k
        so_ref[i] = s2
        yt_sc[pl.ds(h * RWKV_HEAD + i, 1), :] = jnp.sum(s2 * r, axis=0, keepdims=True)
        return carry

    lax.fori_loop(0, RWKV_HEAD, body, 0, unroll=4)
    rows = pl.ds(pl.multiple_of(h * RWKV_HEAD, RWKV_HEAD), RWKV_HEAD)
    y = yt_sc[rows, :]
    yc = y - jnp.mean(y, axis=0, keepdims=True)
    var = jnp.mean(yc * yc, axis=0, keepdims=True)
    yn = yc * lax.rsqrt(var + GN_EPS) * lng_ref[...] + lnb_ref[...]
    bonus = jnp.sum(r * k * rk_ref[...], axis=0, keepdims=True) * v_ref[...]
    yt_sc[rows, :] = (yn + bonus) * g_ref[...]

    @pl.when(h == pl.num_programs(0) - 1)
    def _():
        y_ref[...] = yt_sc[...].T.astype(y_ref.dtype)


def _s_xattn_body(x_ref, g_ref, wq_ref, k_ref, v_ref, wo_ref, o_ref, q_sc, a_sc):
    i = pl.program_id(0)
    bb = k_ref.shape[0]

    @pl.when(i == 0)
    def _():
        q_sc[...] = _dot(_rms(x_ref[...], g_ref[...]), wq_ref[...])

    halves = X_HEAD_DIM // LANES

    def q_lanes(c, h):
        return slice(h * X_HEAD_DIM + c * LANES, h * X_HEAD_DIM + (c + 1) * LANES)

    for j in range(bb):
        row = pl.ds(i * bb + j, 1)
        qrow = q_sc[row, :]
        q8 = jnp.concatenate([qrow[:, q_lanes(c, h)] for c in range(halves) for h in range(X_HEADS)], axis=0)
        k3 = k_ref[j].reshape(N_MEM, SUBLANES, LANES)
        part = jnp.sum(k3 * q8[None], axis=-1, keepdims=True)
        part = jnp.broadcast_to(part, k3.shape)
        s = (part + pltpu.roll(part, X_HEADS, 1)) * (X_HEAD_DIM ** -0.5)
        e = jnp.exp(s - jnp.max(s, axis=0, keepdims=True))
        pr = e / jnp.sum(e, axis=0, keepdims=True)
        o8 = jnp.sum(pr * v_ref[j].reshape(N_MEM, SUBLANES, LANES), axis=0)
        a_sc[row, :] = jnp.concatenate([o8[c * X_HEADS + h:c * X_HEADS + h + 1, :]
                                        for h in range(X_HEADS) for c in range(halves)], axis=1)

    @pl.when(i == pl.num_programs(0) - 1)
    def _():
        o_ref[...] = x_ref[...] + _dot(a_sc[...], wo_ref[...])


def _s_ffn_body(x_ref, g_ref, wug_ref, wuv_ref, sg_ref, sv_ref, cwg_ref, cwv_ref, cbg_ref, cbv_ref,
                wd_ref, o_ref, hg_ref, hv_ref, acc_ref):
    j = pl.program_id(0)

    @pl.when(j == 0)
    def _():
        acc_ref[...] = x_ref[...]

    xn = _rms(x_ref[...], g_ref[...]).astype(BF16)
    hg = jnp.dot(xn, wug_ref[...], preferred_element_type=F32)
    hv = jnp.dot(xn, wuv_ref[...], preferred_element_type=F32)
    cwg = cwg_ref[...]
    cwv = cwv_ref[...]
    cg = cbg_ref[...] + cwg[0:1] * sg_ref[0] + cwg[1:2] * sg_ref[1] + cwg[2:3] * hg
    cv = cbv_ref[...] + cwv[0:1] * sv_ref[0] + cwv[1:2] * sv_ref[1] + cwv[2:3] * hv
    acc_ref[...] += _dot(_gelu(cg) * cv, wd_ref[...])
    hg_ref[...] = hg
    hv_ref[...] = hv

    @pl.when(j == pl.num_programs(0) - 1)
    def _():
        o_ref[...] = acc_ref[...]


def _params(n_grid, vmem_mb):
    return pltpu.CompilerParams(dimension_semantics=("arbitrary",) * n_grid,
                                vmem_limit_bytes=vmem_mb << 20)


def _whole(shape):
    return pl.BlockSpec(tuple(shape), lambda *_: (0,) * len(shape))


def _layer(arr, l):
    shape = arr.shape[1:]
    return pl.BlockSpec((None,) + tuple(shape), lambda *_: (l,) + (0,) * len(shape))


def _sds(shape, dtype=F32):
    return jax.ShapeDtypeStruct(tuple(shape), dtype)


def _pair_blockdiag(w):
    L = w.shape[0]
    w = w.reshape(L, 4, 2, RWKV_HEAD, RWKV_HEAD)
    z = jnp.zeros_like(w[:, :, 0])
    top = jnp.concatenate([w[:, :, 0], z], axis=-1)
    bot = jnp.concatenate([z, w[:, :, 1]], axis=-1)
    return jnp.concatenate([top, bot], axis=-2)


def _vec(a):
    return a.reshape(a.shape[0], 1, -1)


def kernel(x_prompt, x_sample, state_lru_conv, state_lru_h, state_rwkv_shift, state_rwkv_S, state_pool, state_ffn_conv, cache_mem_k, cache_mem_v, mem_prompt, g_mix, w_in, lru_conv_w, lru_conv_b, lru_w_ra, lru_b_ra, lru_w_ix, lru_b_ix, lru_lambda, rwkv_mu, rwkv_w0, rwkv_w2, rwkv_a0, rwkv_a2, rwkv_g2, rwkv_k_k, rwkv_k_a, rwkv_r_k, rwkv_ln_g, rwkv_ln_b, pool_w, pool_scale, gmlp_ln_g, gmlp_ln_b, gmlp_w_s, gmlp_b_s, w_pa, w_pb, w_pc, w_pd, w_o, g_xattn, g_mem, w_xq, w_xk, w_xv, w_xo, g_ffn, w_up, ffn_conv_w, ffn_conv_b, w_down, g_final):
    B, T, D = x_prompt.shape
    SB = x_sample.shape[0]
    assert D == D_MODEL and w_in.shape == (DEPTH, D_MODEL, D_IN) and x_sample.shape[1] == 1
    assert T % TM_MIX == 0 and T % TM_RWKV == 0 and TM_RWKV % RWKV_CHUNK == 0 and SB % S_ATT_BB == 0
    assert SB == LANES and X_HEADS * (X_HEAD_DIM // LANES) == SUBLANES

    w_in_b = jnp.concatenate(
        [w_in[:, :, O_G:], w_in[:, :, O_A:O_B], w_in[:, :, O_D:O_G], w_in[:, :, O_C:O_D],
         jnp.zeros((DEPTH, D_MODEL, P_B - (P_C + D_POOL)), w_in.dtype), w_in[:, :, O_B:O_C]], axis=-1).astype(BF16)
    wra = _pair_blockdiag(lru_w_ra).astype(BF16)
    wix = _pair_blockdiag(lru_w_ix).astype(BF16)
    zer = jnp.zeros((DEPTH, R_DECAY, D_RWKV), F32)
    wwa = jnp.concatenate([jnp.concatenate([rwkv_w2, zer], axis=-1),
                           jnp.concatenate([zer, rwkv_a2], axis=-1)], axis=1).astype(BF16)
    g2 = rwkv_g2.astype(BF16)
    wpool = pool_w.astype(BF16)
    bsb = jnp.repeat(gmlp_b_s, LANES, axis=-1)
    off = PAST_LEN % CHUNK
    wsd = jnp.repeat(gmlp_w_s[:, :, off, off], LANES, axis=-1)[:, None, :]
    bsr = jnp.repeat(gmlp_b_s[:, off, :], LANES, axis=-1)[:, None, :]
    wP = jnp.stack([w_pa, w_pb, w_pc, w_pd], axis=1).astype(BF16)
    wO = w_o.astype(BF16)
    wXq, wXk, wXv, wXo = (w.astype(BF16) for w in (w_xq, w_xk, w_xv, w_xo))
    w_up_b = w_up.astype(BF16)
    wDn = w_down.astype(BF16)
    cwg, cwv = ffn_conv_w[:, :, :D_FF], ffn_conv_w[:, :, D_FF:]
    cbg, cbv = _vec(ffn_conv_b[:, :D_FF]), _vec(ffn_conv_b[:, D_FF:])
    gmix, gxat, gffn, gmem = _vec(g_mix), _vec(g_xattn), _vec(g_ffn), _vec(g_mem)
    lcb, bra, bix, lam = _vec(lru_conv_b), _vec(lru_b_ra), _vec(lru_b_ix), _vec(lru_lambda)
    mu, w0, a0, kkw, kaw = _vec(rwkv_mu), _vec(rwkv_w0), _vec(rwkv_a0), _vec(rwkv_k_k), _vec(rwkv_k_a)
    rk = rwkv_r_k.reshape(DEPTH, 1, D_RWKV)
    rlg, rlb = _vec(rwkv_ln_g), _vec(rwkv_ln_b)
    psc, glg, glb = _vec(pool_scale), _vec(gmlp_ln_g), _vec(gmlp_ln_b)
    gfin = g_final.reshape(1, D_MODEL)
    rk_col, rlg_col, rlb_col = (a.reshape(DEPTH, D_RWKV, 1) for a in (rwkv_r_k, rwkv_ln_g, rwkv_ln_b))
    state_S5 = jnp.transpose(state_rwkv_S, (0, 2, 3, 4, 1))
    s_S5 = None

    def cache_rows(c):
        c = c.reshape(DEPTH, SB, N_MEM, X_HEADS, X_HEAD_DIM // LANES, LANES)
        return jnp.swapaxes(c, 3, 4).reshape(DEPTH, SB, N_MEM * SUBLANES, LANES)

    cache_k, cache_v = cache_rows(cache_mem_k), cache_rows(cache_mem_v)

    kv_shape = _sds((DEPTH, B, N_MEM * SUBLANES, LANES))
    kv_spec = pl.BlockSpec((None, None, N_MEM * SUBLANES, LANES), lambda l, b: (l, b, 0, 0))
    wl_spec = pl.BlockSpec((None, D_MODEL, D_MODEL), lambda l, b: (l, 0, 0))
    p_mk, p_mv = pl.pallas_call(
        _memkv_body, grid=(DEPTH, B),
        in_specs=[pl.BlockSpec((None, N_MEM, D_MODEL), lambda l, b: (b, 0, 0)),
                  pl.BlockSpec((None, 1, D_MODEL), lambda l, b: (l, 0, 0)), wl_spec, wl_spec],
        out_specs=[kv_spec, kv_spec], out_shape=[kv_shape, kv_shape],
        compiler_params=_params(2, 32), name="memkv")(mem_prompt, gmem, wXk, wXv)

    nt = T // TM_MIX
    ntr = T // TM_RWKV
    xt_spec = pl.BlockSpec((None, TM_MIX, D_MODEL), lambda b, t: (b, t, 0))
    yt_spec = pl.BlockSpec((None, TM_MIX, D_LRU), lambda b, t: (b, t, 0))

    def tail_spec(rows, width):
        return pl.BlockSpec((None, rows, width), lambda b, t: (b, 0, 0))

    hp = x_prompt
    hs = x_sample.reshape(SB, D_MODEL)
    outs = {k: [] for k in ("p_conv", "s_conv", "p_h", "s_h", "p_shift", "s_shift", "p_S",
                            "p_pool", "s_pool", "p_ffn", "s_ffn", "s_cv")}

    for l in range(DEPTH):
        L = functools.partial(_layer, l=l)

        def win(start, width):
            assert start % width == 0
            return pl.BlockSpec((None, D_MODEL, width), lambda *_: (l, 0, start // width))

        win_a, win_b = win(P_A, O_B - O_A), win(P_B, O_C - O_B)
        win_c, win_d, win_g = win(P_C, O_D - O_C), win(P_D, O_G - O_D), win(P_G, D_IN - O_G)

        wup_g = pl.BlockSpec((None, D_MODEL, D_FF), lambda *_: (l, 0, 0))
        wup_v = pl.BlockSpec((None, D_MODEL, D_FF), lambda *_: (l, 0, 1))

        yA, p_conv8, p_h8 = pl.pallas_call(
            _p_lru_body, grid=(B, nt),
            in_specs=[xt_spec, L(gmix), win_a, L(lru_conv_w), L(lcb), L(wra), L(bra), L(wix), L(bix), L(lam)],
            out_specs=[yt_spec, tail_spec(SUBLANES, D_LRU), tail_spec(SUBLANES, D_LRU)],
            out_shape=[_sds((B, T, D_LRU), BF16), _sds((B, SUBLANES, D_LRU)), _sds((B, SUBLANES, D_LRU))],
            scratch_shapes=[pltpu.VMEM((SUBLANES, D_LRU), F32), pltpu.VMEM((SUBLANES, D_LRU), F32)],
            compiler_params=_params(2, 48), name="p_lru")(
                hp, gmix, w_in_b, lru_conv_w, lcb, wra, bra, wix, bix, lam)

        yB, p_shift8, p_S = pl.pallas_call(
            _p_rwkv_body, grid=(B, ntr),
            in_specs=[pl.BlockSpec((None, TM_RWKV, D_MODEL), lambda b, t: (b, t, 0)),
                      L(gmix), win_b, L(mu), L(wwa), L(w0), L(a0), L(g2), L(kkw), L(kaw), L(rk), L(rlg), L(rlb)],
            out_specs=[pl.BlockSpec((None, TM_RWKV, D_RWKV), lambda b, t: (b, t, 0)),
                       tail_spec(SUBLANES, D_RWKV_IN),
                       pl.BlockSpec((None, RWKV_HEADS, RWKV_HEAD, RWKV_HEAD), lambda b, t: (b, 0, 0, 0))],
            out_shape=[_sds((B, T, D_RWKV), BF16), _sds((B, SUBLANES, D_RWKV_IN)),
                       _sds((B, RWKV_HEADS, RWKV_HEAD, RWKV_HEAD))],
            scratch_shapes=[pltpu.VMEM((SUBLANES, D_RWKV_IN), F32),
                            pltpu.VMEM((D_RWKV // LANES, LANES, LANES), F32)],
            compiler_params=_params(2, 48), name="p_rwkv")(
                hp, gmix, w_in_b, mu, wwa, w0, a0, g2, kkw, kaw, rk, rlg, rlb)

        yC, p_pool16 = pl.pallas_call(
            _p_pool_body, grid=(B, nt),
            in_specs=[xt_spec, L(gmix), win_c, L(wpool), L(psc)],
            out_specs=[yt_spec, tail_spec(POOL_HIST, D_POOL)],
            out_shape=[_sds((B, T, D_POOL), BF16), _sds((B, POOL_HIST, D_POOL))],
            scratch_shapes=[pltpu.VMEM((POOL_HIST, D_POOL), F32)],
            compiler_params=_params(2, 48), name="p_pool")(hp, gmix, w_in_b, wpool, psc)

        yD = pl.pallas_call(
            _p_gmlp_body, grid=(B, nt),
            in_specs=[xt_spec, L(gmix), win_d, L(glg), L(glb), L(gmlp_w_s), L(bsb)],
            out_specs=yt_spec, out_shape=_sds((B, T, D_GMLP), BF16),
            compiler_params=_params(2, 48), name="p_gmlp")(hp, gmix, w_in_b, glg, glb, gmlp_w_s, bsb)

        def merge(x2d, ys, tm):
            rows = x2d.shape[0]
            xs = pl.BlockSpec((tm, D_MODEL), lambda i: (i, 0))
            ysp = pl.BlockSpec((tm, D_LRU), lambda i: (i, 0))
            return pl.pallas_call(
                _merge_body, grid=(rows // tm,),
                in_specs=[xs, L(gmix), win_g, ysp, ysp, ysp, ysp, L(wP), L(wO)],
                out_specs=xs, out_shape=_sds((rows, D_MODEL)),
                compiler_params=_params(1, 56), name="merge")(x2d, gmix, w_in_b, *ys, wP, wO)

        hp = merge(hp.reshape(B * T, D_MODEL), [y.reshape(B * T, -1) for y in (yA, yB, yC, yD)],
                   TM_MIX).reshape(B, T, D_MODEL)

        kvb_spec = pl.BlockSpec((None, None, N_MEM * SUBLANES, LANES), lambda b, t: (l, b, 0, 0))
        hp = pl.pallas_call(
            _p_xattn_body, grid=(B, nt),
            in_specs=[xt_spec, L(gxat), L(wXq), kvb_spec, kvb_spec, L(wXo)],
            out_specs=xt_spec, out_shape=_sds((B, T, D_MODEL)),
            compiler_params=_params(2, 48), name="p_xattn")(hp, gxat, wXq, p_mk, p_mv, wXo)

        hp, p_ffn8 = pl.pallas_call(
            _p_ffn_body, grid=(B, nt),
            in_specs=[xt_spec, L(gffn), wup_g, wup_v, L(cwg), L(cwv), L(cbg), L(cbv), L(wDn)],
            out_specs=[xt_spec, tail_spec(SUBLANES, 2 * D_FF)],
            out_shape=[_sds((B, T, D_MODEL)), _sds((B, SUBLANES, 2 * D_FF))],
            scratch_shapes=[pltpu.VMEM((SUBLANES, 2 * D_FF), F32)],
            compiler_params=_params(2, 56), name="p_ffn")(hp, gffn, w_up_b, w_up_b, cwg, cwv, cbg, cbv, wDn)

        outs["p_conv"].append(p_conv8[:, SUBLANES - 3:])
        outs["p_h"].append(p_h8[:, SUBLANES - 1])
        outs["p_shift"].append(p_shift8[:, SUBLANES - 1:])
        outs["p_S"].append(p_S)
        outs["p_pool"].append(p_pool16[:, POOL_HIST - POOL_BUF:])
        outs["p_ffn"].append(p_ffn8[:, SUBLANES - 2:])

        lconv = jnp.swapaxes(state_lru_conv[l], 0, 1)
        spool = jnp.swapaxes(state_pool[l], 0, 1)
        sffn = jnp.swapaxes(state_ffn_conv[l], 0, 1)
        shift = state_rwkv_shift[l].reshape(SB, D_RWKV_IN)
        row512 = _sds((SB, D_LRU))
        chan = _sds((D_RWKV, SB))
        mix_in = [hs, gmix[l], w_in_b, w_in_b, w_in_b, w_in_b,
                  lconv, state_lru_h[l], lru_conv_w[l], lcb[l], wra[l], bra[l], wix[l], bix[l], lam[l],
                  shift, mu[l], wwa[l], w0[l], a0[l], g2[l], kkw[l], kaw[l],
                  spool, wpool[l], psc[l],
                  glg[l], glb[l], wsd[l], bsr[l]]
        mix_out = [_sds((SB, D_LRU), BF16), row512, row512, _sds((SB, D_POOL), BF16), row512,
                   _sds((SB, D_GMLP), BF16), row512, _sds((SB, D_RWKV_IN)),
                   chan, chan, chan, chan, chan, chan, chan]
        (yA, s_xa, s_h, yC, s_pc, yD, s_zv, s_p, rT, wT, kT, vT, kkT, aT, gT) = pl.pallas_call(
            _s_mix_body, grid=(1,),
            in_specs=[_whole(mix_in[0].shape), _whole(mix_in[1].shape),
                      win_a, win_b, win_c, win_d]
                     + [_whole(a.shape) for a in mix_in[6:]],
            out_specs=[_whole(o.shape) for o in mix_out], out_shape=mix_out,
            compiler_params=_params(1, 56), name="s_mix")(*mix_in)

        hd_spec = pl.BlockSpec((RWKV_HEAD, SB), lambda h: (h, 0))
        col_spec = pl.BlockSpec((None, RWKV_HEAD, 1), lambda h: (l, h, 0))
        st_spec = pl.BlockSpec((None, None, RWKV_HEAD, RWKV_HEAD, SB), lambda h: (l, h, 0, 0, 0))
        has_prev = s_S5 is not None
        yB, s_S5 = pl.pallas_call(
            functools.partial(_s_rwkv_body, has_prev), grid=(RWKV_HEADS,),
            in_specs=[st_spec] + [hd_spec] * 7 + [col_spec] * 3
                     + ([pl.BlockSpec(memory_space=pl.ANY)] if has_prev else []),
            out_specs=[_whole((SB, D_RWKV)), st_spec],
            out_shape=[_sds((SB, D_RWKV), BF16), _sds(state_S5.shape)],
            scratch_shapes=[pltpu.VMEM((D_RWKV, SB), F32)],
            input_output_aliases={11: 1} if has_prev else {},
            compiler_params=_params(1, 48), name="s_rwkv")(
                state_S5, rT, wT, kT, vT, kkT, aT, gT, rk_col, rlg_col, rlb_col,
                *([s_S5] if has_prev else []))

        hs = merge(hs, [yA, yB, yC, yD], SB)

        kc_spec = pl.BlockSpec((None, S_ATT_BB, N_MEM * SUBLANES, LANES), lambda i: (l, i, 0, 0))
        xs_spec = _whole((SB, D_MODEL))
        hs = pl.pallas_call(
            _s_xattn_body, grid=(SB // S_ATT_BB,),
            in_specs=[xs_spec, L(gxat), L(wXq), kc_spec, kc_spec, L(wXo)],
            out_specs=xs_spec, out_shape=_sds((SB, D_MODEL)),
            scratch_shapes=[pltpu.VMEM((SB, D_MODEL), F32), pltpu.VMEM((SB, D_MODEL), F32)],
            compiler_params=_params(1, 56), name="s_xattn")(
                hs, gxat, wXq, cache_k, cache_v, wXo)

        ncf = D_FF // FF_CW
        wu_spec = pl.BlockSpec((None, D_MODEL, FF_CW), lambda j: (l, 0, j))
        wuv_spec = pl.BlockSpec((None, D_MODEL, FF_CW), lambda j: (l, 0, j + ncf))
        sg_spec = pl.BlockSpec((2, SB, FF_CW), lambda j: (0, 0, j))
        sv_spec = pl.BlockSpec((2, SB, FF_CW), lambda j: (0, 0, j + ncf))
        cw_spec = pl.BlockSpec((None, 3, FF_CW), lambda j: (l, 0, j))
        cb_spec = pl.BlockSpec((None, 1, FF_CW), lambda j: (l, 0, j))
        hh_spec = pl.BlockSpec((SB, FF_CW), lambda j: (0, j))
        hs, s_hg, s_hv = pl.pallas_call(
            _s_ffn_body, grid=(ncf,),
            in_specs=[xs_spec, L(gffn), wu_spec, wuv_spec, sg_spec, sv_spec, cw_spec, cw_spec, cb_spec, cb_spec,
                      pl.BlockSpec((None, FF_CW, D_MODEL), lambda j: (l, j, 0))],
            out_specs=[xs_spec, hh_spec, hh_spec],
            out_shape=[_sds((SB, D_MODEL)), _sds((SB, D_FF)), _sds((SB, D_FF))],
            scratch_shapes=[pltpu.VMEM((SB, D_MODEL), F32)],
            compiler_params=_params(1, 48), name="s_ffn")(
                hs, gffn, w_up_b, w_up_b, sffn, sffn, cwg, cwv, cbg, cbv, wDn)

        outs["s_conv"].append(jnp.concatenate([state_lru_conv[l][:, 1:], s_xa[:, None]], axis=1))
        outs["s_h"].append(s_h)
        outs["s_shift"].append(s_p[:, None])
        outs["s_pool"].append(jnp.concatenate([state_pool[l][:, 1:], s_pc[:, None]], axis=1))
        outs["s_ffn"].append(jnp.concatenate(
            [state_ffn_conv[l][:, 1:], jnp.concatenate([s_hg, s_hv], axis=-1)[:, None]], axis=1))
        outs["s_cv"].append(s_zv[:, None])

    def final_norm(x2d, tm):
        rows = x2d.shape[0]
        spec = pl.BlockSpec((tm, D_MODEL), lambda i: (i, 0))
        return pl.pallas_call(_final_norm_body, grid=(rows // tm,),
                              in_specs=[spec, _whole((1, D_MODEL))], out_specs=spec,
                              out_shape=_sds((rows, D_MODEL)),
                              compiler_params=_params(1, 32), name="final_norm")(x2d, gfin)

    y_prompt = final_norm(hp.reshape(B * T, D_MODEL), 1024).reshape(B, T, D_MODEL)
    y_sample = final_norm(hs, SB).reshape(SB, 1, D_MODEL)
    stk = {k: jnp.stack(v, axis=0) for k, v in outs.items()}
    def mem_out(kv):
        kv = kv.reshape(DEPTH, B, N_MEM, X_HEAD_DIM // LANES, X_HEADS, LANES)
        return jnp.swapaxes(kv, 3, 4).reshape(DEPTH, B, N_MEM, X_HEADS, X_HEAD_DIM)

    return (y_prompt, y_sample,
            stk["p_conv"], stk["s_conv"], stk["p_h"], stk["s_h"],
            stk["p_shift"], stk["s_shift"], stk["p_S"], jnp.transpose(s_S5, (0, 4, 1, 2, 3)),
            stk["p_pool"], stk["s_pool"], stk["p_ffn"], stk["s_ffn"],
            mem_out(p_mk), mem_out(p_mv), stk["s_cv"])
```

```python
import functools
import math

import jax
import jax.numpy as jnp
from jax import lax
from jax.experimental import pallas as pl
from jax.experimental.pallas import tpu as pltpu

F32 = jnp.float32
BF16 = jnp.bfloat16

SUBLANES = 8
LANES = 128

D_MODEL = 1024
DEPTH = 4
N_MEM = 256
D_LRU = 512
LRU_C = 8.0
D_RWKV = 512
RWKV_HEAD = 64
RWKV_HEADS = D_RWKV // RWKV_HEAD
R_DECAY = 64
R_AAA = 64
R_GATE = 128
D_RWKV_IN = 3 * D_RWKV + R_DECAY + R_AAA + R_GATE
GN_EPS = 64e-5
D_POOL = 512
POOL_WINDOWS = (2, 4, 8, 16)
POOL_GW = D_POOL // len(POOL_WINDOWS)
POOL_BUF = max(POOL_WINDOWS) - 1
POOL_HIST = 16
D_GMLP = 512
GMLP_GROUPS = 4
CHUNK = 128
N_BRANCH = 4
X_HEADS = 4
X_HEAD_DIM = D_MODEL // X_HEADS
D_FF = 3 * D_MODEL
EPS = 1e-6
PAST_LEN = 16384

O_A = 0
O_B = 2 * D_LRU
O_C = O_B + D_RWKV_IN
O_D = O_C + D_POOL
O_G = O_D + 2 * D_GMLP
D_IN = O_G + N_BRANCH * D_MODEL
P_G = 0
P_A = P_G + (D_IN - O_G)
P_D = P_A + (O_B - O_A)
P_C = P_D + (O_G - O_D)
P_B = -(-(P_C + D_POOL) // D_RWKV_IN) * D_RWKV_IN

TM_MIX = 512
TM_RWKV = 512
FF_CW = 1536
RWKV_CHUNK = 64
S_ATT_BB = 8


def _dot(a, b):
    return jnp.dot(a.astype(BF16), b.astype(BF16), preferred_element_type=F32)


def _rms(x, g):
    return x * lax.rsqrt(jnp.mean(x * x, axis=-1, keepdims=True) + EPS) * g


def _gelu(x):
    c = math.sqrt(2.0 / math.pi)
    return 0.5 * x * (1.0 + jnp.tanh(c * (x + 0.044715 * (x * x * x))))


def _softplus(x):
    return jnp.maximum(x, 0.0) + jnp.log1p(jnp.exp(-jnp.abs(x)))


def _sigmoid(x):
    return jax.nn.sigmoid(x)


def _shift_rows(hist, cur, s):
    n = hist.shape[0]
    return pltpu.roll(jnp.concatenate([hist, cur], axis=0), s, 0)[n:]


def _head_ones():
    r = lax.broadcasted_iota(jnp.int32, (LANES, LANES), 0) // RWKV_HEAD
    c = lax.broadcasted_iota(jnp.int32, (LANES, LANES), 1) // RWKV_HEAD
    return jnp.where(r == c, 1.0, 0.0).astype(BF16)


def _segsum(x, ones):
    xb = x.astype(BF16)
    return jnp.concatenate([jnp.dot(xb[:, p * LANES:(p + 1) * LANES], ones, preferred_element_type=F32)
                            for p in range(x.shape[1] // LANES)], axis=-1)


def _blockdiag_dot(x, w_ref):
    outs = []
    for p in range(x.shape[1] // LANES):
        outs.append(jnp.dot(x[:, p * LANES:(p + 1) * LANES].astype(BF16), w_ref[p],
                            preferred_element_type=F32))
    return jnp.concatenate(outs, axis=-1)


def _lru_gates(xc, wra_ref, bra_ref, wix_ref, bix_ref, lam_ref):
    r = _sigmoid(_blockdiag_dot(xc, wra_ref) + bra_ref[...])
    i = _sigmoid(_blockdiag_dot(xc, wix_ref) + bix_ref[...])
    log_a = -LRU_C * r * _softplus(-lam_ref[...])
    a = jnp.exp(log_a)
    mult = jnp.sqrt(-jnp.tanh(log_a) * (1.0 + a * a))
    return a, mult * (i * xc)


def _rwkv_pre(p, prev, mu, wwa_ref, w0, a0, g2_ref, k_k, k_a, ones):
    px = p + (prev - p) * mu
    r = px[:, 0:D_RWKV]
    k = px[:, D_RWKV:2 * D_RWKV]
    v = px[:, 2 * D_RWKV:3 * D_RWKV]
    lo = px[:, 3 * D_RWKV:3 * D_RWKV + R_DECAY + R_AAA]
    g_lo = px[:, 3 * D_RWKV + R_DECAY + R_AAA:]
    lane = lax.broadcasted_iota(jnp.int32, lo.shape, 1)
    wa = _dot(jnp.where(lane < R_DECAY, jnp.tanh(lo), lo), wwa_ref[...])
    logw = -math.exp(-0.5) * _sigmoid(w0 + wa[:, :D_RWKV])
    a = _sigmoid(a0 + wa[:, D_RWKV:])
    g = _dot(_sigmoid(g_lo), g2_ref[...])
    kk = k * k_k
    kk = kk * lax.rsqrt(jnp.maximum(_segsum(kk * kk, ones), 1e-24))
    k2 = k * (1.0 + (a - 1.0) * k_a)
    return r, logw, k2, v, kk, a, g


def _rwkv_post(y, r, k2, v, g, r_k, ln_g, ln_b, ones):
    inv = 1.0 / RWKV_HEAD
    mean = _segsum(y, ones) * inv
    yc = y - mean
    var = _segsum(yc * yc, ones) * inv
    yn = yc * lax.rsqrt(var + GN_EPS) * ln_g + ln_b
    bonus = _segsum(r * k2 * r_k, ones) * v
    return (yn + bonus) * g


def _pool_project(d, wp_ref, scale):
    return _blockdiag_dot(d, wp_ref) * scale


def _gmlp_uz(z2, ln_g, ln_b):
    u = _gelu(z2[:, :D_GMLP])
    v = _gelu(z2[:, D_GMLP:])
    mu = jnp.mean(v, axis=-1, keepdims=True)
    vc = v - mu
    var = jnp.mean(vc * vc, axis=-1, keepdims=True)
    return u, vc * lax.rsqrt(var + 1e-5) * ln_g + ln_b


def _p_lru_body(x_ref, g_ref, w_ref, cw_ref, cb_ref, wra_ref, bra_ref, wix_ref, bix_ref, lam_ref,
                y_ref, convo_ref, ho_ref, hist_ref, h_ref, sa_ref, sb_ref, carry_ref):
    @pl.when(pl.program_id(1) == 0)
    def _():
        hist_ref[...] = jnp.zeros_like(hist_ref)
        h_ref[...] = jnp.zeros_like(h_ref)

    z = _dot(_rms(x_ref[...], g_ref[...]), w_ref[...])
    xa = z[:, :D_LRU]
    ga = z[:, D_LRU:]
    tm = xa.shape[0]
    hist = hist_ref[...]
    cw = cw_ref[...]
    xc = cb_ref[...] + cw[3:4] * xa
    for s in (1, 2, 3):
        xc = xc + cw[3 - s:4 - s] * _shift_rows(hist, xa, s)
    a, b = _lru_gates(xc, wra_ref, bra_ref, wix_ref, bix_ref, lam_ref)
    groups = (tm // SUBLANES, SUBLANES, D_LRU)
    a3 = a.reshape(groups)
    b3 = b.reshape(groups)
    sub = lax.broadcasted_iota(jnp.int32, groups, 1)
    s = 1
    while s < SUBLANES:
        m = sub >= s
        b3 = jnp.where(m, a3 * pltpu.roll(b3, s, 1) + b3, b3)
        a3 = jnp.where(m, a3 * pltpu.roll(a3, s, 1), a3)
        s *= 2
    a = a3.reshape(tm, D_LRU)
    b = b3.reshape(tm, D_LRU)
    sa_ref[...] = a
    sb_ref[...] = b

    def carry_group(gi, hprev):
        r0 = pl.multiple_of(gi * SUBLANES, SUBLANES)
        carry_ref[pl.ds(r0, SUBLANES), :] = jnp.broadcast_to(hprev, (SUBLANES, D_LRU))
        last = pl.ds(r0 + SUBLANES - 1, 1)
        return sa_ref[last, :] * hprev + sb_ref[last, :]

    lax.fori_loop(0, tm // SUBLANES, carry_group, h_ref[SUBLANES - 1:SUBLANES, :], unroll=8)
    h = a * carry_ref[...] + b
    y_ref[...] = (h * _gelu(ga)).astype(y_ref.dtype)
    hist_ref[...] = xa[tm - SUBLANES:]
    h_ref[...] = h[tm - SUBLANES:]
    convo_ref[...] = xa[tm - SUBLANES:]
    ho_ref[...] = h[tm - SUBLANES:]


def _pair_blockdiag_rows(x):
    lo = lax.broadcasted_iota(jnp.int32, x.shape, 1) < RWKV_HEAD
    z = jnp.zeros_like(x)
    return jnp.concatenate([jnp.where(lo, x, z), jnp.where(lo, z, x)], axis=0)


def _p_rwkv_body(x_ref, g_ref, w_ref, mu_ref, wwa_ref, w0_ref, a0_ref, g2_ref, kk_ref, ka_ref,
                 rk_ref, lng_ref, lnb_ref,
                 y_ref, shifto_ref, so_ref,
                 hist_ref, st_ref):
    @pl.when(pl.program_id(1) == 0)
    def _():
        hist_ref[...] = jnp.zeros_like(hist_ref)
        st_ref[...] = jnp.zeros_like(st_ref)

    C = RWKV_CHUNK
    ones = _head_ones()
    p = _dot(_rms(x_ref[...], g_ref[...]), w_ref[...])
    tm = p.shape[0]
    nc = tm // C
    npair = D_RWKV // LANES
    prev = _shift_rows(hist_ref[...], p, 1)
    r, logw, k2, v, kk, a, g = _rwkv_pre(p, prev, mu_ref[...], wwa_ref, w0_ref[...], a0_ref[...],
                                          g2_ref, kk_ref[...], ka_ref[...], ones)
    hist_ref[...] = p[tm - SUBLANES:]
    shifto_ref[...] = p[tm - SUBLANES:]
    rowc = lax.broadcasted_iota(jnp.int32, logw.shape, 0) % C
    lg = logw
    s = 1
    while s < C:
        lg = lg + jnp.where(rowc >= s, pltpu.roll(lg, s, 0), 0.0)
        s *= 2
    lg_last = [lg[(c + 1) * C - 1:(c + 1) * C, :] for c in range(nc)]
    lg_end = jnp.concatenate([jnp.broadcast_to(x, (C, D_RWKV)) for x in lg_last], axis=0)
    inv_gam = jnp.exp(-lg)
    to_end = jnp.exp(lg_end - lg)
    b = kk * a
    rt_f = r * jnp.exp(lg)
    at_b, rt_b, bt_b, kt_b, v_b, bh_b, kh_b = (x.astype(BF16) for x in (
        -kk * jnp.exp(lg - logw), rt_f, b * inv_gam, k2 * inv_gam, v, b * to_end, k2 * to_end))

    r2 = lax.broadcasted_iota(jnp.int32, (C, LANES), 0)
    c2 = lax.broadcasted_iota(jnp.int32, (C, LANES), 1) % C
    strict2 = c2 < r2
    incl2 = c2 <= r2
    eye2 = jnp.where(r2 == c2, 1.0, 0.0).astype(F32)
    pair2 = (r2 // 2) == (c2 // 2)
    levels = []
    nb = 2
    while nb < C:
        levels.append(((r2 // (2 * nb)) == (c2 // (2 * nb))) & ((r2 // nb) != (c2 // nb)))
        nb *= 2
    rr = lax.broadcasted_iota(jnp.int32, (LANES, LANES), 0)
    cc = lax.broadcasted_iota(jnp.int32, (LANES, LANES), 1)
    same_head = (rr // RWKV_HEAD) == (cc // RWKV_HEAD)
    eye128 = jnp.where(rr == cc, 1.0, 0.0).astype(F32)
    tn = (((0,), (0,)), ((), ()))
    nt_ = (((1,), (1,)), ((), ()))
    bd = _pair_blockdiag_rows

    def mm(x, y):
        return jnp.dot(x, y, preferred_element_type=F32)

    def blk(x, c, q):
        return x[c * C:(c + 1) * C, q * LANES:(q + 1) * LANES]

    P = [(c, q) for c in range(nc) for q in range(npair)]
    N = range(len(P))
    at, rt, bt, kt, vv, bh, kh = ([blk(x, c, q) for c, q in P] for x in
                                  (at_b, rt_b, bt_b, kt_b, v_b, bh_b, kh_b))
    ar = [jnp.concatenate([at[n], rt[n]], axis=0) for n in N]
    pb = [lax.dot_general(ar[n], bd(bt[n]), nt_, preferred_element_type=F32) for n in N]
    pk = [lax.dot_general(ar[n], bd(kt[n]), nt_, preferred_element_type=F32) for n in N]
    lab = [jnp.where(strict2, x[:C], 0.0) for x in pb]
    mrb = [jnp.where(incl2, x[C:], 0.0).astype(BF16) for x in pb]
    lak = [jnp.where(strict2, x[:C], 0.0).astype(BF16) for x in pk]
    mrk = [jnp.where(incl2, x[C:], 0.0).astype(BF16) for x in pk]
    v_bd = [bd(x) for x in vv]
    lak_v = [mm(lak[n], v_bd[n]) for n in N]
    T = [eye2 + jnp.where(pair2, x, 0.0) for x in lab]
    for mk in levels:
        tb = [x.astype(BF16) for x in T]
        u = [mm(tb[n], bd(jnp.where(mk, lab[n], 0.0).astype(BF16))).astype(BF16) for n in N]
        T = [T[n] + mm(u[n], bd(tb[n])) for n in N]
    tb = [x.astype(BF16) for x in T]
    tab = [mm(tb[n], bd(at[n])).astype(BF16) for n in N]
    cv = [mm(tb[n], bd(lak_v[n].astype(BF16))).astype(BF16) for n in N]
    g_bd = [jnp.where(same_head, lax.dot_general(bh[n], tab[n], tn, preferred_element_type=F32), 0.0)
            .astype(BF16) for n in N]
    q_bd = [jnp.where(same_head, lax.dot_general(jnp.concatenate([bh[n], kh[n]], axis=0),
                                                 jnp.concatenate([cv[n], vv[n]], axis=0), tn,
                                                 preferred_element_type=F32), 0.0) for n in N]
    ry = [(blk(rt_f, c, q) + mm(mrb[n], bd(tab[n]))).astype(BF16) for n, (c, q) in enumerate(P)]
    yc = [mm(jnp.concatenate([mrb[n], mrk[n]], axis=1),
             jnp.concatenate([bd(cv[n]), v_bd[n]], axis=0)) for n in N]
    gcol = [jnp.broadcast_to(jnp.sum(eye128 * jnp.exp(lg_last[c][:, q * LANES:(q + 1) * LANES]),
                                     axis=1, keepdims=True), (LANES, LANES)) for c, q in P]

    st = [st_ref[q] for q in range(npair)]
    y_rows = []
    for c in range(nc):
        sb = [x.astype(BF16) for x in st]
        ns = [c * npair + q for q in range(npair)]
        gs = [mm(g_bd[n], sb[q]) for q, n in enumerate(ns)]
        ys = [mm(ry[n], sb[q]) for q, n in enumerate(ns)]
        st = [st[q] * gcol[n] + gs[q] + q_bd[n] for q, n in enumerate(ns)]
        y_rows.append(jnp.concatenate([ys[q] + yc[n] for q, n in enumerate(ns)], axis=1))
    y = jnp.concatenate(y_rows, axis=0)
    yb = _rwkv_post(y, r, k2, v, g, rk_ref[...], lng_ref[...], lnb_ref[...], ones)
    y_ref[...] = yb.astype(y_ref.dtype)
    for q in range(npair):
        st_ref[q] = st[q]
        so_ref[2 * q] = st[q][:RWKV_HEAD, :RWKV_HEAD].T
        so_ref[2 * q + 1] = st[q][RWKV_HEAD:, RWKV_HEAD:].T


def _p_pool_body(x_ref, g_ref, w_ref, wp_ref, sc_ref, y_ref, poolo_ref, hist_ref):
    t = pl.program_id(1)

    @pl.when(t == 0)
    def _():
        hist_ref[...] = jnp.zeros_like(hist_ref)

    pc = _dot(_rms(x_ref[...], g_ref[...]), w_ref[...])
    tm = pc.shape[0]
    X = jnp.concatenate([hist_ref[...], pc], axis=0)
    pos = t * tm + lax.broadcasted_iota(jnp.int32, (tm, POOL_GW), 0)
    ds = []
    for gi, win in enumerate(POOL_WINDOWS):
        sl = slice(gi * POOL_GW, (gi + 1) * POOL_GW)
        s = X[:, sl]
        sh = 1
        while sh < win:
            s = s + pltpu.roll(s, sh, 0)
            sh *= 2
        cnt = jnp.minimum(pos + 1, win).astype(F32)
        ds.append(s[POOL_HIST:] / cnt - pc[:, sl])
    y = _pool_project(jnp.concatenate(ds, axis=-1), wp_ref, sc_ref[...])
    y_ref[...] = y.astype(y_ref.dtype)
    hist_ref[...] = pc[tm - POOL_HIST:]
    poolo_ref[...] = pc[tm - POOL_HIST:]


def _p_gmlp_body(x_ref, g_ref, w_ref, lng_ref, lnb_ref, ws_ref, bsb_ref, y_ref):
    z2 = _dot(_rms(x_ref[...], g_ref[...]), w_ref[...])
    tm = z2.shape[0]
    u, zv = _gmlp_uz(z2, lng_ref[...], lnb_ref[...])
    tril = (lax.broadcasted_iota(jnp.int32, (CHUNK, CHUNK), 0)
            >= lax.broadcasted_iota(jnp.int32, (CHUNK, CHUNK), 1))
    ws = [jnp.where(tril, ws_ref[gi], 0.0).astype(BF16) for gi in range(GMLP_GROUPS)]
    zb = zv.astype(BF16)
    rows = []
    for c in range(tm // CHUNK):
        cols = [jnp.dot(ws[gi], zb[c * CHUNK:(c + 1) * CHUNK, gi * LANES:(gi + 1) * LANES],
                        preferred_element_type=F32) for gi in range(GMLP_GROUPS)]
        rows.append(jnp.concatenate(cols, axis=-1) + bsb_ref[...])
    s = jnp.concatenate(rows, axis=0)
    y_ref[...] = (u * s).astype(y_ref.dtype)


def _merge_body(x_ref, g_ref, wg_ref, ya_ref, yb_ref, yc_ref, yd_ref, wp_ref, wo_ref, o_ref):
    x = x_ref[...]
    xn = _rms(x, g_ref[...]).astype(BF16)
    merged = None
    for i, y_ref in enumerate((ya_ref, yb_ref, yc_ref, yd_ref)):
        gate = _sigmoid(jnp.dot(xn, wg_ref[:, i * D_MODEL:(i + 1) * D_MODEL],
                                preferred_element_type=F32))
        term = gate * jnp.dot(y_ref[...], wp_ref[i], preferred_element_type=F32)
        merged = term if merged is None else merged + term
    o_ref[...] = x + _dot(merged, wo_ref[...])


def _softmax_rows(s):
    e = jnp.exp(s - jnp.max(s, axis=-1, keepdims=True))
    return e / jnp.sum(e, axis=-1, keepdims=True)


def _kv_rows(half, h):
    return pl.ds(half * X_HEADS + h, N_MEM, stride=SUBLANES)


def _p_xattn_body(x_ref, g_ref, wq_ref, k_ref, v_ref, wo_ref, o_ref, kb_sc, vb_sc):
    halves = X_HEAD_DIM // LANES

    @pl.when(pl.program_id(1) == 0)
    def _():
        for half in range(halves):
            for h in range(X_HEADS):
                kb_sc[half * X_HEADS + h] = k_ref[_kv_rows(half, h), :].astype(BF16)
                vb_sc[half * X_HEADS + h] = v_ref[_kv_rows(half, h), :].astype(BF16)

    x = x_ref[...]
    q = _dot(_rms(x, g_ref[...]), wq_ref[...]).astype(BF16)
    nt_ = (((1,), (1,)), ((), ()))
    outs = []
    for h in range(X_HEADS):
        s = None
        for half in range(halves):
            lo = h * X_HEAD_DIM + half * LANES
            part = lax.dot_general(q[:, lo:lo + LANES], kb_sc[half * X_HEADS + h], nt_,
                                   preferred_element_type=F32)
            s = part if s is None else s + part
        pr = _softmax_rows(s * (X_HEAD_DIM ** -0.5)).astype(BF16)
        for half in range(halves):
            outs.append(jnp.dot(pr, vb_sc[half * X_HEADS + h], preferred_element_type=F32))
    o_ref[...] = x + _dot(jnp.concatenate(outs, axis=-1), wo_ref[...])


def _ffn_conv(hist, h, cw, cb):
    return cb + cw[2:3] * h + cw[1:2] * _shift_rows(hist, h, 1) + cw[0:1] * _shift_rows(hist, h, 2)


def _p_ffn_body(x_ref, g_ref, wug_ref, wuv_ref, cwg_ref, cwv_ref, cbg_ref, cbv_ref, wd_ref,
                o_ref, ffo_ref, hist_ref):
    @pl.when(pl.program_id(1) == 0)
    def _():
        hist_ref[...] = jnp.zeros_like(hist_ref)

    x = x_ref[...]
    tm = x.shape[0]
    xn = _rms(x, g_ref[...]).astype(BF16)
    acc = x
    for j in range(D_FF // FF_CW):
        cs = slice(j * FF_CW, (j + 1) * FF_CW)
        vs = slice(D_FF + j * FF_CW, D_FF + (j + 1) * FF_CW)
        hg = jnp.dot(xn, wug_ref[:, cs], preferred_element_type=F32)
        hv = jnp.dot(xn, wuv_ref[:, cs], preferred_element_type=F32)
        cg = _ffn_conv(hist_ref[:, cs], hg, cwg_ref[:, cs], cbg_ref[:, cs])
        cv = _ffn_conv(hist_ref[:, vs], hv, cwv_ref[:, cs], cbv_ref[:, cs])
        acc = acc + _dot(_gelu(cg) * cv, wd_ref[cs, :])
        hist_ref[:, cs] = hg[tm - SUBLANES:]
        hist_ref[:, vs] = hv[tm - SUBLANES:]
    o_ref[...] = acc
    ffo_ref[...] = hist_ref[...]


def _memkv_body(m_ref, g_ref, wk_ref, wv_ref, k_ref, v_ref):
    mn = _rms(m_ref[...], g_ref[...]).astype(BF16)
    for w_ref, o_ref in ((wk_ref, k_ref), (wv_ref, v_ref)):
        kv = jnp.dot(mn, w_ref[...], preferred_element_type=F32)
        for h in range(X_HEADS):
            for half in range(X_HEAD_DIM // LANES):
                lo = h * X_HEAD_DIM + half * LANES
                o_ref[_kv_rows(half, h), :] = kv[:, lo:lo + LANES]


def _final_norm_body(x_ref, g_ref, o_ref):
    o_ref[...] = _rms(x_ref[...], g_ref[...])


def _s_mix_body(x_ref, g_ref, wa_ref, wb_ref, wc_ref, wd_ref,
                lconv_ref, lh_ref, cw_ref, cb_ref, wra_ref, bra_ref, wix_ref, bix_ref, lam_ref,
                shift_ref, mu_ref, wwa_ref, w0_ref, a0_ref, g2_ref, kk_ref, ka_ref,
                pool_ref, wp_ref, psc_ref,
                lng_ref, lnb_ref, wsd_ref, bsr_ref,
                ya_ref, xa_ref, h_ref, yc_ref, pc_ref, yd_ref, zv_ref, p_ref,
                rt_ref, wt_ref, kt_ref, vt_ref, kkt_ref, at_ref, gt_ref):
    xn = _rms(x_ref[...], g_ref[...]).astype(BF16)
    z = jnp.dot(xn, wa_ref[...], preferred_element_type=F32)
    xa = z[:, :D_LRU]
    cw = cw_ref[...]
    xc = cb_ref[...] + cw[0:1] * lconv_ref[0] + cw[1:2] * lconv_ref[1] + cw[2:3] * lconv_ref[2] + cw[3:4] * xa
    a, b = _lru_gates(xc, wra_ref, bra_ref, wix_ref, bix_ref, lam_ref)
    h = a * lh_ref[...] + b
    ya_ref[...] = (h * _gelu(z[:, D_LRU:])).astype(ya_ref.dtype)
    xa_ref[...] = xa
    h_ref[...] = h
    ones = _head_ones()
    p = jnp.dot(xn, wb_ref[...], preferred_element_type=F32)
    r, logw, k2, v, kk, aa, g = _rwkv_pre(p, shift_ref[...], mu_ref[...], wwa_ref, w0_ref[...],
                                           a0_ref[...], g2_ref, kk_ref[...], ka_ref[...], ones)
    p_ref[...] = p
    for ref, val in ((rt_ref, r), (wt_ref, jnp.exp(logw)), (kt_ref, k2), (vt_ref, v), (kkt_ref, kk),
                     (at_ref, aa), (gt_ref, g)):
        ref[...] = val.T
    pc = jnp.dot(xn, wc_ref[...], preferred_element_type=F32)
    ds = []
    for gi, win in enumerate(POOL_WINDOWS):
        sl = slice(gi * POOL_GW, (gi + 1) * POOL_GW)
        s = pc[:, sl]
        for j in range(POOL_BUF - (win - 1), POOL_BUF):
            s = s + pool_ref[j][:, sl]
        ds.append(s / float(min(PAST_LEN + 1, win)) - pc[:, sl])
    yc_ref[...] = _pool_project(jnp.concatenate(ds, axis=-1), wp_ref, psc_ref[...]).astype(yc_ref.dtype)
    pc_ref[...] = pc
    u, zv = _gmlp_uz(jnp.dot(xn, wd_ref[...], preferred_element_type=F32), lng_ref[...], lnb_ref[...])
    yd_ref[...] = (u * (wsd_ref[...] * zv + bsr_ref[...])).astype(yd_ref.dtype)
    zv_ref[...] = zv


def _s_rwkv_body(has_prev, s_ref, r_ref, w_ref, k_ref, v_ref, kk_ref, a_ref, g_ref, rk_ref, lng_ref,
                 lnb_ref, *rest):
    y_ref, so_ref, yt_sc = rest[1:] if has_prev else rest
    h = pl.program_id(0)
    kk = kk_ref[...]
    w = w_ref[...]
    k = k_ref[...]
    r = r_ref[...]
    bk = kk * a_ref[...]

    def body(i, carry):
        si = s_ref[i]
        sa = -jnp.sum(si * kk, axis=0, keepdims=True)
        s2 = si * w + sa * bk + v_ref[pl.ds(i, 1), :] * k
        so_ref[i] = s2
        yt_sc[pl.ds(h * RWKV_HEAD + i, 1), :] = jnp.sum(s2 * r, axis=0, keepdims=True)
        return carry

    lax.fori_loop(0, RWKV_HEAD, body, 0, unroll=4)
    rows = pl.ds(pl.multiple_of(h * RWKV_HEAD, RWKV_HEAD), RWKV_HEAD)
    y = yt_sc[rows, :]
    yc = y - jnp.mean(y, axis=0, keepdims=True)
    var = jnp.mean(yc * yc, axis=0, keepdims=True)
    yn = yc * lax.rsqrt(var + GN_EPS) * lng_ref[...] + lnb_ref[...]
    bonus = jnp.sum(r * k * rk_ref[...], axis=0, keepdims=True) * v_ref[...]
    yt_sc[rows, :] = (yn + bonus) * g_ref[...]

    @pl.when(h == pl.num_programs(0) - 1)
    def _():
        y_ref[...] = yt_sc[...].T.astype(y_ref.dtype)


def _s_xattn_body(x_ref, g_ref, wq_ref, k_ref, v_ref, wo_ref, o_ref, q_sc, a_sc):
    i = pl.program_id(0)
    bb = k_ref.shape[0]

    @pl.when(i == 0)
    def _():
        q_sc[...] = _dot(_rms(x_ref[...], g_ref[...]), wq_ref[...])

    halves = X_HEAD_DIM // LANES

    def q_lanes(c, h):
        return slice(h * X_HEAD_DIM + c * LANES, h * X_HEAD_DIM + (c + 1) * LANES)

    for j in range(bb):
        row = pl.ds(i * bb + j, 1)
        qrow = q_sc[row, :]
        q8 = jnp.concatenate([qrow[:, q_lanes(c, h)] for c in range(halves) for h in range(X_HEADS)], axis=0)
        k3 = k_ref[j].reshape(N_MEM, SUBLANES, LANES)
        part = jnp.sum(k3 * q8[None], axis=-1, keepdims=True)
        part = jnp.broadcast_to(part, k3.shape)
        s = (part + pltpu.roll(part, X_HEADS, 1)) * (X_HEAD_DIM ** -0.5)
        e = jnp.exp(s - jnp.max(s, axis=0, keepdims=True))
        pr = e / jnp.sum(e, axis=0, keepdims=True)
        o8 = jnp.sum(pr * v_ref[j].reshape(N_MEM, SUBLANES, LANES), axis=0)
        a_sc[row, :] = jnp.concatenate([o8[c * X_HEADS + h:c * X_HEADS + h + 1, :]
                                        for h in range(X_HEADS) for c in range(halves)], axis=1)

    @pl.when(i == pl.num_programs(0) - 1)
    def _():
        o_ref[...] = x_ref[...] + _dot(a_sc[...], wo_ref[...])


def _s_ffn_body(x_ref, g_ref, wug_ref, wuv_ref, sg_ref, sv_ref, cwg_ref, cwv_ref, cbg_ref, cbv_ref,
                wd_ref, o_ref, hg_ref, hv_ref, acc_ref):
    j = pl.program_id(0)

    @pl.when(j == 0)
    def _():
        acc_ref[...] = x_ref[...]

    xn = _rms(x_ref[...], g_ref[...]).astype(BF16)
    hg = jnp.dot(xn, wug_ref[...], preferred_element_type=F32)
    hv = jnp.dot(xn, wuv_ref[...], preferred_element_type=F32)
    cwg = cwg_ref[...]
    cwv = cwv_ref[...]
    cg = cbg_ref[...] + cwg[0:1] * sg_ref[0] + cwg[1:2] * sg_ref[1] + cwg[2:3] * hg
    cv = cbv_ref[...] + cwv[0:1] * sv_ref[0] + cwv[1:2] * sv_ref[1] + cwv[2:3] * hv
    acc_ref[...] += _dot(_gelu(cg) * cv, wd_ref[...])
    hg_ref[...] = hg
    hv_ref[...] = hv

    @pl.when(j == pl.num_programs(0) - 1)
    def _():
        o_ref[...] = acc_ref[...]


def _params(n_grid, vmem_mb):
    return pltpu.CompilerParams(dimension_semantics=("arbitrary",) * n_grid,
                                vmem_limit_bytes=vmem_mb << 20)


def _whole(shape):
    return pl.BlockSpec(tuple(shape), lambda *_: (0,) * len(shape))


def _layer(arr, l):
    shape = arr.shape[1:]
    return pl.BlockSpec((None,) + tuple(shape), lambda *_: (l,) + (0,) * len(shape))


def _sds(shape, dtype=F32):
    return jax.ShapeDtypeStruct(tuple(shape), dtype)


def _pair_blockdiag(w):
    L = w.shape[0]
    w = w.reshape(L, 4, 2, RWKV_HEAD, RWKV_HEAD)
    z = jnp.zeros_like(w[:, :, 0])
    top = jnp.concatenate([w[:, :, 0], z], axis=-1)
    bot = jnp.concatenate([z, w[:, :, 1]], axis=-1)
    return jnp.concatenate([top, bot], axis=-2)


def _vec(a):
    return a.reshape(a.shape[0], 1, -1)


def kernel(x_prompt, x_sample, state_lru_conv, state_lru_h, state_rwkv_shift, state_rwkv_S, state_pool, state_ffn_conv, cache_mem_k, cache_mem_v, mem_prompt, g_mix, w_in, lru_conv_w, lru_conv_b, lru_w_ra, lru_b_ra, lru_w_ix, lru_b_ix, lru_lambda, rwkv_mu, rwkv_w0, rwkv_w2, rwkv_a0, rwkv_a2, rwkv_g2, rwkv_k_k, rwkv_k_a, rwkv_r_k, rwkv_ln_g, rwkv_ln_b, pool_w, pool_scale, gmlp_ln_g, gmlp_ln_b, gmlp_w_s, gmlp_b_s, w_pa, w_pb, w_pc, w_pd, w_o, g_xattn, g_mem, w_xq, w_xk, w_xv, w_xo, g_ffn, w_up, ffn_conv_w, ffn_conv_b, w_down, g_final):
    B, T, D = x_prompt.shape
    SB = x_sample.shape[0]
    assert D == D_MODEL and w_in.shape == (DEPTH, D_MODEL, D_IN) and x_sample.shape[1] == 1
    assert T % TM_MIX == 0 and T % TM_RWKV == 0 and TM_RWKV % RWKV_CHUNK == 0 and SB % S_ATT_BB == 0
    assert SB == LANES and X_HEADS * (X_HEAD_DIM // LANES) == SUBLANES

    w_in_b = jnp.concatenate(
        [w_in[:, :, O_G:], w_in[:, :, O_A:O_B], w_in[:, :, O_D:O_G], w_in[:, :, O_C:O_D],
         jnp.zeros((DEPTH, D_MODEL, P_B - (P_C + D_POOL)), w_in.dtype), w_in[:, :, O_B:O_C]], axis=-1).astype(BF16)
    wra = _pair_blockdiag(lru_w_ra).astype(BF16)
    wix = _pair_blockdiag(lru_w_ix).astype(BF16)
    zer = jnp.zeros((DEPTH, R_DECAY, D_RWKV), F32)
    wwa = jnp.concatenate([jnp.concatenate([rwkv_w2, zer], axis=-1),
                           jnp.concatenate([zer, rwkv_a2], axis=-1)], axis=1).astype(BF16)
    g2 = rwkv_g2.astype(BF16)
    wpool = pool_w.astype(BF16)
    bsb = jnp.repeat(gmlp_b_s, LANES, axis=-1)
    off = PAST_LEN % CHUNK
    wsd = jnp.repeat(gmlp_w_s[:, :, off, off], LANES, axis=-1)[:, None, :]
    bsr = jnp.repeat(gmlp_b_s[:, off, :], LANES, axis=-1)[:, None, :]
    wP = jnp.stack([w_pa, w_pb, w_pc, w_pd], axis=1).astype(BF16)
    wO = w_o.astype(BF16)
    wXq, wXk, wXv, wXo = (w.astype(BF16) for w in (w_xq, w_xk, w_xv, w_xo))
    w_up_b = w_up.astype(BF16)
    wDn = w_down.astype(BF16)
    cwg, cwv = ffn_conv_w[:, :, :D_FF], ffn_conv_w[:, :, D_FF:]
    cbg, cbv = _vec(ffn_conv_b[:, :D_FF]), _vec(ffn_conv_b[:, D_FF:])
    gmix, gxat, gffn, gmem = _vec(g_mix), _vec(g_xattn), _vec(g_ffn), _vec(g_mem)
    lcb, bra, bix, lam = _vec(lru_conv_b), _vec(lru_b_ra), _vec(lru_b_ix), _vec(lru_lambda)
    mu, w0, a0, kkw, kaw = _vec(rwkv_mu), _vec(rwkv_w0), _vec(rwkv_a0), _vec(rwkv_k_k), _vec(rwkv_k_a)
    rk = rwkv_r_k.reshape(DEPTH, 1, D_RWKV)
    rlg, rlb = _vec(rwkv_ln_g), _vec(rwkv_ln_b)
    psc, glg, glb = _vec(pool_scale), _vec(gmlp_ln_g), _vec(gmlp_ln_b)
    gfin = g_final.reshape(1, D_MODEL)
    rk_col, rlg_col, rlb_col = (a.reshape(DEPTH, D_RWKV, 1) for a in (rwkv_r_k, rwkv_ln_g, rwkv_ln_b))
    state_S5 = jnp.transpose(state_rwkv_S, (0, 2, 3, 4, 1))
    s_S5 = None

    def cache_rows(c):
        c = c.reshape(DEPTH, SB, N_MEM, X_HEADS, X_HEAD_DIM // LANES, LANES)
        return jnp.swapaxes(c, 3, 4).reshape(DEPTH, SB, N_MEM * SUBLANES, LANES)

    cache_k, cache_v = cache_rows(cache_mem_k), cache_rows(cache_mem_v)

    kv_shape = _sds((DEPTH, B, N_MEM * SUBLANES, LANES))
    kv_spec = pl.BlockSpec((None, None, N_MEM * SUBLANES, LANES), lambda l, b: (l, b, 0, 0))
    wl_spec = pl.BlockSpec((None, D_MODEL, D_MODEL), lambda l, b: (l, 0, 0))
    p_mk, p_mv = pl.pallas_call(
        _memkv_body, grid=(DEPTH, B),
        in_specs=[pl.BlockSpec((None, N_MEM, D_MODEL), lambda l, b: (b, 0, 0)),
                  pl.BlockSpec((None, 1, D_MODEL), lambda l, b: (l, 0, 0)), wl_spec, wl_spec],
        out_specs=[kv_spec, kv_spec], out_shape=[kv_shape, kv_shape],
        compiler_params=_params(2, 32), name="memkv")(mem_prompt, gmem, wXk, wXv)

    nt = T // TM_MIX
    ntr = T // TM_RWKV
    xt_spec = pl.BlockSpec((None, TM_MIX, D_MODEL), lambda b, t: (b, t, 0))
    yt_spec = pl.BlockSpec((None, TM_MIX, D_LRU), lambda b, t: (b, t, 0))

    def tail_spec(rows, width):
        return pl.BlockSpec((None, rows, width), lambda b, t: (b, 0, 0))

    hp = x_prompt
    hs = x_sample.reshape(SB, D_MODEL)
    outs = {k: [] for k in ("p_conv", "s_conv", "p_h", "s_h", "p_shift", "s_shift", "p_S",
                            "p_pool", "s_pool", "p_ffn", "s_ffn", "s_cv")}

    for l in range(DEPTH):
        L = functools.partial(_layer, l=l)

        def win(start, width):
            assert start % width == 0
            return pl.BlockSpec((None, D_MODEL, width), lambda *_: (l, 0, start // width))

        win_a, win_b = win(P_A, O_B - O_A), win(P_B, O_C - O_B)
        win_c, win_d, win_g = win(P_C, O_D - O_C), win(P_D, O_G - O_D), win(P_G, D_IN - O_G)

        wup_g = pl.BlockSpec((None, D_MODEL, D_FF), lambda *_: (l, 0, 0))
        wup_v = pl.BlockSpec((None, D_MODEL, D_FF), lambda *_: (l, 0, 1))

        yA, p_conv8, p_h8 = pl.pallas_call(
            _p_lru_body, grid=(B, nt),
            in_specs=[xt_spec, L(gmix), win_a, L(lru_conv_w), L(lcb), L(wra), L(bra), L(wix), L(bix), L(lam)],
            out_specs=[yt_spec, tail_spec(SUBLANES, D_LRU), tail_spec(SUBLANES, D_LRU)],
            out_shape=[_sds((B, T, D_LRU), BF16), _sds((B, SUBLANES, D_LRU)), _sds((B, SUBLANES, D_LRU))],
            scratch_shapes=[pltpu.VMEM((SUBLANES, D_LRU), F32)] * 2 + [pltpu.VMEM((TM_MIX, D_LRU), F32)] * 3,
            compiler_params=_params(2, 48), name="p_lru")(
                hp, gmix, w_in_b, lru_conv_w, lcb, wra, bra, wix, bix, lam)

        yB, p_shift8, p_S = pl.pallas_call(
            _p_rwkv_body, grid=(B, ntr),
            in_specs=[pl.BlockSpec((None, TM_RWKV, D_MODEL), lambda b, t: (b, t, 0)),
                      L(gmix), win_b, L(mu), L(wwa), L(w0), L(a0), L(g2), L(kkw), L(kaw), L(rk), L(rlg), L(rlb)],
            out_specs=[pl.BlockSpec((None, TM_RWKV, D_RWKV), lambda b, t: (b, t, 0)),
                       tail_spec(SUBLANES, D_RWKV_IN),
                       pl.BlockSpec((None, RWKV_HEADS, RWKV_HEAD, RWKV_HEAD), lambda b, t: (b, 0, 0, 0))],
            out_shape=[_sds((B, T, D_RWKV), BF16), _sds((B, SUBLANES, D_RWKV_IN)),
                       _sds((B, RWKV_HEADS, RWKV_HEAD, RWKV_HEAD))],
            scratch_shapes=[pltpu.VMEM((SUBLANES, D_RWKV_IN), F32),
                            pltpu.VMEM((D_RWKV // LANES, LANES, LANES), F32)],
            compiler_params=_params(2, 48), name="p_rwkv")(
                hp, gmix, w_in_b, mu, wwa, w0, a0, g2, kkw, kaw, rk, rlg, rlb)

        yC, p_pool16 = pl.pallas_call(
            _p_pool_body, grid=(B, nt),
            in_specs=[xt_spec, L(gmix), win_c, L(wpool), L(psc)],
            out_specs=[yt_spec, tail_spec(POOL_HIST, D_POOL)],
            out_shape=[_sds((B, T, D_POOL), BF16), _sds((B, POOL_HIST, D_POOL))],
            scratch_shapes=[pltpu.VMEM((POOL_HIST, D_POOL), F32)],
            compiler_params=_params(2, 48), name="p_pool")(hp, gmix, w_in_b, wpool, psc)

        yD = pl.pallas_call(
            _p_gmlp_body, grid=(B, nt),
            in_specs=[xt_spec, L(gmix), win_d, L(glg), L(glb), L(gmlp_w_s), L(bsb)],
            out_specs=yt_spec, out_shape=_sds((B, T, D_GMLP), BF16),
            compiler_params=_params(2, 48), name="p_gmlp")(hp, gmix, w_in_b, glg, glb, gmlp_w_s, bsb)

        def merge(x2d, ys, tm):
            rows = x2d.shape[0]
            xs = pl.BlockSpec((tm, D_MODEL), lambda i: (i, 0))
            ysp = pl.BlockSpec((tm, D_LRU), lambda i: (i, 0))
            return pl.pallas_call(
                _merge_body, grid=(rows // tm,),
                in_specs=[xs, L(gmix), win_g, ysp, ysp, ysp, ysp, L(wP), L(wO)],
                out_specs=xs, out_shape=_sds((rows, D_MODEL)),
                compiler_params=_params(1, 56), name="merge")(x2d, gmix, w_in_b, *ys, wP, wO)

        hp = merge(hp.reshape(B * T, D_MODEL), [y.reshape(B * T, -1) for y in (yA, yB, yC, yD)],
                   TM_MIX).reshape(B, T, D_MODEL)

        kvb_spec = pl.BlockSpec((None, None, N_MEM * SUBLANES, LANES), lambda b, t: (l, b, 0, 0))
        hp = pl.pallas_call(
            _p_xattn_body, grid=(B, nt),
            in_specs=[xt_spec, L(gxat), L(wXq), kvb_spec, kvb_spec, L(wXo)],
            out_specs=xt_spec, out_shape=_sds((B, T, D_MODEL)),
            scratch_shapes=[pltpu.VMEM((SUBLANES, N_MEM, LANES), BF16)] * 2,
            compiler_params=_params(2, 48), name="p_xattn")(hp, gxat, wXq, p_mk, p_mv, wXo)

        hp, p_ffn8 = pl.pallas_call(
            _p_ffn_body, grid=(B, nt),
            in_specs=[xt_spec, L(gffn), wup_g, wup_v, L(cwg), L(cwv), L(cbg), L(cbv), L(wDn)],
            out_specs=[xt_spec, tail_spec(SUBLANES, 2 * D_FF)],
            out_shape=[_sds((B, T, D_MODEL)), _sds((B, SUBLANES, 2 * D_FF))],
            scratch_shapes=[pltpu.VMEM((SUBLANES, 2 * D_FF), F32)],
            compiler_params=_params(2, 56), name="p_ffn")(hp, gffn, w_up_b, w_up_b, cwg, cwv, cbg, cbv, wDn)

        outs["p_conv"].append(p_conv8[:, SUBLANES - 3:])
        outs["p_h"].append(p_h8[:, SUBLANES - 1])
        outs["p_shift"].append(p_shift8[:, SUBLANES - 1:])
        outs["p_S"].append(p_S)
        outs["p_pool"].append(p_pool16[:, POOL_HIST - POOL_BUF:])
        outs["p_ffn"].append(p_ffn8[:, SUBLANES - 2:])

        lconv = jnp.swapaxes(state_lru_conv[l], 0, 1)
        spool = jnp.swapaxes(state_pool[l], 0, 1)
        sffn = jnp.swapaxes(state_ffn_conv[l], 0, 1)
        shift = state_rwkv_shift[l].reshape(SB, D_RWKV_IN)
        row512 = _sds((SB, D_LRU))
        chan = _sds((D_RWKV, SB))
        mix_in = [hs, gmix[l], w_in_b, w_in_b, w_in_b, w_in_b,
                  lconv, state_lru_h[l], lru_conv_w[l], lcb[l], wra[l], bra[l], wix[l], bix[l], lam[l],
                  shift, mu[l], wwa[l], w0[l], a0[l], g2[l], kkw[l], kaw[l],
                  spool, wpool[l], psc[l],
                  glg[l], glb[l], wsd[l], bsr[l]]
        mix_out = [_sds((SB, D_LRU), BF16), row512, row512, _sds((SB, D_POOL), BF16), row512,
                   _sds((SB, D_GMLP), BF16), row512, _sds((SB, D_RWKV_IN)),
                   chan, chan, chan, chan, chan, chan, chan]
        (yA, s_xa, s_h, yC, s_pc, yD, s_zv, s_p, rT, wT, kT, vT, kkT, aT, gT) = pl.pallas_call(
            _s_mix_body, grid=(1,),
            in_specs=[_whole(mix_in[0].shape), _whole(mix_in[1].shape),
                      win_a, win_b, win_c, win_d]
                     + [_whole(a.shape) for a in mix_in[6:]],
            out_specs=[_whole(o.shape) for o in mix_out], out_shape=mix_out,
            compiler_params=_params(1, 56), name="s_mix")(*mix_in)

        hd_spec = pl.BlockSpec((RWKV_HEAD, SB), lambda h: (h, 0))
        col_spec = pl.BlockSpec((None, RWKV_HEAD, 1), lambda h: (l, h, 0))
        st_spec = pl.BlockSpec((None, None, RWKV_HEAD, RWKV_HEAD, SB), lambda h: (l, h, 0, 0, 0))
        has_prev = s_S5 is not None
        yB, s_S5 = pl.pallas_call(
            functools.partial(_s_rwkv_body, has_prev), grid=(RWKV_HEADS,),
            in_specs=[st_spec] + [hd_spec] * 7 + [col_spec] * 3
                     + ([pl.BlockSpec(memory_space=pl.ANY)] if has_prev else []),
            out_specs=[_whole((SB, D_RWKV)), st_spec],
            out_shape=[_sds((SB, D_RWKV), BF16), _sds(state_S5.shape)],
            scratch_shapes=[pltpu.VMEM((D_RWKV, SB), F32)],
            input_output_aliases={11: 1} if has_prev else {},
            compiler_params=_params(1, 48), name="s_rwkv")(
                state_S5, rT, wT, kT, vT, kkT, aT, gT, rk_col, rlg_col, rlb_col,
                *([s_S5] if has_prev else []))

        hs = merge(hs, [yA, yB, yC, yD], SB)

        kc_spec = pl.BlockSpec((None, S_ATT_BB, N_MEM * SUBLANES, LANES), lambda i: (l, i, 0, 0))
        xs_spec = _whole((SB, D_MODEL))
        hs = pl.pallas_call(
            _s_xattn_body, grid=(SB // S_ATT_BB,),
            in_specs=[xs_spec, L(gxat), L(wXq), kc_spec, kc_spec, L(wXo)],
            out_specs=xs_spec, out_shape=_sds((SB, D_MODEL)),
            scratch_shapes=[pltpu.VMEM((SB, D_MODEL), F32), pltpu.VMEM((SB, D_MODEL), F32)],
            compiler_params=_params(1, 56), name="s_xattn")(
                hs, gxat, wXq, cache_k, cache_v, wXo)

        ncf = D_FF // FF_CW
        wu_spec = pl.BlockSpec((None, D_MODEL, FF_CW), lambda j: (l, 0, j))
        wuv_spec = pl.BlockSpec((None, D_MODEL, FF_CW), lambda j: (l, 0, j + ncf))
        sg_spec = pl.BlockSpec((2, SB, FF_CW), lambda j: (0, 0, j))
        sv_spec = pl.BlockSpec((2, SB, FF_CW), lambda j: (0, 0, j + ncf))
        cw_spec = pl.BlockSpec((None, 3, FF_CW), lambda j: (l, 0, j))
        cb_spec = pl.BlockSpec((None, 1, FF_CW), lambda j: (l, 0, j))
        hh_spec = pl.BlockSpec((SB, FF_CW), lambda j: (0, j))
        hs, s_hg, s_hv = pl.pallas_call(
            _s_ffn_body, grid=(ncf,),
            in_specs=[xs_spec, L(gffn), wu_spec, wuv_spec, sg_spec, sv_spec, cw_spec, cw_spec, cb_spec, cb_spec,
                      pl.BlockSpec((None, FF_CW, D_MODEL), lambda j: (l, j, 0))],
            out_specs=[xs_spec, hh_spec, hh_spec],
            out_shape=[_sds((SB, D_MODEL)), _sds((SB, D_FF)), _sds((SB, D_FF))],
            scratch_shapes=[pltpu.VMEM((SB, D_MODEL), F32)],
            compiler_params=_params(1, 48), name="s_ffn")(
                hs, gffn, w_up_b, w_up_b, sffn, sffn, cwg, cwv, cbg, cbv, wDn)

        outs["s_conv"].append(jnp.concatenate([state_lru_conv[l][:, 1:], s_xa[:, None]], axis=1))
        outs["s_h"].append(s_h)
        outs["s_shift"].append(s_p[:, None])
        outs["s_pool"].append(jnp.concatenate([state_pool[l][:, 1:], s_pc[:, None]], axis=1))
        outs["s_ffn"].append(jnp.concatenate(
            [state_ffn_conv[l][:, 1:], jnp.concatenate([s_hg, s_hv], axis=-1)[:, None]], axis=1))
        outs["s_cv"].append(s_zv[:, None])

    def final_norm(x2d, tm):
        rows = x2d.shape[0]
        spec = pl.BlockSpec((tm, D_MODEL), lambda i: (i, 0))
        return pl.pallas_call(_final_norm_body, grid=(rows // tm,),
                              in_specs=[spec, _whole((1, D_MODEL))], out_specs=spec,
                              out_shape=_sds((rows, D_MODEL)),
                              compiler_params=_params(1, 32), name="final_norm")(x2d, gfin)

    y_prompt = final_norm(hp.reshape(B * T, D_MODEL), 1024).reshape(B, T, D_MODEL)
    y_sample = final_norm(hs, SB).reshape(SB, 1, D_MODEL)
    stk = {k: jnp.stack(v, axis=0) for k, v in outs.items()}
    def mem_out(kv):
        kv = kv.reshape(DEPTH, B, N_MEM, X_HEAD_DIM // LANES, X_HEADS, LANES)
        return jnp.swapaxes(kv, 3, 4).reshape(DEPTH, B, N_MEM, X_HEADS, X_HEAD_DIM)

    return (y_prompt, y_sample,
            stk["p_conv"], stk["s_conv"], stk["p_h"], stk["s_h"],
            stk["p_shift"], stk["s_shift"], stk["p_S"], jnp.transpose(s_S5, (0, 4, 1, 2, 3)),
            stk["p_pool"], stk["s_pool"], stk["p_ffn"], stk["s_ffn"],
            mem_out(p_mk), mem_out(p_mv), stk["s_cv"])
```

```python
import functools
import math

import jax
import jax.numpy as jnp
from jax import lax
from jax.experimental import pallas as pl
from jax.experimental.pallas import tpu as pltpu

F32 = jnp.float32
BF16 = jnp.bfloat16

SUBLANES = 8
LANES = 128

D_MODEL = 1024
DEPTH = 4
N_MEM = 256
D_LRU = 512
LRU_C = 8.0
D_RWKV = 512
RWKV_HEAD = 64
RWKV_HEADS = D_RWKV // RWKV_HEAD
R_DECAY = 64
R_AAA = 64
R_GATE = 128
D_RWKV_IN = 3 * D_RWKV + R_DECAY + R_AAA + R_GATE
GN_EPS = 64e-5
D_POOL = 512
POOL_WINDOWS = (2, 4, 8, 16)
POOL_GW = D_POOL // len(POOL_WINDOWS)
POOL_BUF = max(POOL_WINDOWS) - 1
POOL_HIST = 16
D_GMLP = 512
GMLP_GROUPS = 4
CHUNK = 128
N_BRANCH = 4
X_HEADS = 4
X_HEAD_DIM = D_MODEL // X_HEADS
D_FF = 3 * D_MODEL
EPS = 1e-6
PAST_LEN = 16384

O_A = 0
O_B = 2 * D_LRU
O_C = O_B + D_RWKV_IN
O_D = O_C + D_POOL
O_G = O_D + 2 * D_GMLP
D_IN = O_G + N_BRANCH * D_MODEL
P_G = 0
P_A = P_G + (D_IN - O_G)
P_D = P_A + (O_B - O_A)
P_C = P_D + (O_G - O_D)
P_B = -(-(P_C + D_POOL) // D_RWKV_IN) * D_RWKV_IN

TM_MIX = 512
TM_RWKV = 512
FF_CW = 1536
RWKV_CHUNK = 64
S_ATT_BB = 8


def _dot(a, b):
    return jnp.dot(a.astype(BF16), b.astype(BF16), preferred_element_type=F32)


def _rms(x, g):
    return x * lax.rsqrt(jnp.mean(x * x, axis=-1, keepdims=True) + EPS) * g


def _gelu(x):
    c = math.sqrt(2.0 / math.pi)
    return 0.5 * x * (1.0 + jnp.tanh(c * (x + 0.044715 * (x * x * x))))


def _softplus(x):
    return jnp.maximum(x, 0.0) + jnp.log1p(jnp.exp(-jnp.abs(x)))


def _sigmoid(x):
    return jax.nn.sigmoid(x)


def _shift_rows(hist, cur, s):
    n = hist.shape[0]
    return pltpu.roll(jnp.concatenate([hist, cur], axis=0), s, 0)[n:]


def _head_ones():
    r = lax.broadcasted_iota(jnp.int32, (LANES, LANES), 0) // RWKV_HEAD
    c = lax.broadcasted_iota(jnp.int32, (LANES, LANES), 1) // RWKV_HEAD
    return jnp.where(r == c, 1.0, 0.0).astype(BF16)


def _segsum(x, ones):
    xb = x.astype(BF16)
    return jnp.concatenate([jnp.dot(xb[:, p * LANES:(p + 1) * LANES], ones, preferred_element_type=F32)
                            for p in range(x.shape[1] // LANES)], axis=-1)


def _blockdiag_dot(x, w_ref):
    outs = []
    for p in range(x.shape[1] // LANES):
        outs.append(jnp.dot(x[:, p * LANES:(p + 1) * LANES].astype(BF16), w_ref[p],
                            preferred_element_type=F32))
    return jnp.concatenate(outs, axis=-1)


def _lru_gates(xc, wra_ref, bra_ref, wix_ref, bix_ref, lam_ref):
    r = _sigmoid(_blockdiag_dot(xc, wra_ref) + bra_ref[...])
    i = _sigmoid(_blockdiag_dot(xc, wix_ref) + bix_ref[...])
    log_a = -LRU_C * r * _softplus(-lam_ref[...])
    a = jnp.exp(log_a)
    mult = jnp.sqrt(-jnp.tanh(log_a) * (1.0 + a * a))
    return a, mult * (i * xc)


def _rwkv_pre(p, prev, mu, wwa_ref, w0, a0, g2_ref, k_k, k_a, ones):
    px = p + (prev - p) * mu
    r = px[:, 0:D_RWKV]
    k = px[:, D_RWKV:2 * D_RWKV]
    v = px[:, 2 * D_RWKV:3 * D_RWKV]
    lo = px[:, 3 * D_RWKV:3 * D_RWKV + R_DECAY + R_AAA]
    g_lo = px[:, 3 * D_RWKV + R_DECAY + R_AAA:]
    lane = lax.broadcasted_iota(jnp.int32, lo.shape, 1)
    wa = _dot(jnp.where(lane < R_DECAY, jnp.tanh(lo), lo), wwa_ref[...])
    logw = -math.exp(-0.5) * _sigmoid(w0 + wa[:, :D_RWKV])
    a = _sigmoid(a0 + wa[:, D_RWKV:])
    g = _dot(_sigmoid(g_lo), g2_ref[...])
    kk = k * k_k
    kk = kk * lax.rsqrt(jnp.maximum(_segsum(kk * kk, ones), 1e-24))
    k2 = k * (1.0 + (a - 1.0) * k_a)
    return r, logw, k2, v, kk, a, g


def _rwkv_post(y, r, k2, v, g, r_k, ln_g, ln_b, ones):
    inv = 1.0 / RWKV_HEAD
    mean = _segsum(y, ones) * inv
    yc = y - mean
    var = _segsum(yc * yc, ones) * inv
    yn = yc * lax.rsqrt(var + GN_EPS) * ln_g + ln_b
    bonus = _segsum(r * k2 * r_k, ones) * v
    return (yn + bonus) * g


def _pool_project(d, wp_ref, scale):
    return _blockdiag_dot(d, wp_ref) * scale


def _gmlp_uz(z2, ln_g, ln_b):
    u = _gelu(z2[:, :D_GMLP])
    v = _gelu(z2[:, D_GMLP:])
    mu = jnp.mean(v, axis=-1, keepdims=True)
    vc = v - mu
    var = jnp.mean(vc * vc, axis=-1, keepdims=True)
    return u, vc * lax.rsqrt(var + 1e-5) * ln_g + ln_b


def _p_lru_body(x_ref, g_ref, w_ref, cw_ref, cb_ref, wra_ref, bra_ref, wix_ref, bix_ref, lam_ref,
                y_ref, convo_ref, ho_ref, hist_ref, h_ref, sa_ref, sb_ref, carry_ref):
    @pl.when(pl.program_id(1) == 0)
    def _():
        hist_ref[...] = jnp.zeros_like(hist_ref)
        h_ref[...] = jnp.zeros_like(h_ref)

    z = _dot(_rms(x_ref[...], g_ref[...]), w_ref[...])
    xa = z[:, :D_LRU]
    ga = z[:, D_LRU:]
    tm = xa.shape[0]
    hist = hist_ref[...]
    cw = cw_ref[...]
    xc = cb_ref[...] + cw[3:4] * xa
    for s in (1, 2, 3):
        xc = xc + cw[3 - s:4 - s] * _shift_rows(hist, xa, s)
    a, b = _lru_gates(xc, wra_ref, bra_ref, wix_ref, bix_ref, lam_ref)
    groups = (tm // SUBLANES, SUBLANES, D_LRU)
    a3 = a.reshape(groups)
    b3 = b.reshape(groups)
    sub = lax.broadcasted_iota(jnp.int32, groups, 1)
    s = 1
    while s < SUBLANES:
        m = sub >= s
        b3 = jnp.where(m, a3 * pltpu.roll(b3, s, 1) + b3, b3)
        a3 = jnp.where(m, a3 * pltpu.roll(a3, s, 1), a3)
        s *= 2
    a = a3.reshape(tm, D_LRU)
    b = b3.reshape(tm, D_LRU)
    sa_ref[...] = a
    sb_ref[...] = b

    def carry_group(gi, hprev):
        r0 = pl.multiple_of(gi * SUBLANES, SUBLANES)
        carry_ref[pl.ds(r0, SUBLANES), :] = jnp.broadcast_to(hprev, (SUBLANES, D_LRU))
        last = pl.ds(r0 + SUBLANES - 1, 1)
        return sa_ref[last, :] * hprev + sb_ref[last, :]

    lax.fori_loop(0, tm // SUBLANES, carry_group, h_ref[SUBLANES - 1:SUBLANES, :], unroll=8)
    h = a * carry_ref[...] + b
    y_ref[...] = (h * _gelu(ga)).astype(y_ref.dtype)
    hist_ref[...] = xa[tm - SUBLANES:]
    h_ref[...] = h[tm - SUBLANES:]
    convo_ref[...] = xa[tm - SUBLANES:]
    ho_ref[...] = h[tm - SUBLANES:]


def _pair_blockdiag_rows(x):
    lo = lax.broadcasted_iota(jnp.int32, x.shape, 1) < RWKV_HEAD
    z = jnp.zeros_like(x)
    return jnp.concatenate([jnp.where(lo, x, z), jnp.where(lo, z, x)], axis=0)


def _p_rwkv_body(x_ref, g_ref, w_ref, mu_ref, wwa_ref, w0_ref, a0_ref, g2_ref, kk_ref, ka_ref,
                 rk_ref, lng_ref, lnb_ref,
                 y_ref, shifto_ref, so_ref,
                 hist_ref, st_ref):
    @pl.when(pl.program_id(1) == 0)
    def _():
        hist_ref[...] = jnp.zeros_like(hist_ref)
        st_ref[...] = jnp.zeros_like(st_ref)

    C = RWKV_CHUNK
    ones = _head_ones()
    p = _dot(_rms(x_ref[...], g_ref[...]), w_ref[...])
    tm = p.shape[0]
    nc = tm // C
    npair = D_RWKV // LANES
    prev = _shift_rows(hist_ref[...], p, 1)
    r, logw, k2, v, kk, a, g = _rwkv_pre(p, prev, mu_ref[...], wwa_ref, w0_ref[...], a0_ref[...],
                                          g2_ref, kk_ref[...], ka_ref[...], ones)
    hist_ref[...] = p[tm - SUBLANES:]
    shifto_ref[...] = p[tm - SUBLANES:]
    rowc = lax.broadcasted_iota(jnp.int32, logw.shape, 0) % C
    lg = logw
    s = 1
    while s < C:
        lg = lg + jnp.where(rowc >= s, pltpu.roll(lg, s, 0), 0.0)
        s *= 2
    lg_last = [lg[(c + 1) * C - 1:(c + 1) * C, :] for c in range(nc)]
    lg_end = jnp.concatenate([jnp.broadcast_to(x, (C, D_RWKV)) for x in lg_last], axis=0)
    inv_gam = jnp.exp(-lg)
    to_end = jnp.exp(lg_end - lg)
    b = kk * a
    rt_f = r * jnp.exp(lg)
    at_b, rt_b, bt_b, kt_b, v_b, bh_b, kh_b = (x.astype(BF16) for x in (
        -kk * jnp.exp(lg - logw), rt_f, b * inv_gam, k2 * inv_gam, v, b * to_end, k2 * to_end))

    r2 = lax.broadcasted_iota(jnp.int32, (C, LANES), 0)
    c2 = lax.broadcasted_iota(jnp.int32, (C, LANES), 1) % C
    strict2 = c2 < r2
    incl2 = c2 <= r2
    eye2 = jnp.where(r2 == c2, 1.0, 0.0).astype(F32)
    pair2 = (r2 // 2) == (c2 // 2)
    levels = []
    nb = 2
    while nb < C:
        levels.append(((r2 // (2 * nb)) == (c2 // (2 * nb))) & ((r2 // nb) != (c2 // nb)))
        nb *= 2
    rr = lax.broadcasted_iota(jnp.int32, (LANES, LANES), 0)
    cc = lax.broadcasted_iota(jnp.int32, (LANES, LANES), 1)
    same_head = (rr // RWKV_HEAD) == (cc // RWKV_HEAD)
    eye128 = jnp.where(rr == cc, 1.0, 0.0).astype(F32)
    tn = (((0,), (0,)), ((), ()))
    nt_ = (((1,), (1,)), ((), ()))
    bd = _pair_blockdiag_rows

    def mm(x, y):
        return jnp.dot(x, y, preferred_element_type=F32)

    def blk(x, c, q):
        return x[c * C:(c + 1) * C, q * LANES:(q + 1) * LANES]

    P = [(c, q) for c in range(nc) for q in range(npair)]
    N = range(len(P))
    at, rt, bt, kt, vv, bh, kh = ([blk(x, c, q) for c, q in P] for x in
                                  (at_b, rt_b, bt_b, kt_b, v_b, bh_b, kh_b))
    ar = [jnp.concatenate([at[n], rt[n]], axis=0) for n in N]
    pb = [lax.dot_general(ar[n], bd(bt[n]), nt_, preferred_element_type=F32) for n in N]
    pk = [lax.dot_general(ar[n], bd(kt[n]), nt_, preferred_element_type=F32) for n in N]
    lab = [jnp.where(strict2, x[:C], 0.0) for x in pb]
    mrb = [jnp.where(incl2, x[C:], 0.0).astype(BF16) for x in pb]
    lak = [jnp.where(strict2, x[:C], 0.0).astype(BF16) for x in pk]
    mrk = [jnp.where(incl2, x[C:], 0.0).astype(BF16) for x in pk]
    v_bd = [bd(x) for x in vv]
    lak_v = [mm(lak[n], v_bd[n]) for n in N]
    T = [eye2 + jnp.where(pair2, x, 0.0) for x in lab]
    for mk in levels:
        tb = [x.astype(BF16) for x in T]
        u = [mm(tb[n], bd(jnp.where(mk, lab[n], 0.0).astype(BF16))).astype(BF16) for n in N]
        T = [T[n] + mm(u[n], bd(tb[n])) for n in N]
    tb = [x.astype(BF16) for x in T]
    tab = [mm(tb[n], bd(at[n])).astype(BF16) for n in N]
    cv = [mm(tb[n], bd(lak_v[n].astype(BF16))).astype(BF16) for n in N]
    g_bd = [jnp.where(same_head, lax.dot_general(bh[n], tab[n], tn, preferred_element_type=F32), 0.0)
            .astype(BF16) for n in N]
    q_bd = [jnp.where(same_head, lax.dot_general(jnp.concatenate([bh[n], kh[n]], axis=0),
                                                 jnp.concatenate([cv[n], vv[n]], axis=0), tn,
                                                 preferred_element_type=F32), 0.0) for n in N]
    ry = [(blk(rt_f, c, q) + mm(mrb[n], bd(tab[n]))).astype(BF16) for n, (c, q) in enumerate(P)]
    yc = [mm(jnp.concatenate([mrb[n], mrk[n]], axis=1),
             jnp.concatenate([bd(cv[n]), v_bd[n]], axis=0)) for n in N]
    gcol = [jnp.broadcast_to(jnp.sum(eye128 * jnp.exp(lg_last[c][:, q * LANES:(q + 1) * LANES]),
                                     axis=1, keepdims=True), (LANES, LANES)) for c, q in P]

    st = [st_ref[q] for q in range(npair)]
    y_rows = []
    for c in range(nc):
        sb = [x.astype(BF16) for x in st]
        ns = [c * npair + q for q in range(npair)]
        gs = [mm(g_bd[n], sb[q]) for q, n in enumerate(ns)]
        ys = [mm(ry[n], sb[q]) for q, n in enumerate(ns)]
        st = [st[q] * gcol[n] + gs[q] + q_bd[n] for q, n in enumerate(ns)]
        y_rows.append(jnp.concatenate([ys[q] + yc[n] for q, n in enumerate(ns)], axis=1))
    y = jnp.concatenate(y_rows, axis=0)
    yb = _rwkv_post(y, r, k2, v, g, rk_ref[...], lng_ref[...], lnb_ref[...], ones)
    y_ref[...] = yb.astype(y_ref.dtype)
    for q in range(npair):
        st_ref[q] = st[q]
        so_ref[2 * q] = st[q][:RWKV_HEAD, :RWKV_HEAD].T
        so_ref[2 * q + 1] = st[q][RWKV_HEAD:, RWKV_HEAD:].T


def _p_pool_body(x_ref, g_ref, w_ref, wp_ref, sc_ref, y_ref, poolo_ref, hist_ref):
    t = pl.program_id(1)

    @pl.when(t == 0)
    def _():
        hist_ref[...] = jnp.zeros_like(hist_ref)

    pc = _dot(_rms(x_ref[...], g_ref[...]), w_ref[...])
    tm = pc.shape[0]
    X = jnp.concatenate([hist_ref[...], pc], axis=0)
    pos = t * tm + lax.broadcasted_iota(jnp.int32, (tm, POOL_GW), 0)
    ds = []
    for gi, win in enumerate(POOL_WINDOWS):
        sl = slice(gi * POOL_GW, (gi + 1) * POOL_GW)
        s = X[:, sl]
        sh = 1
        while sh < win:
            s = s + pltpu.roll(s, sh, 0)
            sh *= 2
        cnt = jnp.minimum(pos + 1, win).astype(F32)
        ds.append(s[POOL_HIST:] / cnt - pc[:, sl])
    y = _pool_project(jnp.concatenate(ds, axis=-1), wp_ref, sc_ref[...])
    y_ref[...] = y.astype(y_ref.dtype)
    hist_ref[...] = pc[tm - POOL_HIST:]
    poolo_ref[...] = pc[tm - POOL_HIST:]


def _p_gmlp_body(x_ref, g_ref, w_ref, lng_ref, lnb_ref, ws_ref, bsb_ref, y_ref):
    z2 = _dot(_rms(x_ref[...], g_ref[...]), w_ref[...])
    tm = z2.shape[0]
    u, zv = _gmlp_uz(z2, lng_ref[...], lnb_ref[...])
    tril = (lax.broadcasted_iota(jnp.int32, (CHUNK, CHUNK), 0)
            >= lax.broadcasted_iota(jnp.int32, (CHUNK, CHUNK), 1))
    ws = [jnp.where(tril, ws_ref[gi], 0.0).astype(BF16) for gi in range(GMLP_GROUPS)]
    zb = zv.astype(BF16)
    rows = []
    for c in range(tm // CHUNK):
        cols = [jnp.dot(ws[gi], zb[c * CHUNK:(c + 1) * CHUNK, gi * LANES:(gi + 1) * LANES],
                        preferred_element_type=F32) for gi in range(GMLP_GROUPS)]
        rows.append(jnp.concatenate(cols, axis=-1) + bsb_ref[...])
    s = jnp.concatenate(rows, axis=0)
    y_ref[...] = (u * s).astype(y_ref.dtype)


def _merge_body(x_ref, g_ref, wg_ref, ya_ref, yb_ref, yc_ref, yd_ref, wp_ref, wo_ref, o_ref):
    x = x_ref[...]
    xn = _rms(x, g_ref[...]).astype(BF16)
    merged = None
    for i, y_ref in enumerate((ya_ref, yb_ref, yc_ref, yd_ref)):
        gate = _sigmoid(jnp.dot(xn, wg_ref[:, i * D_MODEL:(i + 1) * D_MODEL],
                                preferred_element_type=F32))
        term = gate * jnp.dot(y_ref[...], wp_ref[i], preferred_element_type=F32)
        merged = term if merged is None else merged + term
    o_ref[...] = x + _dot(merged, wo_ref[...])


def _softmax_rows(s):
    e = jnp.exp(s - jnp.max(s, axis=-1, keepdims=True))
    return e / jnp.sum(e, axis=-1, keepdims=True)


def _kv_rows(half, h):
    return pl.ds(half * X_HEADS + h, N_MEM, stride=SUBLANES)


def _p_xattn_body(x_ref, g_ref, wq_ref, k_ref, v_ref, wo_ref, o_ref):
    x = x_ref[...]
    q = _dot(_rms(x, g_ref[...]), wq_ref[...]).astype(BF16)
    outs = []
    for h in range(X_HEADS):
        sl = slice(h * X_HEAD_DIM, (h + 1) * X_HEAD_DIM)
        s = lax.dot_general(q[:, sl], k_ref[:, sl], (((1,), (1,)), ((), ())),
                            preferred_element_type=F32) * (X_HEAD_DIM ** -0.5)
        outs.append(jnp.dot(_softmax_rows(s).astype(BF16), v_ref[:, sl], preferred_element_type=F32))
    o_ref[...] = x + _dot(jnp.concatenate(outs, axis=-1), wo_ref[...])


def _ffn_conv(hist, h, cw, cb):
    return cb + cw[2:3] * h + cw[1:2] * _shift_rows(hist, h, 1) + cw[0:1] * _shift_rows(hist, h, 2)


def _p_ffn_body(x_ref, g_ref, wug_ref, wuv_ref, cwg_ref, cwv_ref, cbg_ref, cbv_ref, wd_ref,
                o_ref, ffo_ref, hist_ref):
    @pl.when(pl.program_id(1) == 0)
    def _():
        hist_ref[...] = jnp.zeros_like(hist_ref)

    x = x_ref[...]
    tm = x.shape[0]
    xn = _rms(x, g_ref[...]).astype(BF16)
    acc = x
    for j in range(D_FF // FF_CW):
        cs = slice(j * FF_CW, (j + 1) * FF_CW)
        vs = slice(D_FF + j * FF_CW, D_FF + (j + 1) * FF_CW)
        hg = jnp.dot(xn, wug_ref[:, cs], preferred_element_type=F32)
        hv = jnp.dot(xn, wuv_ref[:, cs], preferred_element_type=F32)
        cg = _ffn_conv(hist_ref[:, cs], hg, cwg_ref[:, cs], cbg_ref[:, cs])
        cv = _ffn_conv(hist_ref[:, vs], hv, cwv_ref[:, cs], cbv_ref[:, cs])
        acc = acc + _dot(_gelu(cg) * cv, wd_ref[cs, :])
        hist_ref[:, cs] = hg[tm - SUBLANES:]
        hist_ref[:, vs] = hv[tm - SUBLANES:]
    o_ref[...] = acc
    ffo_ref[...] = hist_ref[...]


def _memkv_body(m_ref, g_ref, wk_ref, wv_ref, k_ref, v_ref, kb_ref, vb_ref):
    mn = _rms(m_ref[...], g_ref[...]).astype(BF16)
    for w_ref, o_ref, b_ref in ((wk_ref, k_ref, kb_ref), (wv_ref, v_ref, vb_ref)):
        kv = jnp.dot(mn, w_ref[...], preferred_element_type=F32)
        b_ref[...] = kv.astype(BF16)
        for h in range(X_HEADS):
            for half in range(X_HEAD_DIM // LANES):
                lo = h * X_HEAD_DIM + half * LANES
                o_ref[_kv_rows(half, h), :] = kv[:, lo:lo + LANES]


def _final_norm_body(x_ref, g_ref, o_ref):
    o_ref[...] = _rms(x_ref[...], g_ref[...])


def _s_mix_body(x_ref, g_ref, wa_ref, wb_ref, wc_ref, wd_ref,
                lconv_ref, lh_ref, cw_ref, cb_ref, wra_ref, bra_ref, wix_ref, bix_ref, lam_ref,
                shift_ref, mu_ref, wwa_ref, w0_ref, a0_ref, g2_ref, kk_ref, ka_ref,
                pool_ref, wp_ref, psc_ref,
                lng_ref, lnb_ref, wsd_ref, bsr_ref,
                ya_ref, xa_ref, h_ref, yc_ref, pc_ref, yd_ref, zv_ref, p_ref,
                rt_ref, wt_ref, kt_ref, vt_ref, kkt_ref, at_ref, gt_ref):
    xn = _rms(x_ref[...], g_ref[...]).astype(BF16)
    z = jnp.dot(xn, wa_ref[...], preferred_element_type=F32)
    xa = z[:, :D_LRU]
    cw = cw_ref[...]
    xc = cb_ref[...] + cw[0:1] * lconv_ref[0] + cw[1:2] * lconv_ref[1] + cw[2:3] * lconv_ref[2] + cw[3:4] * xa
    a, b = _lru_gates(xc, wra_ref, bra_ref, wix_ref, bix_ref, lam_ref)
    h = a * lh_ref[...] + b
    ya_ref[...] = (h * _gelu(z[:, D_LRU:])).astype(ya_ref.dtype)
    xa_ref[...] = xa
    h_ref[...] = h
    ones = _head_ones()
    p = jnp.dot(xn, wb_ref[...], preferred_element_type=F32)
    r, logw, k2, v, kk, aa, g = _rwkv_pre(p, shift_ref[...], mu_ref[...], wwa_ref, w0_ref[...],
                                           a0_ref[...], g2_ref, kk_ref[...], ka_ref[...], ones)
    p_ref[...] = p
    for ref, val in ((rt_ref, r), (wt_ref, jnp.exp(logw)), (kt_ref, k2), (vt_ref, v), (kkt_ref, kk),
                     (at_ref, aa), (gt_ref, g)):
        ref[...] = val.T
    pc = jnp.dot(xn, wc_ref[...], preferred_element_type=F32)
    ds = []
    for gi, win in enumerate(POOL_WINDOWS):
        sl = slice(gi * POOL_GW, (gi + 1) * POOL_GW)
        s = pc[:, sl]
        for j in range(POOL_BUF - (win - 1), POOL_BUF):
            s = s + pool_ref[j][:, sl]
        ds.append(s / float(min(PAST_LEN + 1, win)) - pc[:, sl])
    yc_ref[...] = _pool_project(jnp.concatenate(ds, axis=-1), wp_ref, psc_ref[...]).astype(yc_ref.dtype)
    pc_ref[...] = pc
    u, zv = _gmlp_uz(jnp.dot(xn, wd_ref[...], preferred_element_type=F32), lng_ref[...], lnb_ref[...])
    yd_ref[...] = (u * (wsd_ref[...] * zv + bsr_ref[...])).astype(yd_ref.dtype)
    zv_ref[...] = zv


def _s_rwkv_body(has_prev, s_ref, r_ref, w_ref, k_ref, v_ref, kk_ref, a_ref, g_ref, rk_ref, lng_ref,
                 lnb_ref, *rest):
    y_ref, so_ref, yt_sc = rest[1:] if has_prev else rest
    h = pl.program_id(0)
    kk = kk_ref[...]
    w = w_ref[...]
    k = k_ref[...]
    r = r_ref[...]
    bk = kk * a_ref[...]

    def body(i, carry):
        si = s_ref[i]
        sa = -jnp.sum(si * kk, axis=0, keepdims=True)
        s2 = si * w + sa * bk + v_ref[pl.ds(i, 1), :] * k
        so_ref[i] = s2
        yt_sc[pl.ds(h * RWKV_HEAD + i, 1), :] = jnp.sum(s2 * r, axis=0, keepdims=True)
        return carry

    lax.fori_loop(0, RWKV_HEAD, body, 0, unroll=4)
    rows = pl.ds(pl.multiple_of(h * RWKV_HEAD, RWKV_HEAD), RWKV_HEAD)
    y = yt_sc[rows, :]
    yc = y - jnp.mean(y, axis=0, keepdims=True)
    var = jnp.mean(yc * yc, axis=0, keepdims=True)
    yn = yc * lax.rsqrt(var + GN_EPS) * lng_ref[...] + lnb_ref[...]
    bonus = jnp.sum(r * k * rk_ref[...], axis=0, keepdims=True) * v_ref[...]
    yt_sc[rows, :] = (yn + bonus) * g_ref[...]

    @pl.when(h == pl.num_programs(0) - 1)
    def _():
        y_ref[...] = yt_sc[...].T.astype(y_ref.dtype)


def _s_xattn_body(x_ref, g_ref, wq_ref, k_ref, v_ref, wo_ref, o_ref, q_sc, a_sc):
    i = pl.program_id(0)
    bb = k_ref.shape[0]

    @pl.when(i == 0)
    def _():
        q_sc[...] = _dot(_rms(x_ref[...], g_ref[...]), wq_ref[...])

    halves = X_HEAD_DIM // LANES

    def q_lanes(c, h):
        return slice(h * X_HEAD_DIM + c * LANES, h * X_HEAD_DIM + (c + 1) * LANES)

    for j in range(bb):
        row = pl.ds(i * bb + j, 1)
        qrow = q_sc[row, :]
        q8 = jnp.concatenate([qrow[:, q_lanes(c, h)] for c in range(halves) for h in range(X_HEADS)], axis=0)
        k3 = k_ref[j].reshape(N_MEM, SUBLANES, LANES)
        part = jnp.sum(k3 * q8[None], axis=-1, keepdims=True)
        part = jnp.broadcast_to(part, k3.shape)
        s = (part + pltpu.roll(part, X_HEADS, 1)) * (X_HEAD_DIM ** -0.5)
        e = jnp.exp(s - jnp.max(s, axis=0, keepdims=True))
        pr = e / jnp.sum(e, axis=0, keepdims=True)
        o8 = jnp.sum(pr * v_ref[j].reshape(N_MEM, SUBLANES, LANES), axis=0)
        a_sc[row, :] = jnp.concatenate([o8[c * X_HEADS + h:c * X_HEADS + h + 1, :]
                                        for h in range(X_HEADS) for c in range(halves)], axis=1)

    @pl.when(i == pl.num_programs(0) - 1)
    def _():
        o_ref[...] = x_ref[...] + _dot(a_sc[...], wo_ref[...])


def _s_ffn_body(x_ref, g_ref, wug_ref, wuv_ref, sg_ref, sv_ref, cwg_ref, cwv_ref, cbg_ref, cbv_ref,
                wd_ref, o_ref, hg_ref, hv_ref, acc_ref):
    j = pl.program_id(0)

    @pl.when(j == 0)
    def _():
        acc_ref[...] = x_ref[...]

    xn = _rms(x_ref[...], g_ref[...]).astype(BF16)
    hg = jnp.dot(xn, wug_ref[...], preferred_element_type=F32)
    hv = jnp.dot(xn, wuv_ref[...], preferred_element_type=F32)
    cwg = cwg_ref[...]
    cwv = cwv_ref[...]
    cg = cbg_ref[...] + cwg[0:1] * sg_ref[0] + cwg[1:2] * sg_ref[1] + cwg[2:3] * hg
    cv = cbv_ref[...] + cwv[0:1] * sv_ref[0] + cwv[1:2] * sv_ref[1] + cwv[2:3] * hv
    acc_ref[...] += _dot(_gelu(cg) * cv, wd_ref[...])
    hg_ref[...] = hg
    hv_ref[...] = hv

    @pl.when(j == pl.num_programs(0) - 1)
    def _():
        o_ref[...] = acc_ref[...]


def _params(n_grid, vmem_mb):
    return pltpu.CompilerParams(dimension_semantics=("arbitrary",) * n_grid,
                                vmem_limit_bytes=vmem_mb << 20)


def _whole(shape):
    return pl.BlockSpec(tuple(shape), lambda *_: (0,) * len(shape))


def _layer(arr, l):
    shape = arr.shape[1:]
    return pl.BlockSpec((None,) + tuple(shape), lambda *_: (l,) + (0,) * len(shape))


def _sds(shape, dtype=F32):
    return jax.ShapeDtypeStruct(tuple(shape), dtype)


def _pair_blockdiag(w):
    L = w.shape[0]
    w = w.reshape(L, 4, 2, RWKV_HEAD, RWKV_HEAD)
    z = jnp.zeros_like(w[:, :, 0])
    top = jnp.concatenate([w[:, :, 0], z], axis=-1)
    bot = jnp.concatenate([z, w[:, :, 1]], axis=-1)
    return jnp.concatenate([top, bot], axis=-2)


def _vec(a):
    return a.reshape(a.shape[0], 1, -1)


def kernel(x_prompt, x_sample, state_lru_conv, state_lru_h, state_rwkv_shift, state_rwkv_S, state_pool, state_ffn_conv, cache_mem_k, cache_mem_v, mem_prompt, g_mix, w_in, lru_conv_w, lru_conv_b, lru_w_ra, lru_b_ra, lru_w_ix, lru_b_ix, lru_lambda, rwkv_mu, rwkv_w0, rwkv_w2, rwkv_a0, rwkv_a2, rwkv_g2, rwkv_k_k, rwkv_k_a, rwkv_r_k, rwkv_ln_g, rwkv_ln_b, pool_w, pool_scale, gmlp_ln_g, gmlp_ln_b, gmlp_w_s, gmlp_b_s, w_pa, w_pb, w_pc, w_pd, w_o, g_xattn, g_mem, w_xq, w_xk, w_xv, w_xo, g_ffn, w_up, ffn_conv_w, ffn_conv_b, w_down, g_final):
    B, T, D = x_prompt.shape
    SB = x_sample.shape[0]
    assert D == D_MODEL and w_in.shape == (DEPTH, D_MODEL, D_IN) and x_sample.shape[1] == 1
    assert T % TM_MIX == 0 and T % TM_RWKV == 0 and TM_RWKV % RWKV_CHUNK == 0 and SB % S_ATT_BB == 0
    assert SB == LANES and X_HEADS * (X_HEAD_DIM // LANES) == SUBLANES

    w_in_b = jnp.concatenate(
        [w_in[:, :, O_G:], w_in[:, :, O_A:O_B], w_in[:, :, O_D:O_G], w_in[:, :, O_C:O_D],
         jnp.zeros((DEPTH, D_MODEL, P_B - (P_C + D_POOL)), w_in.dtype), w_in[:, :, O_B:O_C]], axis=-1).astype(BF16)
    wra = _pair_blockdiag(lru_w_ra).astype(BF16)
    wix = _pair_blockdiag(lru_w_ix).astype(BF16)
    zer = jnp.zeros((DEPTH, R_DECAY, D_RWKV), F32)
    wwa = jnp.concatenate([jnp.concatenate([rwkv_w2, zer], axis=-1),
                           jnp.concatenate([zer, rwkv_a2], axis=-1)], axis=1).astype(BF16)
    g2 = rwkv_g2.astype(BF16)
    wpool = pool_w.astype(BF16)
    bsb = jnp.repeat(gmlp_b_s, LANES, axis=-1)
    off = PAST_LEN % CHUNK
    wsd = jnp.repeat(gmlp_w_s[:, :, off, off], LANES, axis=-1)[:, None, :]
    bsr = jnp.repeat(gmlp_b_s[:, off, :], LANES, axis=-1)[:, None, :]
    wP = jnp.stack([w_pa, w_pb, w_pc, w_pd], axis=1).astype(BF16)
    wO = w_o.astype(BF16)
    wXq, wXk, wXv, wXo = (w.astype(BF16) for w in (w_xq, w_xk, w_xv, w_xo))
    w_up_b = w_up.astype(BF16)
    wDn = w_down.astype(BF16)
    cwg, cwv = ffn_conv_w[:, :, :D_FF], ffn_conv_w[:, :, D_FF:]
    cbg, cbv = _vec(ffn_conv_b[:, :D_FF]), _vec(ffn_conv_b[:, D_FF:])
    gmix, gxat, gffn, gmem = _vec(g_mix), _vec(g_xattn), _vec(g_ffn), _vec(g_mem)
    lcb, bra, bix, lam = _vec(lru_conv_b), _vec(lru_b_ra), _vec(lru_b_ix), _vec(lru_lambda)
    mu, w0, a0, kkw, kaw = _vec(rwkv_mu), _vec(rwkv_w0), _vec(rwkv_a0), _vec(rwkv_k_k), _vec(rwkv_k_a)
    rk = rwkv_r_k.reshape(DEPTH, 1, D_RWKV)
    rlg, rlb = _vec(rwkv_ln_g), _vec(rwkv_ln_b)
    psc, glg, glb = _vec(pool_scale), _vec(gmlp_ln_g), _vec(gmlp_ln_b)
    gfin = g_final.reshape(1, D_MODEL)
    rk_col, rlg_col, rlb_col = (a.reshape(DEPTH, D_RWKV, 1) for a in (rwkv_r_k, rwkv_ln_g, rwkv_ln_b))
    state_S5 = jnp.transpose(state_rwkv_S, (0, 2, 3, 4, 1))
    s_S5 = None

    def cache_rows(c):
        c = c.reshape(DEPTH, SB, N_MEM, X_HEADS, X_HEAD_DIM // LANES, LANES)
        return jnp.swapaxes(c, 3, 4).reshape(DEPTH, SB, N_MEM * SUBLANES, LANES)

    cache_k, cache_v = cache_rows(cache_mem_k), cache_rows(cache_mem_v)

    kv_shape = _sds((DEPTH, B, N_MEM * SUBLANES, LANES))
    kv_spec = pl.BlockSpec((None, None, N_MEM * SUBLANES, LANES), lambda l, b: (l, b, 0, 0))
    wl_spec = pl.BlockSpec((None, D_MODEL, D_MODEL), lambda l, b: (l, 0, 0))
    kvrow_shape = _sds((DEPTH, B, N_MEM, D_MODEL), BF16)
    kvrow_spec = pl.BlockSpec((None, None, N_MEM, D_MODEL), lambda l, b: (l, b, 0, 0))
    p_mk, p_mv, p_mk_b, p_mv_b = pl.pallas_call(
        _memkv_body, grid=(DEPTH, B),
        in_specs=[pl.BlockSpec((None, N_MEM, D_MODEL), lambda l, b: (b, 0, 0)),
                  pl.BlockSpec((None, 1, D_MODEL), lambda l, b: (l, 0, 0)), wl_spec, wl_spec],
        out_specs=[kv_spec, kv_spec, kvrow_spec, kvrow_spec],
        out_shape=[kv_shape, kv_shape, kvrow_shape, kvrow_shape],
        compiler_params=_params(2, 32), name="memkv")(mem_prompt, gmem, wXk, wXv)

    nt = T // TM_MIX
    ntr = T // TM_RWKV
    xt_spec = pl.BlockSpec((None, TM_MIX, D_MODEL), lambda b, t: (b, t, 0))
    yt_spec = pl.BlockSpec((None, TM_MIX, D_LRU), lambda b, t: (b, t, 0))

    def tail_spec(rows, width):
        return pl.BlockSpec((None, rows, width), lambda b, t: (b, 0, 0))

    hp = x_prompt
    hs = x_sample.reshape(SB, D_MODEL)
    outs = {k: [] for k in ("p_conv", "s_conv", "p_h", "s_h", "p_shift", "s_shift", "p_S",
                            "p_pool", "s_pool", "p_ffn", "s_ffn", "s_cv")}

    for l in range(DEPTH):
        L = functools.partial(_layer, l=l)

        def win(start, width):
            assert start % width == 0
            return pl.BlockSpec((None, D_MODEL, width), lambda *_: (l, 0, start // width))

        win_a, win_b = win(P_A, O_B - O_A), win(P_B, O_C - O_B)
        win_c, win_d, win_g = win(P_C, O_D - O_C), win(P_D, O_G - O_D), win(P_G, D_IN - O_G)

        wup_g = pl.BlockSpec((None, D_MODEL, D_FF), lambda *_: (l, 0, 0))
        wup_v = pl.BlockSpec((None, D_MODEL, D_FF), lambda *_: (l, 0, 1))

        yA, p_conv8, p_h8 = pl.pallas_call(
            _p_lru_body, grid=(B, nt),
            in_specs=[xt_spec, L(gmix), win_a, L(lru_conv_w), L(lcb), L(wra), L(bra), L(wix), L(bix), L(lam)],
            out_specs=[yt_spec, tail_spec(SUBLANES, D_LRU), tail_spec(SUBLANES, D_LRU)],
            out_shape=[_sds((B, T, D_LRU), BF16), _sds((B, SUBLANES, D_LRU)), _sds((B, SUBLANES, D_LRU))],
            scratch_shapes=[pltpu.VMEM((SUBLANES, D_LRU), F32)] * 2 + [pltpu.VMEM((TM_MIX, D_LRU), F32)] * 3,
            compiler_params=_params(2, 48), name="p_lru")(
                hp, gmix, w_in_b, lru_conv_w, lcb, wra, bra, wix, bix, lam)

        yB, p_shift8, p_S = pl.pallas_call(
            _p_rwkv_body, grid=(B, ntr),
            in_specs=[pl.BlockSpec((None, TM_RWKV, D_MODEL), lambda b, t: (b, t, 0)),
                      L(gmix), win_b, L(mu), L(wwa), L(w0), L(a0), L(g2), L(kkw), L(kaw), L(rk), L(rlg), L(rlb)],
            out_specs=[pl.BlockSpec((None, TM_RWKV, D_RWKV), lambda b, t: (b, t, 0)),
                       tail_spec(SUBLANES, D_RWKV_IN),
                       pl.BlockSpec((None, RWKV_HEADS, RWKV_HEAD, RWKV_HEAD), lambda b, t: (b, 0, 0, 0))],
            out_shape=[_sds((B, T, D_RWKV), BF16), _sds((B, SUBLANES, D_RWKV_IN)),
                       _sds((B, RWKV_HEADS, RWKV_HEAD, RWKV_HEAD))],
            scratch_shapes=[pltpu.VMEM((SUBLANES, D_RWKV_IN), F32),
                            pltpu.VMEM((D_RWKV // LANES, LANES, LANES), F32)],
            compiler_params=_params(2, 48), name="p_rwkv")(
                hp, gmix, w_in_b, mu, wwa, w0, a0, g2, kkw, kaw, rk, rlg, rlb)

        yC, p_pool16 = pl.pallas_call(
            _p_pool_body, grid=(B, nt),
            in_specs=[xt_spec, L(gmix), win_c, L(wpool), L(psc)],
            out_specs=[yt_spec, tail_spec(POOL_HIST, D_POOL)],
            out_shape=[_sds((B, T, D_POOL), BF16), _sds((B, POOL_HIST, D_POOL))],
            scratch_shapes=[pltpu.VMEM((POOL_HIST, D_POOL), F32)],
            compiler_params=_params(2, 48), name="p_pool")(hp, gmix, w_in_b, wpool, psc)

        yD = pl.pallas_call(
            _p_gmlp_body, grid=(B, nt),
            in_specs=[xt_spec, L(gmix), win_d, L(glg), L(glb), L(gmlp_w_s), L(bsb)],
            out_specs=yt_spec, out_shape=_sds((B, T, D_GMLP), BF16),
            compiler_params=_params(2, 48), name="p_gmlp")(hp, gmix, w_in_b, glg, glb, gmlp_w_s, bsb)

        def merge(x2d, ys, tm):
            rows = x2d.shape[0]
            xs = pl.BlockSpec((tm, D_MODEL), lambda i: (i, 0))
            ysp = pl.BlockSpec((tm, D_LRU), lambda i: (i, 0))
            return pl.pallas_call(
                _merge_body, grid=(rows // tm,),
                in_specs=[xs, L(gmix), win_g, ysp, ysp, ysp, ysp, L(wP), L(wO)],
                out_specs=xs, out_shape=_sds((rows, D_MODEL)),
                compiler_params=_params(1, 56), name="merge")(x2d, gmix, w_in_b, *ys, wP, wO)

        hp = merge(hp.reshape(B * T, D_MODEL), [y.reshape(B * T, -1) for y in (yA, yB, yC, yD)],
                   TM_MIX).reshape(B, T, D_MODEL)

        kvb_spec = pl.BlockSpec((None, None, N_MEM, D_MODEL), lambda b, t: (l, b, 0, 0))
        hp = pl.pallas_call(
            _p_xattn_body, grid=(B, nt),
            in_specs=[xt_spec, L(gxat), L(wXq), kvb_spec, kvb_spec, L(wXo)],
            out_specs=xt_spec, out_shape=_sds((B, T, D_MODEL)),
            compiler_params=_params(2, 48), name="p_xattn")(hp, gxat, wXq, p_mk_b, p_mv_b, wXo)

        hp, p_ffn8 = pl.pallas_call(
            _p_ffn_body, grid=(B, nt),
            in_specs=[xt_spec, L(gffn), wup_g, wup_v, L(cwg), L(cwv), L(cbg), L(cbv), L(wDn)],
            out_specs=[xt_spec, tail_spec(SUBLANES, 2 * D_FF)],
            out_shape=[_sds((B, T, D_MODEL)), _sds((B, SUBLANES, 2 * D_FF))],
            scratch_shapes=[pltpu.VMEM((SUBLANES, 2 * D_FF), F32)],
            compiler_params=_params(2, 56), name="p_ffn")(hp, gffn, w_up_b, w_up_b, cwg, cwv, cbg, cbv, wDn)

        outs["p_conv"].append(p_conv8[:, SUBLANES - 3:])
        outs["p_h"].append(p_h8[:, SUBLANES - 1])
        outs["p_shift"].append(p_shift8[:, SUBLANES - 1:])
        outs["p_S"].append(p_S)
        outs["p_pool"].append(p_pool16[:, POOL_HIST - POOL_BUF:])
        outs["p_ffn"].append(p_ffn8[:, SUBLANES - 2:])

        lconv = jnp.swapaxes(state_lru_conv[l], 0, 1)
        spool = jnp.swapaxes(state_pool[l], 0, 1)
        sffn = jnp.swapaxes(state_ffn_conv[l], 0, 1)
        shift = state_rwkv_shift[l].reshape(SB, D_RWKV_IN)
        row512 = _sds((SB, D_LRU))
        chan = _sds((D_RWKV, SB))
        mix_in = [hs, gmix[l], w_in_b, w_in_b, w_in_b, w_in_b,
                  lconv, state_lru_h[l], lru_conv_w[l], lcb[l], wra[l], bra[l], wix[l], bix[l], lam[l],
                  shift, mu[l], wwa[l], w0[l], a0[l], g2[l], kkw[l], kaw[l],
                  spool, wpool[l], psc[l],
                  glg[l], glb[l], wsd[l], bsr[l]]
        mix_out = [_sds((SB, D_LRU), BF16), row512, row512, _sds((SB, D_POOL), BF16), row512,
                   _sds((SB, D_GMLP), BF16), row512, _sds((SB, D_RWKV_IN)),
                   chan, chan, chan, chan, chan, chan, chan]
        (yA, s_xa, s_h, yC, s_pc, yD, s_zv, s_p, rT, wT, kT, vT, kkT, aT, gT) = pl.pallas_call(
            _s_mix_body, grid=(1,),
            in_specs=[_whole(mix_in[0].shape), _whole(mix_in[1].shape),
                      win_a, win_b, win_c, win_d]
                     + [_whole(a.shape) for a in mix_in[6:]],
            out_specs=[_whole(o.shape) for o in mix_out], out_shape=mix_out,
            compiler_params=_params(1, 56), name="s_mix")(*mix_in)

        hd_spec = pl.BlockSpec((RWKV_HEAD, SB), lambda h: (h, 0))
        col_spec = pl.BlockSpec((None, RWKV_HEAD, 1), lambda h: (l, h, 0))
        st_spec = pl.BlockSpec((None, None, RWKV_HEAD, RWKV_HEAD, SB), lambda h: (l, h, 0, 0, 0))
        has_prev = s_S5 is not None
        yB, s_S5 = pl.pallas_call(
            functools.partial(_s_rwkv_body, has_prev), grid=(RWKV_HEADS,),
            in_specs=[st_spec] + [hd_spec] * 7 + [col_spec] * 3
                     + ([pl.BlockSpec(memory_space=pl.ANY)] if has_prev else []),
            out_specs=[_whole((SB, D_RWKV)), st_spec],
            out_shape=[_sds((SB, D_RWKV), BF16), _sds(state_S5.shape)],
            scratch_shapes=[pltpu.VMEM((D_RWKV, SB), F32)],
            input_output_aliases={11: 1} if has_prev else {},
            compiler_params=_params(1, 48), name="s_rwkv")(
                state_S5, rT, wT, kT, vT, kkT, aT, gT, rk_col, rlg_col, rlb_col,
                *([s_S5] if has_prev else []))

        hs = merge(hs, [yA, yB, yC, yD], SB)

        kc_spec = pl.BlockSpec((None, S_ATT_BB, N_MEM * SUBLANES, LANES), lambda i: (l, i, 0, 0))
        xs_spec = _whole((SB, D_MODEL))
        hs = pl.pallas_call(
            _s_xattn_body, grid=(SB // S_ATT_BB,),
            in_specs=[xs_spec, L(gxat), L(wXq), kc_spec, kc_spec, L(wXo)],
            out_specs=xs_spec, out_shape=_sds((SB, D_MODEL)),
            scratch_shapes=[pltpu.VMEM((SB, D_MODEL), F32), pltpu.VMEM((SB, D_MODEL), F32)],
            compiler_params=_params(1, 56), name="s_xattn")(
                hs, gxat, wXq, cache_k, cache_v, wXo)

        ncf = D_FF // FF_CW
        wu_spec = pl.BlockSpec((None, D_MODEL, FF_CW), lambda j: (l, 0, j))
        wuv_spec = pl.BlockSpec((None, D_MODEL, FF_CW), lambda j: (l, 0, j + ncf))
        sg_spec = pl.BlockSpec((2, SB, FF_CW), lambda j: (0, 0, j))
        sv_spec = pl.BlockSpec((2, SB, FF_CW), lambda j: (0, 0, j + ncf))
        cw_spec = pl.BlockSpec((None, 3, FF_CW), lambda j: (l, 0, j))
        cb_spec = pl.BlockSpec((None, 1, FF_CW), lambda j: (l, 0, j))
        hh_spec = pl.BlockSpec((SB, FF_CW), lambda j: (0, j))
        hs, s_hg, s_hv = pl.pallas_call(
            _s_ffn_body, grid=(ncf,),
            in_specs=[xs_spec, L(gffn), wu_spec, wuv_spec, sg_spec, sv_spec, cw_spec, cw_spec, cb_spec, cb_spec,
                      pl.BlockSpec((None, FF_CW, D_MODEL), lambda j: (l, j, 0))],
            out_specs=[xs_spec, hh_spec, hh_spec],
            out_shape=[_sds((SB, D_MODEL)), _sds((SB, D_FF)), _sds((SB, D_FF))],
            scratch_shapes=[pltpu.VMEM((SB, D_MODEL), F32)],
            compiler_params=_params(1, 48), name="s_ffn")(
                hs, gffn, w_up_b, w_up_b, sffn, sffn, cwg, cwv, cbg, cbv, wDn)

        outs["s_conv"].append(jnp.concatenate([state_lru_conv[l][:, 1:], s_xa[:, None]], axis=1))
        outs["s_h"].append(s_h)
        outs["s_shift"].append(s_p[:, None])
        outs["s_pool"].append(jnp.concatenate([state_pool[l][:, 1:], s_pc[:, None]], axis=1))
        outs["s_ffn"].append(jnp.concatenate(
            [state_ffn_conv[l][:, 1:], jnp.concatenate([s_hg, s_hv], axis=-1)[:, None]], axis=1))
        outs["s_cv"].append(s_zv[:, None])

    def final_norm(x2d, tm):
        rows = x2d.shape[0]
        spec = pl.BlockSpec((tm, D_MODEL), lambda i: (i, 0))
        return pl.pallas_call(_final_norm_body, grid=(rows // tm,),
                              in_specs=[spec, _whole((1, D_MODEL))], out_specs=spec,
                              out_shape=_sds((rows, D_MODEL)),
                              compiler_params=_params(1, 32), name="final_norm")(x2d, gfin)

    y_prompt = final_norm(hp.reshape(B * T, D_MODEL), 1024).reshape(B, T, D_MODEL)
    y_sample = final_norm(hs, SB).reshape(SB, 1, D_MODEL)
    stk = {k: jnp.stack(v, axis=0) for k, v in outs.items()}
    def mem_out(kv):
        kv = kv.reshape(DEPTH, B, N_MEM, X_HEAD_DIM // LANES, X_HEADS, LANES)
        return jnp.swapaxes(kv, 3, 4).reshape(DEPTH, B, N_MEM, X_HEADS, X_HEAD_DIM)

    return (y_prompt, y_sample,
            stk["p_conv"], stk["s_conv"], stk["p_h"], stk["s_h"],
            stk["p_shift"], stk["s_shift"], stk["p_S"], jnp.transpose(s_S5, (0, 4, 1, 2, 3)),
            stk["p_pool"], stk["s_pool"], stk["p_ffn"], stk["s_ffn"],
            mem_out(p_mk), mem_out(p_mv), stk["s_cv"])
```

```python
import functools
import math

import jax
import jax.numpy as jnp
from jax import lax
from jax.experimental import pallas as pl
from jax.experimental.pallas import tpu as pltpu

F32 = jnp.float32
BF16 = jnp.bfloat16

SUBLANES = 8
LANES = 128

D_MODEL = 1024
DEPTH = 4
N_MEM = 256
D_LRU = 512
LRU_C = 8.0
D_RWKV = 512
RWKV_HEAD = 64
RWKV_HEADS = D_RWKV // RWKV_HEAD
R_DECAY = 64
R_AAA = 64
R_GATE = 128
D_RWKV_IN = 3 * D_RWKV + R_DECAY + R_AAA + R_GATE
GN_EPS = 64e-5
D_POOL = 512
POOL_WINDOWS = (2, 4, 8, 16)
POOL_GW = D_POOL // len(POOL_WINDOWS)
POOL_BUF = max(POOL_WINDOWS) - 1
POOL_HIST = 16
D_GMLP = 512
GMLP_GROUPS = 4
CHUNK = 128
N_BRANCH = 4
X_HEADS = 4
X_HEAD_DIM = D_MODEL // X_HEADS
D_FF = 3 * D_MODEL
EPS = 1e-6
PAST_LEN = 16384

O_A = 0
O_B = 2 * D_LRU
O_C = O_B + D_RWKV_IN
O_D = O_C + D_POOL
O_G = O_D + 2 * D_GMLP
D_IN = O_G + N_BRANCH * D_MODEL
P_G = 0
P_A = P_G + (D_IN - O_G)
P_D = P_A + (O_B - O_A)
P_C = P_D + (O_G - O_D)
P_B = -(-(P_C + D_POOL) // D_RWKV_IN) * D_RWKV_IN

TM_MIX = 512
TM_RWKV = 512
FF_CW = 1536
RWKV_CHUNK = 64
S_ATT_BB = 8


def _dot(a, b):
    return jnp.dot(a.astype(BF16), b.astype(BF16), preferred_element_type=F32)


def _rms(x, g):
    return x * lax.rsqrt(jnp.mean(x * x, axis=-1, keepdims=True) + EPS) * g


def _gelu(x):
    c = math.sqrt(2.0 / math.pi)
    return 0.5 * x * (1.0 + jnp.tanh(c * (x + 0.044715 * (x * x * x))))


def _softplus(x):
    return jnp.maximum(x, 0.0) + jnp.log1p(jnp.exp(-jnp.abs(x)))


def _sigmoid(x):
    return jax.nn.sigmoid(x)


def _shift_rows(hist, cur, s):
    n = hist.shape[0]
    return pltpu.roll(jnp.concatenate([hist, cur], axis=0), s, 0)[n:]


def _head_ones():
    r = lax.broadcasted_iota(jnp.int32, (LANES, LANES), 0) // RWKV_HEAD
    c = lax.broadcasted_iota(jnp.int32, (LANES, LANES), 1) // RWKV_HEAD
    return jnp.where(r == c, 1.0, 0.0).astype(BF16)


def _segsum(x, ones):
    xb = x.astype(BF16)
    return jnp.concatenate([jnp.dot(xb[:, p * LANES:(p + 1) * LANES], ones, preferred_element_type=F32)
                            for p in range(x.shape[1] // LANES)], axis=-1)


def _blockdiag_dot(x, w_ref):
    outs = []
    for p in range(x.shape[1] // LANES):
        outs.append(jnp.dot(x[:, p * LANES:(p + 1) * LANES].astype(BF16), w_ref[p],
                            preferred_element_type=F32))
    return jnp.concatenate(outs, axis=-1)


def _lru_gates(xc, wra_ref, bra_ref, wix_ref, bix_ref, lam_ref):
    r = _sigmoid(_blockdiag_dot(xc, wra_ref) + bra_ref[...])
    i = _sigmoid(_blockdiag_dot(xc, wix_ref) + bix_ref[...])
    log_a = -LRU_C * r * _softplus(-lam_ref[...])
    a = jnp.exp(log_a)
    mult = jnp.sqrt(-jnp.tanh(log_a) * (1.0 + a * a))
    return a, mult * (i * xc)


def _rwkv_pre(p, prev, mu, wwa_ref, w0, a0, g2_ref, k_k, k_a, ones):
    px = p + (prev - p) * mu
    r = px[:, 0:D_RWKV]
    k = px[:, D_RWKV:2 * D_RWKV]
    v = px[:, 2 * D_RWKV:3 * D_RWKV]
    lo = px[:, 3 * D_RWKV:3 * D_RWKV + R_DECAY + R_AAA]
    g_lo = px[:, 3 * D_RWKV + R_DECAY + R_AAA:]
    lane = lax.broadcasted_iota(jnp.int32, lo.shape, 1)
    wa = _dot(jnp.where(lane < R_DECAY, jnp.tanh(lo), lo), wwa_ref[...])
    logw = -math.exp(-0.5) * _sigmoid(w0 + wa[:, :D_RWKV])
    a = _sigmoid(a0 + wa[:, D_RWKV:])
    g = _dot(_sigmoid(g_lo), g2_ref[...])
    kk = k * k_k
    kk = kk * lax.rsqrt(jnp.maximum(_segsum(kk * kk, ones), 1e-24))
    k2 = k * (1.0 + (a - 1.0) * k_a)
    return r, logw, k2, v, kk, a, g


def _rwkv_post(y, r, k2, v, g, r_k, ln_g, ln_b, ones):
    inv = 1.0 / RWKV_HEAD
    mean = _segsum(y, ones) * inv
    yc = y - mean
    var = _segsum(yc * yc, ones) * inv
    yn = yc * lax.rsqrt(var + GN_EPS) * ln_g + ln_b
    bonus = _segsum(r * k2 * r_k, ones) * v
    return (yn + bonus) * g


def _pool_project(d, wp_ref, scale):
    return _blockdiag_dot(d, wp_ref) * scale


def _gmlp_uz(z2, ln_g, ln_b):
    u = _gelu(z2[:, :D_GMLP])
    v = _gelu(z2[:, D_GMLP:])
    mu = jnp.mean(v, axis=-1, keepdims=True)
    vc = v - mu
    var = jnp.mean(vc * vc, axis=-1, keepdims=True)
    return u, vc * lax.rsqrt(var + 1e-5) * ln_g + ln_b


def _p_lru_body(x_ref, g_ref, w_ref, cw_ref, cb_ref, wra_ref, bra_ref, wix_ref, bix_ref, lam_ref,
                y_ref, convo_ref, ho_ref, hist_ref, h_ref, sa_ref, sb_ref, carry_ref):
    @pl.when(pl.program_id(1) == 0)
    def _():
        hist_ref[...] = jnp.zeros_like(hist_ref)
        h_ref[...] = jnp.zeros_like(h_ref)

    z = _dot(_rms(x_ref[...], g_ref[...]), w_ref[...])
    xa = z[:, :D_LRU]
    ga = z[:, D_LRU:]
    tm = xa.shape[0]
    hist = hist_ref[...]
    cw = cw_ref[...]
    xc = cb_ref[...] + cw[3:4] * xa
    for s in (1, 2, 3):
        xc = xc + cw[3 - s:4 - s] * _shift_rows(hist, xa, s)
    a, b = _lru_gates(xc, wra_ref, bra_ref, wix_ref, bix_ref, lam_ref)
    groups = (tm // SUBLANES, SUBLANES, D_LRU)
    a3 = a.reshape(groups)
    b3 = b.reshape(groups)
    sub = lax.broadcasted_iota(jnp.int32, groups, 1)
    s = 1
    while s < SUBLANES:
        m = sub >= s
        b3 = jnp.where(m, a3 * pltpu.roll(b3, s, 1) + b3, b3)
        a3 = jnp.where(m, a3 * pltpu.roll(a3, s, 1), a3)
        s *= 2
    a = a3.reshape(tm, D_LRU)
    b = b3.reshape(tm, D_LRU)
    sa_ref[...] = a
    sb_ref[...] = b

    def carry_group(gi, hprev):
        r0 = pl.multiple_of(gi * SUBLANES, SUBLANES)
        carry_ref[pl.ds(r0, SUBLANES), :] = jnp.broadcast_to(hprev, (SUBLANES, D_LRU))
        last = pl.ds(r0 + SUBLANES - 1, 1)
        return sa_ref[last, :] * hprev + sb_ref[last, :]

    lax.fori_loop(0, tm // SUBLANES, carry_group, h_ref[SUBLANES - 1:SUBLANES, :], unroll=8)
    h = a * carry_ref[...] + b
    y_ref[...] = (h * _gelu(ga)).astype(y_ref.dtype)
    hist_ref[...] = xa[tm - SUBLANES:]
    h_ref[...] = h[tm - SUBLANES:]
    convo_ref[...] = xa[tm - SUBLANES:]
    ho_ref[...] = h[tm - SUBLANES:]


def _pair_blockdiag_rows(x):
    lo = lax.broadcasted_iota(jnp.int32, x.shape, 1) < RWKV_HEAD
    z = jnp.zeros_like(x)
    return jnp.concatenate([jnp.where(lo, x, z), jnp.where(lo, z, x)], axis=0)


def _p_rwkv_body(x_ref, g_ref, w_ref, mu_ref, wwa_ref, w0_ref, a0_ref, g2_ref, kk_ref, ka_ref,
                 rk_ref, lng_ref, lnb_ref,
                 y_ref, shifto_ref, so_ref,
                 hist_ref, st_ref):
    @pl.when(pl.program_id(1) == 0)
    def _():
        hist_ref[...] = jnp.zeros_like(hist_ref)
        st_ref[...] = jnp.zeros_like(st_ref)

    C = RWKV_CHUNK
    ones = _head_ones()
    p = _dot(_rms(x_ref[...], g_ref[...]), w_ref[...])
    tm = p.shape[0]
    nc = tm // C
    npair = D_RWKV // LANES
    prev = _shift_rows(hist_ref[...], p, 1)
    r, logw, k2, v, kk, a, g = _rwkv_pre(p, prev, mu_ref[...], wwa_ref, w0_ref[...], a0_ref[...],
                                          g2_ref, kk_ref[...], ka_ref[...], ones)
    hist_ref[...] = p[tm - SUBLANES:]
    shifto_ref[...] = p[tm - SUBLANES:]
    rowc = lax.broadcasted_iota(jnp.int32, logw.shape, 0) % C
    lg = logw
    s = 1
    while s < C:
        lg = lg + jnp.where(rowc >= s, pltpu.roll(lg, s, 0), 0.0)
        s *= 2
    lg_last = [lg[(c + 1) * C - 1:(c + 1) * C, :] for c in range(nc)]
    lg_end = jnp.concatenate([jnp.broadcast_to(x, (C, D_RWKV)) for x in lg_last], axis=0)
    inv_gam = jnp.exp(-lg)
    to_end = jnp.exp(lg_end - lg)
    b = kk * a
    rt_f = r * jnp.exp(lg)
    at_b, rt_b, bt_b, kt_b, v_b, bh_b, kh_b = (x.astype(BF16) for x in (
        -kk * jnp.exp(lg - logw), rt_f, b * inv_gam, k2 * inv_gam, v, b * to_end, k2 * to_end))

    r2 = lax.broadcasted_iota(jnp.int32, (C, LANES), 0)
    c2 = lax.broadcasted_iota(jnp.int32, (C, LANES), 1) % C
    strict2 = c2 < r2
    incl2 = c2 <= r2
    eye2 = jnp.where(r2 == c2, 1.0, 0.0).astype(F32)
    pair2 = (r2 // 2) == (c2 // 2)
    levels = []
    nb = 2
    while nb < C:
        levels.append(((r2 // (2 * nb)) == (c2 // (2 * nb))) & ((r2 // nb) != (c2 // nb)))
        nb *= 2
    rr = lax.broadcasted_iota(jnp.int32, (LANES, LANES), 0)
    cc = lax.broadcasted_iota(jnp.int32, (LANES, LANES), 1)
    same_head = (rr // RWKV_HEAD) == (cc // RWKV_HEAD)
    eye128 = jnp.where(rr == cc, 1.0, 0.0).astype(F32)
    tn = (((0,), (0,)), ((), ()))
    nt_ = (((1,), (1,)), ((), ()))
    bd = _pair_blockdiag_rows

    def mm(x, y):
        return jnp.dot(x, y, preferred_element_type=F32)

    def blk(x, c, q):
        return x[c * C:(c + 1) * C, q * LANES:(q + 1) * LANES]

    P = [(c, q) for c in range(nc) for q in range(npair)]
    N = range(len(P))
    at, rt, bt, kt, vv, bh, kh = ([blk(x, c, q) for c, q in P] for x in
                                  (at_b, rt_b, bt_b, kt_b, v_b, bh_b, kh_b))
    ar = [jnp.concatenate([at[n], rt[n]], axis=0) for n in N]
    pb = [lax.dot_general(ar[n], bd(bt[n]), nt_, preferred_element_type=F32) for n in N]
    pk = [lax.dot_general(ar[n], bd(kt[n]), nt_, preferred_element_type=F32) for n in N]
    lab = [jnp.where(strict2, x[:C], 0.0) for x in pb]
    mrb = [jnp.where(incl2, x[C:], 0.0).astype(BF16) for x in pb]
    lak = [jnp.where(strict2, x[:C], 0.0).astype(BF16) for x in pk]
    mrk = [jnp.where(incl2, x[C:], 0.0).astype(BF16) for x in pk]
    v_bd = [bd(x) for x in vv]
    lak_v = [mm(lak[n], v_bd[n]) for n in N]
    T = [eye2 + jnp.where(pair2, x, 0.0) for x in lab]
    for mk in levels:
        tb = [x.astype(BF16) for x in T]
        u = [mm(tb[n], bd(jnp.where(mk, lab[n], 0.0).astype(BF16))).astype(BF16) for n in N]
        T = [T[n] + mm(u[n], bd(tb[n])) for n in N]
    tb = [x.astype(BF16) for x in T]
    tab = [mm(tb[n], bd(at[n])).astype(BF16) for n in N]
    cv = [mm(tb[n], bd(lak_v[n].astype(BF16))).astype(BF16) for n in N]
    g_bd = [jnp.where(same_head, lax.dot_general(bh[n], tab[n], tn, preferred_element_type=F32), 0.0)
            .astype(BF16) for n in N]
    q_bd = [jnp.where(same_head, lax.dot_general(jnp.concatenate([bh[n], kh[n]], axis=0),
                                                 jnp.concatenate([cv[n], vv[n]], axis=0), tn,
                                                 preferred_element_type=F32), 0.0) for n in N]
    ry = [(blk(rt_f, c, q) + mm(mrb[n], bd(tab[n]))).astype(BF16) for n, (c, q) in enumerate(P)]
    yc = [mm(jnp.concatenate([mrb[n], mrk[n]], axis=1),
             jnp.concatenate([bd(cv[n]), v_bd[n]], axis=0)) for n in N]
    gcol = [jnp.broadcast_to(jnp.sum(eye128 * jnp.exp(lg_last[c][:, q * LANES:(q + 1) * LANES]),
                                     axis=1, keepdims=True), (LANES, LANES)) for c, q in P]

    st = [st_ref[q] for q in range(npair)]
    y_rows = []
    for c in range(nc):
        sb = [x.astype(BF16) for x in st]
        ns = [c * npair + q for q in range(npair)]
        gs = [mm(g_bd[n], sb[q]) for q, n in enumerate(ns)]
        ys = [mm(ry[n], sb[q]) for q, n in enumerate(ns)]
        st = [st[q] * gcol[n] + gs[q] + q_bd[n] for q, n in enumerate(ns)]
        y_rows.append(jnp.concatenate([ys[q] + yc[n] for q, n in enumerate(ns)], axis=1))
    y = jnp.concatenate(y_rows, axis=0)
    yb = _rwkv_post(y, r, k2, v, g, rk_ref[...], lng_ref[...], lnb_ref[...], ones)
    y_ref[...] = yb.astype(y_ref.dtype)
    for q in range(npair):
        st_ref[q] = st[q]
        so_ref[2 * q] = st[q][:RWKV_HEAD, :RWKV_HEAD].T
        so_ref[2 * q + 1] = st[q][RWKV_HEAD:, RWKV_HEAD:].T


def _p_pool_body(x_ref, g_ref, w_ref, wp_ref, sc_ref, y_ref, poolo_ref, hist_ref):
    t = pl.program_id(1)

    @pl.when(t == 0)
    def _():
        hist_ref[...] = jnp.zeros_like(hist_ref)

    pc = _dot(_rms(x_ref[...], g_ref[...]), w_ref[...])
    tm = pc.shape[0]
    X = jnp.concatenate([hist_ref[...], pc], axis=0)
    pos = t * tm + lax.broadcasted_iota(jnp.int32, (tm, POOL_GW), 0)
    ds = []
    for gi, win in enumerate(POOL_WINDOWS):
        sl = slice(gi * POOL_GW, (gi + 1) * POOL_GW)
        s = X[:, sl]
        sh = 1
        while sh < win:
            s = s + pltpu.roll(s, sh, 0)
            sh *= 2
        cnt = jnp.minimum(pos + 1, win).astype(F32)
        ds.append(s[POOL_HIST:] / cnt - pc[:, sl])
    y = _pool_project(jnp.concatenate(ds, axis=-1), wp_ref, sc_ref[...])
    y_ref[...] = y.astype(y_ref.dtype)
    hist_ref[...] = pc[tm - POOL_HIST:]
    poolo_ref[...] = pc[tm - POOL_HIST:]


def _p_gmlp_body(x_ref, g_ref, w_ref, lng_ref, lnb_ref, ws_ref, bsb_ref, y_ref):
    z2 = _dot(_rms(x_ref[...], g_ref[...]), w_ref[...])
    tm = z2.shape[0]
    u, zv = _gmlp_uz(z2, lng_ref[...], lnb_ref[...])
    tril = (lax.broadcasted_iota(jnp.int32, (CHUNK, CHUNK), 0)
            >= lax.broadcasted_iota(jnp.int32, (CHUNK, CHUNK), 1))
    ws = [jnp.where(tril, ws_ref[gi], 0.0).astype(BF16) for gi in range(GMLP_GROUPS)]
    zb = zv.astype(BF16)
    rows = []
    for c in range(tm // CHUNK):
        cols = [jnp.dot(ws[gi], zb[c * CHUNK:(c + 1) * CHUNK, gi * LANES:(gi + 1) * LANES],
                        preferred_element_type=F32) for gi in range(GMLP_GROUPS)]
        rows.append(jnp.concatenate(cols, axis=-1) + bsb_ref[...])
    s = jnp.concatenate(rows, axis=0)
    y_ref[...] = (u * s).astype(y_ref.dtype)


def _merge_body(x_ref, g_ref, wg_ref, ya_ref, yb_ref, yc_ref, yd_ref, wp_ref, wo_ref, o_ref):
    x = x_ref[...]
    xn = _rms(x, g_ref[...]).astype(BF16)
    merged = None
    for i, y_ref in enumerate((ya_ref, yb_ref, yc_ref, yd_ref)):
        gate = _sigmoid(jnp.dot(xn, wg_ref[:, i * D_MODEL:(i + 1) * D_MODEL],
                                preferred_element_type=F32))
        term = gate * jnp.dot(y_ref[...], wp_ref[i], preferred_element_type=F32)
        merged = term if merged is None else merged + term
    o_ref[...] = x + _dot(merged, wo_ref[...])


def _softmax_rows(s):
    e = jnp.exp(s - jnp.max(s, axis=-1, keepdims=True))
    return e / jnp.sum(e, axis=-1, keepdims=True)


def _kv_rows(half, h):
    return pl.ds(half * X_HEADS + h, N_MEM, stride=SUBLANES)


def _p_xattn_body(x_ref, g_ref, wq_ref, k_ref, v_ref, wo_ref, o_ref):
    x = x_ref[...]
    q = _dot(_rms(x, g_ref[...]), wq_ref[...]).astype(BF16)
    outs = []
    for h in range(X_HEADS):
        sl = slice(h * X_HEAD_DIM, (h + 1) * X_HEAD_DIM)
        s = lax.dot_general(q[:, sl], k_ref[:, sl], (((1,), (1,)), ((), ())),
                            preferred_element_type=F32) * (X_HEAD_DIM ** -0.5)
        outs.append(jnp.dot(_softmax_rows(s).astype(BF16), v_ref[:, sl], preferred_element_type=F32))
    o_ref[...] = x + _dot(jnp.concatenate(outs, axis=-1), wo_ref[...])


def _ffn_conv(hist, h, cw, cb):
    return cb + cw[2:3] * h + cw[1:2] * _shift_rows(hist, h, 1) + cw[0:1] * _shift_rows(hist, h, 2)


def _p_ffn_body(x_ref, g_ref, wug_ref, wuv_ref, cwg_ref, cwv_ref, cbg_ref, cbv_ref, wd_ref,
                o_ref, ffo_ref, hist_ref):
    @pl.when(pl.program_id(1) == 0)
    def _():
        hist_ref[...] = jnp.zeros_like(hist_ref)

    x = x_ref[...]
    tm = x.shape[0]
    xn = _rms(x, g_ref[...]).astype(BF16)
    acc = x
    for j in range(D_FF // FF_CW):
        cs = slice(j * FF_CW, (j + 1) * FF_CW)
        vs = slice(D_FF + j * FF_CW, D_FF + (j + 1) * FF_CW)
        hg = jnp.dot(xn, wug_ref[:, cs], preferred_element_type=F32)
        hv = jnp.dot(xn, wuv_ref[:, cs], preferred_element_type=F32)
        cg = _ffn_conv(hist_ref[:, cs], hg, cwg_ref[:, cs], cbg_ref[:, cs])
        cv = _ffn_conv(hist_ref[:, vs], hv, cwv_ref[:, cs], cbv_ref[:, cs])
        acc = acc + _dot(_gelu(cg) * cv, wd_ref[cs, :])
        hist_ref[:, cs] = hg[tm - SUBLANES:]
        hist_ref[:, vs] = hv[tm - SUBLANES:]
    o_ref[...] = acc
    ffo_ref[...] = hist_ref[...]


def _memkv_body(m_ref, g_ref, wk_ref, wv_ref, k_ref, v_ref, kb_ref, vb_ref):
    mn = _rms(m_ref[...], g_ref[...]).astype(BF16)
    for w_ref, o_ref, b_ref in ((wk_ref, k_ref, kb_ref), (wv_ref, v_ref, vb_ref)):
        kv = jnp.dot(mn, w_ref[...], preferred_element_type=F32)
        b_ref[...] = kv.astype(BF16)
        for h in range(X_HEADS):
            for half in range(X_HEAD_DIM // LANES):
                lo = h * X_HEAD_DIM + half * LANES
                o_ref[_kv_rows(half, h), :] = kv[:, lo:lo + LANES]


def _final_norm_body(x_ref, g_ref, o_ref):
    o_ref[...] = _rms(x_ref[...], g_ref[...])


def _s_mix_body(x_ref, g_ref, wa_ref, wb_ref, wc_ref, wd_ref,
                lconv_ref, lh_ref, cw_ref, cb_ref, wra_ref, bra_ref, wix_ref, bix_ref, lam_ref,
                shift_ref, mu_ref, wwa_ref, w0_ref, a0_ref, g2_ref, kk_ref, ka_ref,
                pool_ref, wp_ref, psc_ref,
                lng_ref, lnb_ref, wsd_ref, bsr_ref,
                ya_ref, xa_ref, h_ref, yc_ref, pc_ref, yd_ref, zv_ref, p_ref,
                rt_ref, wt_ref, kt_ref, vt_ref, kkt_ref, at_ref, gt_ref):
    xn = _rms(x_ref[...], g_ref[...]).astype(BF16)
    z = jnp.dot(xn, wa_ref[...], preferred_element_type=F32)
    xa = z[:, :D_LRU]
    cw = cw_ref[...]
    xc = cb_ref[...] + cw[0:1] * lconv_ref[0] + cw[1:2] * lconv_ref[1] + cw[2:3] * lconv_ref[2] + cw[3:4] * xa
    a, b = _lru_gates(xc, wra_ref, bra_ref, wix_ref, bix_ref, lam_ref)
    h = a * lh_ref[...] + b
    ya_ref[...] = (h * _gelu(z[:, D_LRU:])).astype(ya_ref.dtype)
    xa_ref[...] = xa
    h_ref[...] = h
    ones = _head_ones()
    p = jnp.dot(xn, wb_ref[...], preferred_element_type=F32)
    r, logw, k2, v, kk, aa, g = _rwkv_pre(p, shift_ref[...], mu_ref[...], wwa_ref, w0_ref[...],
                                           a0_ref[...], g2_ref, kk_ref[...], ka_ref[...], ones)
    p_ref[...] = p
    for ref, val in ((rt_ref, r), (wt_ref, jnp.exp(logw)), (kt_ref, k2), (vt_ref, v), (kkt_ref, kk),
                     (at_ref, aa), (gt_ref, g)):
        ref[...] = val.T
    pc = jnp.dot(xn, wc_ref[...], preferred_element_type=F32)
    ds = []
    for gi, win in enumerate(POOL_WINDOWS):
        sl = slice(gi * POOL_GW, (gi + 1) * POOL_GW)
        s = pc[:, sl]
        for j in range(POOL_BUF - (win - 1), POOL_BUF):
            s = s + pool_ref[j][:, sl]
        ds.append(s / float(min(PAST_LEN + 1, win)) - pc[:, sl])
    yc_ref[...] = _pool_project(jnp.concatenate(ds, axis=-1), wp_ref, psc_ref[...]).astype(yc_ref.dtype)
    pc_ref[...] = pc
    u, zv = _gmlp_uz(jnp.dot(xn, wd_ref[...], preferred_element_type=F32), lng_ref[...], lnb_ref[...])
    yd_ref[...] = (u * (wsd_ref[...] * zv + bsr_ref[...])).astype(yd_ref.dtype)
    zv_ref[...] = zv


def _s_rwkv_body(s_ref, r_ref, w_ref, k_ref, v_ref, kk_ref, a_ref, g_ref, rk_ref, lng_ref, lnb_ref,
                 all_layers_ref, y_ref, so_ref, yt_sc):
    del all_layers_ref
    h = pl.program_id(0)
    kk = kk_ref[...]
    w = w_ref[...]
    k = k_ref[...]
    r = r_ref[...]
    bk = kk * a_ref[...]

    def body(i, carry):
        si = s_ref[i]
        sa = -jnp.sum(si * kk, axis=0, keepdims=True)
        s2 = si * w + sa * bk + v_ref[pl.ds(i, 1), :] * k
        so_ref[i] = s2
        yt_sc[pl.ds(h * RWKV_HEAD + i, 1), :] = jnp.sum(s2 * r, axis=0, keepdims=True)
        return carry

    lax.fori_loop(0, RWKV_HEAD, body, 0, unroll=4)
    rows = pl.ds(pl.multiple_of(h * RWKV_HEAD, RWKV_HEAD), RWKV_HEAD)
    y = yt_sc[rows, :]
    yc = y - jnp.mean(y, axis=0, keepdims=True)
    var = jnp.mean(yc * yc, axis=0, keepdims=True)
    yn = yc * lax.rsqrt(var + GN_EPS) * lng_ref[...] + lnb_ref[...]
    bonus = jnp.sum(r * k * rk_ref[...], axis=0, keepdims=True) * v_ref[...]
    yt_sc[rows, :] = (yn + bonus) * g_ref[...]

    @pl.when(h == pl.num_programs(0) - 1)
    def _():
        y_ref[...] = yt_sc[...].T.astype(y_ref.dtype)


def _s_xattn_body(x_ref, g_ref, wq_ref, k_ref, v_ref, wo_ref, o_ref, q_sc, a_sc):
    i = pl.program_id(0)
    bb = k_ref.shape[0]

    @pl.when(i == 0)
    def _():
        q_sc[...] = _dot(_rms(x_ref[...], g_ref[...]), wq_ref[...])

    halves = X_HEAD_DIM // LANES

    def q_lanes(c, h):
        return slice(h * X_HEAD_DIM + c * LANES, h * X_HEAD_DIM + (c + 1) * LANES)

    for j in range(bb):
        row = pl.ds(i * bb + j, 1)
        qrow = q_sc[row, :]
        q8 = jnp.concatenate([qrow[:, q_lanes(c, h)] for c in range(halves) for h in range(X_HEADS)], axis=0)
        k3 = k_ref[j].reshape(N_MEM, SUBLANES, LANES)
        part = jnp.sum(k3 * q8[None], axis=-1, keepdims=True)
        part = jnp.broadcast_to(part, k3.shape)
        s = (part + pltpu.roll(part, X_HEADS, 1)) * (X_HEAD_DIM ** -0.5)
        e = jnp.exp(s - jnp.max(s, axis=0, keepdims=True))
        pr = e / jnp.sum(e, axis=0, keepdims=True)
        o8 = jnp.sum(pr * v_ref[j].reshape(N_MEM, SUBLANES, LANES), axis=0)
        a_sc[row, :] = jnp.concatenate([o8[c * X_HEADS + h:c * X_HEADS + h + 1, :]
                                        for h in range(X_HEADS) for c in range(halves)], axis=1)

    @pl.when(i == pl.num_programs(0) - 1)
    def _():
        o_ref[...] = x_ref[...] + _dot(a_sc[...], wo_ref[...])


def _s_ffn_body(x_ref, g_ref, wug_ref, wuv_ref, sg_ref, sv_ref, cwg_ref, cwv_ref, cbg_ref, cbv_ref,
                wd_ref, o_ref, hg_ref, hv_ref, acc_ref):
    j = pl.program_id(0)

    @pl.when(j == 0)
    def _():
        acc_ref[...] = x_ref[...]

    xn = _rms(x_ref[...], g_ref[...]).astype(BF16)
    hg = jnp.dot(xn, wug_ref[...], preferred_element_type=F32)
    hv = jnp.dot(xn, wuv_ref[...], preferred_element_type=F32)
    cwg = cwg_ref[...]
    cwv = cwv_ref[...]
    cg = cbg_ref[...] + cwg[0:1] * sg_ref[0] + cwg[1:2] * sg_ref[1] + cwg[2:3] * hg
    cv = cbv_ref[...] + cwv[0:1] * sv_ref[0] + cwv[1:2] * sv_ref[1] + cwv[2:3] * hv
    acc_ref[...] += _dot(_gelu(cg) * cv, wd_ref[...])
    hg_ref[...] = hg
    hv_ref[...] = hv

    @pl.when(j == pl.num_programs(0) - 1)
    def _():
        o_ref[...] = acc_ref[...]


def _params(n_grid, vmem_mb):
    return pltpu.CompilerParams(dimension_semantics=("arbitrary",) * n_grid,
                                vmem_limit_bytes=vmem_mb << 20)


def _whole(shape):
    return pl.BlockSpec(tuple(shape), lambda *_: (0,) * len(shape))


def _layer(arr, l):
    shape = arr.shape[1:]
    return pl.BlockSpec((None,) + tuple(shape), lambda *_: (l,) + (0,) * len(shape))


def _sds(shape, dtype=F32):
    return jax.ShapeDtypeStruct(tuple(shape), dtype)


def _pair_blockdiag(w):
    L = w.shape[0]
    w = w.reshape(L, 4, 2, RWKV_HEAD, RWKV_HEAD)
    z = jnp.zeros_like(w[:, :, 0])
    top = jnp.concatenate([w[:, :, 0], z], axis=-1)
    bot = jnp.concatenate([z, w[:, :, 1]], axis=-1)
    return jnp.concatenate([top, bot], axis=-2)


def _vec(a):
    return a.reshape(a.shape[0], 1, -1)


def kernel(x_prompt, x_sample, state_lru_conv, state_lru_h, state_rwkv_shift, state_rwkv_S, state_pool, state_ffn_conv, cache_mem_k, cache_mem_v, mem_prompt, g_mix, w_in, lru_conv_w, lru_conv_b, lru_w_ra, lru_b_ra, lru_w_ix, lru_b_ix, lru_lambda, rwkv_mu, rwkv_w0, rwkv_w2, rwkv_a0, rwkv_a2, rwkv_g2, rwkv_k_k, rwkv_k_a, rwkv_r_k, rwkv_ln_g, rwkv_ln_b, pool_w, pool_scale, gmlp_ln_g, gmlp_ln_b, gmlp_w_s, gmlp_b_s, w_pa, w_pb, w_pc, w_pd, w_o, g_xattn, g_mem, w_xq, w_xk, w_xv, w_xo, g_ffn, w_up, ffn_conv_w, ffn_conv_b, w_down, g_final):
    B, T, D = x_prompt.shape
    SB = x_sample.shape[0]
    assert D == D_MODEL and w_in.shape == (DEPTH, D_MODEL, D_IN) and x_sample.shape[1] == 1
    assert T % TM_MIX == 0 and T % TM_RWKV == 0 and TM_RWKV % RWKV_CHUNK == 0 and SB % S_ATT_BB == 0
    assert SB == LANES and X_HEADS * (X_HEAD_DIM // LANES) == SUBLANES

    w_in_b = jnp.concatenate(
        [w_in[:, :, O_G:], w_in[:, :, O_A:O_B], w_in[:, :, O_D:O_G], w_in[:, :, O_C:O_D],
         jnp.zeros((DEPTH, D_MODEL, P_B - (P_C + D_POOL)), w_in.dtype), w_in[:, :, O_B:O_C]], axis=-1).astype(BF16)
    wra = _pair_blockdiag(lru_w_ra).astype(BF16)
    wix = _pair_blockdiag(lru_w_ix).astype(BF16)
    zer = jnp.zeros((DEPTH, R_DECAY, D_RWKV), F32)
    wwa = jnp.concatenate([jnp.concatenate([rwkv_w2, zer], axis=-1),
                           jnp.concatenate([zer, rwkv_a2], axis=-1)], axis=1).astype(BF16)
    g2 = rwkv_g2.astype(BF16)
    wpool = pool_w.astype(BF16)
    bsb = jnp.repeat(gmlp_b_s, LANES, axis=-1)
    off = PAST_LEN % CHUNK
    wsd = jnp.repeat(gmlp_w_s[:, :, off, off], LANES, axis=-1)[:, None, :]
    bsr = jnp.repeat(gmlp_b_s[:, off, :], LANES, axis=-1)[:, None, :]
    wP = jnp.stack([w_pa, w_pb, w_pc, w_pd], axis=1).astype(BF16)
    wO = w_o.astype(BF16)
    wXq, wXk, wXv, wXo = (w.astype(BF16) for w in (w_xq, w_xk, w_xv, w_xo))
    w_up_b = w_up.astype(BF16)
    wDn = w_down.astype(BF16)
    cwg, cwv = ffn_conv_w[:, :, :D_FF], ffn_conv_w[:, :, D_FF:]
    cbg, cbv = _vec(ffn_conv_b[:, :D_FF]), _vec(ffn_conv_b[:, D_FF:])
    gmix, gxat, gffn, gmem = _vec(g_mix), _vec(g_xattn), _vec(g_ffn), _vec(g_mem)
    lcb, bra, bix, lam = _vec(lru_conv_b), _vec(lru_b_ra), _vec(lru_b_ix), _vec(lru_lambda)
    mu, w0, a0, kkw, kaw = _vec(rwkv_mu), _vec(rwkv_w0), _vec(rwkv_a0), _vec(rwkv_k_k), _vec(rwkv_k_a)
    rk = rwkv_r_k.reshape(DEPTH, 1, D_RWKV)
    rlg, rlb = _vec(rwkv_ln_g), _vec(rwkv_ln_b)
    psc, glg, glb = _vec(pool_scale), _vec(gmlp_ln_g), _vec(gmlp_ln_b)
    gfin = g_final.reshape(1, D_MODEL)
    rk_col, rlg_col, rlb_col = (a.reshape(DEPTH, D_RWKV, 1) for a in (rwkv_r_k, rwkv_ln_g, rwkv_ln_b))
    state_S5 = jnp.transpose(state_rwkv_S, (0, 2, 3, 4, 1))
    s_S5 = jnp.zeros(state_S5.shape, F32)

    def cache_rows(c):
        c = c.reshape(DEPTH, SB, N_MEM, X_HEADS, X_HEAD_DIM // LANES, LANES)
        return jnp.swapaxes(c, 3, 4).reshape(DEPTH, SB, N_MEM * SUBLANES, LANES)

    cache_k, cache_v = cache_rows(cache_mem_k), cache_rows(cache_mem_v)

    kv_shape = _sds((DEPTH, B, N_MEM * SUBLANES, LANES))
    kv_spec = pl.BlockSpec((None, None, N_MEM * SUBLANES, LANES), lambda l, b: (l, b, 0, 0))
    wl_spec = pl.BlockSpec((None, D_MODEL, D_MODEL), lambda l, b: (l, 0, 0))
    kvrow_shape = _sds((DEPTH, B, N_MEM, D_MODEL), BF16)
    kvrow_spec = pl.BlockSpec((None, None, N_MEM, D_MODEL), lambda l, b: (l, b, 0, 0))
    p_mk, p_mv, p_mk_b, p_mv_b = pl.pallas_call(
        _memkv_body, grid=(DEPTH, B),
        in_specs=[pl.BlockSpec((None, N_MEM, D_MODEL), lambda l, b: (b, 0, 0)),
                  pl.BlockSpec((None, 1, D_MODEL), lambda l, b: (l, 0, 0)), wl_spec, wl_spec],
        out_specs=[kv_spec, kv_spec, kvrow_spec, kvrow_spec],
        out_shape=[kv_shape, kv_shape, kvrow_shape, kvrow_shape],
        compiler_params=_params(2, 32), name="memkv")(mem_prompt, gmem, wXk, wXv)

    nt = T // TM_MIX
    ntr = T // TM_RWKV
    xt_spec = pl.BlockSpec((None, TM_MIX, D_MODEL), lambda b, t: (b, t, 0))
    yt_spec = pl.BlockSpec((None, TM_MIX, D_LRU), lambda b, t: (b, t, 0))

    def tail_spec(rows, width):
        return pl.BlockSpec((None, rows, width), lambda b, t: (b, 0, 0))

    hp = x_prompt
    hs = x_sample.reshape(SB, D_MODEL)
    outs = {k: [] for k in ("p_conv", "s_conv", "p_h", "s_h", "p_shift", "s_shift", "p_S",
                            "p_pool", "s_pool", "p_ffn", "s_ffn", "s_cv")}

    for l in range(DEPTH):
        L = functools.partial(_layer, l=l)

        def win(start, width):
            assert start % width == 0
            return pl.BlockSpec((None, D_MODEL, width), lambda *_: (l, 0, start // width))

        win_a, win_b = win(P_A, O_B - O_A), win(P_B, O_C - O_B)
        win_c, win_d, win_g = win(P_C, O_D - O_C), win(P_D, O_G - O_D), win(P_G, D_IN - O_G)

        wup_g = pl.BlockSpec((None, D_MODEL, D_FF), lambda *_: (l, 0, 0))
        wup_v = pl.BlockSpec((None, D_MODEL, D_FF), lambda *_: (l, 0, 1))

        yA, p_conv8, p_h8 = pl.pallas_call(
            _p_lru_body, grid=(B, nt),
            in_specs=[xt_spec, L(gmix), win_a, L(lru_conv_w), L(lcb), L(wra), L(bra), L(wix), L(bix), L(lam)],
            out_specs=[yt_spec, tail_spec(SUBLANES, D_LRU), tail_spec(SUBLANES, D_LRU)],
            out_shape=[_sds((B, T, D_LRU), BF16), _sds((B, SUBLANES, D_LRU)), _sds((B, SUBLANES, D_LRU))],
            scratch_shapes=[pltpu.VMEM((SUBLANES, D_LRU), F32)] * 2 + [pltpu.VMEM((TM_MIX, D_LRU), F32)] * 3,
            compiler_params=_params(2, 48), name="p_lru")(
                hp, gmix, w_in_b, lru_conv_w, lcb, wra, bra, wix, bix, lam)

        yB, p_shift8, p_S = pl.pallas_call(
            _p_rwkv_body, grid=(B, ntr),
            in_specs=[pl.BlockSpec((None, TM_RWKV, D_MODEL), lambda b, t: (b, t, 0)),
                      L(gmix), win_b, L(mu), L(wwa), L(w0), L(a0), L(g2), L(kkw), L(kaw), L(rk), L(rlg), L(rlb)],
            out_specs=[pl.BlockSpec((None, TM_RWKV, D_RWKV), lambda b, t: (b, t, 0)),
                       tail_spec(SUBLANES, D_RWKV_IN),
                       pl.BlockSpec((None, RWKV_HEADS, RWKV_HEAD, RWKV_HEAD), lambda b, t: (b, 0, 0, 0))],
            out_shape=[_sds((B, T, D_RWKV), BF16), _sds((B, SUBLANES, D_RWKV_IN)),
                       _sds((B, RWKV_HEADS, RWKV_HEAD, RWKV_HEAD))],
            scratch_shapes=[pltpu.VMEM((SUBLANES, D_RWKV_IN), F32),
                            pltpu.VMEM((D_RWKV // LANES, LANES, LANES), F32)],
            compiler_params=_params(2, 48), name="p_rwkv")(
                hp, gmix, w_in_b, mu, wwa, w0, a0, g2, kkw, kaw, rk, rlg, rlb)

        yC, p_pool16 = pl.pallas_call(
            _p_pool_body, grid=(B, nt),
            in_specs=[xt_spec, L(gmix), win_c, L(wpool), L(psc)],
            out_specs=[yt_spec, tail_spec(POOL_HIST, D_POOL)],
            out_shape=[_sds((B, T, D_POOL), BF16), _sds((B, POOL_HIST, D_POOL))],
            scratch_shapes=[pltpu.VMEM((POOL_HIST, D_POOL), F32)],
            compiler_params=_params(2, 48), name="p_pool")(hp, gmix, w_in_b, wpool, psc)

        yD = pl.pallas_call(
            _p_gmlp_body, grid=(B, nt),
            in_specs=[xt_spec, L(gmix), win_d, L(glg), L(glb), L(gmlp_w_s), L(bsb)],
            out_specs=yt_spec, out_shape=_sds((B, T, D_GMLP), BF16),
            compiler_params=_params(2, 48), name="p_gmlp")(hp, gmix, w_in_b, glg, glb, gmlp_w_s, bsb)

        def merge(x2d, ys, tm):
            rows = x2d.shape[0]
            xs = pl.BlockSpec((tm, D_MODEL), lambda i: (i, 0))
            ysp = pl.BlockSpec((tm, D_LRU), lambda i: (i, 0))
            return pl.pallas_call(
                _merge_body, grid=(rows // tm,),
                in_specs=[xs, L(gmix), win_g, ysp, ysp, ysp, ysp, L(wP), L(wO)],
                out_specs=xs, out_shape=_sds((rows, D_MODEL)),
                compiler_params=_params(1, 56), name="merge")(x2d, gmix, w_in_b, *ys, wP, wO)

        hp = merge(hp.reshape(B * T, D_MODEL), [y.reshape(B * T, -1) for y in (yA, yB, yC, yD)],
                   TM_MIX).reshape(B, T, D_MODEL)

        kvb_spec = pl.BlockSpec((None, None, N_MEM, D_MODEL), lambda b, t: (l, b, 0, 0))
        hp = pl.pallas_call(
            _p_xattn_body, grid=(B, nt),
            in_specs=[xt_spec, L(gxat), L(wXq), kvb_spec, kvb_spec, L(wXo)],
            out_specs=xt_spec, out_shape=_sds((B, T, D_MODEL)),
            compiler_params=_params(2, 48), name="p_xattn")(hp, gxat, wXq, p_mk_b, p_mv_b, wXo)

        hp, p_ffn8 = pl.pallas_call(
            _p_ffn_body, grid=(B, nt),
            in_specs=[xt_spec, L(gffn), wup_g, wup_v, L(cwg), L(cwv), L(cbg), L(cbv), L(wDn)],
            out_specs=[xt_spec, tail_spec(SUBLANES, 2 * D_FF)],
            out_shape=[_sds((B, T, D_MODEL)), _sds((B, SUBLANES, 2 * D_FF))],
            scratch_shapes=[pltpu.VMEM((SUBLANES, 2 * D_FF), F32)],
            compiler_params=_params(2, 56), name="p_ffn")(hp, gffn, w_up_b, w_up_b, cwg, cwv, cbg, cbv, wDn)

        outs["p_conv"].append(p_conv8[:, SUBLANES - 3:])
        outs["p_h"].append(p_h8[:, SUBLANES - 1])
        outs["p_shift"].append(p_shift8[:, SUBLANES - 1:])
        outs["p_S"].append(p_S)
        outs["p_pool"].append(p_pool16[:, POOL_HIST - POOL_BUF:])
        outs["p_ffn"].append(p_ffn8[:, SUBLANES - 2:])

        lconv = jnp.swapaxes(state_lru_conv[l], 0, 1)
        spool = jnp.swapaxes(state_pool[l], 0, 1)
        sffn = jnp.swapaxes(state_ffn_conv[l], 0, 1)
        shift = state_rwkv_shift[l].reshape(SB, D_RWKV_IN)
        row512 = _sds((SB, D_LRU))
        chan = _sds((D_RWKV, SB))
        mix_in = [hs, gmix[l], w_in_b, w_in_b, w_in_b, w_in_b,
                  lconv, state_lru_h[l], lru_conv_w[l], lcb[l], wra[l], bra[l], wix[l], bix[l], lam[l],
                  shift, mu[l], wwa[l], w0[l], a0[l], g2[l], kkw[l], kaw[l],
                  spool, wpool[l], psc[l],
                  glg[l], glb[l], wsd[l], bsr[l]]
        mix_out = [_sds((SB, D_LRU), BF16), row512, row512, _sds((SB, D_POOL), BF16), row512,
                   _sds((SB, D_GMLP), BF16), row512, _sds((SB, D_RWKV_IN)),
                   chan, chan, chan, chan, chan, chan, chan]
        (yA, s_xa, s_h, yC, s_pc, yD, s_zv, s_p, rT, wT, kT, vT, kkT, aT, gT) = pl.pallas_call(
            _s_mix_body, grid=(1,),
            in_specs=[_whole(mix_in[0].shape), _whole(mix_in[1].shape),
                      win_a, win_b, win_c, win_d]
                     + [_whole(a.shape) for a in mix_in[6:]],
            out_specs=[_whole(o.shape) for o in mix_out], out_shape=mix_out,
            compiler_params=_params(1, 56), name="s_mix")(*mix_in)

        hd_spec = pl.BlockSpec((RWKV_HEAD, SB), lambda h: (h, 0))
        col_spec = pl.BlockSpec((None, RWKV_HEAD, 1), lambda h: (l, h, 0))
        st_spec = pl.BlockSpec((None, None, RWKV_HEAD, RWKV_HEAD, SB), lambda h: (l, h, 0, 0, 0))
        yB, s_S5 = pl.pallas_call(
            _s_rwkv_body, grid=(RWKV_HEADS,),
            in_specs=[st_spec] + [hd_spec] * 7 + [col_spec] * 3 + [pl.BlockSpec(memory_space=pl.ANY)],
            out_specs=[_whole((SB, D_RWKV)), st_spec],
            out_shape=[_sds((SB, D_RWKV), BF16), _sds(state_S5.shape)],
            scratch_shapes=[pltpu.VMEM((D_RWKV, SB), F32)],
            input_output_aliases={11: 1},
            compiler_params=_params(1, 48), name="s_rwkv")(
                state_S5, rT, wT, kT, vT, kkT, aT, gT, rk_col, rlg_col, rlb_col, s_S5)

        hs = merge(hs, [yA, yB, yC, yD], SB)

        kc_spec = pl.BlockSpec((None, S_ATT_BB, N_MEM * SUBLANES, LANES), lambda i: (l, i, 0, 0))
        xs_spec = _whole((SB, D_MODEL))
        hs = pl.pallas_call(
            _s_xattn_body, grid=(SB // S_ATT_BB,),
            in_specs=[xs_spec, L(gxat), L(wXq), kc_spec, kc_spec, L(wXo)],
            out_specs=xs_spec, out_shape=_sds((SB, D_MODEL)),
            scratch_shapes=[pltpu.VMEM((SB, D_MODEL), F32), pltpu.VMEM((SB, D_MODEL), F32)],
            compiler_params=_params(1, 56), name="s_xattn")(
                hs, gxat, wXq, cache_k, cache_v, wXo)

        ncf = D_FF // FF_CW
        wu_spec = pl.BlockSpec((None, D_MODEL, FF_CW), lambda j: (l, 0, j))
        wuv_spec = pl.BlockSpec((None, D_MODEL, FF_CW), lambda j: (l, 0, j + ncf))
        sg_spec = pl.BlockSpec((2, SB, FF_CW), lambda j: (0, 0, j))
        sv_spec = pl.BlockSpec((2, SB, FF_CW), lambda j: (0, 0, j + ncf))
        cw_spec = pl.BlockSpec((None, 3, FF_CW), lambda j: (l, 0, j))
        cb_spec = pl.BlockSpec((None, 1, FF_CW), lambda j: (l, 0, j))
        hh_spec = pl.BlockSpec((SB, FF_CW), lambda j: (0, j))
        hs, s_hg, s_hv = pl.pallas_call(
            _s_ffn_body, grid=(ncf,),
            in_specs=[xs_spec, L(gffn), wu_spec, wuv_spec, sg_spec, sv_spec, cw_spec, cw_spec, cb_spec, cb_spec,
                      pl.BlockSpec((None, FF_CW, D_MODEL), lambda j: (l, j, 0))],
            out_specs=[xs_spec, hh_spec, hh_spec],
            out_shape=[_sds((SB, D_MODEL)), _sds((SB, D_FF)), _sds((SB, D_FF))],
            scratch_shapes=[pltpu.VMEM((SB, D_MODEL), F32)],
            compiler_params=_params(1, 48), name="s_ffn")(
                hs, gffn, w_up_b, w_up_b, sffn, sffn, cwg, cwv, cbg, cbv, wDn)

        outs["s_conv"].append(jnp.concatenate([state_lru_conv[l][:, 1:], s_xa[:, None]], axis=1))
        outs["s_h"].append(s_h)
        outs["s_shift"].append(s_p[:, None])
        outs["s_pool"].append(jnp.concatenate([state_pool[l][:, 1:], s_pc[:, None]], axis=1))
        outs["s_ffn"].append(jnp.concatenate(
            [state_ffn_conv[l][:, 1:], jnp.concatenate([s_hg, s_hv], axis=-1)[:, None]], axis=1))
        outs["s_cv"].append(s_zv[:, None])

    def final_norm(x2d, tm):
        rows = x2d.shape[0]
        spec = pl.BlockSpec((tm, D_MODEL), lambda i: (i, 0))
        return pl.pallas_call(_final_norm_body, grid=(rows // tm,),
                              in_specs=[spec, _whole((1, D_MODEL))], out_specs=spec,
                              out_shape=_sds((rows, D_MODEL)),
                              compiler_params=_params(1, 32), name="final_norm")(x2d, gfin)

    y_prompt = final_norm(hp.reshape(B * T, D_MODEL), 1024).reshape(B, T, D_MODEL)
    y_sample = final_norm(hs, SB).reshape(SB, 1, D_MODEL)
    stk = {k: jnp.stack(v, axis=0) for k, v in outs.items()}
    def mem_out(kv):
        kv = kv.reshape(DEPTH, B, N_MEM, X_HEAD_DIM // LANES, X_HEADS, LANES)
        return jnp.swapaxes(kv, 3, 4).reshape(DEPTH, B, N_MEM, X_HEADS, X_HEAD_DIM)

    return (y_prompt, y_sample,
            stk["p_conv"], stk["s_conv"], stk["p_h"], stk["s_h"],
            stk["p_shift"], stk["s_shift"], stk["p_S"], jnp.transpose(s_S5, (0, 4, 1, 2, 3)),
            stk["p_pool"], stk["s_pool"], stk["p_ffn"], stk["s_ffn"],
            mem_out(p_mk), mem_out(p_mv), stk["s_cv"])
```

```python
import functools
import math

import jax
import jax.numpy as jnp
from jax import lax
from jax.experimental import pallas as pl
from jax.experimental.pallas import tpu as pltpu

F32 = jnp.float32
BF16 = jnp.bfloat16

SUBLANES = 8
LANES = 128

D_MODEL = 1024
DEPTH = 4
N_MEM = 256
D_LRU = 512
LRU_C = 8.0
D_RWKV = 512
RWKV_HEAD = 64
RWKV_HEADS = D_RWKV // RWKV_HEAD
R_DECAY = 64
R_AAA = 64
R_GATE = 128
D_RWKV_IN = 3 * D_RWKV + R_DECAY + R_AAA + R_GATE
GN_EPS = 64e-5
KK_NORM_FLOOR = 1e-24
GMLP_LN_EPS = 1e-5
D_POOL = 512
POOL_WINDOWS = (2, 4, 8, 16)
POOL_GW = D_POOL // len(POOL_WINDOWS)
POOL_BUF = max(POOL_WINDOWS) - 1
POOL_HIST = 16
D_GMLP = 512
GMLP_GROUPS = 4
CHUNK = 128
N_BRANCH = 4
X_HEADS = 4
X_HEAD_DIM = D_MODEL // X_HEADS
D_FF = 3 * D_MODEL
EPS = 1e-6
PAST_LEN = 16384

O_A = 0
O_B = 2 * D_LRU
O_C = O_B + D_RWKV_IN
O_D = O_C + D_POOL
O_G = O_D + 2 * D_GMLP
D_IN = O_G + N_BRANCH * D_MODEL
P_G = 0
P_A = P_G + (D_IN - O_G)
P_D = P_A + (O_B - O_A)
P_C = P_D + (O_G - O_D)
P_B = -(-(P_C + D_POOL) // D_RWKV_IN) * D_RWKV_IN

TM_MIX = 512
TM_RWKV = 512
FF_CW = 1536
RWKV_CHUNK = 64
S_ATT_BB = 8


def _dot(a, b):
    return jnp.dot(a.astype(BF16), b.astype(BF16), preferred_element_type=F32)


def _rms(x, g):
    return x * lax.rsqrt(jnp.mean(x * x, axis=-1, keepdims=True) + EPS) * g


def _gelu(x):
    c = math.sqrt(2.0 / math.pi)
    return 0.5 * x * (1.0 + jnp.tanh(c * (x + 0.044715 * (x * x * x))))


def _softplus(x):
    return jnp.maximum(x, 0.0) + jnp.log1p(jnp.exp(-jnp.abs(x)))


def _sigmoid(x):
    return jax.nn.sigmoid(x)


def _shift_rows(hist, cur, s):
    n = hist.shape[0]
    return pltpu.roll(jnp.concatenate([hist, cur], axis=0), s, 0)[n:]


def _head_ones():
    r = lax.broadcasted_iota(jnp.int32, (LANES, LANES), 0) // RWKV_HEAD
    c = lax.broadcasted_iota(jnp.int32, (LANES, LANES), 1) // RWKV_HEAD
    return jnp.where(r == c, 1.0, 0.0).astype(BF16)


def _segsum(x, ones):
    xb = x.astype(BF16)
    return jnp.concatenate([jnp.dot(xb[:, p * LANES:(p + 1) * LANES], ones, preferred_element_type=F32)
                            for p in range(x.shape[1] // LANES)], axis=-1)


def _blockdiag_dot(x, w_ref):
    outs = []
    for p in range(x.shape[1] // LANES):
        outs.append(jnp.dot(x[:, p * LANES:(p + 1) * LANES].astype(BF16), w_ref[p],
                            preferred_element_type=F32))
    return jnp.concatenate(outs, axis=-1)


def _lru_gates(xc, wra_ref, bra_ref, wix_ref, bix_ref, lam_ref):
    r = _sigmoid(_blockdiag_dot(xc, wra_ref) + bra_ref[...])
    i = _sigmoid(_blockdiag_dot(xc, wix_ref) + bix_ref[...])
    log_a = -LRU_C * r * _softplus(-lam_ref[...])
    a = jnp.exp(log_a)
    mult = jnp.sqrt(-jnp.tanh(log_a) * (1.0 + a * a))
    return a, mult * (i * xc)


def _rwkv_pre(p, prev, mu, wwa_ref, w0, a0, g2_ref, k_k, k_a, ones):
    px = p + (prev - p) * mu
    r = px[:, 0:D_RWKV]
    k = px[:, D_RWKV:2 * D_RWKV]
    v = px[:, 2 * D_RWKV:3 * D_RWKV]
    lo = px[:, 3 * D_RWKV:3 * D_RWKV + R_DECAY + R_AAA]
    g_lo = px[:, 3 * D_RWKV + R_DECAY + R_AAA:]
    lane = lax.broadcasted_iota(jnp.int32, lo.shape, 1)
    wa = _dot(jnp.where(lane < R_DECAY, jnp.tanh(lo), lo), wwa_ref[...])
    logw = -math.exp(-0.5) * _sigmoid(w0 + wa[:, :D_RWKV])
    a = _sigmoid(a0 + wa[:, D_RWKV:])
    g = _dot(_sigmoid(g_lo), g2_ref[...])
    kk = k * k_k
    kk = kk * lax.rsqrt(jnp.maximum(_segsum(kk * kk, ones), KK_NORM_FLOOR))
    k2 = k * (1.0 + (a - 1.0) * k_a)
    return r, logw, k2, v, kk, a, g


def _rwkv_post(y, r, k2, v, g, r_k, ln_g, ln_b, ones):
    inv = 1.0 / RWKV_HEAD
    mean = _segsum(y, ones) * inv
    yc = y - mean
    var = _segsum(yc * yc, ones) * inv
    yn = yc * lax.rsqrt(var + GN_EPS) * ln_g + ln_b
    bonus = _segsum(r * k2 * r_k, ones) * v
    return (yn + bonus) * g


def _pool_project(d, wp_ref, scale):
    return _blockdiag_dot(d, wp_ref) * scale


def _gmlp_uz(z2, ln_g, ln_b):
    u = _gelu(z2[:, :D_GMLP])
    v = _gelu(z2[:, D_GMLP:])
    mu = jnp.mean(v, axis=-1, keepdims=True)
    vc = v - mu
    var = jnp.mean(vc * vc, axis=-1, keepdims=True)
    return u, vc * lax.rsqrt(var + GMLP_LN_EPS) * ln_g + ln_b


def _lru_tile(xn, w_ref, cw_ref, cb_ref, wra_ref, bra_ref, wix_ref, bix_ref, lam_ref,
              y_ref, convo_ref, ho_ref, hist_ref, h_ref, sa_ref, sb_ref, carry_ref):
    @pl.when(pl.program_id(1) == 0)
    def _():
        hist_ref[...] = jnp.zeros_like(hist_ref)
        h_ref[...] = jnp.zeros_like(h_ref)

    z = jnp.dot(xn, w_ref[...], preferred_element_type=F32)
    xa = z[:, :D_LRU]
    ga = z[:, D_LRU:]
    tm = xa.shape[0]
    hist = hist_ref[...]
    cw = cw_ref[...]
    xc = cb_ref[...] + cw[3:4] * xa
    for s in (1, 2, 3):
        xc = xc + cw[3 - s:4 - s] * _shift_rows(hist, xa, s)
    a, b = _lru_gates(xc, wra_ref, bra_ref, wix_ref, bix_ref, lam_ref)
    groups = (tm // SUBLANES, SUBLANES, D_LRU)
    a3 = a.reshape(groups)
    b3 = b.reshape(groups)
    sub = lax.broadcasted_iota(jnp.int32, groups, 1)
    s = 1
    while s < SUBLANES:
        m = sub >= s
        b3 = jnp.where(m, a3 * pltpu.roll(b3, s, 1) + b3, b3)
        a3 = jnp.where(m, a3 * pltpu.roll(a3, s, 1), a3)
        s *= 2
    a = a3.reshape(tm, D_LRU)
    b = b3.reshape(tm, D_LRU)
    sa_ref[...] = a
    sb_ref[...] = b

    def carry_group(gi, hprev):
        r0 = pl.multiple_of(gi * SUBLANES, SUBLANES)
        carry_ref[pl.ds(r0, SUBLANES), :] = jnp.broadcast_to(hprev, (SUBLANES, D_LRU))
        last = pl.ds(r0 + SUBLANES - 1, 1)
        return sa_ref[last, :] * hprev + sb_ref[last, :]

    lax.fori_loop(0, tm // SUBLANES, carry_group, h_ref[SUBLANES - 1:SUBLANES, :], unroll=8)
    h = a * carry_ref[...] + b
    y_ref[...] = (h * _gelu(ga)).astype(y_ref.dtype)
    hist_ref[...] = xa[tm - SUBLANES:]
    h_ref[...] = h[tm - SUBLANES:]
    convo_ref[...] = xa[tm - SUBLANES:]
    ho_ref[...] = h[tm - SUBLANES:]


def _pair_blockdiag_rows(x):
    lo = lax.broadcasted_iota(jnp.int32, x.shape, 1) < RWKV_HEAD
    z = jnp.zeros_like(x)
    return jnp.concatenate([jnp.where(lo, x, z), jnp.where(lo, z, x)], axis=0)


def _p_rwkv_body(x_ref, g_ref, w_ref, mu_ref, wwa_ref, w0_ref, a0_ref, g2_ref, kk_ref, ka_ref,
                 rk_ref, lng_ref, lnb_ref,
                 y_ref, shifto_ref, so_ref,
                 hist_ref, st_ref):
    @pl.when(pl.program_id(1) == 0)
    def _():
        hist_ref[...] = jnp.zeros_like(hist_ref)
        st_ref[...] = jnp.zeros_like(st_ref)

    C = RWKV_CHUNK
    ones = _head_ones()
    p = _dot(_rms(x_ref[...], g_ref[...]), w_ref[...])
    tm = p.shape[0]
    nc = tm // C
    npair = D_RWKV // LANES
    prev = _shift_rows(hist_ref[...], p, 1)
    r, logw, k2, v, kk, a, g = _rwkv_pre(p, prev, mu_ref[...], wwa_ref, w0_ref[...], a0_ref[...],
                                          g2_ref, kk_ref[...], ka_ref[...], ones)
    hist_ref[...] = p[tm - SUBLANES:]
    shifto_ref[...] = p[tm - SUBLANES:]
    rowc = lax.broadcasted_iota(jnp.int32, logw.shape, 0) % C
    lg = logw
    s = 1
    while s < C:
        lg = lg + jnp.where(rowc >= s, pltpu.roll(lg, s, 0), 0.0)
        s *= 2
    lg_last = [lg[(c + 1) * C - 1:(c + 1) * C, :] for c in range(nc)]
    lg_end = jnp.concatenate([jnp.broadcast_to(x, (C, D_RWKV)) for x in lg_last], axis=0)
    inv_gam = jnp.exp(-lg)
    to_end = jnp.exp(lg_end - lg)
    b = kk * a
    rt_f = r * jnp.exp(lg)
    at_b, rt_b, bt_b, kt_b, v_b, bh_b, kh_b = (x.astype(BF16) for x in (
        -kk * jnp.exp(lg - logw), rt_f, b * inv_gam, k2 * inv_gam, v, b * to_end, k2 * to_end))

    r2 = lax.broadcasted_iota(jnp.int32, (C, LANES), 0)
    c2 = lax.broadcasted_iota(jnp.int32, (C, LANES), 1) % C
    strict2 = c2 < r2
    incl2 = c2 <= r2
    eye2 = jnp.where(r2 == c2, 1.0, 0.0).astype(F32)
    pair2 = (r2 // 2) == (c2 // 2)
    levels = []
    nb = 2
    while nb < C:
        levels.append(((r2 // (2 * nb)) == (c2 // (2 * nb))) & ((r2 // nb) != (c2 // nb)))
        nb *= 2
    rr = lax.broadcasted_iota(jnp.int32, (LANES, LANES), 0)
    cc = lax.broadcasted_iota(jnp.int32, (LANES, LANES), 1)
    same_head = (rr // RWKV_HEAD) == (cc // RWKV_HEAD)
    eye128 = jnp.where(rr == cc, 1.0, 0.0).astype(F32)
    tn = (((0,), (0,)), ((), ()))
    nt_ = (((1,), (1,)), ((), ()))
    bd = _pair_blockdiag_rows

    def mm(x, y):
        return jnp.dot(x, y, preferred_element_type=F32)

    def blk(x, c, q):
        return x[c * C:(c + 1) * C, q * LANES:(q + 1) * LANES]

    P = [(c, q) for c in range(nc) for q in range(npair)]
    N = range(len(P))
    at, rt, bt, kt, vv, bh, kh = ([blk(x, c, q) for c, q in P] for x in
                                  (at_b, rt_b, bt_b, kt_b, v_b, bh_b, kh_b))
    ar = [jnp.concatenate([at[n], rt[n]], axis=0) for n in N]
    pb = [lax.dot_general(ar[n], bd(bt[n]), nt_, preferred_element_type=F32) for n in N]
    pk = [lax.dot_general(ar[n], bd(kt[n]), nt_, preferred_element_type=F32) for n in N]
    lab = [jnp.where(strict2, x[:C], 0.0) for x in pb]
    mrb = [jnp.where(incl2, x[C:], 0.0).astype(BF16) for x in pb]
    lak = [jnp.where(strict2, x[:C], 0.0).astype(BF16) for x in pk]
    mrk = [jnp.where(incl2, x[C:], 0.0).astype(BF16) for x in pk]
    v_bd = [bd(x) for x in vv]
    lak_v = [mm(lak[n], v_bd[n]) for n in N]
    T = [eye2 + jnp.where(pair2, x, 0.0) for x in lab]
    for mk in levels:
        tb = [x.astype(BF16) for x in T]
        u = [mm(tb[n], bd(jnp.where(mk, lab[n], 0.0).astype(BF16))).astype(BF16) for n in N]
        T = [T[n] + mm(u[n], bd(tb[n])) for n in N]
    tb = [x.astype(BF16) for x in T]
    tab = [mm(tb[n], bd(at[n])).astype(BF16) for n in N]
    cv = [mm(tb[n], bd(lak_v[n].astype(BF16))).astype(BF16) for n in N]
    g_bd = [jnp.where(same_head, lax.dot_general(bh[n], tab[n], tn, preferred_element_type=F32), 0.0)
            .astype(BF16) for n in N]
    q_bd = [jnp.where(same_head, lax.dot_general(jnp.concatenate([bh[n], kh[n]], axis=0),
                                                 jnp.concatenate([cv[n], vv[n]], axis=0), tn,
                                                 preferred_element_type=F32), 0.0) for n in N]
    ry = [(blk(rt_f, c, q) + mm(mrb[n], bd(tab[n]))).astype(BF16) for n, (c, q) in enumerate(P)]
    yc = [mm(jnp.concatenate([mrb[n], mrk[n]], axis=1),
             jnp.concatenate([bd(cv[n]), v_bd[n]], axis=0)) for n in N]
    gcol = [jnp.broadcast_to(jnp.sum(eye128 * jnp.exp(lg_last[c][:, q * LANES:(q + 1) * LANES]),
                                     axis=1, keepdims=True), (LANES, LANES)) for c, q in P]

    st = [st_ref[q] for q in range(npair)]
    y_rows = []
    for c in range(nc):
        sb = [x.astype(BF16) for x in st]
        ns = [c * npair + q for q in range(npair)]
        gs = [mm(g_bd[n], sb[q]) for q, n in enumerate(ns)]
        ys = [mm(ry[n], sb[q]) for q, n in enumerate(ns)]
        st = [st[q] * gcol[n] + gs[q] + q_bd[n] for q, n in enumerate(ns)]
        y_rows.append(jnp.concatenate([ys[q] + yc[n] for q, n in enumerate(ns)], axis=1))
    y = jnp.concatenate(y_rows, axis=0)
    yb = _rwkv_post(y, r, k2, v, g, rk_ref[...], lng_ref[...], lnb_ref[...], ones)
    y_ref[...] = yb.astype(y_ref.dtype)
    for q in range(npair):
        st_ref[q] = st[q]
        so_ref[2 * q] = st[q][:RWKV_HEAD, :RWKV_HEAD].T
        so_ref[2 * q + 1] = st[q][RWKV_HEAD:, RWKV_HEAD:].T


def _pool_tile(xn, w_ref, wp_ref, sc_ref, y_ref, poolo_ref, hist_ref):
    t = pl.program_id(1)

    @pl.when(t == 0)
    def _():
        hist_ref[...] = jnp.zeros_like(hist_ref)

    pc = jnp.dot(xn, w_ref[...], preferred_element_type=F32)
    tm = pc.shape[0]
    X = jnp.concatenate([hist_ref[...], pc], axis=0)
    pos = t * tm + lax.broadcasted_iota(jnp.int32, (tm, POOL_GW), 0)
    ds = []
    for gi, win in enumerate(POOL_WINDOWS):
        sl = slice(gi * POOL_GW, (gi + 1) * POOL_GW)
        s = X[:, sl]
        sh = 1
        while sh < win:
            s = s + pltpu.roll(s, sh, 0)
            sh *= 2
        cnt = jnp.minimum(pos + 1, win).astype(F32)
        ds.append(s[POOL_HIST:] / cnt - pc[:, sl])
    y = _pool_project(jnp.concatenate(ds, axis=-1), wp_ref, sc_ref[...])
    y_ref[...] = y.astype(y_ref.dtype)
    hist_ref[...] = pc[tm - POOL_HIST:]
    poolo_ref[...] = pc[tm - POOL_HIST:]


def _gmlp_tile(xn, w_ref, lng_ref, lnb_ref, ws_ref, bsb_ref, y_ref):
    z2 = jnp.dot(xn, w_ref[...], preferred_element_type=F32)
    tm = z2.shape[0]
    u, zv = _gmlp_uz(z2, lng_ref[...], lnb_ref[...])
    tril = (lax.broadcasted_iota(jnp.int32, (CHUNK, CHUNK), 0)
            >= lax.broadcasted_iota(jnp.int32, (CHUNK, CHUNK), 1))
    ws = [jnp.where(tril, ws_ref[gi], 0.0).astype(BF16) for gi in range(GMLP_GROUPS)]
    zb = zv.astype(BF16)
    rows = []
    for c in range(tm // CHUNK):
        cols = [jnp.dot(ws[gi], zb[c * CHUNK:(c + 1) * CHUNK, gi * LANES:(gi + 1) * LANES],
                        preferred_element_type=F32) for gi in range(GMLP_GROUPS)]
        rows.append(jnp.concatenate(cols, axis=-1) + bsb_ref[...])
    s = jnp.concatenate(rows, axis=0)
    y_ref[...] = (u * s).astype(y_ref.dtype)


N_LRU_IN, N_POOL_IN, N_GMLP_IN = 8, 3, 5


def _p_mix3_body(x_ref, g_ref, *refs):
    i0, i1, i2 = N_LRU_IN, N_LRU_IN + N_POOL_IN, N_LRU_IN + N_POOL_IN + N_GMLP_IN
    lru_in, pool_in, gmlp_in = refs[:i0], refs[i0:i1], refs[i1:i2]
    ya_ref, convo_ref, ho_ref, yc_ref, poolo_ref, yd_ref = refs[i2:i2 + 6]
    lru_sc, pool_sc = refs[i2 + 6:i2 + 11], refs[i2 + 11:]
    xn = _rms(x_ref[...], g_ref[...]).astype(BF16)
    _lru_tile(xn, *lru_in, ya_ref, convo_ref, ho_ref, *lru_sc)
    _pool_tile(xn, *pool_in, yc_ref, poolo_ref, *pool_sc)
    _gmlp_tile(xn, *gmlp_in, yd_ref)


def _merge_body(x_ref, g_ref, wg_ref, ya_ref, yb_ref, yc_ref, yd_ref, wp_ref, wo_ref, o_ref):
    x = x_ref[...]
    xn = _rms(x, g_ref[...]).astype(BF16)
    merged = None
    for i, y_ref in enumerate((ya_ref, yb_ref, yc_ref, yd_ref)):
        gate = _sigmoid(jnp.dot(xn, wg_ref[:, i * D_MODEL:(i + 1) * D_MODEL],
                                preferred_element_type=F32))
        term = gate * jnp.dot(y_ref[...], wp_ref[i], preferred_element_type=F32)
        merged = term if merged is None else merged + term
    o_ref[...] = x + _dot(merged, wo_ref[...])


def _softmax_rows(s):
    e = jnp.exp(s - jnp.max(s, axis=-1, keepdims=True))
    return e / jnp.sum(e, axis=-1, keepdims=True)


def _kv_rows(half, h):
    return pl.ds(half * X_HEADS + h, N_MEM, stride=SUBLANES)


def _p_xattn_body(x_ref, g_ref, wq_ref, k_ref, v_ref, wo_ref, o_ref):
    x = x_ref[...]
    q = _dot(_rms(x, g_ref[...]), wq_ref[...]).astype(BF16)
    outs = []
    for h in range(X_HEADS):
        sl = slice(h * X_HEAD_DIM, (h + 1) * X_HEAD_DIM)
        s = lax.dot_general(q[:, sl], k_ref[:, sl], (((1,), (1,)), ((), ())),
                            preferred_element_type=F32) * (X_HEAD_DIM ** -0.5)
        outs.append(jnp.dot(_softmax_rows(s).astype(BF16), v_ref[:, sl], preferred_element_type=F32))
    o_ref[...] = x + _dot(jnp.concatenate(outs, axis=-1), wo_ref[...])


def _ffn_conv(hist, h, cw, cb):
    return cb + cw[2:3] * h + cw[1:2] * _shift_rows(hist, h, 1) + cw[0:1] * _shift_rows(hist, h, 2)


def _p_ffn_body(x_ref, g_ref, wug_ref, wuv_ref, cwg_ref, cwv_ref, cbg_ref, cbv_ref, wd_ref,
                o_ref, ffo_ref, hist_ref):
    @pl.when(pl.program_id(1) == 0)
    def _():
        hist_ref[...] = jnp.zeros_like(hist_ref)

    x = x_ref[...]
    tm = x.shape[0]
    xn = _rms(x, g_ref[...]).astype(BF16)
    acc = x
    for j in range(D_FF // FF_CW):
        cs = slice(j * FF_CW, (j + 1) * FF_CW)
        vs = slice(D_FF + j * FF_CW, D_FF + (j + 1) * FF_CW)
        hg = jnp.dot(xn, wug_ref[:, cs], preferred_element_type=F32)
        hv = jnp.dot(xn, wuv_ref[:, cs], preferred_element_type=F32)
        cg = _ffn_conv(hist_ref[:, cs], hg, cwg_ref[:, cs], cbg_ref[:, cs])
        cv = _ffn_conv(hist_ref[:, vs], hv, cwv_ref[:, cs], cbv_ref[:, cs])
        acc = acc + _dot(_gelu(cg) * cv, wd_ref[cs, :])
        hist_ref[:, cs] = hg[tm - SUBLANES:]
        hist_ref[:, vs] = hv[tm - SUBLANES:]
    o_ref[...] = acc
    ffo_ref[...] = hist_ref[...]


def _memkv_body(m_ref, g_ref, wk_ref, wv_ref, k_ref, v_ref, kb_ref, vb_ref):
    mn = _rms(m_ref[...], g_ref[...]).astype(BF16)
    for w_ref, o_ref, b_ref in ((wk_ref, k_ref, kb_ref), (wv_ref, v_ref, vb_ref)):
        kv = jnp.dot(mn, w_ref[...], preferred_element_type=F32)
        b_ref[...] = kv.astype(BF16)
        for h in range(X_HEADS):
            for half in range(X_HEAD_DIM // LANES):
                lo = h * X_HEAD_DIM + half * LANES
                o_ref[_kv_rows(half, h), :] = kv[:, lo:lo + LANES]


def _final_norm_body(x_ref, g_ref, o_ref):
    o_ref[...] = _rms(x_ref[...], g_ref[...])


def _s_mix_body(x_ref, g_ref, wa_ref, wb_ref, wc_ref, wd_ref,
                lconv_ref, lh_ref, cw_ref, cb_ref, wra_ref, bra_ref, wix_ref, bix_ref, lam_ref,
                shift_ref, mu_ref, wwa_ref, w0_ref, a0_ref, g2_ref, kk_ref, ka_ref,
                pool_ref, wp_ref, psc_ref,
                lng_ref, lnb_ref, wsd_ref, bsr_ref,
                ya_ref, xa_ref, h_ref, yc_ref, pc_ref, yd_ref, zv_ref, p_ref,
                rt_ref, wt_ref, kt_ref, vt_ref, kkt_ref, at_ref, gt_ref):
    xn = _rms(x_ref[...], g_ref[...]).astype(BF16)
    z = jnp.dot(xn, wa_ref[...], preferred_element_type=F32)
    xa = z[:, :D_LRU]
    cw = cw_ref[...]
    xc = cb_ref[...] + cw[0:1] * lconv_ref[0] + cw[1:2] * lconv_ref[1] + cw[2:3] * lconv_ref[2] + cw[3:4] * xa
    a, b = _lru_gates(xc, wra_ref, bra_ref, wix_ref, bix_ref, lam_ref)
    h = a * lh_ref[...] + b
    ya_ref[...] = (h * _gelu(z[:, D_LRU:])).astype(ya_ref.dtype)
    xa_ref[...] = xa
    h_ref[...] = h
    ones = _head_ones()
    p = jnp.dot(xn, wb_ref[...], preferred_element_type=F32)
    r, logw, k2, v, kk, aa, g = _rwkv_pre(p, shift_ref[...], mu_ref[...], wwa_ref, w0_ref[...],
                                           a0_ref[...], g2_ref, kk_ref[...], ka_ref[...], ones)
    p_ref[...] = p
    for ref, val in ((rt_ref, r), (wt_ref, jnp.exp(logw)), (kt_ref, k2), (vt_ref, v), (kkt_ref, kk),
                     (at_ref, aa), (gt_ref, g)):
        ref[...] = val.T
    pc = jnp.dot(xn, wc_ref[...], preferred_element_type=F32)
    ds = []
    for gi, win in enumerate(POOL_WINDOWS):
        sl = slice(gi * POOL_GW, (gi + 1) * POOL_GW)
        s = pc[:, sl]
        for j in range(POOL_BUF - (win - 1), POOL_BUF):
            s = s + pool_ref[j][:, sl]
        ds.append(s / float(min(PAST_LEN + 1, win)) - pc[:, sl])
    yc_ref[...] = _pool_project(jnp.concatenate(ds, axis=-1), wp_ref, psc_ref[...]).astype(yc_ref.dtype)
    pc_ref[...] = pc
    u, zv = _gmlp_uz(jnp.dot(xn, wd_ref[...], preferred_element_type=F32), lng_ref[...], lnb_ref[...])
    yd_ref[...] = (u * (wsd_ref[...] * zv + bsr_ref[...])).astype(yd_ref.dtype)
    zv_ref[...] = zv


def _s_rwkv_body(s_ref, r_ref, w_ref, k_ref, v_ref, kk_ref, a_ref, g_ref, rk_ref, lng_ref, lnb_ref,
                 all_layers_ref, y_ref, so_ref, yt_sc):
    del all_layers_ref
    h = pl.program_id(0)
    kk = kk_ref[...]
    w = w_ref[...]
    k = k_ref[...]
    r = r_ref[...]
    bk = kk * a_ref[...]

    def body(i, carry):
        si = s_ref[i]
        sa = -jnp.sum(si * kk, axis=0, keepdims=True)
        s2 = si * w + sa * bk + v_ref[pl.ds(i, 1), :] * k
        so_ref[i] = s2
        yt_sc[pl.ds(h * RWKV_HEAD + i, 1), :] = jnp.sum(s2 * r, axis=0, keepdims=True)
        return carry

    lax.fori_loop(0, RWKV_HEAD, body, 0, unroll=4)
    rows = pl.ds(pl.multiple_of(h * RWKV_HEAD, RWKV_HEAD), RWKV_HEAD)
    y = yt_sc[rows, :]
    yc = y - jnp.mean(y, axis=0, keepdims=True)
    var = jnp.mean(yc * yc, axis=0, keepdims=True)
    yn = yc * lax.rsqrt(var + GN_EPS) * lng_ref[...] + lnb_ref[...]
    bonus = jnp.sum(r * k * rk_ref[...], axis=0, keepdims=True) * v_ref[...]
    yt_sc[rows, :] = (yn + bonus) * g_ref[...]

    @pl.when(h == pl.num_programs(0) - 1)
    def _():
        y_ref[...] = yt_sc[...].T.astype(y_ref.dtype)


def _s_xattn_body(x_ref, g_ref, wq_ref, k_ref, v_ref, wo_ref, o_ref, q_sc, a_sc):
    i = pl.program_id(0)
    bb = k_ref.shape[0]

    @pl.when(i == 0)
    def _():
        q_sc[...] = _dot(_rms(x_ref[...], g_ref[...]), wq_ref[...])

    halves = X_HEAD_DIM // LANES

    def q_lanes(c, h):
        return slice(h * X_HEAD_DIM + c * LANES, h * X_HEAD_DIM + (c + 1) * LANES)

    for j in range(bb):
        row = pl.ds(i * bb + j, 1)
        qrow = q_sc[row, :]
        q8 = jnp.concatenate([qrow[:, q_lanes(c, h)] for c in range(halves) for h in range(X_HEADS)], axis=0)
        k3 = k_ref[j].reshape(N_MEM, SUBLANES, LANES)
        part = jnp.sum(k3 * q8[None], axis=-1, keepdims=True)
        part = jnp.broadcast_to(part, k3.shape)
        s = (part + pltpu.roll(part, X_HEADS, 1)) * (X_HEAD_DIM ** -0.5)
        e = jnp.exp(s - jnp.max(s, axis=0, keepdims=True))
        pr = e / jnp.sum(e, axis=0, keepdims=True)
        o8 = jnp.sum(pr * v_ref[j].reshape(N_MEM, SUBLANES, LANES), axis=0)
        a_sc[row, :] = jnp.concatenate([o8[c * X_HEADS + h:c * X_HEADS + h + 1, :]
                                        for h in range(X_HEADS) for c in range(halves)], axis=1)

    @pl.when(i == pl.num_programs(0) - 1)
    def _():
        o_ref[...] = x_ref[...] + _dot(a_sc[...], wo_ref[...])


def _s_ffn_body(x_ref, g_ref, wug_ref, wuv_ref, sg_ref, sv_ref, cwg_ref, cwv_ref, cbg_ref, cbv_ref,
                wd_ref, o_ref, hg_ref, hv_ref, acc_ref):
    j = pl.program_id(0)

    @pl.when(j == 0)
    def _():
        acc_ref[...] = x_ref[...]

    xn = _rms(x_ref[...], g_ref[...]).astype(BF16)
    hg = jnp.dot(xn, wug_ref[...], preferred_element_type=F32)
    hv = jnp.dot(xn, wuv_ref[...], preferred_element_type=F32)
    cwg = cwg_ref[...]
    cwv = cwv_ref[...]
    cg = cbg_ref[...] + cwg[0:1] * sg_ref[0] + cwg[1:2] * sg_ref[1] + cwg[2:3] * hg
    cv = cbv_ref[...] + cwv[0:1] * sv_ref[0] + cwv[1:2] * sv_ref[1] + cwv[2:3] * hv
    acc_ref[...] += _dot(_gelu(cg) * cv, wd_ref[...])
    hg_ref[...] = hg
    hv_ref[...] = hv

    @pl.when(j == pl.num_programs(0) - 1)
    def _():
        o_ref[...] = acc_ref[...]


def _params(n_grid, vmem_mb):
    return pltpu.CompilerParams(dimension_semantics=("arbitrary",) * n_grid,
                                vmem_limit_bytes=vmem_mb << 20)


def _whole(shape):
    return pl.BlockSpec(tuple(shape), lambda *_: (0,) * len(shape))


def _layer(arr, l):
    shape = arr.shape[1:]
    return pl.BlockSpec((None,) + tuple(shape), lambda *_: (l,) + (0,) * len(shape))


def _sds(shape, dtype=F32):
    return jax.ShapeDtypeStruct(tuple(shape), dtype)


def _pair_blockdiag(w):
    L = w.shape[0]
    w = w.reshape(L, 4, 2, RWKV_HEAD, RWKV_HEAD)
    z = jnp.zeros_like(w[:, :, 0])
    top = jnp.concatenate([w[:, :, 0], z], axis=-1)
    bot = jnp.concatenate([z, w[:, :, 1]], axis=-1)
    return jnp.concatenate([top, bot], axis=-2)


def _vec(a):
    return a.reshape(a.shape[0], 1, -1)


def kernel(x_prompt, x_sample, state_lru_conv, state_lru_h, state_rwkv_shift, state_rwkv_S, state_pool, state_ffn_conv, cache_mem_k, cache_mem_v, mem_prompt, g_mix, w_in, lru_conv_w, lru_conv_b, lru_w_ra, lru_b_ra, lru_w_ix, lru_b_ix, lru_lambda, rwkv_mu, rwkv_w0, rwkv_w2, rwkv_a0, rwkv_a2, rwkv_g2, rwkv_k_k, rwkv_k_a, rwkv_r_k, rwkv_ln_g, rwkv_ln_b, pool_w, pool_scale, gmlp_ln_g, gmlp_ln_b, gmlp_w_s, gmlp_b_s, w_pa, w_pb, w_pc, w_pd, w_o, g_xattn, g_mem, w_xq, w_xk, w_xv, w_xo, g_ffn, w_up, ffn_conv_w, ffn_conv_b, w_down, g_final):
    B, T, D = x_prompt.shape
    SB = x_sample.shape[0]
    assert D == D_MODEL and w_in.shape == (DEPTH, D_MODEL, D_IN) and x_sample.shape[1] == 1
    assert T % TM_MIX == 0 and T % TM_RWKV == 0 and TM_RWKV % RWKV_CHUNK == 0 and SB % S_ATT_BB == 0
    assert SB == LANES and X_HEADS * (X_HEAD_DIM // LANES) == SUBLANES

    w_in_b = jnp.concatenate(
        [w_in[:, :, O_G:], w_in[:, :, O_A:O_B], w_in[:, :, O_D:O_G], w_in[:, :, O_C:O_D],
         jnp.zeros((DEPTH, D_MODEL, P_B - (P_C + D_POOL)), w_in.dtype), w_in[:, :, O_B:O_C]], axis=-1).astype(BF16)
    wra = _pair_blockdiag(lru_w_ra).astype(BF16)
    wix = _pair_blockdiag(lru_w_ix).astype(BF16)
    zer = jnp.zeros((DEPTH, R_DECAY, D_RWKV), F32)
    wwa = jnp.concatenate([jnp.concatenate([rwkv_w2, zer], axis=-1),
                           jnp.concatenate([zer, rwkv_a2], axis=-1)], axis=1).astype(BF16)
    g2 = rwkv_g2.astype(BF16)
    wpool = pool_w.astype(BF16)
    bsb = jnp.repeat(gmlp_b_s, LANES, axis=-1)
    off = PAST_LEN % CHUNK
    wsd = jnp.repeat(gmlp_w_s[:, :, off, off], LANES, axis=-1)[:, None, :]
    bsr = jnp.repeat(gmlp_b_s[:, off, :], LANES, axis=-1)[:, None, :]
    wP = jnp.stack([w_pa, w_pb, w_pc, w_pd], axis=1).astype(BF16)
    wO = w_o.astype(BF16)
    wXq, wXk, wXv, wXo = (w.astype(BF16) for w in (w_xq, w_xk, w_xv, w_xo))
    w_up_b = w_up.astype(BF16)
    wDn = w_down.astype(BF16)
    cwg, cwv = ffn_conv_w[:, :, :D_FF], ffn_conv_w[:, :, D_FF:]
    cbg, cbv = _vec(ffn_conv_b[:, :D_FF]), _vec(ffn_conv_b[:, D_FF:])
    gmix, gxat, gffn, gmem = _vec(g_mix), _vec(g_xattn), _vec(g_ffn), _vec(g_mem)
    lcb, bra, bix, lam = _vec(lru_conv_b), _vec(lru_b_ra), _vec(lru_b_ix), _vec(lru_lambda)
    mu, w0, a0, kkw, kaw = _vec(rwkv_mu), _vec(rwkv_w0), _vec(rwkv_a0), _vec(rwkv_k_k), _vec(rwkv_k_a)
    rk = rwkv_r_k.reshape(DEPTH, 1, D_RWKV)
    rlg, rlb = _vec(rwkv_ln_g), _vec(rwkv_ln_b)
    psc, glg, glb = _vec(pool_scale), _vec(gmlp_ln_g), _vec(gmlp_ln_b)
    gfin = g_final.reshape(1, D_MODEL)
    rk_col, rlg_col, rlb_col = (a.reshape(DEPTH, D_RWKV, 1) for a in (rwkv_r_k, rwkv_ln_g, rwkv_ln_b))
    state_S5 = jnp.transpose(state_rwkv_S, (0, 2, 3, 4, 1))
    s_S5 = jnp.zeros(state_S5.shape, F32)

    def cache_rows(c):
        c = c.reshape(DEPTH, SB, N_MEM, X_HEADS, X_HEAD_DIM // LANES, LANES)
        return jnp.swapaxes(c, 3, 4).reshape(DEPTH, SB, N_MEM * SUBLANES, LANES)

    cache_k, cache_v = cache_rows(cache_mem_k), cache_rows(cache_mem_v)

    kv_shape = _sds((DEPTH, B, N_MEM * SUBLANES, LANES))
    kv_spec = pl.BlockSpec((None, None, N_MEM * SUBLANES, LANES), lambda l, b: (l, b, 0, 0))
    wl_spec = pl.BlockSpec((None, D_MODEL, D_MODEL), lambda l, b: (l, 0, 0))
    kvrow_shape = _sds((DEPTH, B, N_MEM, D_MODEL), BF16)
    kvrow_spec = pl.BlockSpec((None, None, N_MEM, D_MODEL), lambda l, b: (l, b, 0, 0))
    p_mk, p_mv, p_mk_b, p_mv_b = pl.pallas_call(
        _memkv_body, grid=(DEPTH, B),
        in_specs=[pl.BlockSpec((None, N_MEM, D_MODEL), lambda l, b: (b, 0, 0)),
                  pl.BlockSpec((None, 1, D_MODEL), lambda l, b: (l, 0, 0)), wl_spec, wl_spec],
        out_specs=[kv_spec, kv_spec, kvrow_spec, kvrow_spec],
        out_shape=[kv_shape, kv_shape, kvrow_shape, kvrow_shape],
        compiler_params=_params(2, 32), name="memkv")(mem_prompt, gmem, wXk, wXv)

    nt = T // TM_MIX
    ntr = T // TM_RWKV
    xt_spec = pl.BlockSpec((None, TM_MIX, D_MODEL), lambda b, t: (b, t, 0))
    yt_spec = pl.BlockSpec((None, TM_MIX, D_LRU), lambda b, t: (b, t, 0))

    def tail_spec(rows, width):
        return pl.BlockSpec((None, rows, width), lambda b, t: (b, 0, 0))

    hp = x_prompt
    hs = x_sample.reshape(SB, D_MODEL)
    outs = {k: [] for k in ("p_conv", "s_conv", "p_h", "s_h", "p_shift", "s_shift", "p_S",
                            "p_pool", "s_pool", "p_ffn", "s_ffn", "s_cv")}

    for l in range(DEPTH):
        L = functools.partial(_layer, l=l)

        def win(start, width):
            assert start % width == 0
            return pl.BlockSpec((None, D_MODEL, width), lambda *_: (l, 0, start // width))

        win_a, win_b = win(P_A, O_B - O_A), win(P_B, O_C - O_B)
        win_c, win_d, win_g = win(P_C, O_D - O_C), win(P_D, O_G - O_D), win(P_G, D_IN - O_G)

        wup_g = pl.BlockSpec((None, D_MODEL, D_FF), lambda *_: (l, 0, 0))
        wup_v = pl.BlockSpec((None, D_MODEL, D_FF), lambda *_: (l, 0, 1))

        lru_in = [(w_in_b, win_a)] + [(a, L(a)) for a in (lru_conv_w, lcb, wra, bra, wix, bix, lam)]
        pool_in = [(w_in_b, win_c), (wpool, L(wpool)), (psc, L(psc))]
        gmlp_in = [(w_in_b, win_d)] + [(a, L(a)) for a in (glg, glb, gmlp_w_s, bsb)]
        assert (len(lru_in), len(pool_in), len(gmlp_in)) == (N_LRU_IN, N_POOL_IN, N_GMLP_IN)
        mix3_in = [(hp, xt_spec), (gmix, L(gmix))] + lru_in + pool_in + gmlp_in
        yA, p_conv8, p_h8, yC, p_pool16, yD = pl.pallas_call(
            _p_mix3_body, grid=(B, nt),
            in_specs=[s for _, s in mix3_in],
            out_specs=[yt_spec, tail_spec(SUBLANES, D_LRU), tail_spec(SUBLANES, D_LRU),
                       yt_spec, tail_spec(POOL_HIST, D_POOL), yt_spec],
            out_shape=[_sds((B, T, D_LRU), BF16), _sds((B, SUBLANES, D_LRU)), _sds((B, SUBLANES, D_LRU)),
                       _sds((B, T, D_POOL), BF16), _sds((B, POOL_HIST, D_POOL)), _sds((B, T, D_GMLP), BF16)],
            scratch_shapes=[pltpu.VMEM((SUBLANES, D_LRU), F32)] * 2 + [pltpu.VMEM((TM_MIX, D_LRU), F32)] * 3
                           + [pltpu.VMEM((POOL_HIST, D_POOL), F32)],
            compiler_params=_params(2, 56), name="p_mix3")(*[a for a, _ in mix3_in])

        yB, p_shift8, p_S = pl.pallas_call(
            _p_rwkv_body, grid=(B, ntr),
            in_specs=[pl.BlockSpec((None, TM_RWKV, D_MODEL), lambda b, t: (b, t, 0)),
                      L(gmix), win_b, L(mu), L(wwa), L(w0), L(a0), L(g2), L(kkw), L(kaw), L(rk), L(rlg), L(rlb)],
            out_specs=[pl.BlockSpec((None, TM_RWKV, D_RWKV), lambda b, t: (b, t, 0)),
                       tail_spec(SUBLANES, D_RWKV_IN),
                       pl.BlockSpec((None, RWKV_HEADS, RWKV_HEAD, RWKV_HEAD), lambda b, t: (b, 0, 0, 0))],
            out_shape=[_sds((B, T, D_RWKV), BF16), _sds((B, SUBLANES, D_RWKV_IN)),
                       _sds((B, RWKV_HEADS, RWKV_HEAD, RWKV_HEAD))],
            scratch_shapes=[pltpu.VMEM((SUBLANES, D_RWKV_IN), F32),
                            pltpu.VMEM((D_RWKV // LANES, LANES, LANES), F32)],
            compiler_params=_params(2, 48), name="p_rwkv")(
                hp, gmix, w_in_b, mu, wwa, w0, a0, g2, kkw, kaw, rk, rlg, rlb)

        def merge(x2d, ys, tm):
            rows = x2d.shape[0]
            xs = pl.BlockSpec((tm, D_MODEL), lambda i: (i, 0))
            ysp = pl.BlockSpec((tm, D_LRU), lambda i: (i, 0))
            return pl.pallas_call(
                _merge_body, grid=(rows // tm,),
                in_specs=[xs, L(gmix), win_g, ysp, ysp, ysp, ysp, L(wP), L(wO)],
                out_specs=xs, out_shape=_sds((rows, D_MODEL)),
                compiler_params=_params(1, 56), name="merge")(x2d, gmix, w_in_b, *ys, wP, wO)

        hp = merge(hp.reshape(B * T, D_MODEL), [y.reshape(B * T, -1) for y in (yA, yB, yC, yD)],
                   TM_MIX).reshape(B, T, D_MODEL)

        kvb_spec = pl.BlockSpec((None, None, N_MEM, D_MODEL), lambda b, t: (l, b, 0, 0))
        hp = pl.pallas_call(
            _p_xattn_body, grid=(B, nt),
            in_specs=[xt_spec, L(gxat), L(wXq), kvb_spec, kvb_spec, L(wXo)],
            out_specs=xt_spec, out_shape=_sds((B, T, D_MODEL)),
            compiler_params=_params(2, 48), name="p_xattn")(hp, gxat, wXq, p_mk_b, p_mv_b, wXo)

        hp, p_ffn8 = pl.pallas_call(
            _p_ffn_body, grid=(B, nt),
            in_specs=[xt_spec, L(gffn), wup_g, wup_v, L(cwg), L(cwv), L(cbg), L(cbv), L(wDn)],
            out_specs=[xt_spec, tail_spec(SUBLANES, 2 * D_FF)],
            out_shape=[_sds((B, T, D_MODEL)), _sds((B, SUBLANES, 2 * D_FF))],
            scratch_shapes=[pltpu.VMEM((SUBLANES, 2 * D_FF), F32)],
            compiler_params=_params(2, 56), name="p_ffn")(hp, gffn, w_up_b, w_up_b, cwg, cwv, cbg, cbv, wDn)

        outs["p_conv"].append(p_conv8[:, SUBLANES - 3:])
        outs["p_h"].append(p_h8[:, SUBLANES - 1])
        outs["p_shift"].append(p_shift8[:, SUBLANES - 1:])
        outs["p_S"].append(p_S)
        outs["p_pool"].append(p_pool16[:, POOL_HIST - POOL_BUF:])
        outs["p_ffn"].append(p_ffn8[:, SUBLANES - 2:])

        lconv = jnp.swapaxes(state_lru_conv[l], 0, 1)
        spool = jnp.swapaxes(state_pool[l], 0, 1)
        sffn = jnp.swapaxes(state_ffn_conv[l], 0, 1)
        shift = state_rwkv_shift[l].reshape(SB, D_RWKV_IN)
        row512 = _sds((SB, D_LRU))
        chan = _sds((D_RWKV, SB))
        mix_in = [hs, gmix[l], w_in_b, w_in_b, w_in_b, w_in_b,
                  lconv, state_lru_h[l], lru_conv_w[l], lcb[l], wra[l], bra[l], wix[l], bix[l], lam[l],
                  shift, mu[l], wwa[l], w0[l], a0[l], g2[l], kkw[l], kaw[l],
                  spool, wpool[l], psc[l],
                  glg[l], glb[l], wsd[l], bsr[l]]
        mix_out = [_sds((SB, D_LRU), BF16), row512, row512, _sds((SB, D_POOL), BF16), row512,
                   _sds((SB, D_GMLP), BF16), row512, _sds((SB, D_RWKV_IN)),
                   chan, chan, chan, chan, chan, chan, chan]
        (yA, s_xa, s_h, yC, s_pc, yD, s_zv, s_p, rT, wT, kT, vT, kkT, aT, gT) = pl.pallas_call(
            _s_mix_body, grid=(1,),
            in_specs=[_whole(mix_in[0].shape), _whole(mix_in[1].shape),
                      win_a, win_b, win_c, win_d]
                     + [_whole(a.shape) for a in mix_in[6:]],
            out_specs=[_whole(o.shape) for o in mix_out], out_shape=mix_out,
            compiler_params=_params(1, 56), name="s_mix")(*mix_in)

        hd_spec = pl.BlockSpec((RWKV_HEAD, SB), lambda h: (h, 0))
        col_spec = pl.BlockSpec((None, RWKV_HEAD, 1), lambda h: (l, h, 0))
        st_spec = pl.BlockSpec((None, None, RWKV_HEAD, RWKV_HEAD, SB), lambda h: (l, h, 0, 0, 0))
        yB, s_S5 = pl.pallas_call(
            _s_rwkv_body, grid=(RWKV_HEADS,),
            in_specs=[st_spec] + [hd_spec] * 7 + [col_spec] * 3 + [pl.BlockSpec(memory_space=pl.ANY)],
            out_specs=[_whole((SB, D_RWKV)), st_spec],
            out_shape=[_sds((SB, D_RWKV), BF16), _sds(state_S5.shape)],
            scratch_shapes=[pltpu.VMEM((D_RWKV, SB), F32)],
            input_output_aliases={11: 1},
            compiler_params=_params(1, 48), name="s_rwkv")(
                state_S5, rT, wT, kT, vT, kkT, aT, gT, rk_col, rlg_col, rlb_col, s_S5)

        hs = merge(hs, [yA, yB, yC, yD], SB)

        kc_spec = pl.BlockSpec((None, S_ATT_BB, N_MEM * SUBLANES, LANES), lambda i: (l, i, 0, 0))
        xs_spec = _whole((SB, D_MODEL))
        hs = pl.pallas_call(
            _s_xattn_body, grid=(SB // S_ATT_BB,),
            in_specs=[xs_spec, L(gxat), L(wXq), kc_spec, kc_spec, L(wXo)],
            out_specs=xs_spec, out_shape=_sds((SB, D_MODEL)),
            scratch_shapes=[pltpu.VMEM((SB, D_MODEL), F32), pltpu.VMEM((SB, D_MODEL), F32)],
            compiler_params=_params(1, 56), name="s_xattn")(
                hs, gxat, wXq, cache_k, cache_v, wXo)

        ncf = D_FF // FF_CW
        wu_spec = pl.BlockSpec((None, D_MODEL, FF_CW), lambda j: (l, 0, j))
        wuv_spec = pl.BlockSpec((None, D_MODEL, FF_CW), lambda j: (l, 0, j + ncf))
        sg_spec = pl.BlockSpec((2, SB, FF_CW), lambda j: (0, 0, j))
        sv_spec = pl.BlockSpec((2, SB, FF_CW), lambda j: (0, 0, j + ncf))
        cw_spec = pl.BlockSpec((None, 3, FF_CW), lambda j: (l, 0, j))
        cb_spec = pl.BlockSpec((None, 1, FF_CW), lambda j: (l, 0, j))
        hh_spec = pl.BlockSpec((SB, FF_CW), lambda j: (0, j))
        hs, s_hg, s_hv = pl.pallas_call(
            _s_ffn_body, grid=(ncf,),
            in_specs=[xs_spec, L(gffn), wu_spec, wuv_spec, sg_spec, sv_spec, cw_spec, cw_spec, cb_spec, cb_spec,
                      pl.BlockSpec((None, FF_CW, D_MODEL), lambda j: (l, j, 0))],
            out_specs=[xs_spec, hh_spec, hh_spec],
            out_shape=[_sds((SB, D_MODEL)), _sds((SB, D_FF)), _sds((SB, D_FF))],
            scratch_shapes=[pltpu.VMEM((SB, D_MODEL), F32)],
            compiler_params=_params(1, 48), name="s_ffn")(
                hs, gffn, w_up_b, w_up_b, sffn, sffn, cwg, cwv, cbg, cbv, wDn)

        outs["s_conv"].append(jnp.concatenate([state_lru_conv[l][:, 1:], s_xa[:, None]], axis=1))
        outs["s_h"].append(s_h)
        outs["s_shift"].append(s_p[:, None])
        outs["s_pool"].append(jnp.concatenate([state_pool[l][:, 1:], s_pc[:, None]], axis=1))
        outs["s_ffn"].append(jnp.concatenate(
            [state_ffn_conv[l][:, 1:], jnp.concatenate([s_hg, s_hv], axis=-1)[:, None]], axis=1))
        outs["s_cv"].append(s_zv[:, None])

    def final_norm(x2d, tm):
        rows = x2d.shape[0]
        spec = pl.BlockSpec((tm, D_MODEL), lambda i: (i, 0))
        return pl.pallas_call(_final_norm_body, grid=(rows // tm,),
                              in_specs=[spec, _whole((1, D_MODEL))], out_specs=spec,
                              out_shape=_sds((rows, D_MODEL)),
                              compiler_params=_params(1, 32), name="final_norm")(x2d, gfin)

    y_prompt = final_norm(hp.reshape(B * T, D_MODEL), 1024).reshape(B, T, D_MODEL)
    y_sample = final_norm(hs, SB).reshape(SB, 1, D_MODEL)
    stk = {k: jnp.stack(v, axis=0) for k, v in outs.items()}
    def mem_out(kv):
        kv = kv.reshape(DEPTH, B, N_MEM, X_HEAD_DIM // LANES, X_HEADS, LANES)
        return jnp.swapaxes(kv, 3, 4).reshape(DEPTH, B, N_MEM, X_HEADS, X_HEAD_DIM)

    return (y_prompt, y_sample,
            stk["p_conv"], stk["s_conv"], stk["p_h"], stk["s_h"],
            stk["p_shift"], stk["s_shift"], stk["p_S"], jnp.transpose(s_S5, (0, 4, 1, 2, 3)),
            stk["p_pool"], stk["s_pool"], stk["p_ffn"], stk["s_ffn"],
            mem_out(p_mk), mem_out(p_mv), stk["s_cv"])
```
